```python
import jax
import jax.numpy as jnp
from jax import lax
import numpy as np

D_MODEL = 1024
BATCH = 8
SEQ = 2048
DEPTH = 2

HEAD_DIM = 64
GMLP_GROUPS = 8
GMLP_WIDTH = GMLP_GROUPS * HEAD_DIM
CHUNK = 128
FOX_HEADS = 8
FOX_WIDTH = FOX_HEADS * HEAD_DIM
Q_BLOCK = 128
SWA_Q_HEADS = 8
SWA_KV_HEADS = 2
SWA_GROUP = SWA_Q_HEADS // SWA_KV_HEADS
SWA_WIDTH = SWA_Q_HEADS * HEAD_DIM
KV_WIDTH = SWA_KV_HEADS * HEAD_DIM
WINDOW = 128
CONV_CH = 512
CONV_WIDTH = 31
ROPE_THETA = 500000.0
ROT_DIM = HEAD_DIM // 4
N_GROUPS = 4
EXPERTS_PER_GROUP = 8
N_EXPERTS = N_GROUPS * EXPERTS_PER_GROUP
TOP_K = 2
D_EXPERT = 256
D_PLE = 256
EPS = 1e-6
N_EVEN = (DEPTH + 1) // 2
N_ODD = DEPTH // 2
EVEN_IN = 2 * GMLP_WIDTH + 3 * FOX_WIDTH + FOX_HEADS
EVEN_SPLITS = (2 * GMLP_WIDTH, 2 * GMLP_WIDTH + FOX_WIDTH, 2 * GMLP_WIDTH + 2 * FOX_WIDTH, 2 * GMLP_WIDTH + 3 * FOX_WIDTH)
ODD_IN = SWA_WIDTH + 2 * KV_WIDTH + 2 * CONV_CH
ODD_SPLITS = (SWA_WIDTH, SWA_WIDTH + KV_WIDTH, SWA_WIDTH + 2 * KV_WIDTH)

kernel_name = 'hybrid_gmlp_fox_swa_conformer_hmoe'


def rmsnorm(x, g):
    xf = x.astype(jnp.float32)
    y = xf * lax.rsqrt(jnp.mean(xf * xf, axis=-1, keepdims=True) + EPS)
    return (y * g.astype(jnp.float32)).astype(x.dtype)


def layernorm(x, g, b):
    xf = x.astype(jnp.float32)
    mu = jnp.mean(xf, axis=-1, keepdims=True)
    var = jnp.mean(jnp.square(xf - mu), axis=-1, keepdims=True)
    y = (xf - mu) * lax.rsqrt(var + EPS) * g.astype(jnp.float32) + b.astype(jnp.float32)
    return y.astype(x.dtype)


def partial_rope(x, pos):
    half = ROT_DIM // 2
    inv_freq = ROPE_THETA ** (-jnp.arange(half, dtype=jnp.float32) * 2.0 / ROT_DIM)
    ang = pos.astype(jnp.float32)[:, None] * inv_freq[None, :]
    cos = jnp.cos(ang)[None, :, None, :]
    sin = jnp.sin(ang)[None, :, None, :]
    xr = x[..., :ROT_DIM].astype(jnp.float32)
    x1, x2 = xr[..., :half], xr[..., half:]
    rot = jnp.concatenate([x1 * cos - x2 * sin, x2 * cos + x1 * sin], axis=-1).astype(x.dtype)
    return jnp.concatenate([rot, x[..., ROT_DIM:]], axis=-1)


def chunked_gmlp(z, ln_g, ln_b, w_s, b_s):
    B, S, _ = z.shape
    z = jax.nn.gelu(z)
    u, v = z[..., :GMLP_WIDTH], z[..., GMLP_WIDTH:]
    v = layernorm(v, ln_g, ln_b)
    v = v.reshape(B, S // CHUNK, CHUNK, GMLP_GROUPS, HEAD_DIM)
    causal = jnp.tril(jnp.ones((CHUNK, CHUNK), dtype=bool))
    w = jnp.where(causal[None], w_s, jnp.zeros_like(w_s))
    mixed = jnp.einsum('gts,bcsgd->bctgd', w, v) + b_s.T[None, None, :, :, None]
    return u * mixed.reshape(B, S, GMLP_WIDTH)


def forgetting_attention(q, k, v, f_logit, q_g, k_g):
    B, S, H, Dh = q.shape
    q = rmsnorm(q, q_g)
    k = rmsnorm(k, k_g)
    c = jnp.cumsum(jax.nn.log_sigmoid(f_logit.astype(jnp.float32)), axis=1)
    c_k = c.transpose(0, 2, 1)
    n_blk = S // Q_BLOCK
    q_blocks = q.reshape(B, n_blk, Q_BLOCK, H, Dh).transpose(1, 0, 2, 3, 4)
    c_blocks = c.reshape(B, n_blk, Q_BLOCK, H).transpose(1, 0, 3, 2)
    k_pos = jnp.arange(S)
    scale = Dh ** -0.5

    def one_block(args):
        q_i, c_i, i = args
        s = jnp.einsum('bqhd,bkhd->bhqk', q_i, k, preferred_element_type=jnp.float32) * scale
        s = s + c_i[..., :, None] - c_k[:, :, None, :]
        q_pos = i * Q_BLOCK + jnp.arange(Q_BLOCK)
        s = jnp.where(q_pos[:, None] >= k_pos[None, :], s, -jnp.inf)
        prob = jax.nn.softmax(s, axis=-1)
        return jnp.einsum('bhqk,bkhd->bqhd', prob.astype(v.dtype), v)

    out = lax.map(one_block, (q_blocks, c_blocks, jnp.arange(n_blk)))
    return out.transpose(1, 0, 2, 3, 4).reshape(B, S, H * Dh)


def sliding_window_attention(q, k, v, sinks, q_g, k_g, pos):
    B, S, _, Dh = q.shape
    q = partial_rope(rmsnorm(q, q_g), pos)
    k = partial_rope(rmsnorm(k, k_g), pos)
    n_blk = S // WINDOW
    qb = q.reshape(B, n_blk, WINDOW, SWA_KV_HEADS, SWA_GROUP, Dh)

    def band(t):
        tb = t.reshape(B, n_blk, WINDOW, SWA_KV_HEADS, Dh)
        prev = jnp.pad(tb, ((0, 0), (1, 0), (0, 0), (0, 0), (0, 0)))[:, :-1]
        return jnp.concatenate([prev, tb], axis=2)

    kb, vb = band(k), band(v)
    s = jnp.einsum('bnqkgd,bnskd->bnkgqs', qb, kb, preferred_element_type=jnp.float32) * (Dh ** -0.5)
    i = jnp.arange(WINDOW)[:, None]
    j = jnp.arange(2 * WINDOW)[None, :]
    in_band = (j > i) & (j <= i + WINDOW)
    not_pad = (jnp.arange(n_blk)[:, None, None] > 0) | (j >= WINDOW)[None]
    mask = in_band[None] & not_pad
    s = jnp.where(mask[None, :, None, None], s, -jnp.inf)
    sink = sinks.astype(jnp.float32).reshape(SWA_KV_HEADS, SWA_GROUP)[None, None, :, :, None, None]
    sink = jnp.broadcast_to(sink, s.shape[:-1] + (1,))
    prob = jax.nn.softmax(jnp.concatenate([s, sink], axis=-1), axis=-1)[..., :-1]
    o = jnp.einsum('bnkgqs,bnskd->bnqkgd', prob.astype(v.dtype), vb)
    return o.reshape(B, S, SWA_WIDTH)


def conformer_conv(z, w_dw, ln_g, ln_b):
    a, gate = z[..., :CONV_CH], z[..., CONV_CH:]
    h = a * jax.nn.sigmoid(gate)
    h = lax.conv_general_dilated(h, w_dw[:, None, :].astype(h.dtype), window_strides=(1,),
                                 padding=((CONV_WIDTH - 1, 0),),
                                 dimension_numbers=('NWC', 'WIO', 'NWC'),
                                 feature_group_count=CONV_CH)
    h = layernorm(h, ln_g, ln_b)
    return jax.nn.silu(h)


def hierarchical_moe(x, w_coarse, b_coarse, w_fine, b_fine, w_gate, w_up, w_down):
    coarse = (x @ w_coarse).astype(jnp.float32) + b_coarse.astype(jnp.float32)
    p_coarse = jax.nn.softmax(coarse, axis=-1)
    g_idx = jnp.argmax(coarse, axis=-1)
    p_g = jnp.take_along_axis(p_coarse, g_idx[:, None], axis=-1)
    fine_all = jnp.einsum('td,gde->tge', x, w_fine).astype(jnp.float32) + b_fine.astype(jnp.float32)
    fine = jnp.take_along_axis(fine_all, g_idx[:, None, None], axis=1)[:, 0]
    top_v, top_i = lax.top_k(fine, TOP_K)
    p_e = jax.nn.softmax(top_v, axis=-1) * p_g
    expert_id = g_idx[:, None] * EXPERTS_PER_GROUP + top_i
    combine = jnp.einsum('tk,tke->te', p_e, jax.nn.one_hot(expert_id, N_EXPERTS, dtype=jnp.float32))
    h = jax.nn.silu(jnp.einsum('td,edf->tef', x, w_gate)) * jnp.einsum('td,edf->tef', x, w_up)
    h = h * combine.astype(h.dtype)[..., None]
    return jnp.einsum('tef,efd->td', h, w_down)


def setup_inputs(seed: int = 0) -> dict:
    key = jax.random.key(seed)
    ks = jax.random.split(key, 32)
    f32 = jnp.float32

    def nrm(k, shape, scale):
        return jax.random.normal(k, shape, f32) * scale

    def gain(k, shape):
        return 1.0 + 0.05 * jax.random.normal(k, shape, f32)

    D = D_MODEL
    return {
        'x': nrm(ks[0], (BATCH, SEQ, D), 1.0),
        'p': nrm(ks[1], (DEPTH, BATCH, SEQ, D_PLE), 1.0),
        'norm_mix': gain(ks[2], (DEPTH, D)),
        'even_w_in': nrm(ks[3], (N_EVEN, D, EVEN_IN), D ** -0.5),
        'fox_b_f': 2.0 + 0.5 * jax.random.normal(ks[4], (N_EVEN, FOX_HEADS), f32),
        'gmlp_ln_g': gain(ks[5], (N_EVEN, GMLP_WIDTH)),
        'gmlp_ln_b': nrm(ks[6], (N_EVEN, GMLP_WIDTH), 0.02),
        'gmlp_w_s': nrm(ks[7], (N_EVEN, GMLP_GROUPS, CHUNK, CHUNK), CHUNK ** -0.5),
        'gmlp_b_s': 1.0 + 0.1 * jax.random.normal(ks[8], (N_EVEN, GMLP_GROUPS, CHUNK), f32),
        'fox_q_norm': gain(ks[9], (N_EVEN, HEAD_DIM)),
        'fox_k_norm': gain(ks[10], (N_EVEN, HEAD_DIM)),
        'even_w_out': nrm(ks[11], (N_EVEN, GMLP_WIDTH + FOX_WIDTH, D), (GMLP_WIDTH + FOX_WIDTH) ** -0.5),
        'odd_w_in': nrm(ks[12], (N_ODD, D, ODD_IN), D ** -0.5),
        'swa_q_norm': gain(ks[13], (N_ODD, HEAD_DIM)),
        'swa_k_norm': gain(ks[14], (N_ODD, HEAD_DIM)),
        'swa_sinks': nrm(ks[15], (N_ODD, SWA_Q_HEADS), 0.5),
        'conv_w': nrm(ks[16], (N_ODD, CONV_WIDTH, CONV_CH), CONV_WIDTH ** -0.5),
        'conv_ln_g': gain(ks[17], (N_ODD, CONV_CH)),
        'conv_ln_b': nrm(ks[18], (N_ODD, CONV_CH), 0.02),
        'odd_w_out': nrm(ks[19], (N_ODD, SWA_WIDTH + CONV_CH, D), (SWA_WIDTH + CONV_CH) ** -0.5),
        'norm_ffn': gain(ks[20], (DEPTH, D)),
        'moe_w_coarse': nrm(ks[21], (DEPTH, D, N_GROUPS), D ** -0.5),
        'moe_b_coarse': nrm(ks[22], (DEPTH, N_GROUPS), 0.01),
        'moe_w_fine': nrm(ks[23], (DEPTH, N_GROUPS, D, EXPERTS_PER_GROUP), D ** -0.5),
        'moe_b_fine': nrm(ks[24], (DEPTH, N_GROUPS, EXPERTS_PER_GROUP), 0.01),
        'moe_w_gate': nrm(ks[25], (DEPTH, N_EXPERTS, D, D_EXPERT), D ** -0.5),
        'moe_w_up': nrm(ks[26], (DEPTH, N_EXPERTS, D, D_EXPERT), D ** -0.5),
        'moe_w_down': nrm(ks[27], (DEPTH, N_EXPERTS, D_EXPERT, D), D_EXPERT ** -0.5),
        'ple_w_proj': nrm(ks[28], (DEPTH, D_PLE, D), D_PLE ** -0.5),
        'ple_norm': gain(ks[29], (DEPTH, D)),
        'ple_w_gate': nrm(ks[30], (DEPTH, D, D), D ** -0.5),
    }


def reference(x, p, norm_mix, even_w_in, fox_b_f, gmlp_ln_g, gmlp_ln_b, gmlp_w_s, gmlp_b_s,
              fox_q_norm, fox_k_norm, even_w_out, odd_w_in, swa_q_norm, swa_k_norm, swa_sinks,
              conv_w, conv_ln_g, conv_ln_b, odd_w_out, norm_ffn, moe_w_coarse, moe_b_coarse,
              moe_w_fine, moe_b_fine, moe_w_gate, moe_w_up, moe_w_down, ple_w_proj, ple_norm,
              ple_w_gate):
    B, S, D = x.shape
    pos = jnp.arange(S)
    h = x
    for i in range(DEPTH):
        a = rmsnorm(h, norm_mix[i])
        j = i // 2
        if i % 2 == 0:
            z = a @ even_w_in[j]
            z_a, q, k, v, f = jnp.split(z, EVEN_SPLITS, axis=-1)
            y_a = chunked_gmlp(z_a, gmlp_ln_g[j], gmlp_ln_b[j], gmlp_w_s[j], gmlp_b_s[j])
            y_b = forgetting_attention(q.reshape(B, S, FOX_HEADS, HEAD_DIM),
                                       k.reshape(B, S, FOX_HEADS, HEAD_DIM),
                                       v.reshape(B, S, FOX_HEADS, HEAD_DIM),
                                       f + fox_b_f[j], fox_q_norm[j], fox_k_norm[j])
            mix = jnp.concatenate([y_a, y_b], axis=-1) @ even_w_out[j]
        else:
            z = a @ odd_w_in[j]
            q, k, v, z_d = jnp.split(z, ODD_SPLITS, axis=-1)
            y_c = sliding_window_attention(q.reshape(B, S, SWA_Q_HEADS, HEAD_DIM),
                                           k.reshape(B, S, SWA_KV_HEADS, HEAD_DIM),
                                           v.reshape(B, S, SWA_KV_HEADS, HEAD_DIM),
                                           swa_sinks[j], swa_q_norm[j], swa_k_norm[j], pos)
            y_d = conformer_conv(z_d, conv_w[j], conv_ln_g[j], conv_ln_b[j])
            mix = jnp.concatenate([y_c, y_d], axis=-1) @ odd_w_out[j]
        h = h + mix
        m = rmsnorm(h, norm_ffn[i]).reshape(B * S, D)
        h = h + hierarchical_moe(m, moe_w_coarse[i], moe_b_coarse[i], moe_w_fine[i], moe_b_fine[i],
                                 moe_w_gate[i], moe_w_up[i], moe_w_down[i]).reshape(B, S, D)
        gate = jax.nn.sigmoid(rmsnorm(h, ple_norm[i]) @ ple_w_gate[i])
        h = h + gate * (p[i] @ ple_w_proj[i])
    return h
```

```python
import functools

import jax
import jax.numpy as jnp
from jax import lax
from jax.experimental import pallas as pl
from jax.experimental.pallas import tpu as pltpu

F32 = jnp.float32
BF16 = jnp.bfloat16
HIGHEST = lax.Precision.HIGHEST

D_MODEL = 1024
HEAD_DIM = 64
LANES = 128
GMLP_WIDTH = 512
CHUNK = 128
FOX_WIDTH = 512
FOX_HEADS = 8
SWA_WIDTH = 512
KV_WIDTH = 128
WINDOW = 128
CONV_CH = 512
CONV_WIDTH = 31
CONV_HALO = 32
ROPE_THETA = 500000.0
ROT_DIM = 16
N_GROUPS = 4
EXPERTS_PER_GROUP = 8
N_EXPERTS = 32
D_EXPERT = 256
D_PLE = 256
EPS = 1e-6
NEG = -1e30

EXPERT_TILE = 256
ROW_TILE = 512
GATHER_TILE = 256
DISPATCH_TILE = 512
ATTN_TILE = 256
CONV_TILE = 256
VMEM_LIMIT = 56 * 1024 * 1024


def _cparams(n_axes=1):
    return pltpu.CompilerParams(dimension_semantics=("arbitrary",) * n_axes,
                                vmem_limit_bytes=VMEM_LIMIT)


def _rms(x, gain):
    return x * lax.rsqrt(jnp.mean(x * x, axis=-1, keepdims=True) + EPS) * gain


def _layernorm(x, g, b):
    mu = jnp.mean(x, axis=-1, keepdims=True)
    xc = x - mu
    var = jnp.mean(xc * xc, axis=-1, keepdims=True)
    return xc * lax.rsqrt(var + EPS) * g + b


def _head_rms(z, gain):
    r = lax.broadcasted_iota(jnp.int32, (LANES, LANES), 0) // HEAD_DIM
    c = lax.broadcasted_iota(jnp.int32, (LANES, LANES), 1) // HEAD_DIM
    bd = jnp.where(r == c, 1.0 / HEAD_DIM, 0.0).astype(F32)
    outs = []
    for j in range(z.shape[1] // LANES):
        zj = z[:, j * LANES:(j + 1) * LANES]
        ms = jnp.dot(zj * zj, bd, precision=HIGHEST, preferred_element_type=F32)
        outs.append(zj * lax.rsqrt(ms + EPS))
    zn = outs[0] if len(outs) == 1 else jnp.concatenate(outs, axis=1)
    return zn * gain


def _lane_lo(shape):
    return (lax.broadcasted_iota(jnp.int32, shape, len(shape) - 1) % LANES) < HEAD_DIM


def _proj_even_kernel(h_ref, g_ref, w_ref, bf_ref, lng_ref, lnb_ref, ws_ref, bs_ref, qg_ref, kg_ref,
                      ya_ref, q_ref, k_ref, v_ref, c_ref, ct_ref, carry_ref, *, tm, tiles_per_seq):
    i = pl.program_id(0)
    a = _rms(h_ref[...], g_ref[...]).astype(BF16)

    za = jax.nn.gelu(jnp.dot(a, w_ref[:, 0:2 * GMLP_WIDTH], preferred_element_type=F32))
    u = za[:, :GMLP_WIDTH]
    vln = _layernorm(za[:, GMLP_WIDTH:], lng_ref[...], lnb_ref[...]).astype(BF16)
    rr = lax.broadcasted_iota(jnp.int32, (CHUNK, CHUNK), 0)
    cc = lax.broadcasted_iota(jnp.int32, (CHUNK, CHUNK), 1)
    lo = _lane_lo((CHUNK, LANES))
    for j in range(GMLP_WIDTH // LANES):
        w_a = jnp.where(rr >= cc, ws_ref[2 * j], 0.0).astype(BF16)
        w_b = jnp.where(rr >= cc, ws_ref[2 * j + 1], 0.0).astype(BF16)
        cols = slice(j * LANES, (j + 1) * LANES)
        for c in range(tm // CHUNK):
            rows = slice(c * CHUNK, (c + 1) * CHUNK)
            vp = vln[rows, cols]
            mixed = jnp.where(lo, jnp.dot(w_a, vp, preferred_element_type=F32),
                              jnp.dot(w_b, vp, preferred_element_type=F32)) + bs_ref[:, cols]
            ya_ref[rows, cols] = (u[rows, cols] * mixed).astype(BF16)

    q0 = 2 * GMLP_WIDTH
    zq = jnp.dot(a, w_ref[:, q0:q0 + FOX_WIDTH], preferred_element_type=F32)
    q_ref[...] = (_head_rms(zq, qg_ref[...]) * (HEAD_DIM ** -0.5)).astype(BF16)
    zk = jnp.dot(a, w_ref[:, q0 + FOX_WIDTH:q0 + 2 * FOX_WIDTH], preferred_element_type=F32)
    k_ref[...] = _head_rms(zk, kg_ref[...]).astype(BF16)
    v_ref[...] = jnp.dot(a, w_ref[:, q0 + 2 * FOX_WIDTH:q0 + 3 * FOX_WIDTH],
                         preferred_element_type=F32).astype(BF16)

    zf = jnp.dot(a, w_ref[:, q0 + 3 * FOX_WIDTH:], preferred_element_type=F32) + bf_ref[...]
    ls = jnp.minimum(zf, 0.0) - jnp.log(1.0 + jnp.exp(-jnp.abs(zf)))

    @pl.when(i % tiles_per_seq == 0)
    def _():
        carry_ref[...] = jnp.zeros_like(carry_ref)

    tr = lax.broadcasted_iota(jnp.int32, (tm, tm), 0)
    tc = lax.broadcasted_iota(jnp.int32, (tm, tm), 1)
    tri = jnp.where(tr >= tc, 1.0, 0.0).astype(F32)
    c = jnp.dot(tri, ls, precision=HIGHEST, preferred_element_type=F32) + carry_ref[...]
    carry_ref[...] = c[tm - 1:tm, :]
    c_ref[...] = c
    ct_ref[0] = c.T[0:FOX_HEADS, :]


def _proj_even(h, g, w, bf, lng, lnb, ws, bs_full, qg, kg, *, seq):
    T = h.shape[0]
    tm = min(ROW_TILE, seq)
    n_in = w.shape[1]
    const = lambda *shape: pl.BlockSpec(shape, lambda i: (0,) * len(shape))
    row = lambda width: pl.BlockSpec((tm, width), lambda i: (i, 0))
    tps = seq // tm
    return pl.pallas_call(
        functools.partial(_proj_even_kernel, tm=tm, tiles_per_seq=tps),
        grid=(T // tm,),
        in_specs=[row(D_MODEL), const(1, D_MODEL), const(D_MODEL, n_in), const(1, LANES),
                  const(1, GMLP_WIDTH), const(1, GMLP_WIDTH), const(8, CHUNK, CHUNK),
                  const(CHUNK, GMLP_WIDTH), const(1, FOX_WIDTH), const(1, FOX_WIDTH)],
        out_specs=[row(GMLP_WIDTH), row(FOX_WIDTH), row(FOX_WIDTH), row(FOX_WIDTH), row(LANES),
                   pl.BlockSpec((1, FOX_HEADS, tm), lambda i: (i // tps, 0, i % tps))],
        out_shape=[jax.ShapeDtypeStruct((T, GMLP_WIDTH), BF16),
                   jax.ShapeDtypeStruct((T, FOX_WIDTH), BF16),
                   jax.ShapeDtypeStruct((T, FOX_WIDTH), BF16),
                   jax.ShapeDtypeStruct((T, FOX_WIDTH), BF16),
                   jax.ShapeDtypeStruct((T, LANES), F32),
                   jax.ShapeDtypeStruct((T // seq, FOX_HEADS, seq), F32)],
        scratch_shapes=[pltpu.VMEM((1, LANES), F32)],
        compiler_params=_cparams(1),
        name="proj_even",
    )(h, g, w, bf, lng, lnb, ws, bs_full, qg, kg)


def _fox_kernel(q_ref, k_ref, v_ref, c_ref, ct_ref, o_ref, *, tq):
    i = pl.program_id(1)
    lo = _lane_lo((1, LANES))
    rr = lax.broadcasted_iota(jnp.int32, (tq, tq), 0)
    cc = lax.broadcasted_iota(jnp.int32, (tq, tq), 1)
    causal = rr >= cc
    cq = c_ref[...]
    nt = (((1,), (1,)), ((), ()))
    for hp in range(FOX_WIDTH // LANES):
        cols = slice(hp * LANES, (hp + 1) * LANES)
        q = q_ref[:, cols]
        zero = jnp.zeros_like(q)
        q_a = jnp.where(lo, q, zero)
        q_b = jnp.where(lo, zero, q)
        cq_a = cq[:, 2 * hp:2 * hp + 1]
        cq_b = cq[:, 2 * hp + 1:2 * hp + 2]

        def scores(j, masked):
            start = pl.multiple_of(j * tq, tq)
            ks = k_ref[pl.ds(start, tq), cols]
            ck_a = ct_ref[0, 2 * hp:2 * hp + 1, pl.ds(start, tq)]
            ck_b = ct_ref[0, 2 * hp + 1:2 * hp + 2, pl.ds(start, tq)]
            s_a = lax.dot_general(q_a, ks, nt, preferred_element_type=F32) + cq_a - ck_a
            s_b = lax.dot_general(q_b, ks, nt, preferred_element_type=F32) + cq_b - ck_b
            if masked:
                s_a = jnp.where(causal, s_a, NEG)
                s_b = jnp.where(causal, s_b, NEG)
            return s_a, s_b, v_ref[pl.ds(start, tq), cols]

        def update(carry, s_a, s_b, vs):
            m_a, l_a, m_b, l_b, acc = carry
            n_a = jnp.maximum(m_a, jnp.max(s_a, axis=-1, keepdims=True))
            n_b = jnp.maximum(m_b, jnp.max(s_b, axis=-1, keepdims=True))
            p_a = jnp.exp(s_a - n_a)
            p_b = jnp.exp(s_b - n_b)
            al_a = jnp.exp(m_a - n_a)
            al_b = jnp.exp(m_b - n_b)
            l_a = l_a * al_a + jnp.sum(p_a, axis=-1, keepdims=True)
            l_b = l_b * al_b + jnp.sum(p_b, axis=-1, keepdims=True)
            pv = jnp.where(lo, jnp.dot(p_a.astype(BF16), vs, preferred_element_type=F32),
                           jnp.dot(p_b.astype(BF16), vs, preferred_element_type=F32))
            acc = acc * jnp.where(lo, al_a, al_b) + pv
            return n_a, l_a, n_b, l_b, acc

        def body(j, carry):
            return update(carry, *scores(j, False))

        col = lambda v: jnp.full((tq, 1), v, F32)
        init = (col(NEG), col(0.0), col(NEG), col(0.0), jnp.zeros((tq, LANES), F32))
        carry = lax.fori_loop(0, i, body, init)
        _, l_a, _, l_b, acc = update(carry, *scores(i, True))
        o_ref[:, cols] = (acc / jnp.where(lo, l_a, l_b)).astype(BF16)


def _fox_attention(q, k, v, c, ct, *, seq):
    T = q.shape[0]
    B = T // seq
    tq = min(ATTN_TILE, seq)
    nq = seq // tq
    return pl.pallas_call(
        functools.partial(_fox_kernel, tq=tq),
        grid=(B, nq),
        in_specs=[pl.BlockSpec((tq, FOX_WIDTH), lambda b, i: (b * nq + i, 0)),
                  pl.BlockSpec((seq, FOX_WIDTH), lambda b, i: (b, 0)),
                  pl.BlockSpec((seq, FOX_WIDTH), lambda b, i: (b, 0)),
                  pl.BlockSpec((tq, LANES), lambda b, i: (b * nq + i, 0)),
                  pl.BlockSpec((1, FOX_HEADS, seq), lambda b, i: (b, 0, 0))],
        out_specs=pl.BlockSpec((tq, FOX_WIDTH), lambda b, i: (b * nq + i, 0)),
        out_shape=jax.ShapeDtypeStruct((T, FOX_WIDTH), BF16),
        compiler_params=_cparams(2),
        name="fox_attention",
    )(q, k, v, c, ct)


def _outproj_router_kernel(h_ref, ya_ref, yb_ref, wo_ref, g_ref, wr_ref, br_ref,
                           h1_ref, m_ref, route_ref, cnt_ref, carry_ref, *, tm):
    i = pl.program_id(0)
    half = wo_ref.shape[0] // 2
    mix = (jnp.dot(ya_ref[...], wo_ref[0:half, :], preferred_element_type=F32)
           + jnp.dot(yb_ref[...], wo_ref[half:, :], preferred_element_type=F32))
    h1 = h_ref[...] + mix
    h1_ref[...] = h1
    m = _rms(h1, g_ref[...])
    m_ref[...] = m

    logits = jnp.dot(m, wr_ref[...], precision=HIGHEST, preferred_element_type=F32) + br_ref[...]
    lane_i = lax.broadcasted_iota(jnp.int32, (tm, LANES), 1)
    lane = lane_i.astype(F32)
    group_of_lane = (lane_i // EXPERTS_PER_GROUP).astype(F32)
    is_coarse = (lane_i >= N_EXPERTS) & (lane_i < N_EXPERTS + N_GROUPS)
    coarse = jnp.where(is_coarse, logits, NEG)
    cmax = jnp.max(coarse, axis=-1, keepdims=True)
    gidx = jnp.min(jnp.where(coarse == cmax, lane - N_EXPERTS, float(LANES)), axis=-1, keepdims=True)
    p_g = 1.0 / jnp.sum(jnp.where(is_coarse, jnp.exp(coarse - cmax), 0.0), axis=-1, keepdims=True)
    in_group = (lane_i < N_EXPERTS) & (group_of_lane == gidx)
    fine = jnp.where(in_group, logits, NEG)
    v1 = jnp.max(fine, axis=-1, keepdims=True)
    i1 = jnp.min(jnp.where(fine == v1, lane, float(LANES)), axis=-1, keepdims=True)
    fine2 = jnp.where(lane == i1, NEG, fine)
    v2 = jnp.max(fine2, axis=-1, keepdims=True)
    i2 = jnp.min(jnp.where(fine2 == v2, lane, float(LANES)), axis=-1, keepdims=True)
    e2 = jnp.exp(v2 - v1)
    w1 = p_g / (1.0 + e2)
    w2 = p_g * e2 / (1.0 + e2)

    @pl.when(i == 0)
    def _():
        carry_ref[...] = jnp.zeros_like(carry_ref)

    hit1 = lane == i1
    hit2 = lane == i2
    onehot = jnp.where(hit1 | hit2, 1.0, 0.0).astype(F32)
    tr = lax.broadcasted_iota(jnp.int32, (tm, tm), 0)
    tc = lax.broadcasted_iota(jnp.int32, (tm, tm), 1)
    strict = jnp.where(tr > tc, 1.0, 0.0).astype(BF16)
    before = jnp.dot(strict, onehot.astype(BF16), preferred_element_type=F32) + carry_ref[...]
    r1 = jnp.sum(jnp.where(hit1, before, 0.0), axis=-1, keepdims=True)
    r2 = jnp.sum(jnp.where(hit2, before, 0.0), axis=-1, keepdims=True)
    total = carry_ref[...] + jnp.sum(onehot, axis=0, keepdims=True)
    carry_ref[...] = total
    cnt_ref[...] = jnp.broadcast_to(total, cnt_ref.shape)

    route = jnp.where(lane == 0, i1, 0.0)
    route = jnp.where(lane == 1, i2, route)
    route = jnp.where(lane == 2, r1, route)
    route = jnp.where(lane == 3, r2, route)
    route = jnp.where(lane == 4, w1, route)
    route = jnp.where(lane == 5, w2, route)
    route_ref[...] = route


def _outproj_router(h, ya, yb, wo, g, wr, br):
    T = h.shape[0]
    tm = min(ROW_TILE, T)
    const = lambda *shape: pl.BlockSpec(shape, lambda i: (0,) * len(shape))
    row = lambda width: pl.BlockSpec((tm, width), lambda i: (i, 0))
    return pl.pallas_call(
        functools.partial(_outproj_router_kernel, tm=tm),
        grid=(T // tm,),
        in_specs=[row(D_MODEL), row(ya.shape[1]), row(yb.shape[1]), const(*wo.shape),
                  const(1, D_MODEL), const(D_MODEL, LANES), const(1, LANES)],
        out_specs=[row(D_MODEL), row(D_MODEL), row(LANES), const(8, LANES)],
        out_shape=[jax.ShapeDtypeStruct((T, D_MODEL), F32),
                   jax.ShapeDtypeStruct((T, D_MODEL), F32),
                   jax.ShapeDtypeStruct((T, LANES), F32),
                   jax.ShapeDtypeStruct((8, LANES), F32)],
        scratch_shapes=[pltpu.VMEM((1, LANES), F32)],
        compiler_params=_cparams(1),
        name="outproj_router",
    )(h, ya, yb, wo, g, wr, br)


def _dispatch_kernel(pos_ref, m_hbm, xs_in_hbm, xs_hbm, sem, *, tile):
    del xs_in_hbm
    i = pl.program_id(0)
    n = pl.num_programs(0)

    def row_copy(t, k):
        return pltpu.make_async_copy(m_hbm.at[pl.ds(i * tile + t, 1)],
                                     xs_hbm.at[pl.ds(pos_ref[0, 0, 2 * t + k], 1)], sem)

    def wait_tile():
        pltpu.make_async_copy(m_hbm.at[pl.ds(0, 2 * tile)], xs_hbm.at[pl.ds(0, 2 * tile)], sem).wait()

    def issue(t, carry):
        row_copy(t, 0).start()
        row_copy(t, 1).start()
        return carry

    lax.fori_loop(0, tile, issue, 0)

    @pl.when(i > 0)
    def _():
        wait_tile()

    @pl.when(i == n - 1)
    def _():
        wait_tile()


def _dispatch(pos, m, n_rows):
    T = m.shape[0]
    tile = min(DISPATCH_TILE, T)
    pos3 = pos.reshape(T // tile, 1, 2 * tile)
    xs0 = jnp.zeros((n_rows, D_MODEL), F32)
    return pl.pallas_call(
        functools.partial(_dispatch_kernel, tile=tile),
        grid=(T // tile,),
        in_specs=[pl.BlockSpec((1, 1, 2 * tile), lambda i: (i, 0, 0), memory_space=pltpu.SMEM),
                  pl.BlockSpec(memory_space=pl.ANY),
                  pl.BlockSpec(memory_space=pl.ANY)],
        out_specs=pl.BlockSpec(memory_space=pl.ANY),
        out_shape=jax.ShapeDtypeStruct((n_rows, D_MODEL), F32),
        scratch_shapes=[pltpu.SemaphoreType.DMA(())],
        input_output_aliases={2: 0},
        compiler_params=pltpu.CompilerParams(dimension_semantics=("arbitrary",),
                                             has_side_effects=True),
        name="moe_dispatch",
    )(pos3, m, xs0)


def _experts_kernel(te_ref, nused_ref, xs_ref, wg_ref, wu_ref, wd_ref, ys_ref, wgu_b, wd_b):
    j = pl.program_id(0)
    prev = te_ref[jnp.maximum(j - 1, 0)]

    @pl.when((j == 0) | (te_ref[j] != prev))
    def _():
        wgu_b[:, :D_EXPERT] = wg_ref[0].astype(BF16)
        wgu_b[:, D_EXPERT:] = wu_ref[0].astype(BF16)
        wd_b[...] = wd_ref[0].astype(BF16)

    @pl.when(j < nused_ref[0])
    def _():
        x = xs_ref[...].astype(BF16)
        gu = jnp.dot(x, wgu_b[...], preferred_element_type=F32)
        g = gu[:, :D_EXPERT]
        act = g * jax.nn.sigmoid(g) * gu[:, D_EXPERT:]
        ys_ref[...] = jnp.dot(act.astype(BF16), wd_b[...], preferred_element_type=F32)

    @pl.when(j >= nused_ref[0])
    def _():
        ys_ref[...] = jnp.zeros_like(ys_ref)


def _experts(tile_expert, n_used, xs, wg, wu, wd):
    n_rows = xs.shape[0]
    nt = n_rows // EXPERT_TILE
    grid_spec = pltpu.PrefetchScalarGridSpec(
        num_scalar_prefetch=2,
        grid=(nt,),
        in_specs=[pl.BlockSpec((EXPERT_TILE, D_MODEL), lambda j, te, nu: (j, 0)),
                  pl.BlockSpec((1, D_MODEL, D_EXPERT), lambda j, te, nu: (te[j], 0, 0)),
                  pl.BlockSpec((1, D_MODEL, D_EXPERT), lambda j, te, nu: (te[j], 0, 0)),
                  pl.BlockSpec((1, D_EXPERT, D_MODEL), lambda j, te, nu: (te[j], 0, 0))],
        out_specs=pl.BlockSpec((EXPERT_TILE, D_MODEL), lambda j, te, nu: (j, 0)),
        scratch_shapes=[pltpu.VMEM((D_MODEL, 2 * D_EXPERT), BF16),
                        pltpu.VMEM((D_EXPERT, D_MODEL), BF16)],
    )
    return pl.pallas_call(
        _experts_kernel,
        grid_spec=grid_spec,
        out_shape=jax.ShapeDtypeStruct((n_rows, D_MODEL), F32),
        compiler_params=_cparams(1),
        name="moe_experts",
    )(tile_expert, n_used, xs, wg, wu, wd)


def _combine_ple_kernel(pos_ref, route_ref, h1_ref, ys_hbm, p_ref, wp_ref, g_ref, wgate_ref,
                        o_ref, ybuf, sem, *, tile):
    def row_copy(t, k):
        return pltpu.make_async_copy(ys_hbm.at[pl.ds(pos_ref[0, 0, 2 * t + k], 1)],
                                     ybuf.at[k, pl.ds(t, 1)], sem)

    def issue(t, carry):
        row_copy(t, 0).start()
        row_copy(t, 1).start()
        return carry

    lax.fori_loop(0, tile, issue, 0)
    ple = jnp.dot(p_ref[...].astype(BF16), wp_ref[...], preferred_element_type=F32)
    pltpu.make_async_copy(ys_hbm.at[pl.ds(0, tile)], ybuf.at[0], sem).wait()
    pltpu.make_async_copy(ys_hbm.at[pl.ds(0, tile)], ybuf.at[1], sem).wait()

    route = route_ref[...]
    h2 = h1_ref[...] + route[:, 4:5] * ybuf[0] + route[:, 5:6] * ybuf[1]
    gate = jax.nn.sigmoid(jnp.dot(_rms(h2, g_ref[...]).astype(BF16), wgate_ref[...],
                                  preferred_element_type=F32))
    o_ref[...] = h2 + gate * ple


def _combine_ple(pos, route, h1, ys, p, wp, g, wgate):
    T = h1.shape[0]
    tile = min(GATHER_TILE, T)
    pos3 = pos.reshape(T // tile, 1, 2 * tile)
    const = lambda *shape: pl.BlockSpec(shape, lambda i: (0,) * len(shape))
    row = lambda width: pl.BlockSpec((tile, width), lambda i: (i, 0))
    return pl.pallas_call(
        functools.partial(_combine_ple_kernel, tile=tile),
        grid=(T // tile,),
        in_specs=[pl.BlockSpec((1, 1, 2 * tile), lambda i: (i, 0, 0), memory_space=pltpu.SMEM),
                  row(LANES), row(D_MODEL), pl.BlockSpec(memory_space=pl.ANY), row(D_PLE),
                  const(D_PLE, D_MODEL), const(1, D_MODEL), const(D_MODEL, D_MODEL)],
        out_specs=row(D_MODEL),
        out_shape=jax.ShapeDtypeStruct((T, D_MODEL), F32),
        scratch_shapes=[pltpu.VMEM((2, tile, D_MODEL), F32), pltpu.SemaphoreType.DMA(())],
        compiler_params=_cparams(1),
        name="combine_ple",
    )(pos3, route, h1, ys, p, wp, g, wgate)


def _rope(z, cos, sin_lo, sin_hi):
    half = ROT_DIM // 2
    outs = []
    for j in range(z.shape[1] // LANES):
        zj = z[:, j * LANES:(j + 1) * LANES]
        outs.append(zj * cos + pltpu.roll(zj, LANES - half, 1) * sin_lo + pltpu.roll(zj, half, 1) * sin_hi)
    return outs[0] if len(outs) == 1 else jnp.concatenate(outs, axis=1)


def _proj_odd_kernel(h_ref, g_ref, w_ref, qg_ref, kg_ref, cos_ref, slo_ref, shi_ref,
                     q_ref, k_ref, v_ref, glu_ref):
    a = _rms(h_ref[...], g_ref[...]).astype(BF16)
    cos, slo, shi = cos_ref[...], slo_ref[...], shi_ref[...]
    zq = jnp.dot(a, w_ref[:, 0:SWA_WIDTH], preferred_element_type=F32)
    q_ref[...] = (_rope(_head_rms(zq, qg_ref[...]), cos, slo, shi) * (HEAD_DIM ** -0.5)).astype(BF16)
    zk = jnp.dot(a, w_ref[:, SWA_WIDTH:SWA_WIDTH + KV_WIDTH], preferred_element_type=F32)
    k_ref[...] = _rope(_head_rms(zk, kg_ref[...]), cos, slo, shi).astype(BF16)
    v0 = SWA_WIDTH + KV_WIDTH
    v_ref[...] = jnp.dot(a, w_ref[:, v0:v0 + KV_WIDTH], preferred_element_type=F32).astype(BF16)
    d0 = v0 + KV_WIDTH
    zd = jnp.dot(a, w_ref[:, d0:d0 + 2 * CONV_CH], preferred_element_type=F32)
    glu_ref[...] = zd[:, :CONV_CH] * jax.nn.sigmoid(zd[:, CONV_CH:])


def _proj_odd(h, g, w, qg, kg, cos, slo, shi, *, seq):
    T = h.shape[0]
    tm = min(ROW_TILE, seq)
    tps = seq // tm
    const = lambda *shape: pl.BlockSpec(shape, lambda i: (0,) * len(shape))
    row = lambda width: pl.BlockSpec((tm, width), lambda i: (i, 0))
    tab = pl.BlockSpec((tm, LANES), lambda i: (i % tps, 0))
    return pl.pallas_call(
        _proj_odd_kernel,
        grid=(T // tm,),
        in_specs=[row(D_MODEL), const(1, D_MODEL), const(*w.shape), const(1, SWA_WIDTH),
                  const(1, KV_WIDTH), tab, tab, tab],
        out_specs=[row(SWA_WIDTH), row(KV_WIDTH), row(KV_WIDTH), row(CONV_CH)],
        out_shape=[jax.ShapeDtypeStruct((T, SWA_WIDTH), BF16),
                   jax.ShapeDtypeStruct((T, KV_WIDTH), BF16),
                   jax.ShapeDtypeStruct((T, KV_WIDTH), BF16),
                   jax.ShapeDtypeStruct((T, CONV_CH), F32)],
        compiler_params=_cparams(1),
        name="proj_odd",
    )(h, g, w, qg, kg, cos, slo, shi)


def _swa_kernel(sink_ref, q_ref, k_ref, v_ref, o_ref, *, seq):
    lo = _lane_lo((1, LANES))
    nt = (((1,), (1,)), ((), ()))
    W = WINDOW
    qi = lax.broadcasted_iota(jnp.int32, (W, 2 * W), 0)
    kj = lax.broadcasted_iota(jnp.int32, (W, 2 * W), 1)
    band = (kj > qi) & (kj <= qi + W)

    def block(n, kstart, mask):
        qrow = pl.ds(pl.multiple_of(n * W, W), W)
        kwin = pl.ds(pl.multiple_of(kstart, W), 2 * W)
        ks = k_ref[kwin, :]
        vs = v_ref[kwin, :]
        for j in range(SWA_WIDTH // LANES):
            cols = slice(j * LANES, (j + 1) * LANES)
            q = q_ref[qrow, cols]
            zero = jnp.zeros_like(q)
            outs = []
            for head, qh in ((j, jnp.where(lo, q, zero)), (4 + j, jnp.where(lo, zero, q))):
                s = jnp.where(mask, lax.dot_general(qh, ks, nt, preferred_element_type=F32), NEG)
                sink = sink_ref[head]
                m = jnp.maximum(jnp.max(s, axis=-1, keepdims=True), sink)
                p = jnp.exp(s - m)
                l = jnp.sum(p, axis=-1, keepdims=True) + jnp.exp(sink - m)
                outs.append(jnp.dot(p.astype(BF16), vs, preferred_element_type=F32) / l)
            o_ref[qrow, cols] = jnp.where(lo, outs[0], outs[1]).astype(BF16)

    block(0, 0, kj <= qi)

    def body(n, carry):
        block(n, (n - 1) * W, band)
        return carry

    lax.fori_loop(1, seq // W, body, 0)


def _swa_attention(sinks, q, k, v, *, seq):
    T = q.shape[0]
    B = T // seq
    return pl.pallas_call(
        functools.partial(_swa_kernel, seq=seq),
        grid=(B,),
        in_specs=[pl.BlockSpec(memory_space=pltpu.SMEM),
                  pl.BlockSpec((seq, SWA_WIDTH), lambda b: (b, 0)),
                  pl.BlockSpec((seq, KV_WIDTH), lambda b: (b, 0)),
                  pl.BlockSpec((seq, KV_WIDTH), lambda b: (b, 0))],
        out_specs=pl.BlockSpec((seq, SWA_WIDTH), lambda b: (b, 0)),
        out_shape=jax.ShapeDtypeStruct((T, SWA_WIDTH), BF16),
        compiler_params=_cparams(1),
        name="swa_attention",
    )(sinks, q, k, v)


def _conv_kernel(prev_ref, cur_ref, w_ref, g_ref, b_ref, o_ref, pad_ref, *, tile, sub):
    r = pl.program_id(1)
    tail = prev_ref[tile - CONV_HALO:, :]
    pad_ref[0:CONV_HALO, :] = jnp.where(r > 0, tail, jnp.zeros_like(tail))
    pad_ref[CONV_HALO:, :] = cur_ref[...]
    w = w_ref[...]
    first = CONV_HALO - (CONV_WIDTH - 1)
    for s in range(tile // sub):
        acc = jnp.zeros((sub, CONV_CH), F32)
        for j in range(CONV_WIDTH):
            start = s * sub + first + j
            acc = acc + pad_ref[start:start + sub, :] * w[j:j + 1, :]
        y = _layernorm(acc, g_ref[...], b_ref[...])
        o_ref[s * sub:(s + 1) * sub, :] = (y * jax.nn.sigmoid(y)).astype(BF16)


def _conv_module(glu, w, g, b, *, seq):
    T = glu.shape[0]
    B = T // seq
    tile = min(CONV_TILE, seq)
    nr = seq // tile
    const = lambda *shape: pl.BlockSpec(shape, lambda bb, r: (0,) * len(shape))
    return pl.pallas_call(
        functools.partial(_conv_kernel, tile=tile, sub=64),
        grid=(B, nr),
        in_specs=[pl.BlockSpec((tile, CONV_CH), lambda bb, r: (bb * nr + jnp.maximum(r - 1, 0), 0)),
                  pl.BlockSpec((tile, CONV_CH), lambda bb, r: (bb * nr + r, 0)),
                  const(CONV_WIDTH, CONV_CH), const(1, CONV_CH), const(1, CONV_CH)],
        out_specs=pl.BlockSpec((tile, CONV_CH), lambda bb, r: (bb * nr + r, 0)),
        out_shape=jax.ShapeDtypeStruct((T, CONV_CH), BF16),
        scratch_shapes=[pltpu.VMEM((CONV_HALO + tile, CONV_CH), F32)],
        compiler_params=_cparams(2),
        name="conv_module",
    )(glu, glu, w, g, b)


def _routing_tables(route, counts, n_tiles):
    e = route[:, 0:2].astype(jnp.int32)
    rank = route[:, 2:4].astype(jnp.int32)
    cnt = counts[0, :N_EXPERTS].astype(jnp.int32)
    tiles = (cnt + EXPERT_TILE - 1) // EXPERT_TILE
    tile_end = jnp.cumsum(tiles)
    offset = (tile_end - tiles) * EXPERT_TILE
    pos = (offset[e] + rank).reshape(-1)
    n_used = tile_end[-1]
    tile_id = jnp.minimum(jnp.arange(n_tiles, dtype=jnp.int32), n_used - 1)
    tile_expert = jnp.sum((tile_end[None, :] <= tile_id[:, None]).astype(jnp.int32), axis=1)
    return pos, tile_expert, n_used.reshape(1).astype(jnp.int32)


def _moe_ple(h, ya, yb, wo, norm_ffn, wr, br, wg, wu, wd, p, wp, ple_norm, wgate):
    T = h.shape[0]
    n_tiles = (2 * T) // EXPERT_TILE + N_EXPERTS
    h1, m, route, counts = _outproj_router(h, ya, yb, wo, norm_ffn, wr, br)
    pos, tile_expert, n_used = _routing_tables(route, counts, n_tiles)
    xs = _dispatch(pos, m, n_tiles * EXPERT_TILE)
    ys = _experts(tile_expert, n_used, xs, wg, wu, wd)
    return _combine_ple(pos, route, h1, ys, p, wp, ple_norm, wgate)


def _router_weights(w_coarse, b_coarse, w_fine, b_fine):
    wf = w_fine.transpose(1, 0, 2).reshape(D_MODEL, N_EXPERTS)
    wr = jnp.concatenate([wf, w_coarse, jnp.zeros((D_MODEL, LANES - N_EXPERTS - N_GROUPS), F32)], axis=1)
    br = jnp.concatenate([b_fine.reshape(-1), b_coarse, jnp.zeros((LANES - N_EXPERTS - N_GROUPS,), F32)])
    return wr, br.reshape(1, LANES)


def _rope_tables(seq):
    half = ROT_DIM // 2
    inv_freq = ROPE_THETA ** (-jnp.arange(half, dtype=F32) * 2.0 / ROT_DIM)
    ang = jnp.arange(seq, dtype=F32)[:, None] * inv_freq[None, :]
    cos, sin = jnp.cos(ang), jnp.sin(ang)
    zeros = jnp.zeros((seq, HEAD_DIM - ROT_DIM), F32)
    z8 = jnp.zeros((seq, half), F32)
    cos_h = jnp.concatenate([cos, cos, zeros + 1.0], axis=1)
    slo_h = jnp.concatenate([-sin, z8, zeros], axis=1)
    shi_h = jnp.concatenate([z8, sin, zeros], axis=1)
    two = lambda t: jnp.concatenate([t, t], axis=1)
    return two(cos_h), two(slo_h), two(shi_h)


def kernel(x, p, norm_mix, even_w_in, fox_b_f, gmlp_ln_g, gmlp_ln_b, gmlp_w_s, gmlp_b_s, fox_q_norm, fox_k_norm, even_w_out, odd_w_in, swa_q_norm, swa_k_norm, swa_sinks, conv_w, conv_ln_g, conv_ln_b, odd_w_out, norm_ffn, moe_w_coarse, moe_b_coarse, moe_w_fine, moe_b_fine, moe_w_gate, moe_w_up, moe_w_down, ple_w_proj, ple_norm, ple_w_gate):
    B, S, D = x.shape
    T = B * S
    h = x.reshape(T, D)
    p = p.reshape(p.shape[0], T, D_PLE)
    row = lambda v: v.reshape(1, -1)

    def moe_args(i):
        wr, br = _router_weights(moe_w_coarse[i], moe_b_coarse[i], moe_w_fine[i], moe_b_fine[i])
        return (row(norm_ffn[i]), wr, br, moe_w_gate[i], moe_w_up[i], moe_w_down[i], p[i],
                ple_w_proj[i].astype(BF16), row(ple_norm[i]), ple_w_gate[i].astype(BF16))

    n_in = even_w_in.shape[2]
    w_in = jnp.pad(even_w_in[0], ((0, 0), (0, 2 * GMLP_WIDTH + 3 * FOX_WIDTH + LANES - n_in))).astype(BF16)
    b_f = jnp.pad(fox_b_f[0], (0, LANES - FOX_HEADS)).reshape(1, LANES)
    bs_full = jnp.repeat(gmlp_b_s[0].T, HEAD_DIM, axis=1)
    ya, q, k, v, c, ct = _proj_even(
        h, row(norm_mix[0]), w_in, b_f, row(gmlp_ln_g[0]), row(gmlp_ln_b[0]), gmlp_w_s[0], bs_full,
        row(jnp.tile(fox_q_norm[0], FOX_HEADS)), row(jnp.tile(fox_k_norm[0], FOX_HEADS)), seq=S)
    yb = _fox_attention(q, k, v, c, ct, seq=S)
    h = _moe_ple(h, ya, yb, even_w_out[0].astype(BF16), *moe_args(0))

    order = jnp.array([0, 4, 1, 5, 2, 6, 3, 7])
    cols = (order[:, None] * HEAD_DIM + jnp.arange(HEAD_DIM)[None, :]).reshape(-1)
    w_odd = jnp.concatenate([odd_w_in[0][:, :SWA_WIDTH][:, cols], odd_w_in[0][:, SWA_WIDTH:]], axis=1).astype(BF16)
    w_out_odd = jnp.concatenate([odd_w_out[0][:SWA_WIDTH][cols], odd_w_out[0][SWA_WIDTH:]], axis=0).astype(BF16)
    cos, slo, shi = _rope_tables(S)
    q, k, v, glu = _proj_odd(h, row(norm_mix[1]), w_odd, row(jnp.tile(swa_q_norm[0], 8)),
                             row(jnp.tile(swa_k_norm[0], 2)), cos, slo, shi, seq=S)
    yc = _swa_attention(swa_sinks[0], q, k, v, seq=S)
    yd = _conv_module(glu, conv_w[0], row(conv_ln_g[0]), row(conv_ln_b[0]), seq=S)
    h = _moe_ple(h, yc, yd, w_out_odd, *moe_args(1))
    return h.reshape(B, S, D)
```

```python
import functools

import jax
import jax.numpy as jnp
from jax import lax
from jax.experimental import pallas as pl
from jax.experimental.pallas import tpu as pltpu

F32 = jnp.float32
BF16 = jnp.bfloat16
HIGHEST = lax.Precision.HIGHEST

D_MODEL = 1024
HEAD_DIM = 64
LANES = 128
GMLP_WIDTH = 512
CHUNK = 128
FOX_WIDTH = 512
FOX_HEADS = 8
SWA_WIDTH = 512
KV_WIDTH = 128
WINDOW = 128
CONV_CH = 512
CONV_WIDTH = 31
CONV_HALO = 32
ROPE_THETA = 500000.0
ROT_DIM = 16
N_GROUPS = 4
EXPERTS_PER_GROUP = 8
N_EXPERTS = 32
D_EXPERT = 256
D_PLE = 256
EPS = 1e-6
NEG = -1e30

EXPERT_TILE = 256
ROW_TILE = 512
GATHER_TILE = 256
DISPATCH_TILE = 512
ATTN_TILE = 256
CONV_TILE = 256
VMEM_LIMIT = 56 * 1024 * 1024


def _cparams(n_axes=1):
    return pltpu.CompilerParams(dimension_semantics=("arbitrary",) * n_axes,
                                vmem_limit_bytes=VMEM_LIMIT)


def _rms(x, gain):
    return x * lax.rsqrt(jnp.mean(x * x, axis=-1, keepdims=True) + EPS) * gain


def _layernorm(x, g, b):
    mu = jnp.mean(x, axis=-1, keepdims=True)
    xc = x - mu
    var = jnp.mean(xc * xc, axis=-1, keepdims=True)
    return xc * lax.rsqrt(var + EPS) * g + b


def _head_rms(z, gain):
    r = lax.broadcasted_iota(jnp.int32, (LANES, LANES), 0) // HEAD_DIM
    c = lax.broadcasted_iota(jnp.int32, (LANES, LANES), 1) // HEAD_DIM
    bd = jnp.where(r == c, 1.0 / HEAD_DIM, 0.0).astype(F32)
    outs = []
    for j in range(z.shape[1] // LANES):
        zj = z[:, j * LANES:(j + 1) * LANES]
        ms = jnp.dot(zj * zj, bd, precision=HIGHEST, preferred_element_type=F32)
        outs.append(zj * lax.rsqrt(ms + EPS))
    zn = outs[0] if len(outs) == 1 else jnp.concatenate(outs, axis=1)
    return zn * gain


def _lane_lo(shape):
    return (lax.broadcasted_iota(jnp.int32, shape, len(shape) - 1) % LANES) < HEAD_DIM


def _proj_even_kernel(h_ref, g_ref, w_ref, bf_ref, lng_ref, lnb_ref, ws_ref, bs_ref, qg_ref, kg_ref,
                      ya_ref, q_ref, k_ref, v_ref, c_ref, ct_ref, carry_ref, *, tm, tiles_per_seq):
    i = pl.program_id(0)
    a = _rms(h_ref[...], g_ref[...]).astype(BF16)

    za = jax.nn.gelu(jnp.dot(a, w_ref[:, 0:2 * GMLP_WIDTH], preferred_element_type=F32))
    u = za[:, :GMLP_WIDTH]
    vln = _layernorm(za[:, GMLP_WIDTH:], lng_ref[...], lnb_ref[...]).astype(BF16)
    rr = lax.broadcasted_iota(jnp.int32, (CHUNK, CHUNK), 0)
    cc = lax.broadcasted_iota(jnp.int32, (CHUNK, CHUNK), 1)
    lo = _lane_lo((CHUNK, LANES))
    for j in range(GMLP_WIDTH // LANES):
        w_a = jnp.where(rr >= cc, ws_ref[2 * j], 0.0).astype(BF16)
        w_b = jnp.where(rr >= cc, ws_ref[2 * j + 1], 0.0).astype(BF16)
        cols = slice(j * LANES, (j + 1) * LANES)
        for c in range(tm // CHUNK):
            rows = slice(c * CHUNK, (c + 1) * CHUNK)
            vp = vln[rows, cols]
            mixed = jnp.where(lo, jnp.dot(w_a, vp, preferred_element_type=F32),
                              jnp.dot(w_b, vp, preferred_element_type=F32)) + bs_ref[:, cols]
            ya_ref[rows, cols] = (u[rows, cols] * mixed).astype(BF16)

    q0 = 2 * GMLP_WIDTH
    zq = jnp.dot(a, w_ref[:, q0:q0 + FOX_WIDTH], preferred_element_type=F32)
    q_ref[...] = (_head_rms(zq, qg_ref[...]) * (HEAD_DIM ** -0.5)).astype(BF16)
    zk = jnp.dot(a, w_ref[:, q0 + FOX_WIDTH:q0 + 2 * FOX_WIDTH], preferred_element_type=F32)
    k_ref[...] = _head_rms(zk, kg_ref[...]).astype(BF16)
    v_ref[...] = jnp.dot(a, w_ref[:, q0 + 2 * FOX_WIDTH:q0 + 3 * FOX_WIDTH],
                         preferred_element_type=F32).astype(BF16)

    zf = jnp.dot(a, w_ref[:, q0 + 3 * FOX_WIDTH:], preferred_element_type=F32) + bf_ref[...]
    ls = jnp.minimum(zf, 0.0) - jnp.log(1.0 + jnp.exp(-jnp.abs(zf)))

    @pl.when(i % tiles_per_seq == 0)
    def _():
        carry_ref[...] = jnp.zeros_like(carry_ref)

    tr = lax.broadcasted_iota(jnp.int32, (tm, tm), 0)
    tc = lax.broadcasted_iota(jnp.int32, (tm, tm), 1)
    tri = jnp.where(tr >= tc, 1.0, 0.0).astype(F32)
    c = jnp.dot(tri, ls, precision=HIGHEST, preferred_element_type=F32) + carry_ref[...]
    carry_ref[...] = c[tm - 1:tm, :]
    c_ref[...] = c
    ct_ref[0] = c.T[0:FOX_HEADS, :]


def _proj_even(h, g, w, bf, lng, lnb, ws, bs_full, qg, kg, *, seq):
    T = h.shape[0]
    tm = min(ROW_TILE, seq)
    n_in = w.shape[1]
    const = lambda *shape: pl.BlockSpec(shape, lambda i: (0,) * len(shape))
    row = lambda width: pl.BlockSpec((tm, width), lambda i: (i, 0))
    tps = seq // tm
    return pl.pallas_call(
        functools.partial(_proj_even_kernel, tm=tm, tiles_per_seq=tps),
        grid=(T // tm,),
        in_specs=[row(D_MODEL), const(1, D_MODEL), const(D_MODEL, n_in), const(1, LANES),
                  const(1, GMLP_WIDTH), const(1, GMLP_WIDTH), const(8, CHUNK, CHUNK),
                  const(CHUNK, GMLP_WIDTH), const(1, FOX_WIDTH), const(1, FOX_WIDTH)],
        out_specs=[row(GMLP_WIDTH), row(FOX_WIDTH), row(FOX_WIDTH), row(FOX_WIDTH), row(LANES),
                   pl.BlockSpec((1, FOX_HEADS, tm), lambda i: (i // tps, 0, i % tps))],
        out_shape=[jax.ShapeDtypeStruct((T, GMLP_WIDTH), BF16),
                   jax.ShapeDtypeStruct((T, FOX_WIDTH), BF16),
                   jax.ShapeDtypeStruct((T, FOX_WIDTH), BF16),
                   jax.ShapeDtypeStruct((T, FOX_WIDTH), BF16),
                   jax.ShapeDtypeStruct((T, LANES), F32),
                   jax.ShapeDtypeStruct((T // seq, FOX_HEADS, seq), F32)],
        scratch_shapes=[pltpu.VMEM((1, LANES), F32)],
        compiler_params=_cparams(1),
        name="proj_even",
    )(h, g, w, bf, lng, lnb, ws, bs_full, qg, kg)


def _fox_kernel(q_ref, k_ref, v_ref, c_ref, ct_ref, o_ref, *, tq):
    i = pl.program_id(1)
    lo = _lane_lo((1, LANES))
    rr = lax.broadcasted_iota(jnp.int32, (tq, tq), 0)
    cc = lax.broadcasted_iota(jnp.int32, (tq, tq), 1)
    causal = rr >= cc
    cq = c_ref[...]
    nt = (((1,), (1,)), ((), ()))
    for hp in range(FOX_WIDTH // LANES):
        cols = slice(hp * LANES, (hp + 1) * LANES)
        q = q_ref[:, cols]
        zero = jnp.zeros_like(q)
        q_a = jnp.where(lo, q, zero)
        q_b = jnp.where(lo, zero, q)
        cq_a = cq[:, 2 * hp:2 * hp + 1]
        cq_b = cq[:, 2 * hp + 1:2 * hp + 2]

        def scores(j, masked):
            start = pl.multiple_of(j * tq, tq)
            ks = k_ref[pl.ds(start, tq), cols]
            ck_a = ct_ref[0, 2 * hp:2 * hp + 1, pl.ds(start, tq)]
            ck_b = ct_ref[0, 2 * hp + 1:2 * hp + 2, pl.ds(start, tq)]
            s_a = lax.dot_general(q_a, ks, nt, preferred_element_type=F32) + cq_a - ck_a
            s_b = lax.dot_general(q_b, ks, nt, preferred_element_type=F32) + cq_b - ck_b
            if masked:
                s_a = jnp.where(causal, s_a, NEG)
                s_b = jnp.where(causal, s_b, NEG)
            return s_a, s_b, v_ref[pl.ds(start, tq), cols]

        def update(carry, s_a, s_b, vs):
            m_a, l_a, m_b, l_b, acc = carry
            n_a = jnp.maximum(m_a, jnp.max(s_a, axis=-1, keepdims=True))
            n_b = jnp.maximum(m_b, jnp.max(s_b, axis=-1, keepdims=True))
            p_a = jnp.exp(s_a - n_a)
            p_b = jnp.exp(s_b - n_b)
            al_a = jnp.exp(m_a - n_a)
            al_b = jnp.exp(m_b - n_b)
            l_a = l_a * al_a + jnp.sum(p_a, axis=-1, keepdims=True)
            l_b = l_b * al_b + jnp.sum(p_b, axis=-1, keepdims=True)
            pv = jnp.where(lo, jnp.dot(p_a.astype(BF16), vs, preferred_element_type=F32),
                           jnp.dot(p_b.astype(BF16), vs, preferred_element_type=F32))
            acc = acc * jnp.where(lo, al_a, al_b) + pv
            return n_a, l_a, n_b, l_b, acc

        def body(j, carry):
            return update(carry, *scores(j, False))

        col = lambda v: jnp.full((tq, 1), v, F32)
        init = (col(NEG), col(0.0), col(NEG), col(0.0), jnp.zeros((tq, LANES), F32))
        carry = lax.fori_loop(0, i, body, init)
        _, l_a, _, l_b, acc = update(carry, *scores(i, True))
        o_ref[:, cols] = (acc / jnp.where(lo, l_a, l_b)).astype(BF16)


def _fox_attention(q, k, v, c, ct, *, seq):
    T = q.shape[0]
    B = T // seq
    tq = min(ATTN_TILE, seq)
    nq = seq // tq
    return pl.pallas_call(
        functools.partial(_fox_kernel, tq=tq),
        grid=(B, nq),
        in_specs=[pl.BlockSpec((tq, FOX_WIDTH), lambda b, i: (b * nq + i, 0)),
                  pl.BlockSpec((seq, FOX_WIDTH), lambda b, i: (b, 0)),
                  pl.BlockSpec((seq, FOX_WIDTH), lambda b, i: (b, 0)),
                  pl.BlockSpec((tq, LANES), lambda b, i: (b * nq + i, 0)),
                  pl.BlockSpec((1, FOX_HEADS, seq), lambda b, i: (b, 0, 0))],
        out_specs=pl.BlockSpec((tq, FOX_WIDTH), lambda b, i: (b * nq + i, 0)),
        out_shape=jax.ShapeDtypeStruct((T, FOX_WIDTH), BF16),
        compiler_params=_cparams(2),
        name="fox_attention",
    )(q, k, v, c, ct)


def _outproj_router_kernel(h_ref, ya_ref, yb_ref, wo_ref, g_ref, wr_ref, br_ref,
                           h1_ref, m_ref, route_ref, cnt_ref, carry_ref, *, tm):
    i = pl.program_id(0)
    half = wo_ref.shape[0] // 2
    mix = (jnp.dot(ya_ref[...], wo_ref[0:half, :], preferred_element_type=F32)
           + jnp.dot(yb_ref[...], wo_ref[half:, :], preferred_element_type=F32))
    h1 = h_ref[...] + mix
    h1_ref[...] = h1
    m = _rms(h1, g_ref[...])
    m_ref[...] = m

    logits = jnp.dot(m, wr_ref[...], precision=HIGHEST, preferred_element_type=F32) + br_ref[...]
    lane_i = lax.broadcasted_iota(jnp.int32, (tm, LANES), 1)
    lane = lane_i.astype(F32)
    group_of_lane = (lane_i // EXPERTS_PER_GROUP).astype(F32)
    is_coarse = (lane_i >= N_EXPERTS) & (lane_i < N_EXPERTS + N_GROUPS)
    coarse = jnp.where(is_coarse, logits, NEG)
    cmax = jnp.max(coarse, axis=-1, keepdims=True)
    gidx = jnp.min(jnp.where(coarse == cmax, lane - N_EXPERTS, float(LANES)), axis=-1, keepdims=True)
    p_g = 1.0 / jnp.sum(jnp.where(is_coarse, jnp.exp(coarse - cmax), 0.0), axis=-1, keepdims=True)
    in_group = (lane_i < N_EXPERTS) & (group_of_lane == gidx)
    fine = jnp.where(in_group, logits, NEG)
    v1 = jnp.max(fine, axis=-1, keepdims=True)
    i1 = jnp.min(jnp.where(fine == v1, lane, float(LANES)), axis=-1, keepdims=True)
    fine2 = jnp.where(lane == i1, NEG, fine)
    v2 = jnp.max(fine2, axis=-1, keepdims=True)
    i2 = jnp.min(jnp.where(fine2 == v2, lane, float(LANES)), axis=-1, keepdims=True)
    e2 = jnp.exp(v2 - v1)
    w1 = p_g / (1.0 + e2)
    w2 = p_g * e2 / (1.0 + e2)

    @pl.when(i == 0)
    def _():
        carry_ref[...] = jnp.zeros_like(carry_ref)

    hit1 = lane == i1
    hit2 = lane == i2
    onehot = jnp.where(hit1 | hit2, 1.0, 0.0).astype(F32)
    tr = lax.broadcasted_iota(jnp.int32, (tm, tm), 0)
    tc = lax.broadcasted_iota(jnp.int32, (tm, tm), 1)
    strict = jnp.where(tr > tc, 1.0, 0.0).astype(BF16)
    before = jnp.dot(strict, onehot.astype(BF16), preferred_element_type=F32) + carry_ref[...]
    r1 = jnp.sum(jnp.where(hit1, before, 0.0), axis=-1, keepdims=True)
    r2 = jnp.sum(jnp.where(hit2, before, 0.0), axis=-1, keepdims=True)
    total = carry_ref[...] + jnp.sum(onehot, axis=0, keepdims=True)
    carry_ref[...] = total
    cnt_ref[...] = jnp.broadcast_to(total, cnt_ref.shape)

    route = jnp.where(lane == 0, i1, 0.0)
    route = jnp.where(lane == 1, i2, route)
    route = jnp.where(lane == 2, r1, route)
    route = jnp.where(lane == 3, r2, route)
    route = jnp.where(lane == 4, w1, route)
    route = jnp.where(lane == 5, w2, route)
    route_ref[...] = route


def _outproj_router(h, ya, yb, wo, g, wr, br):
    T = h.shape[0]
    tm = min(ROW_TILE, T)
    const = lambda *shape: pl.BlockSpec(shape, lambda i: (0,) * len(shape))
    row = lambda width: pl.BlockSpec((tm, width), lambda i: (i, 0))
    return pl.pallas_call(
        functools.partial(_outproj_router_kernel, tm=tm),
        grid=(T // tm,),
        in_specs=[row(D_MODEL), row(ya.shape[1]), row(yb.shape[1]), const(*wo.shape),
                  const(1, D_MODEL), const(D_MODEL, LANES), const(1, LANES)],
        out_specs=[row(D_MODEL), row(D_MODEL), row(LANES), const(8, LANES)],
        out_shape=[jax.ShapeDtypeStruct((T, D_MODEL), F32),
                   jax.ShapeDtypeStruct((T, D_MODEL), F32),
                   jax.ShapeDtypeStruct((T, LANES), F32),
                   jax.ShapeDtypeStruct((8, LANES), F32)],
        scratch_shapes=[pltpu.VMEM((1, LANES), F32)],
        compiler_params=_cparams(1),
        name="outproj_router",
    )(h, ya, yb, wo, g, wr, br)


def _dispatch_kernel(pos_ref, m_ref, xs_in_hbm, xs_hbm, sem, *, tile):
    del xs_in_hbm

    def row_copy(t, k):
        return pltpu.make_async_copy(m_ref.at[pl.ds(t, 1)],
                                     xs_hbm.at[pl.ds(pos_ref[0, 0, 2 * t + k], 1)], sem)

    def issue(t, carry):
        row_copy(t, 0).start()
        row_copy(t, 1).start()
        return carry

    lax.fori_loop(0, tile, issue, 0)
    for _ in range(2):
        pltpu.make_async_copy(m_ref, xs_hbm.at[pl.ds(0, tile)], sem).wait()


def _dispatch(pos, m, n_rows):
    T = m.shape[0]
    tile = min(DISPATCH_TILE, T)
    pos3 = pos.reshape(T // tile, 1, 2 * tile)
    xs0 = jnp.zeros((n_rows, D_MODEL), F32)
    return pl.pallas_call(
        functools.partial(_dispatch_kernel, tile=tile),
        grid=(T // tile,),
        in_specs=[pl.BlockSpec((1, 1, 2 * tile), lambda i: (i, 0, 0), memory_space=pltpu.SMEM),
                  pl.BlockSpec((tile, D_MODEL), lambda i: (i, 0)),
                  pl.BlockSpec(memory_space=pl.ANY)],
        out_specs=pl.BlockSpec(memory_space=pl.ANY),
        out_shape=jax.ShapeDtypeStruct((n_rows, D_MODEL), F32),
        scratch_shapes=[pltpu.SemaphoreType.DMA(())],
        input_output_aliases={2: 0},
        compiler_params=pltpu.CompilerParams(dimension_semantics=("arbitrary",),
                                             has_side_effects=True),
        name="moe_dispatch",
    )(pos3, m, xs0)


def _experts_kernel(te_ref, nused_ref, xs_ref, wg_ref, wu_ref, wd_ref, ys_ref, wgu_b, wd_b):
    j = pl.program_id(0)
    prev = te_ref[jnp.maximum(j - 1, 0)]

    @pl.when((j == 0) | (te_ref[j] != prev))
    def _():
        wgu_b[:, :D_EXPERT] = wg_ref[...].astype(BF16)
        wgu_b[:, D_EXPERT:] = wu_ref[...].astype(BF16)
        wd_b[...] = wd_ref[...].astype(BF16)

    @pl.when(j < nused_ref[0])
    def _():
        x = xs_ref[...].astype(BF16)
        gu = jnp.dot(x, wgu_b[...], preferred_element_type=F32)
        g = gu[:, :D_EXPERT]
        act = g * jax.nn.sigmoid(g) * gu[:, D_EXPERT:]
        ys_ref[...] = jnp.dot(act.astype(BF16), wd_b[...], preferred_element_type=F32)

    @pl.when(j >= nused_ref[0])
    def _():
        ys_ref[...] = jnp.zeros_like(ys_ref)


def _experts(layer, tile_expert, n_used, xs, wg, wu, wd):
    n_rows = xs.shape[0]
    nt = n_rows // EXPERT_TILE
    grid_spec = pltpu.PrefetchScalarGridSpec(
        num_scalar_prefetch=2,
        grid=(nt,),
        in_specs=[pl.BlockSpec((EXPERT_TILE, D_MODEL), lambda j, te, nu: (j, 0)),
                  pl.BlockSpec((None, None, D_MODEL, D_EXPERT), lambda j, te, nu: (layer, te[j], 0, 0)),
                  pl.BlockSpec((None, None, D_MODEL, D_EXPERT), lambda j, te, nu: (layer, te[j], 0, 0)),
                  pl.BlockSpec((None, None, D_EXPERT, D_MODEL), lambda j, te, nu: (layer, te[j], 0, 0))],
        out_specs=pl.BlockSpec((EXPERT_TILE, D_MODEL), lambda j, te, nu: (j, 0)),
        scratch_shapes=[pltpu.VMEM((D_MODEL, 2 * D_EXPERT), BF16),
                        pltpu.VMEM((D_EXPERT, D_MODEL), BF16)],
    )
    return pl.pallas_call(
        _experts_kernel,
        grid_spec=grid_spec,
        out_shape=jax.ShapeDtypeStruct((n_rows, D_MODEL), F32),
        compiler_params=_cparams(1),
        name="moe_experts",
    )(tile_expert, n_used, xs, wg, wu, wd)


def _combine_ple_kernel(pos_ref, route_ref, h1_ref, ys_hbm, p_ref, wp_ref, g_ref, wgate_ref,
                        o_ref, ybuf, sem, *, tile):
    def row_copy(t, k):
        return pltpu.make_async_copy(ys_hbm.at[pl.ds(pos_ref[0, 0, 2 * t + k], 1)],
                                     ybuf.at[k, pl.ds(t, 1)], sem)

    def issue(t, carry):
        row_copy(t, 0).start()
        row_copy(t, 1).start()
        return carry

    lax.fori_loop(0, tile, issue, 0)
    ple = jnp.dot(p_ref[...].astype(BF16), wp_ref[...], preferred_element_type=F32)
    pltpu.make_async_copy(ys_hbm.at[pl.ds(0, tile)], ybuf.at[0], sem).wait()
    pltpu.make_async_copy(ys_hbm.at[pl.ds(0, tile)], ybuf.at[1], sem).wait()

    route = route_ref[...]
    h2 = h1_ref[...] + route[:, 4:5] * ybuf[0] + route[:, 5:6] * ybuf[1]
    gate = jax.nn.sigmoid(jnp.dot(_rms(h2, g_ref[...]).astype(BF16), wgate_ref[...],
                                  preferred_element_type=F32))
    o_ref[...] = h2 + gate * ple


def _combine_ple(layer, pos, route, h1, ys, p, wp, g, wgate):
    T = h1.shape[0]
    tile = min(GATHER_TILE, T)
    pos3 = pos.reshape(T // tile, 1, 2 * tile)
    const = lambda *shape: pl.BlockSpec(shape, lambda i: (0,) * len(shape))
    row = lambda width: pl.BlockSpec((tile, width), lambda i: (i, 0))
    return pl.pallas_call(
        functools.partial(_combine_ple_kernel, tile=tile),
        grid=(T // tile,),
        in_specs=[pl.BlockSpec((1, 1, 2 * tile), lambda i: (i, 0, 0), memory_space=pltpu.SMEM),
                  row(LANES), row(D_MODEL), pl.BlockSpec(memory_space=pl.ANY),
                  pl.BlockSpec((None, tile, D_PLE), lambda i: (layer, i, 0)),
                  const(D_PLE, D_MODEL), const(1, D_MODEL), const(D_MODEL, D_MODEL)],
        out_specs=row(D_MODEL),
        out_shape=jax.ShapeDtypeStruct((T, D_MODEL), F32),
        scratch_shapes=[pltpu.VMEM((2, tile, D_MODEL), F32), pltpu.SemaphoreType.DMA(())],
        compiler_params=_cparams(1),
        name="combine_ple",
    )(pos3, route, h1, ys, p, wp, g, wgate)


def _rope(z, cos, sin_lo, sin_hi):
    half = ROT_DIM // 2
    outs = []
    for j in range(z.shape[1] // LANES):
        zj = z[:, j * LANES:(j + 1) * LANES]
        outs.append(zj * cos + pltpu.roll(zj, LANES - half, 1) * sin_lo + pltpu.roll(zj, half, 1) * sin_hi)
    return outs[0] if len(outs) == 1 else jnp.concatenate(outs, axis=1)


def _proj_odd_kernel(h_ref, g_ref, w_ref, qg_ref, kg_ref, cos_ref, slo_ref, shi_ref,
                     q_ref, k_ref, v_ref, glu_ref):
    a = _rms(h_ref[...], g_ref[...]).astype(BF16)
    cos, slo, shi = cos_ref[...], slo_ref[...], shi_ref[...]
    zq = jnp.dot(a, w_ref[:, 0:SWA_WIDTH], preferred_element_type=F32)
    q_ref[...] = (_rope(_head_rms(zq, qg_ref[...]), cos, slo, shi) * (HEAD_DIM ** -0.5)).astype(BF16)
    zk = jnp.dot(a, w_ref[:, SWA_WIDTH:SWA_WIDTH + KV_WIDTH], preferred_element_type=F32)
    k_ref[...] = _rope(_head_rms(zk, kg_ref[...]), cos, slo, shi).astype(BF16)
    v0 = SWA_WIDTH + KV_WIDTH
    v_ref[...] = jnp.dot(a, w_ref[:, v0:v0 + KV_WIDTH], preferred_element_type=F32).astype(BF16)
    d0 = v0 + KV_WIDTH
    zd = jnp.dot(a, w_ref[:, d0:d0 + 2 * CONV_CH], preferred_element_type=F32)
    glu_ref[...] = zd[:, :CONV_CH] * jax.nn.sigmoid(zd[:, CONV_CH:])


def _proj_odd(h, g, w, qg, kg, cos, slo, shi, *, seq):
    T = h.shape[0]
    tm = min(ROW_TILE, seq)
    tps = seq // tm
    const = lambda *shape: pl.BlockSpec(shape, lambda i: (0,) * len(shape))
    row = lambda width: pl.BlockSpec((tm, width), lambda i: (i, 0))
    tab = pl.BlockSpec((tm, LANES), lambda i: (i % tps, 0))
    return pl.pallas_call(
        _proj_odd_kernel,
        grid=(T // tm,),
        in_specs=[row(D_MODEL), const(1, D_MODEL), const(*w.shape), const(1, SWA_WIDTH),
                  const(1, KV_WIDTH), tab, tab, tab],
        out_specs=[row(SWA_WIDTH), row(KV_WIDTH), row(KV_WIDTH), row(CONV_CH)],
        out_shape=[jax.ShapeDtypeStruct((T, SWA_WIDTH), BF16),
                   jax.ShapeDtypeStruct((T, KV_WIDTH), BF16),
                   jax.ShapeDtypeStruct((T, KV_WIDTH), BF16),
                   jax.ShapeDtypeStruct((T, CONV_CH), F32)],
        compiler_params=_cparams(1),
        name="proj_odd",
    )(h, g, w, qg, kg, cos, slo, shi)


def _swa_kernel(sink_ref, q_ref, k_ref, v_ref, o_ref, *, seq):
    lo = _lane_lo((1, LANES))
    nt = (((1,), (1,)), ((), ()))
    W = WINDOW
    qi = lax.broadcasted_iota(jnp.int32, (W, 2 * W), 0)
    kj = lax.broadcasted_iota(jnp.int32, (W, 2 * W), 1)
    band = (kj > qi) & (kj <= qi + W)

    def block(n, kstart, mask):
        qrow = pl.ds(pl.multiple_of(n * W, W), W)
        kwin = pl.ds(pl.multiple_of(kstart, W), 2 * W)
        ks = k_ref[kwin, :]
        vs = v_ref[kwin, :]
        for j in range(SWA_WIDTH // LANES):
            cols = slice(j * LANES, (j + 1) * LANES)
            q = q_ref[qrow, cols]
            zero = jnp.zeros_like(q)
            outs = []
            for head, qh in ((j, jnp.where(lo, q, zero)), (4 + j, jnp.where(lo, zero, q))):
                s = jnp.where(mask, lax.dot_general(qh, ks, nt, preferred_element_type=F32), NEG)
                sink = sink_ref[head]
                m = jnp.maximum(jnp.max(s, axis=-1, keepdims=True), sink)
                p = jnp.exp(s - m)
                l = jnp.sum(p, axis=-1, keepdims=True) + jnp.exp(sink - m)
                outs.append(jnp.dot(p.astype(BF16), vs, preferred_element_type=F32) / l)
            o_ref[qrow, cols] = jnp.where(lo, outs[0], outs[1]).astype(BF16)

    block(0, 0, kj <= qi)

    def body(n, carry):
        block(n, (n - 1) * W, band)
        return carry

    lax.fori_loop(1, seq // W, body, 0)


def _swa_attention(sinks, q, k, v, *, seq):
    T = q.shape[0]
    B = T // seq
    return pl.pallas_call(
        functools.partial(_swa_kernel, seq=seq),
        grid=(B,),
        in_specs=[pl.BlockSpec(memory_space=pltpu.SMEM),
                  pl.BlockSpec((seq, SWA_WIDTH), lambda b: (b, 0)),
                  pl.BlockSpec((seq, KV_WIDTH), lambda b: (b, 0)),
                  pl.BlockSpec((seq, KV_WIDTH), lambda b: (b, 0))],
        out_specs=pl.BlockSpec((seq, SWA_WIDTH), lambda b: (b, 0)),
        out_shape=jax.ShapeDtypeStruct((T, SWA_WIDTH), BF16),
        compiler_params=_cparams(1),
        name="swa_attention",
    )(sinks, q, k, v)


def _conv_kernel(prev_ref, cur_ref, w_ref, g_ref, b_ref, o_ref, pad_ref, *, tile, sub):
    r = pl.program_id(1)
    tail = prev_ref[tile - CONV_HALO:, :]
    pad_ref[0:CONV_HALO, :] = jnp.where(r > 0, tail, jnp.zeros_like(tail))
    pad_ref[CONV_HALO:, :] = cur_ref[...]
    w = w_ref[...]
    first = CONV_HALO - (CONV_WIDTH - 1)
    for s in range(tile // sub):
        acc = jnp.zeros((sub, CONV_CH), F32)
        for j in range(CONV_WIDTH):
            start = s * sub + first + j
            acc = acc + pad_ref[start:start + sub, :] * w[j:j + 1, :]
        y = _layernorm(acc, g_ref[...], b_ref[...])
        o_ref[s * sub:(s + 1) * sub, :] = (y * jax.nn.sigmoid(y)).astype(BF16)


def _conv_module(glu, w, g, b, *, seq):
    T = glu.shape[0]
    B = T // seq
    tile = min(CONV_TILE, seq)
    nr = seq // tile
    const = lambda *shape: pl.BlockSpec(shape, lambda bb, r: (0,) * len(shape))
    return pl.pallas_call(
        functools.partial(_conv_kernel, tile=tile, sub=64),
        grid=(B, nr),
        in_specs=[pl.BlockSpec((tile, CONV_CH), lambda bb, r: (bb * nr + jnp.maximum(r - 1, 0), 0)),
                  pl.BlockSpec((tile, CONV_CH), lambda bb, r: (bb * nr + r, 0)),
                  const(CONV_WIDTH, CONV_CH), const(1, CONV_CH), const(1, CONV_CH)],
        out_specs=pl.BlockSpec((tile, CONV_CH), lambda bb, r: (bb * nr + r, 0)),
        out_shape=jax.ShapeDtypeStruct((T, CONV_CH), BF16),
        scratch_shapes=[pltpu.VMEM((CONV_HALO + tile, CONV_CH), F32)],
        compiler_params=_cparams(2),
        name="conv_module",
    )(glu, glu, w, g, b)


def _routing_tables(route, counts, n_tiles):
    e = route[:, 0:2].astype(jnp.int32)
    rank = route[:, 2:4].astype(jnp.int32)
    cnt = counts[0, :N_EXPERTS].astype(jnp.int32)
    tiles = (cnt + EXPERT_TILE - 1) // EXPERT_TILE
    tile_end = jnp.cumsum(tiles)
    offset = (tile_end - tiles) * EXPERT_TILE
    onehot = e[:, :, None] == jnp.arange(N_EXPERTS, dtype=jnp.int32)
    pos = (rank + jnp.sum(jnp.where(onehot, offset, 0), axis=-1)).reshape(-1)
    n_used = tile_end[-1]
    tile_id = jnp.minimum(jnp.arange(n_tiles, dtype=jnp.int32), n_used - 1)
    tile_expert = jnp.sum((tile_end[None, :] <= tile_id[:, None]).astype(jnp.int32), axis=1)
    return pos, tile_expert, n_used.reshape(1).astype(jnp.int32)


def _moe_ple(h, ya, yb, wo, layer, norm_ffn, wr, br, wg, wu, wd, p, wp, ple_norm, wgate):
    T = h.shape[0]
    n_tiles = (2 * T) // EXPERT_TILE + N_EXPERTS
    h1, m, route, counts = _outproj_router(h, ya, yb, wo, norm_ffn, wr, br)
    pos, tile_expert, n_used = _routing_tables(route, counts, n_tiles)
    xs = _dispatch(pos, m, n_tiles * EXPERT_TILE)
    ys = _experts(layer, tile_expert, n_used, xs, wg, wu, wd)
    return _combine_ple(layer, pos, route, h1, ys, p, wp, ple_norm, wgate)


def _router_weights(w_coarse, b_coarse, w_fine, b_fine):
    wf = w_fine.transpose(1, 0, 2).reshape(D_MODEL, N_EXPERTS)
    wr = jnp.concatenate([wf, w_coarse, jnp.zeros((D_MODEL, LANES - N_EXPERTS - N_GROUPS), F32)], axis=1)
    br = jnp.concatenate([b_fine.reshape(-1), b_coarse, jnp.zeros((LANES - N_EXPERTS - N_GROUPS,), F32)])
    return wr, br.reshape(1, LANES)


def _rope_tables(seq):
    half = ROT_DIM // 2
    inv_freq = ROPE_THETA ** (-jnp.arange(half, dtype=F32) * 2.0 / ROT_DIM)
    ang = jnp.arange(seq, dtype=F32)[:, None] * inv_freq[None, :]
    cos, sin = jnp.cos(ang), jnp.sin(ang)
    zeros = jnp.zeros((seq, HEAD_DIM - ROT_DIM), F32)
    z8 = jnp.zeros((seq, half), F32)
    cos_h = jnp.concatenate([cos, cos, zeros + 1.0], axis=1)
    slo_h = jnp.concatenate([-sin, z8, zeros], axis=1)
    shi_h = jnp.concatenate([z8, sin, zeros], axis=1)
    two = lambda t: jnp.concatenate([t, t], axis=1)
    return two(cos_h), two(slo_h), two(shi_h)


def kernel(x, p, norm_mix, even_w_in, fox_b_f, gmlp_ln_g, gmlp_ln_b, gmlp_w_s, gmlp_b_s, fox_q_norm, fox_k_norm, even_w_out, odd_w_in, swa_q_norm, swa_k_norm, swa_sinks, conv_w, conv_ln_g, conv_ln_b, odd_w_out, norm_ffn, moe_w_coarse, moe_b_coarse, moe_w_fine, moe_b_fine, moe_w_gate, moe_w_up, moe_w_down, ple_w_proj, ple_norm, ple_w_gate):
    B, S, D = x.shape
    T = B * S
    h = x.reshape(T, D)
    p = p.reshape(p.shape[0], T, D_PLE)
    row = lambda v: v.reshape(1, -1)

    def moe_args(i):
        wr, br = _router_weights(moe_w_coarse[i], moe_b_coarse[i], moe_w_fine[i], moe_b_fine[i])
        return (i, row(norm_ffn[i]), wr, br, moe_w_gate, moe_w_up, moe_w_down, p,
                ple_w_proj[i].astype(BF16), row(ple_norm[i]), ple_w_gate[i].astype(BF16))

    n_in = even_w_in.shape[2]
    w_in = jnp.pad(even_w_in[0], ((0, 0), (0, 2 * GMLP_WIDTH + 3 * FOX_WIDTH + LANES - n_in))).astype(BF16)
    b_f = jnp.pad(fox_b_f[0], (0, LANES - FOX_HEADS)).reshape(1, LANES)
    bs_full = jnp.repeat(gmlp_b_s[0].T, HEAD_DIM, axis=1)
    ya, q, k, v, c, ct = _proj_even(
        h, row(norm_mix[0]), w_in, b_f, row(gmlp_ln_g[0]), row(gmlp_ln_b[0]), gmlp_w_s[0], bs_full,
        row(jnp.tile(fox_q_norm[0], FOX_HEADS)), row(jnp.tile(fox_k_norm[0], FOX_HEADS)), seq=S)
    yb = _fox_attention(q, k, v, c, ct, seq=S)
    h = _moe_ple(h, ya, yb, even_w_out[0].astype(BF16), *moe_args(0))

    order = jnp.array([0, 4, 1, 5, 2, 6, 3, 7])
    cols = (order[:, None] * HEAD_DIM + jnp.arange(HEAD_DIM)[None, :]).reshape(-1)
    w_odd = jnp.concatenate([odd_w_in[0][:, :SWA_WIDTH][:, cols], odd_w_in[0][:, SWA_WIDTH:]], axis=1).astype(BF16)
    w_out_odd = jnp.concatenate([odd_w_out[0][:SWA_WIDTH][cols], odd_w_out[0][SWA_WIDTH:]], axis=0).astype(BF16)
    cos, slo, shi = _rope_tables(S)
    q, k, v, glu = _proj_odd(h, row(norm_mix[1]), w_odd, row(jnp.tile(swa_q_norm[0], 8)),
                             row(jnp.tile(swa_k_norm[0], 2)), cos, slo, shi, seq=S)
    yc = _swa_attention(swa_sinks[0], q, k, v, seq=S)
    yd = _conv_module(glu, conv_w[0], row(conv_ln_g[0]), row(conv_ln_b[0]), seq=S)
    h = _moe_ple(h, yc, yd, w_out_odd, *moe_args(1))
    return h.reshape(B, S, D)
```

```python
import functools

import jax
import jax.numpy as jnp
import numpy as np
from jax import lax
from jax.experimental import pallas as pl
from jax.experimental.pallas import tpu as pltpu

F32 = jnp.float32
BF16 = jnp.bfloat16
HIGHEST = lax.Precision.HIGHEST

D_MODEL = 1024
HEAD_DIM = 64
LANES = 128
GMLP_WIDTH = 512
CHUNK = 128
FOX_WIDTH = 512
FOX_HEADS = 8
SWA_WIDTH = 512
KV_WIDTH = 128
WINDOW = 128
CONV_CH = 512
CONV_WIDTH = 31
CONV_HALO = 32
ROPE_THETA = 500000.0
ROT_DIM = 16
N_GROUPS = 4
EXPERTS_PER_GROUP = 8
N_EXPERTS = 32
D_EXPERT = 256
D_PLE = 256
EPS = 1e-6
NEG = -1e30
LOG2E = 1.4426950408889634
BIAS_PIECES = 3

EXPERT_TILE = 256
ROW_TILE = 512
GATHER_TILE = 256
DISPATCH_TILE = 1024
DMA_UNROLL = 8
ATTN_TILE = 256
CONV_TILE = 256
VMEM_LIMIT = 56 * 1024 * 1024


def _cparams(n_axes=1):
    return pltpu.CompilerParams(dimension_semantics=("arbitrary",) * n_axes,
                                vmem_limit_bytes=VMEM_LIMIT)


def _rms(x, gain):
    return x * lax.rsqrt(jnp.mean(x * x, axis=-1, keepdims=True) + EPS) * gain


def _layernorm(x, g, b):
    mu = jnp.mean(x, axis=-1, keepdims=True)
    xc = x - mu
    var = jnp.mean(xc * xc, axis=-1, keepdims=True)
    return xc * lax.rsqrt(var + EPS) * g + b


def _head_rms(z, gain):
    r = lax.broadcasted_iota(jnp.int32, (LANES, LANES), 0) // HEAD_DIM
    c = lax.broadcasted_iota(jnp.int32, (LANES, LANES), 1) // HEAD_DIM
    bd = jnp.where(r == c, 1.0 / HEAD_DIM, 0.0).astype(F32)
    outs = []
    for j in range(z.shape[1] // LANES):
        zj = z[:, j * LANES:(j + 1) * LANES]
        ms = jnp.dot(zj * zj, bd, precision=HIGHEST, preferred_element_type=F32)
        outs.append(zj * lax.rsqrt(ms + EPS))
    zn = outs[0] if len(outs) == 1 else jnp.concatenate(outs, axis=1)
    return zn * gain


def _lane_lo(shape):
    return (lax.broadcasted_iota(jnp.int32, shape, len(shape) - 1) % LANES) < HEAD_DIM


def _proj_even_kernel(h_ref, g_ref, w_ref, bf_ref, lng_ref, lnb_ref, ws_ref, bs_ref, qg_ref, kg_ref,
                      pq_ref, pk_ref, oneq_ref, onek_ref,
                      ya_ref, q_ref, k_ref, v_ref, xq_ref, xk_ref, carry_ref, *, tm, tiles_per_seq):
    i = pl.program_id(0)
    a = _rms(h_ref[...], g_ref[...]).astype(BF16)

    za = jax.nn.gelu(jnp.dot(a, w_ref[:, 0:2 * GMLP_WIDTH], preferred_element_type=F32))
    u = za[:, :GMLP_WIDTH]
    vln = _layernorm(za[:, GMLP_WIDTH:], lng_ref[...], lnb_ref[...]).astype(BF16)
    rr = lax.broadcasted_iota(jnp.int32, (CHUNK, CHUNK), 0)
    cc = lax.broadcasted_iota(jnp.int32, (CHUNK, CHUNK), 1)
    lo = _lane_lo((CHUNK, LANES))
    for j in range(GMLP_WIDTH // LANES):
        w_a = jnp.where(rr >= cc, ws_ref[2 * j], 0.0).astype(BF16)
        w_b = jnp.where(rr >= cc, ws_ref[2 * j + 1], 0.0).astype(BF16)
        cols = slice(j * LANES, (j + 1) * LANES)
        for c in range(tm // CHUNK):
            rows = slice(c * CHUNK, (c + 1) * CHUNK)
            vp = vln[rows, cols]
            mixed = jnp.where(lo, jnp.dot(w_a, vp, preferred_element_type=F32),
                              jnp.dot(w_b, vp, preferred_element_type=F32)) + bs_ref[:, cols]
            ya_ref[rows, cols] = (u[rows, cols] * mixed).astype(BF16)

    q0 = 2 * GMLP_WIDTH
    zq = jnp.dot(a, w_ref[:, q0:q0 + FOX_WIDTH], preferred_element_type=F32)
    q_ref[...] = (_head_rms(zq, qg_ref[...]) * (LOG2E * HEAD_DIM ** -0.5)).astype(BF16)
    zk = jnp.dot(a, w_ref[:, q0 + FOX_WIDTH:q0 + 2 * FOX_WIDTH], preferred_element_type=F32)
    k_ref[...] = _head_rms(zk, kg_ref[...]).astype(BF16)
    v_ref[...] = jnp.dot(a, w_ref[:, q0 + 2 * FOX_WIDTH:q0 + 3 * FOX_WIDTH],
                         preferred_element_type=F32).astype(BF16)

    zf = jnp.dot(a, w_ref[:, q0 + 3 * FOX_WIDTH:], preferred_element_type=F32) + bf_ref[...]
    ls = jnp.minimum(zf, 0.0) - jnp.log(1.0 + jnp.exp(-jnp.abs(zf)))

    @pl.when(i % tiles_per_seq == 0)
    def _():
        carry_ref[...] = jnp.zeros_like(carry_ref)

    tr = lax.broadcasted_iota(jnp.int32, (tm, tm), 0)
    tc = lax.broadcasted_iota(jnp.int32, (tm, tm), 1)
    tri = jnp.where(tr >= tc, 1.0, 0.0).astype(F32)
    c = jnp.dot(tri, ls, precision=HIGHEST, preferred_element_type=F32) + carry_ref[...]
    carry_ref[...] = c[tm - 1:tm, :]

    c2 = c * LOG2E
    hi = c2.astype(BF16)
    r1 = c2 - hi.astype(F32)
    mid = r1.astype(BF16)
    low = (r1 - mid.astype(F32)).astype(BF16)
    pieces = jnp.concatenate([hi, mid, low], axis=1)
    xq_ref[...] = (jnp.dot(pieces, pq_ref[...], preferred_element_type=F32) + oneq_ref[...]).astype(BF16)
    xk_ref[...] = (jnp.dot(pieces, pk_ref[...], preferred_element_type=F32) + onek_ref[...]).astype(BF16)


def _proj_even(h, g, w, bf, lng, lnb, ws, bs_full, qg, kg, *, seq):
    T = h.shape[0]
    tm = min(ROW_TILE, seq)
    n_in = w.shape[1]
    const = lambda *shape: pl.BlockSpec(shape, lambda i: (0,) * len(shape))
    row = lambda width: pl.BlockSpec((tm, width), lambda i: (i, 0))
    tps = seq // tm
    return pl.pallas_call(
        functools.partial(_proj_even_kernel, tm=tm, tiles_per_seq=tps),
        grid=(T // tm,),
        in_specs=[row(D_MODEL), const(1, D_MODEL), const(D_MODEL, n_in), const(1, LANES),
                  const(1, GMLP_WIDTH), const(1, GMLP_WIDTH), const(8, CHUNK, CHUNK),
                  const(CHUNK, GMLP_WIDTH), const(1, FOX_WIDTH), const(1, FOX_WIDTH),
                  const(BIAS_PIECES * LANES, FOX_WIDTH), const(BIAS_PIECES * LANES, FOX_WIDTH),
                  const(1, FOX_WIDTH), const(1, FOX_WIDTH)],
        out_specs=[row(GMLP_WIDTH)] + [row(FOX_WIDTH)] * 5,
        out_shape=[jax.ShapeDtypeStruct((T, GMLP_WIDTH), BF16)]
                  + [jax.ShapeDtypeStruct((T, FOX_WIDTH), BF16)] * 5,
        scratch_shapes=[pltpu.VMEM((1, LANES), F32)],
        compiler_params=_cparams(1),
        name="proj_even",
    )(h, g, w, bf, lng, lnb, ws, bs_full, qg, kg, *_bias_placement())


def _bias_placement():
    pq = np.zeros((BIAS_PIECES * LANES, FOX_WIDTH), np.float32)
    pk = np.zeros((BIAS_PIECES * LANES, FOX_WIDTH), np.float32)
    oneq = np.zeros((1, FOX_WIDTH), np.float32)
    onek = np.zeros((1, FOX_WIDTH), np.float32)
    for head in range(FOX_HEADS):
        base = (head // 2) * LANES + (HEAD_DIM if head % 2 == 0 else 0)
        for piece in range(BIAS_PIECES):
            pq[piece * LANES + head, base + piece] = 1.0
            onek[0, base + piece] = 1.0
            pk[piece * LANES + head, base + BIAS_PIECES + piece] = -1.0
            oneq[0, base + BIAS_PIECES + piece] = 1.0
    return (jnp.asarray(pq, BF16), jnp.asarray(pk, BF16), jnp.asarray(oneq), jnp.asarray(onek))


def _fox_kernel(q_ref, xq_ref, k_ref, xk_ref, v_ref, o_ref, *, tq):
    i = pl.program_id(1)
    half = tq // 2
    lo = _lane_lo((1, LANES))
    rr = lax.broadcasted_iota(jnp.int32, (half, tq), 0)
    cc = lax.broadcasted_iota(jnp.int32, (half, tq), 1)
    nt = (((1,), (1,)), ((), ()))
    for hp in range(FOX_WIDTH // LANES):
        cols = slice(hp * LANES, (hp + 1) * LANES)
        q_aug = []
        for hf in range(2):
            rows = slice(hf * half, (hf + 1) * half)
            q, xq = q_ref[rows, cols], xq_ref[rows, cols]
            q_aug.append((jnp.where(lo, q, xq), jnp.where(lo, xq, q)))

        def step(j, carry, masked):
            start = pl.multiple_of(j * tq, tq)
            ks, xk = k_ref[pl.ds(start, tq), cols], xk_ref[pl.ds(start, tq), cols]
            vs = v_ref[pl.ds(start, tq), cols]
            one = jnp.ones_like(vs)
            k_aug = (jnp.where(lo, ks, xk), jnp.where(lo, xk, ks))
            v_aug = (jnp.where(lo, vs, one), jnp.where(lo, one, vs))
            out = []
            for hf in range(2):
                for head in range(2):
                    m, acc = carry[2 * hf + head]
                    s = lax.dot_general(q_aug[hf][head], k_aug[head], nt, preferred_element_type=F32)
                    if masked:
                        s = jnp.where(cc <= rr + hf * half, s, NEG)
                    n = jnp.maximum(m, jnp.max(s, axis=-1, keepdims=True))
                    p = jnp.exp2(s - n).astype(BF16)
                    acc = acc * jnp.exp2(m - n) + jnp.dot(p, v_aug[head], preferred_element_type=F32)
                    out.append((n, acc))
            return tuple(out)

        init = tuple((jnp.full((half, 1), NEG, F32), jnp.zeros((half, LANES), F32)) for _ in range(4))
        carry = lax.fori_loop(0, i, lambda j, c: step(j, c, False), init)
        carry = step(i, carry, True)
        for hf in range(2):
            norm = [acc / pltpu.roll(acc, HEAD_DIM, 1) for _, acc in carry[2 * hf:2 * hf + 2]]
            o_ref[hf * half:(hf + 1) * half, cols] = jnp.where(lo, norm[0], norm[1]).astype(BF16)


def _fox_attention(q, xq, k, xk, v, *, seq):
    T = q.shape[0]
    B = T // seq
    tq = min(ATTN_TILE, seq)
    nq = seq // tq
    tile = pl.BlockSpec((tq, FOX_WIDTH), lambda b, i: (b * nq + i, 0))
    whole = pl.BlockSpec((seq, FOX_WIDTH), lambda b, i: (b, 0))
    return pl.pallas_call(
        functools.partial(_fox_kernel, tq=tq),
        grid=(B, nq),
        in_specs=[tile, tile, whole, whole, whole],
        out_specs=tile,
        out_shape=jax.ShapeDtypeStruct((T, FOX_WIDTH), BF16),
        compiler_params=_cparams(2),
        name="fox_attention",
    )(q, xq, k, xk, v)


def _outproj_router_kernel(h_ref, ya_ref, yb_ref, wo_ref, g_ref, wr_ref, br_ref,
                           h1_ref, m_ref, route_ref, cnt_ref, carry_ref, *, tm):
    i = pl.program_id(0)
    half = wo_ref.shape[0] // 2
    mix = (jnp.dot(ya_ref[...], wo_ref[0:half, :], preferred_element_type=F32)
           + jnp.dot(yb_ref[...], wo_ref[half:, :], preferred_element_type=F32))
    h1 = h_ref[...] + mix
    h1_ref[...] = h1
    m = _rms(h1, g_ref[...])
    m_hi = m.astype(BF16)
    m_ref[...] = m

    m_lo = (m - m_hi.astype(F32)).astype(BF16)
    hh = jnp.dot(m_hi, wr_ref[...], preferred_element_type=F32)
    lh = jnp.dot(m_lo, wr_ref[:, :LANES], preferred_element_type=F32)
    logits = hh[:, :LANES] + (hh[:, LANES:] + lh) + br_ref[...]
    lane_i = lax.broadcasted_iota(jnp.int32, (tm, LANES), 1)
    lane = lane_i.astype(F32)
    group_of_lane = (lane_i // EXPERTS_PER_GROUP).astype(F32)
    is_coarse = (lane_i >= N_EXPERTS) & (lane_i < N_EXPERTS + N_GROUPS)
    coarse = jnp.where(is_coarse, logits, NEG)
    cmax = jnp.max(coarse, axis=-1, keepdims=True)
    gidx = jnp.min(jnp.where(coarse == cmax, lane - N_EXPERTS, float(LANES)), axis=-1, keepdims=True)
    p_g = 1.0 / jnp.sum(jnp.where(is_coarse, jnp.exp(coarse - cmax), 0.0), axis=-1, keepdims=True)
    in_group = (lane_i < N_EXPERTS) & (group_of_lane == gidx)
    fine = jnp.where(in_group, logits, NEG)
    v1 = jnp.max(fine, axis=-1, keepdims=True)
    i1 = jnp.min(jnp.where(fine == v1, lane, float(LANES)), axis=-1, keepdims=True)
    fine2 = jnp.where(lane == i1, NEG, fine)
    v2 = jnp.max(fine2, axis=-1, keepdims=True)
    i2 = jnp.min(jnp.where(fine2 == v2, lane, float(LANES)), axis=-1, keepdims=True)
    e2 = jnp.exp(v2 - v1)
    w1 = p_g / (1.0 + e2)
    w2 = p_g * e2 / (1.0 + e2)

    @pl.when(i == 0)
    def _():
        carry_ref[...] = jnp.zeros_like(carry_ref)

    hit1 = lane == i1
    hit2 = lane == i2
    onehot = jnp.where(hit1 | hit2, 1.0, 0.0).astype(F32)
    tr = lax.broadcasted_iota(jnp.int32, (tm, tm), 0)
    tc = lax.broadcasted_iota(jnp.int32, (tm, tm), 1)
    strict = jnp.where(tr > tc, 1.0, 0.0).astype(BF16)
    before = jnp.dot(strict, onehot.astype(BF16), preferred_element_type=F32) + carry_ref[...]
    r1 = jnp.sum(jnp.where(hit1, before, 0.0), axis=-1, keepdims=True)
    r2 = jnp.sum(jnp.where(hit2, before, 0.0), axis=-1, keepdims=True)
    total = carry_ref[...] + jnp.sum(onehot, axis=0, keepdims=True)
    carry_ref[...] = total
    cnt_ref[...] = jnp.broadcast_to(total, cnt_ref.shape)

    route = jnp.where(lane == 0, i1, 0.0)
    route = jnp.where(lane == 1, i2, route)
    route = jnp.where(lane == 2, r1, route)
    route = jnp.where(lane == 3, r2, route)
    route = jnp.where(lane == 4, w1, route)
    route = jnp.where(lane == 5, w2, route)
    route_ref[...] = route


def _outproj_router(h, ya, yb, wo, g, wr, br):
    T = h.shape[0]
    tm = min(ROW_TILE, T)
    const = lambda *shape: pl.BlockSpec(shape, lambda i: (0,) * len(shape))
    row = lambda width: pl.BlockSpec((tm, width), lambda i: (i, 0))
    return pl.pallas_call(
        functools.partial(_outproj_router_kernel, tm=tm),
        grid=(T // tm,),
        in_specs=[row(D_MODEL), row(ya.shape[1]), row(yb.shape[1]), const(*wo.shape),
                  const(1, D_MODEL), const(D_MODEL, 2 * LANES), const(1, LANES)],
        out_specs=[row(D_MODEL), row(D_MODEL), row(LANES), const(8, LANES)],
        out_shape=[jax.ShapeDtypeStruct((T, D_MODEL), F32),
                   jax.ShapeDtypeStruct((T, D_MODEL), F32),
                   jax.ShapeDtypeStruct((T, LANES), F32),
                   jax.ShapeDtypeStruct((8, LANES), F32)],
        scratch_shapes=[pltpu.VMEM((1, LANES), F32)],
        compiler_params=_cparams(1),
        name="outproj_router",
    )(h, ya, yb, wo, g, wr, br)


def _dispatch_kernel(pos_ref, m_ref, xs_in_hbm, xs_hbm, sem, *, tile):
    del xs_in_hbm

    def row_copy(t, k):
        return pltpu.make_async_copy(m_ref.at[pl.ds(t, 1)],
                                     xs_hbm.at[pl.ds(pos_ref[0, 0, 2 * t + k], 1)], sem)

    def issue(c, carry):
        for u in range(DMA_UNROLL):
            row_copy(c * DMA_UNROLL + u, 0).start(priority=0)
            row_copy(c * DMA_UNROLL + u, 1).start(priority=1)
        return carry

    lax.fori_loop(0, tile // DMA_UNROLL, issue, 0)
    for _ in range(2):
        pltpu.make_async_copy(m_ref, xs_hbm.at[pl.ds(0, tile)], sem).wait()


def _dispatch(pos, m, n_rows):
    T, width = m.shape
    tile = min(DISPATCH_TILE, T)
    pos3 = pos.reshape(T // tile, 1, 2 * tile)
    xs0 = jnp.zeros((n_rows, width), m.dtype)
    return pl.pallas_call(
        functools.partial(_dispatch_kernel, tile=tile),
        grid=(T // tile,),
        in_specs=[pl.BlockSpec((1, 1, 2 * tile), lambda i: (i, 0, 0), memory_space=pltpu.SMEM),
                  pl.BlockSpec((tile, width), lambda i: (i, 0)),
                  pl.BlockSpec(memory_space=pl.ANY)],
        out_specs=pl.BlockSpec(memory_space=pl.ANY),
        out_shape=jax.ShapeDtypeStruct((n_rows, width), m.dtype),
        scratch_shapes=[pltpu.SemaphoreType.DMA(())],
        input_output_aliases={2: 0},
        compiler_params=pltpu.CompilerParams(dimension_semantics=("arbitrary",),
                                             has_side_effects=True),
        name="moe_dispatch",
    )(pos3, m, xs0)


def _experts_kernel(te_ref, nused_ref, xs_ref, wg_ref, wu_ref, wd_ref, ys_ref, wgu_b, wd_b):
    j = pl.program_id(0)
    prev = te_ref[jnp.maximum(j - 1, 0)]

    @pl.when((j == 0) | (te_ref[j] != prev))
    def _():
        wgu_b[:, :D_EXPERT] = wg_ref[...].astype(BF16)
        wgu_b[:, D_EXPERT:] = wu_ref[...].astype(BF16)
        wd_b[...] = wd_ref[...].astype(BF16)

    @pl.when(j < nused_ref[0])
    def _():
        gu = jnp.dot(xs_ref[...].astype(BF16), wgu_b[...], preferred_element_type=F32)
        g = gu[:, :D_EXPERT]
        act = g * jax.nn.sigmoid(g) * gu[:, D_EXPERT:]
        ys_ref[...] = jnp.dot(act.astype(BF16), wd_b[...], preferred_element_type=F32)

    @pl.when(j >= nused_ref[0])
    def _():
        ys_ref[...] = jnp.zeros_like(ys_ref)


def _experts(layer, tile_expert, n_used, xs, wg, wu, wd):
    n_rows = xs.shape[0]
    nt = n_rows // EXPERT_TILE
    grid_spec = pltpu.PrefetchScalarGridSpec(
        num_scalar_prefetch=2,
        grid=(nt,),
        in_specs=[pl.BlockSpec((EXPERT_TILE, D_MODEL), lambda j, te, nu: (j, 0)),
                  pl.BlockSpec((None, None, D_MODEL, D_EXPERT), lambda j, te, nu: (layer, te[j], 0, 0)),
                  pl.BlockSpec((None, None, D_MODEL, D_EXPERT), lambda j, te, nu: (layer, te[j], 0, 0)),
                  pl.BlockSpec((None, None, D_EXPERT, D_MODEL), lambda j, te, nu: (layer, te[j], 0, 0))],
        out_specs=pl.BlockSpec((EXPERT_TILE, D_MODEL), lambda j, te, nu: (j, 0)),
        scratch_shapes=[pltpu.VMEM((D_MODEL, 2 * D_EXPERT), BF16),
                        pltpu.VMEM((D_EXPERT, D_MODEL), BF16)],
    )
    return pl.pallas_call(
        _experts_kernel,
        grid_spec=grid_spec,
        out_shape=jax.ShapeDtypeStruct((n_rows, D_MODEL), F32),
        compiler_params=_cparams(1),
        name="moe_experts",
    )(tile_expert, n_used, xs, wg, wu, wd)


def _combine_ple_kernel(pos_ref, next_pos_ref, route_ref, h1_ref, ys_hbm, p_ref, wp_ref, g_ref, wgate_ref,
                        o_ref, ybuf, sems, *, tile):
    i = pl.program_id(0)
    n = pl.num_programs(0)
    slot = i % 2

    def gather(table, s):
        def issue(c, carry):
            for u in range(DMA_UNROLL):
                t = c * DMA_UNROLL + u
                for k in range(2):
                    pltpu.make_async_copy(ys_hbm.at[pl.ds(table[0, 0, 2 * t + k], 1)],
                                          ybuf.at[s, k, pl.ds(t, 1)], sems.at[s]).start(priority=k)
            return carry

        lax.fori_loop(0, tile // DMA_UNROLL, issue, 0)

    @pl.when(i == 0)
    def _():
        gather(pos_ref, 0)

    @pl.when(i + 1 < n)
    def _():
        gather(next_pos_ref, 1 - slot)

    ple = jnp.dot(p_ref[...].astype(BF16), wp_ref[...], preferred_element_type=F32)
    for k in range(2):
        pltpu.make_async_copy(ys_hbm.at[pl.ds(0, tile)], ybuf.at[slot, k], sems.at[slot]).wait()

    route = route_ref[...]
    h2 = h1_ref[...] + route[:, 4:5] * ybuf[slot, 0] + route[:, 5:6] * ybuf[slot, 1]
    gate = jax.nn.sigmoid(jnp.dot(_rms(h2, g_ref[...]).astype(BF16), wgate_ref[...],
                                  preferred_element_type=F32))
    o_ref[...] = h2 + gate * ple


def _combine_ple(layer, pos, route, h1, ys, p, wp, g, wgate):
    T = h1.shape[0]
    tile = min(GATHER_TILE, T)
    n = T // tile
    pos3 = pos.reshape(n, 1, 2 * tile)
    const = lambda *shape: pl.BlockSpec(shape, lambda i: (0,) * len(shape))
    row = lambda width: pl.BlockSpec((tile, width), lambda i: (i, 0))
    return pl.pallas_call(
        functools.partial(_combine_ple_kernel, tile=tile),
        grid=(n,),
        in_specs=[pl.BlockSpec((1, 1, 2 * tile), lambda i: (i, 0, 0), memory_space=pltpu.SMEM),
                  pl.BlockSpec((1, 1, 2 * tile), lambda i: (jnp.minimum(i + 1, n - 1), 0, 0),
                               memory_space=pltpu.SMEM),
                  row(LANES), row(D_MODEL), pl.BlockSpec(memory_space=pl.ANY),
                  pl.BlockSpec((None, tile, D_PLE), lambda i: (layer, i, 0)),
                  const(D_PLE, D_MODEL), const(1, D_MODEL), const(D_MODEL, D_MODEL)],
        out_specs=row(D_MODEL),
        out_shape=jax.ShapeDtypeStruct((T, D_MODEL), F32),
        scratch_shapes=[pltpu.VMEM((2, 2, tile, D_MODEL), F32), pltpu.SemaphoreType.DMA((2,))],
        compiler_params=_cparams(1),
        name="combine_ple",
    )(pos3, pos3, route, h1, ys, p, wp, g, wgate)


def _rope(z, cos, sin_lo, sin_hi):
    half = ROT_DIM // 2
    outs = []
    for j in range(z.shape[1] // LANES):
        zj = z[:, j * LANES:(j + 1) * LANES]
        outs.append(zj * cos + pltpu.roll(zj, LANES - half, 1) * sin_lo + pltpu.roll(zj, half, 1) * sin_hi)
    return outs[0] if len(outs) == 1 else jnp.concatenate(outs, axis=1)


def _proj_odd_kernel(h_ref, g_ref, w_ref, qg_ref, kg_ref, cos_ref, slo_ref, shi_ref,
                     q_ref, k_ref, v_ref, glu_ref):
    a = _rms(h_ref[...], g_ref[...]).astype(BF16)
    cos, slo, shi = cos_ref[...], slo_ref[...], shi_ref[...]
    zq = jnp.dot(a, w_ref[:, 0:SWA_WIDTH], preferred_element_type=F32)
    q_ref[...] = (_rope(_head_rms(zq, qg_ref[...]), cos, slo, shi) * (HEAD_DIM ** -0.5)).astype(BF16)
    zk = jnp.dot(a, w_ref[:, SWA_WIDTH:SWA_WIDTH + KV_WIDTH], preferred_element_type=F32)
    k_ref[...] = _rope(_head_rms(zk, kg_ref[...]), cos, slo, shi).astype(BF16)
    v0 = SWA_WIDTH + KV_WIDTH
    v_ref[...] = jnp.dot(a, w_ref[:, v0:v0 + KV_WIDTH], preferred_element_type=F32).astype(BF16)
    d0 = v0 + KV_WIDTH
    zd = jnp.dot(a, w_ref[:, d0:d0 + 2 * CONV_CH], preferred_element_type=F32)
    glu_ref[...] = zd[:, :CONV_CH] * jax.nn.sigmoid(zd[:, CONV_CH:])


def _proj_odd(h, g, w, qg, kg, cos, slo, shi, *, seq):
    T = h.shape[0]
    tm = min(ROW_TILE, seq)
    tps = seq // tm
    const = lambda *shape: pl.BlockSpec(shape, lambda i: (0,) * len(shape))
    row = lambda width: pl.BlockSpec((tm, width), lambda i: (i, 0))
    tab = pl.BlockSpec((tm, LANES), lambda i: (i % tps, 0))
    return pl.pallas_call(
        _proj_odd_kernel,
        grid=(T // tm,),
        in_specs=[row(D_MODEL), const(1, D_MODEL), const(*w.shape), const(1, SWA_WIDTH),
                  const(1, KV_WIDTH), tab, tab, tab],
        out_specs=[row(SWA_WIDTH), row(KV_WIDTH), row(KV_WIDTH), row(CONV_CH)],
        out_shape=[jax.ShapeDtypeStruct((T, SWA_WIDTH), BF16),
                   jax.ShapeDtypeStruct((T, KV_WIDTH), BF16),
                   jax.ShapeDtypeStruct((T, KV_WIDTH), BF16),
                   jax.ShapeDtypeStruct((T, CONV_CH), F32)],
        compiler_params=_cparams(1),
        name="proj_odd",
    )(h, g, w, qg, kg, cos, slo, shi)


def _swa_kernel(sink_ref, q_ref, k_ref, v_ref, o_ref, *, seq):
    lo = _lane_lo((1, LANES))
    nt = (((1,), (1,)), ((), ()))
    W = WINDOW
    qi = lax.broadcasted_iota(jnp.int32, (W, 2 * W), 0)
    kj = lax.broadcasted_iota(jnp.int32, (W, 2 * W), 1)
    band = (kj > qi) & (kj <= qi + W)

    def block(n, kstart, mask):
        qrow = pl.ds(pl.multiple_of(n * W, W), W)
        kwin = pl.ds(pl.multiple_of(kstart, W), 2 * W)
        ks = k_ref[kwin, :]
        vs = v_ref[kwin, :]
        for j in range(SWA_WIDTH // LANES):
            cols = slice(j * LANES, (j + 1) * LANES)
            q = q_ref[qrow, cols]
            zero = jnp.zeros_like(q)
            outs = []
            for head, qh in ((j, jnp.where(lo, q, zero)), (4 + j, jnp.where(lo, zero, q))):
                s = jnp.where(mask, lax.dot_general(qh, ks, nt, preferred_element_type=F32), NEG)
                sink = sink_ref[head]
                m = jnp.maximum(jnp.max(s, axis=-1, keepdims=True), sink)
                p = jnp.exp(s - m)
                l = jnp.sum(p, axis=-1, keepdims=True) + jnp.exp(sink - m)
                outs.append(jnp.dot(p.astype(BF16), vs, preferred_element_type=F32) / l)
            o_ref[qrow, cols] = jnp.where(lo, outs[0], outs[1]).astype(BF16)

    block(0, 0, kj <= qi)

    def body(n, carry):
        block(n, (n - 1) * W, band)
        return carry

    lax.fori_loop(1, seq // W, body, 0)


def _swa_attention(sinks, q, k, v, *, seq):
    T = q.shape[0]
    B = T // seq
    return pl.pallas_call(
        functools.partial(_swa_kernel, seq=seq),
        grid=(B,),
        in_specs=[pl.BlockSpec(memory_space=pltpu.SMEM),
                  pl.BlockSpec((seq, SWA_WIDTH), lambda b: (b, 0)),
                  pl.BlockSpec((seq, KV_WIDTH), lambda b: (b, 0)),
                  pl.BlockSpec((seq, KV_WIDTH), lambda b: (b, 0))],
        out_specs=pl.BlockSpec((seq, SWA_WIDTH), lambda b: (b, 0)),
        out_shape=jax.ShapeDtypeStruct((T, SWA_WIDTH), BF16),
        compiler_params=_cparams(1),
        name="swa_attention",
    )(sinks, q, k, v)


def _conv_kernel(prev_ref, cur_ref, w_ref, g_ref, b_ref, o_ref, pad_ref, *, tile, sub):
    r = pl.program_id(1)
    tail = prev_ref[tile - CONV_HALO:, :]
    pad_ref[0:CONV_HALO, :] = jnp.where(r > 0, tail, jnp.zeros_like(tail))
    pad_ref[CONV_HALO:, :] = cur_ref[...]
    w = w_ref[...]
    first = CONV_HALO - (CONV_WIDTH - 1)
    for s in range(tile // sub):
        acc = jnp.zeros((sub, CONV_CH), F32)
        for j in range(CONV_WIDTH):
            start = s * sub + first + j
            acc = acc + pad_ref[start:start + sub, :] * w[j:j + 1, :]
        y = _layernorm(acc, g_ref[...], b_ref[...])
        o_ref[s * sub:(s + 1) * sub, :] = (y * jax.nn.sigmoid(y)).astype(BF16)


def _conv_module(glu, w, g, b, *, seq):
    T = glu.shape[0]
    B = T // seq
    tile = min(CONV_TILE, seq)
    nr = seq // tile
    const = lambda *shape: pl.BlockSpec(shape, lambda bb, r: (0,) * len(shape))
    return pl.pallas_call(
        functools.partial(_conv_kernel, tile=tile, sub=64),
        grid=(B, nr),
        in_specs=[pl.BlockSpec((tile, CONV_CH), lambda bb, r: (bb * nr + jnp.maximum(r - 1, 0), 0)),
                  pl.BlockSpec((tile, CONV_CH), lambda bb, r: (bb * nr + r, 0)),
                  const(CONV_WIDTH, CONV_CH), const(1, CONV_CH), const(1, CONV_CH)],
        out_specs=pl.BlockSpec((tile, CONV_CH), lambda bb, r: (bb * nr + r, 0)),
        out_shape=jax.ShapeDtypeStruct((T, CONV_CH), BF16),
        scratch_shapes=[pltpu.VMEM((CONV_HALO + tile, CONV_CH), F32)],
        compiler_params=_cparams(2),
        name="conv_module",
    )(glu, glu, w, g, b)


def _routing_tables(route, counts, n_tiles):
    e = route[:, 0:2].astype(jnp.int32)
    rank = route[:, 2:4].astype(jnp.int32)
    cnt = counts[0, :N_EXPERTS].astype(jnp.int32)
    tiles = (cnt + EXPERT_TILE - 1) // EXPERT_TILE
    tile_end = jnp.cumsum(tiles)
    offset = (tile_end - tiles) * EXPERT_TILE
    onehot = e[:, :, None] == jnp.arange(N_EXPERTS, dtype=jnp.int32)
    pos = (rank + jnp.sum(jnp.where(onehot, offset, 0), axis=-1)).reshape(-1)
    n_used = tile_end[-1]
    tile_id = jnp.minimum(jnp.arange(n_tiles, dtype=jnp.int32), n_used - 1)
    tile_expert = jnp.sum((tile_end[None, :] <= tile_id[:, None]).astype(jnp.int32), axis=1)
    return pos, tile_expert, n_used.reshape(1).astype(jnp.int32)


def _moe_ple(h, ya, yb, wo, layer, norm_ffn, wr, br, wg, wu, wd, p, wp, ple_norm, wgate):
    T = h.shape[0]
    n_tiles = (2 * T) // EXPERT_TILE + N_EXPERTS
    h1, m, route, counts = _outproj_router(h, ya, yb, wo, norm_ffn, wr, br)
    pos, tile_expert, n_used = _routing_tables(route, counts, n_tiles)
    xs = _dispatch(pos, m, n_tiles * EXPERT_TILE)
    ys = _experts(layer, tile_expert, n_used, xs, wg, wu, wd)
    return _combine_ple(layer, pos, route, h1, ys, p, wp, ple_norm, wgate)


def _router_weights(w_coarse, b_coarse, w_fine, b_fine):
    wf = w_fine.transpose(1, 0, 2).reshape(D_MODEL, N_EXPERTS)
    wr = jnp.concatenate([wf, w_coarse, jnp.zeros((D_MODEL, LANES - N_EXPERTS - N_GROUPS), F32)], axis=1)
    br = jnp.concatenate([b_fine.reshape(-1), b_coarse, jnp.zeros((LANES - N_EXPERTS - N_GROUPS,), F32)])
    w_hi = wr.astype(BF16)
    w_lo = (wr - w_hi.astype(F32)).astype(BF16)
    return jnp.concatenate([w_hi, w_lo], axis=1), br.reshape(1, LANES)


def _rope_tables(seq):
    half = ROT_DIM // 2
    inv_freq = ROPE_THETA ** (-jnp.arange(half, dtype=F32) * 2.0 / ROT_DIM)
    ang = jnp.arange(seq, dtype=F32)[:, None] * inv_freq[None, :]
    cos, sin = jnp.cos(ang), jnp.sin(ang)
    zeros = jnp.zeros((seq, HEAD_DIM - ROT_DIM), F32)
    z8 = jnp.zeros((seq, half), F32)
    cos_h = jnp.concatenate([cos, cos, zeros + 1.0], axis=1)
    slo_h = jnp.concatenate([-sin, z8, zeros], axis=1)
    shi_h = jnp.concatenate([z8, sin, zeros], axis=1)
    two = lambda t: jnp.concatenate([t, t], axis=1)
    return two(cos_h), two(slo_h), two(shi_h)


def kernel(x, p, norm_mix, even_w_in, fox_b_f, gmlp_ln_g, gmlp_ln_b, gmlp_w_s, gmlp_b_s, fox_q_norm, fox_k_norm, even_w_out, odd_w_in, swa_q_norm, swa_k_norm, swa_sinks, conv_w, conv_ln_g, conv_ln_b, odd_w_out, norm_ffn, moe_w_coarse, moe_b_coarse, moe_w_fine, moe_b_fine, moe_w_gate, moe_w_up, moe_w_down, ple_w_proj, ple_norm, ple_w_gate):
    B, S, D = x.shape
    T = B * S
    h = x.reshape(T, D)
    p = p.reshape(p.shape[0], T, D_PLE)
    row = lambda v: v.reshape(1, -1)

    def moe_args(i):
        wr, br = _router_weights(moe_w_coarse[i], moe_b_coarse[i], moe_w_fine[i], moe_b_fine[i])
        return (i, row(norm_ffn[i]), wr, br, moe_w_gate, moe_w_up, moe_w_down, p,
                ple_w_proj[i].astype(BF16), row(ple_norm[i]), ple_w_gate[i].astype(BF16))

    n_in = even_w_in.shape[2]
    w_in = jnp.pad(even_w_in[0], ((0, 0), (0, 2 * GMLP_WIDTH + 3 * FOX_WIDTH + LANES - n_in))).astype(BF16)
    b_f = jnp.pad(fox_b_f[0], (0, LANES - FOX_HEADS)).reshape(1, LANES)
    bs_full = jnp.repeat(gmlp_b_s[0].T, HEAD_DIM, axis=1)
    ya, q, k, v, xq, xk = _proj_even(
        h, row(norm_mix[0]), w_in, b_f, row(gmlp_ln_g[0]), row(gmlp_ln_b[0]), gmlp_w_s[0], bs_full,
        row(jnp.tile(fox_q_norm[0], FOX_HEADS)), row(jnp.tile(fox_k_norm[0], FOX_HEADS)), seq=S)
    yb = _fox_attention(q, xq, k, xk, v, seq=S)
    h = _moe_ple(h, ya, yb, even_w_out[0].astype(BF16), *moe_args(0))

    order = jnp.array([0, 4, 1, 5, 2, 6, 3, 7])
    cols = (order[:, None] * HEAD_DIM + jnp.arange(HEAD_DIM)[None, :]).reshape(-1)
    w_odd = jnp.concatenate([odd_w_in[0][:, :SWA_WIDTH][:, cols], odd_w_in[0][:, SWA_WIDTH:]], axis=1).astype(BF16)
    w_out_odd = jnp.concatenate([odd_w_out[0][:SWA_WIDTH][cols], odd_w_out[0][SWA_WIDTH:]], axis=0).astype(BF16)
    cos, slo, shi = _rope_tables(S)
    q, k, v, glu = _proj_odd(h, row(norm_mix[1]), w_odd, row(jnp.tile(swa_q_norm[0], 8)),
                             row(jnp.tile(swa_k_norm[0], 2)), cos, slo, shi, seq=S)
    yc = _swa_attention(swa_sinks[0], q, k, v, seq=S)
    yd = _conv_module(glu, conv_w[0], row(conv_ln_g[0]), row(conv_ln_b[0]), seq=S)
    h = _moe_ple(h, yc, yd, w_out_odd, *moe_args(1))
    return h.reshape(B, S, D)
```

```python
import functools

import jax
import jax.numpy as jnp
import numpy as np
from jax import lax
from jax.experimental import pallas as pl
from jax.experimental.pallas import tpu as pltpu

F32 = jnp.float32
BF16 = jnp.bfloat16
HIGHEST = lax.Precision.HIGHEST

D_MODEL = 1024
HEAD_DIM = 64
LANES = 128
GMLP_WIDTH = 512
CHUNK = 128
FOX_WIDTH = 512
FOX_HEADS = 8
SWA_WIDTH = 512
KV_WIDTH = 128
WINDOW = 128
CONV_CH = 512
CONV_WIDTH = 31
CONV_HALO = 32
ROPE_THETA = 500000.0
ROT_DIM = 16
N_GROUPS = 4
EXPERTS_PER_GROUP = 8
N_EXPERTS = 32
D_EXPERT = 256
D_PLE = 256
EPS = 1e-6
NEG = -1e30
LOG2E = 1.4426950408889634
BIAS_PIECES = 3

EXPERT_TILE = 256
ROW_TILE = 512
GATHER_TILE = 256
DISPATCH_TILE = 1024
DMA_UNROLL = 8
ATTN_TILE = 256
CONV_TILE = 256
VMEM_LIMIT = 56 * 1024 * 1024


def _cparams(n_axes=1, flags=None):
    return pltpu.CompilerParams(dimension_semantics=("arbitrary",) * n_axes,
                                vmem_limit_bytes=VMEM_LIMIT, flags=flags)


def _rms(x, gain):
    return x * lax.rsqrt(jnp.mean(x * x, axis=-1, keepdims=True) + EPS) * gain


def _layernorm(x, g, b):
    mu = jnp.mean(x, axis=-1, keepdims=True)
    xc = x - mu
    var = jnp.mean(xc * xc, axis=-1, keepdims=True)
    return xc * lax.rsqrt(var + EPS) * g + b


def _head_rms(z, gain):
    r = lax.broadcasted_iota(jnp.int32, (LANES, LANES), 0) // HEAD_DIM
    c = lax.broadcasted_iota(jnp.int32, (LANES, LANES), 1) // HEAD_DIM
    bd = jnp.where(r == c, 1.0 / HEAD_DIM, 0.0).astype(F32)
    outs = []
    for j in range(z.shape[1] // LANES):
        zj = z[:, j * LANES:(j + 1) * LANES]
        ms = jnp.dot(zj * zj, bd, precision=HIGHEST, preferred_element_type=F32)
        outs.append(zj * lax.rsqrt(ms + EPS))
    zn = outs[0] if len(outs) == 1 else jnp.concatenate(outs, axis=1)
    return zn * gain


def _lane_lo(shape):
    return (lax.broadcasted_iota(jnp.int32, shape, len(shape) - 1) % LANES) < HEAD_DIM


def _proj_even_kernel(h_ref, g_ref, w_ref, bf_ref, lng_ref, lnb_ref, ws_ref, bs_ref, qg_ref, kg_ref,
                      pq_ref, pk_ref, oneq_ref, onek_ref,
                      ya_ref, q_ref, k_ref, v_ref, xq_ref, xk_ref, carry_ref, *, tm, tiles_per_seq):
    i = pl.program_id(0)
    a = _rms(h_ref[...], g_ref[...]).astype(BF16)

    za = jax.nn.gelu(jnp.dot(a, w_ref[:, 0:2 * GMLP_WIDTH], preferred_element_type=F32))
    u = za[:, :GMLP_WIDTH]
    vln = _layernorm(za[:, GMLP_WIDTH:], lng_ref[...], lnb_ref[...]).astype(BF16)
    rr = lax.broadcasted_iota(jnp.int32, (CHUNK, CHUNK), 0)
    cc = lax.broadcasted_iota(jnp.int32, (CHUNK, CHUNK), 1)
    lo = _lane_lo((CHUNK, LANES))
    for j in range(GMLP_WIDTH // LANES):
        w_a = jnp.where(rr >= cc, ws_ref[2 * j], 0.0).astype(BF16)
        w_b = jnp.where(rr >= cc, ws_ref[2 * j + 1], 0.0).astype(BF16)
        cols = slice(j * LANES, (j + 1) * LANES)
        for c in range(tm // CHUNK):
            rows = slice(c * CHUNK, (c + 1) * CHUNK)
            vp = vln[rows, cols]
            mixed = jnp.where(lo, jnp.dot(w_a, vp, preferred_element_type=F32),
                              jnp.dot(w_b, vp, preferred_element_type=F32)) + bs_ref[:, cols]
            ya_ref[rows, cols] = (u[rows, cols] * mixed).astype(BF16)

    q0 = 2 * GMLP_WIDTH
    zq = jnp.dot(a, w_ref[:, q0:q0 + FOX_WIDTH], preferred_element_type=F32)
    q_ref[...] = (_head_rms(zq, qg_ref[...]) * (LOG2E * HEAD_DIM ** -0.5)).astype(BF16)
    zk = jnp.dot(a, w_ref[:, q0 + FOX_WIDTH:q0 + 2 * FOX_WIDTH], preferred_element_type=F32)
    k_ref[...] = _head_rms(zk, kg_ref[...]).astype(BF16)
    v_ref[...] = jnp.dot(a, w_ref[:, q0 + 2 * FOX_WIDTH:q0 + 3 * FOX_WIDTH],
                         preferred_element_type=F32).astype(BF16)

    zf = jnp.dot(a, w_ref[:, q0 + 3 * FOX_WIDTH:], preferred_element_type=F32) + bf_ref[...]
    ls = jnp.minimum(zf, 0.0) - jnp.log(1.0 + jnp.exp(-jnp.abs(zf)))

    @pl.when(i % tiles_per_seq == 0)
    def _():
        carry_ref[...] = jnp.zeros_like(carry_ref)

    tr = lax.broadcasted_iota(jnp.int32, (tm, tm), 0)
    tc = lax.broadcasted_iota(jnp.int32, (tm, tm), 1)
    tri = jnp.where(tr >= tc, 1.0, 0.0).astype(F32)
    c = jnp.dot(tri, ls, precision=HIGHEST, preferred_element_type=F32) + carry_ref[...]
    carry_ref[...] = c[tm - 1:tm, :]

    c2 = c * LOG2E
    hi = c2.astype(BF16)
    r1 = c2 - hi.astype(F32)
    mid = r1.astype(BF16)
    low = (r1 - mid.astype(F32)).astype(BF16)
    pieces = jnp.concatenate([hi, mid, low], axis=1)
    xq_ref[...] = (jnp.dot(pieces, pq_ref[...], preferred_element_type=F32) + oneq_ref[...]).astype(BF16)
    xk_ref[...] = (jnp.dot(pieces, pk_ref[...], preferred_element_type=F32) + onek_ref[...]).astype(BF16)


def _proj_even(h, g, w, bf, lng, lnb, ws, bs_full, qg, kg, *, seq):
    T = h.shape[0]
    tm = min(ROW_TILE, seq)
    n_in = w.shape[1]
    const = lambda *shape: pl.BlockSpec(shape, lambda i: (0,) * len(shape))
    row = lambda width: pl.BlockSpec((tm, width), lambda i: (i, 0))
    tps = seq // tm
    return pl.pallas_call(
        functools.partial(_proj_even_kernel, tm=tm, tiles_per_seq=tps),
        grid=(T // tm,),
        in_specs=[row(D_MODEL), const(1, D_MODEL), const(D_MODEL, n_in), const(1, LANES),
                  const(1, GMLP_WIDTH), const(1, GMLP_WIDTH), const(8, CHUNK, CHUNK),
                  const(CHUNK, GMLP_WIDTH), const(1, FOX_WIDTH), const(1, FOX_WIDTH),
                  const(BIAS_PIECES * LANES, FOX_WIDTH), const(BIAS_PIECES * LANES, FOX_WIDTH),
                  const(1, FOX_WIDTH), const(1, FOX_WIDTH)],
        out_specs=[row(GMLP_WIDTH)] + [row(FOX_WIDTH)] * 5,
        out_shape=[jax.ShapeDtypeStruct((T, GMLP_WIDTH), BF16)]
                  + [jax.ShapeDtypeStruct((T, FOX_WIDTH), BF16)] * 5,
        scratch_shapes=[pltpu.VMEM((1, LANES), F32)],
        compiler_params=_cparams(1),
        name="proj_even",
    )(h, g, w, bf, lng, lnb, ws, bs_full, qg, kg, *_bias_placement())


def _bias_placement():
    pq = np.zeros((BIAS_PIECES * LANES, FOX_WIDTH), np.float32)
    pk = np.zeros((BIAS_PIECES * LANES, FOX_WIDTH), np.float32)
    oneq = np.zeros((1, FOX_WIDTH), np.float32)
    onek = np.zeros((1, FOX_WIDTH), np.float32)
    for head in range(FOX_HEADS):
        base = (head // 2) * LANES + (HEAD_DIM if head % 2 == 0 else 0)
        for piece in range(BIAS_PIECES):
            pq[piece * LANES + head, base + piece] = 1.0
            onek[0, base + piece] = 1.0
            pk[piece * LANES + head, base + BIAS_PIECES + piece] = -1.0
            oneq[0, base + BIAS_PIECES + piece] = 1.0
    return (jnp.asarray(pq, BF16), jnp.asarray(pk, BF16), jnp.asarray(oneq), jnp.asarray(onek))


def _fox_tile(n_tiles, q_ref, xq_ref, k_ref, xk_ref, v_ref, o_ref, *, tq):
    half = tq // 2
    lo = _lane_lo((1, LANES))
    rr = lax.broadcasted_iota(jnp.int32, (half, tq), 0)
    cc = lax.broadcasted_iota(jnp.int32, (half, tq), 1)
    nt = (((1,), (1,)), ((), ()))
    chains = [(hf, head) for hf in range(2) for head in range(2)]
    q_aug = []
    for hf in range(2):
        rows = slice(hf * half, (hf + 1) * half)
        q, xq = q_ref[rows, :], xq_ref[rows, :]
        q_aug.append((jnp.where(lo, q, xq), jnp.where(lo, xq, q)))

    def score_products(j):
        rows = slice(j * tq, (j + 1) * tq)
        ks, xk = k_ref[rows, :], xk_ref[rows, :]
        k_aug = (jnp.where(lo, ks, xk), jnp.where(lo, xk, ks))
        return [lax.dot_general(q_aug[hf][head], k_aug[head], nt, preferred_element_type=F32)
                for hf, head in chains]

    maxes = [jnp.full((half, 1), NEG, F32) for _ in chains]
    accs = [jnp.zeros((half, LANES), F32) for _ in chains]
    scores = score_products(0)
    for j in range(n_tiles):
        next_scores = score_products(j + 1) if j + 1 < n_tiles else None
        vs = v_ref[j * tq:(j + 1) * tq, :]
        one = jnp.ones_like(vs)
        v_aug = (jnp.where(lo, vs, one), jnp.where(lo, one, vs))
        for c, (hf, head) in enumerate(chains):
            s = scores[c]
            if j == n_tiles - 1:
                s = jnp.where(cc <= rr + hf * half, s, NEG)
            n = jnp.maximum(maxes[c], jnp.max(s, axis=-1, keepdims=True))
            p = jnp.exp2(s - n).astype(BF16)
            accs[c] = accs[c] * jnp.exp2(maxes[c] - n) + jnp.dot(p, v_aug[head], preferred_element_type=F32)
            maxes[c] = n
        scores = next_scores
    for hf in range(2):
        norm = [acc / pltpu.roll(acc, HEAD_DIM, 1) for acc in accs[2 * hf:2 * hf + 2]]
        o_ref[hf * half:(hf + 1) * half, :] = jnp.where(lo, norm[0], norm[1]).astype(BF16)


def _fox_kernel(q_ref, xq_ref, k_ref, xk_ref, v_ref, o_ref, *, tq, nq):
    i = pl.program_id(2)
    for c in range(nq):
        pl.when(i == c)(functools.partial(_fox_tile, c + 1, q_ref, xq_ref, k_ref, xk_ref, v_ref, o_ref, tq=tq))


def _fox_attention(q, xq, k, xk, v, *, seq):
    T = q.shape[0]
    B = T // seq
    tq = min(ATTN_TILE, seq)
    nq = seq // tq
    tile = pl.BlockSpec((tq, LANES), lambda b, hp, i: (b * nq + i, hp))
    whole = pl.BlockSpec((seq, LANES), lambda b, hp, i: (b, hp))
    return pl.pallas_call(
        functools.partial(_fox_kernel, tq=tq, nq=nq),
        grid=(B, FOX_WIDTH // LANES, nq),
        in_specs=[tile, tile, whole, whole, whole],
        out_specs=tile,
        out_shape=jax.ShapeDtypeStruct((T, FOX_WIDTH), BF16),
        compiler_params=_cparams(3),
        name="fox_attention",
    )(q, xq, k, xk, v)


def _outproj_router_kernel(h_ref, ya_ref, yb_ref, wo_ref, g_ref, wr_ref, br_ref,
                           h1_ref, m_ref, route_ref, cnt_ref, carry_ref, *, tm):
    i = pl.program_id(0)
    half = wo_ref.shape[0] // 2
    mix = (jnp.dot(ya_ref[...], wo_ref[0:half, :], preferred_element_type=F32)
           + jnp.dot(yb_ref[...], wo_ref[half:, :], preferred_element_type=F32))
    h1 = h_ref[...] + mix
    h1_ref[...] = h1
    m = _rms(h1, g_ref[...])
    m_hi = m.astype(BF16)
    m_ref[...] = m

    m_lo = (m - m_hi.astype(F32)).astype(BF16)
    hh = jnp.dot(m_hi, wr_ref[...], preferred_element_type=F32)
    lh = jnp.dot(m_lo, wr_ref[:, :LANES], preferred_element_type=F32)
    logits = hh[:, :LANES] + (hh[:, LANES:] + lh) + br_ref[...]
    lane_i = lax.broadcasted_iota(jnp.int32, (tm, LANES), 1)
    lane = lane_i.astype(F32)
    group_of_lane = (lane_i // EXPERTS_PER_GROUP).astype(F32)
    is_coarse = (lane_i >= N_EXPERTS) & (lane_i < N_EXPERTS + N_GROUPS)
    coarse = jnp.where(is_coarse, logits, NEG)
    cmax = jnp.max(coarse, axis=-1, keepdims=True)
    gidx = jnp.min(jnp.where(coarse == cmax, lane - N_EXPERTS, float(LANES)), axis=-1, keepdims=True)
    p_g = 1.0 / jnp.sum(jnp.where(is_coarse, jnp.exp(coarse - cmax), 0.0), axis=-1, keepdims=True)
    in_group = (lane_i < N_EXPERTS) & (group_of_lane == gidx)
    fine = jnp.where(in_group, logits, NEG)
    v1 = jnp.max(fine, axis=-1, keepdims=True)
    i1 = jnp.min(jnp.where(fine == v1, lane, float(LANES)), axis=-1, keepdims=True)
    fine2 = jnp.where(lane == i1, NEG, fine)
    v2 = jnp.max(fine2, axis=-1, keepdims=True)
    i2 = jnp.min(jnp.where(fine2 == v2, lane, float(LANES)), axis=-1, keepdims=True)
    e2 = jnp.exp(v2 - v1)
    w1 = p_g / (1.0 + e2)
    w2 = p_g * e2 / (1.0 + e2)

    @pl.when(i == 0)
    def _():
        carry_ref[...] = jnp.zeros_like(carry_ref)

    hit1 = lane == i1
    hit2 = lane == i2
    onehot = jnp.where(hit1 | hit2, 1.0, 0.0).astype(F32)
    tr = lax.broadcasted_iota(jnp.int32, (tm, tm), 0)
    tc = lax.broadcasted_iota(jnp.int32, (tm, tm), 1)
    strict = jnp.where(tr > tc, 1.0, 0.0).astype(BF16)
    before = jnp.dot(strict, onehot.astype(BF16), preferred_element_type=F32) + carry_ref[...]
    r1 = jnp.sum(jnp.where(hit1, before, 0.0), axis=-1, keepdims=True)
    r2 = jnp.sum(jnp.where(hit2, before, 0.0), axis=-1, keepdims=True)
    total = carry_ref[...] + jnp.sum(onehot, axis=0, keepdims=True)
    carry_ref[...] = total
    cnt_ref[...] = jnp.broadcast_to(total, cnt_ref.shape)

    route = jnp.where(lane == 0, i1, 0.0)
    route = jnp.where(lane == 1, i2, route)
    route = jnp.where(lane == 2, r1, route)
    route = jnp.where(lane == 3, r2, route)
    route = jnp.where(lane == 4, w1, route)
    route = jnp.where(lane == 5, w2, route)
    route_ref[...] = route


def _outproj_router(h, ya, yb, wo, g, wr, br):
    T = h.shape[0]
    tm = min(ROW_TILE, T)
    const = lambda *shape: pl.BlockSpec(shape, lambda i: (0,) * len(shape))
    row = lambda width: pl.BlockSpec((tm, width), lambda i: (i, 0))
    return pl.pallas_call(
        functools.partial(_outproj_router_kernel, tm=tm),
        grid=(T // tm,),
        in_specs=[row(D_MODEL), row(ya.shape[1]), row(yb.shape[1]), const(*wo.shape),
                  const(1, D_MODEL), const(D_MODEL, 2 * LANES), const(1, LANES)],
        out_specs=[row(D_MODEL), row(D_MODEL), row(LANES), const(8, LANES)],
        out_shape=[jax.ShapeDtypeStruct((T, D_MODEL), F32),
                   jax.ShapeDtypeStruct((T, D_MODEL), F32),
                   jax.ShapeDtypeStruct((T, LANES), F32),
                   jax.ShapeDtypeStruct((8, LANES), F32)],
        scratch_shapes=[pltpu.VMEM((1, LANES), F32)],
        compiler_params=_cparams(1),
        name="outproj_router",
    )(h, ya, yb, wo, g, wr, br)


def _dispatch_kernel(pos_ref, m_ref, xs_in_hbm, xs_hbm, sem, *, tile):
    del xs_in_hbm

    def row_copy(t, k):
        return pltpu.make_async_copy(m_ref.at[pl.ds(t, 1)],
                                     xs_hbm.at[pl.ds(pos_ref[0, 0, 2 * t + k], 1)], sem)

    def issue(c, carry):
        for u in range(DMA_UNROLL):
            row_copy(c * DMA_UNROLL + u, 0).start(priority=0)
            row_copy(c * DMA_UNROLL + u, 1).start(priority=1)
        return carry

    lax.fori_loop(0, tile // DMA_UNROLL, issue, 0)
    for _ in range(2):
        pltpu.make_async_copy(m_ref, xs_hbm.at[pl.ds(0, tile)], sem).wait()


def _dispatch(pos, m, n_rows):
    T, width = m.shape
    tile = min(DISPATCH_TILE, T)
    pos3 = pos.reshape(T // tile, 1, 2 * tile)
    xs0 = jnp.zeros((n_rows, width), m.dtype)
    return pl.pallas_call(
        functools.partial(_dispatch_kernel, tile=tile),
        grid=(T // tile,),
        in_specs=[pl.BlockSpec((1, 1, 2 * tile), lambda i: (i, 0, 0), memory_space=pltpu.SMEM),
                  pl.BlockSpec((tile, width), lambda i: (i, 0)),
                  pl.BlockSpec(memory_space=pl.ANY)],
        out_specs=pl.BlockSpec(memory_space=pl.ANY),
        out_shape=jax.ShapeDtypeStruct((n_rows, width), m.dtype),
        scratch_shapes=[pltpu.SemaphoreType.DMA(())],
        input_output_aliases={2: 0},
        compiler_params=pltpu.CompilerParams(dimension_semantics=("arbitrary",),
                                             has_side_effects=True),
        name="moe_dispatch",
    )(pos3, m, xs0)


def _experts_kernel(te_ref, nused_ref, xs_ref, wg_ref, wu_ref, wd_ref, ys_ref, wgu_b, wd_b):
    j = pl.program_id(0)
    prev = te_ref[jnp.maximum(j - 1, 0)]

    @pl.when((j == 0) | (te_ref[j] != prev))
    def _():
        wgu_b[:, :D_EXPERT] = wg_ref[...].astype(BF16)
        wgu_b[:, D_EXPERT:] = wu_ref[...].astype(BF16)
        wd_b[...] = wd_ref[...].astype(BF16)

    @pl.when(j < nused_ref[0])
    def _():
        gu = jnp.dot(xs_ref[...].astype(BF16), wgu_b[...], preferred_element_type=F32)
        g = gu[:, :D_EXPERT]
        act = g * jax.nn.sigmoid(g) * gu[:, D_EXPERT:]
        ys_ref[...] = jnp.dot(act.astype(BF16), wd_b[...], preferred_element_type=F32)

    @pl.when(j >= nused_ref[0])
    def _():
        ys_ref[...] = jnp.zeros_like(ys_ref)


def _experts(layer, tile_expert, n_used, xs, wg, wu, wd):
    n_rows = xs.shape[0]
    nt = n_rows // EXPERT_TILE
    grid_spec = pltpu.PrefetchScalarGridSpec(
        num_scalar_prefetch=2,
        grid=(nt,),
        in_specs=[pl.BlockSpec((EXPERT_TILE, D_MODEL), lambda j, te, nu: (j, 0)),
                  pl.BlockSpec((None, None, D_MODEL, D_EXPERT), lambda j, te, nu: (layer, te[j], 0, 0)),
                  pl.BlockSpec((None, None, D_MODEL, D_EXPERT), lambda j, te, nu: (layer, te[j], 0, 0)),
                  pl.BlockSpec((None, None, D_EXPERT, D_MODEL), lambda j, te, nu: (layer, te[j], 0, 0))],
        out_specs=pl.BlockSpec((EXPERT_TILE, D_MODEL), lambda j, te, nu: (j, 0)),
        scratch_shapes=[pltpu.VMEM((D_MODEL, 2 * D_EXPERT), BF16),
                        pltpu.VMEM((D_EXPERT, D_MODEL), BF16)],
    )
    return pl.pallas_call(
        _experts_kernel,
        grid_spec=grid_spec,
        out_shape=jax.ShapeDtypeStruct((n_rows, D_MODEL), F32),
        compiler_params=_cparams(1),
        name="moe_experts",
    )(tile_expert, n_used, xs, wg, wu, wd)


def _combine_ple_kernel(pos_ref, next_pos_ref, route_ref, h1_ref, ys_hbm, p_ref, wp_ref, g_ref, wgate_ref,
                        o_ref, ybuf, sems, *, tile):
    i = pl.program_id(0)
    n = pl.num_programs(0)
    slot = i % 2

    def gather(table, s):
        def issue(c, carry):
            for u in range(DMA_UNROLL):
                t = c * DMA_UNROLL + u
                for k in range(2):
                    pltpu.make_async_copy(ys_hbm.at[pl.ds(table[0, 0, 2 * t + k], 1)],
                                          ybuf.at[s, k, pl.ds(t, 1)], sems.at[s]).start(priority=k)
            return carry

        lax.fori_loop(0, tile // DMA_UNROLL, issue, 0)

    @pl.when(i == 0)
    def _():
        gather(pos_ref, 0)

    @pl.when(i + 1 < n)
    def _():
        gather(next_pos_ref, 1 - slot)

    ple = jnp.dot(p_ref[...].astype(BF16), wp_ref[...], preferred_element_type=F32)
    for k in range(2):
        pltpu.make_async_copy(ys_hbm.at[pl.ds(0, tile)], ybuf.at[slot, k], sems.at[slot]).wait()

    route = route_ref[...]
    h2 = h1_ref[...] + route[:, 4:5] * ybuf[slot, 0] + route[:, 5:6] * ybuf[slot, 1]
    gate = jax.nn.sigmoid(jnp.dot(_rms(h2, g_ref[...]).astype(BF16), wgate_ref[...],
                                  preferred_element_type=F32))
    o_ref[...] = h2 + gate * ple


def _combine_ple(layer, pos, route, h1, ys, p, wp, g, wgate):
    T = h1.shape[0]
    tile = min(GATHER_TILE, T)
    n = T // tile
    pos3 = pos.reshape(n, 1, 2 * tile)
    const = lambda *shape: pl.BlockSpec(shape, lambda i: (0,) * len(shape))
    row = lambda width: pl.BlockSpec((tile, width), lambda i: (i, 0))
    return pl.pallas_call(
        functools.partial(_combine_ple_kernel, tile=tile),
        grid=(n,),
        in_specs=[pl.BlockSpec((1, 1, 2 * tile), lambda i: (i, 0, 0), memory_space=pltpu.SMEM),
                  pl.BlockSpec((1, 1, 2 * tile), lambda i: (jnp.minimum(i + 1, n - 1), 0, 0),
                               memory_space=pltpu.SMEM),
                  row(LANES), row(D_MODEL), pl.BlockSpec(memory_space=pl.ANY),
                  pl.BlockSpec((None, tile, D_PLE), lambda i: (layer, i, 0)),
                  const(D_PLE, D_MODEL), const(1, D_MODEL), const(D_MODEL, D_MODEL)],
        out_specs=row(D_MODEL),
        out_shape=jax.ShapeDtypeStruct((T, D_MODEL), F32),
        scratch_shapes=[pltpu.VMEM((2, 2, tile, D_MODEL), F32), pltpu.SemaphoreType.DMA((2,))],
        compiler_params=_cparams(1),
        name="combine_ple",
    )(pos3, pos3, route, h1, ys, p, wp, g, wgate)


def _rope(z, cos, sin_lo, sin_hi):
    half = ROT_DIM // 2
    outs = []
    for j in range(z.shape[1] // LANES):
        zj = z[:, j * LANES:(j + 1) * LANES]
        outs.append(zj * cos + pltpu.roll(zj, LANES - half, 1) * sin_lo + pltpu.roll(zj, half, 1) * sin_hi)
    return outs[0] if len(outs) == 1 else jnp.concatenate(outs, axis=1)


def _proj_odd_kernel(h_ref, g_ref, w_ref, qg_ref, kg_ref, cos_ref, slo_ref, shi_ref,
                     q_ref, k_ref, v_ref, glu_ref):
    a = _rms(h_ref[...], g_ref[...]).astype(BF16)
    cos, slo, shi = cos_ref[...], slo_ref[...], shi_ref[...]
    zq = jnp.dot(a, w_ref[:, 0:SWA_WIDTH], preferred_element_type=F32)
    q_ref[...] = (_rope(_head_rms(zq, qg_ref[...]), cos, slo, shi) * (HEAD_DIM ** -0.5)).astype(BF16)
    zk = jnp.dot(a, w_ref[:, SWA_WIDTH:SWA_WIDTH + KV_WIDTH], preferred_element_type=F32)
    k_ref[...] = _rope(_head_rms(zk, kg_ref[...]), cos, slo, shi).astype(BF16)
    v0 = SWA_WIDTH + KV_WIDTH
    v_ref[...] = jnp.dot(a, w_ref[:, v0:v0 + KV_WIDTH], preferred_element_type=F32).astype(BF16)
    d0 = v0 + KV_WIDTH
    zd = jnp.dot(a, w_ref[:, d0:d0 + 2 * CONV_CH], preferred_element_type=F32)
    glu_ref[...] = zd[:, :CONV_CH] * jax.nn.sigmoid(zd[:, CONV_CH:])


def _proj_odd(h, g, w, qg, kg, cos, slo, shi, *, seq):
    T = h.shape[0]
    tm = min(ROW_TILE, seq)
    tps = seq // tm
    const = lambda *shape: pl.BlockSpec(shape, lambda i: (0,) * len(shape))
    row = lambda width: pl.BlockSpec((tm, width), lambda i: (i, 0))
    tab = pl.BlockSpec((tm, LANES), lambda i: (i % tps, 0))
    return pl.pallas_call(
        _proj_odd_kernel,
        grid=(T // tm,),
        in_specs=[row(D_MODEL), const(1, D_MODEL), const(*w.shape), const(1, SWA_WIDTH),
                  const(1, KV_WIDTH), tab, tab, tab],
        out_specs=[row(SWA_WIDTH), row(KV_WIDTH), row(KV_WIDTH), row(CONV_CH)],
        out_shape=[jax.ShapeDtypeStruct((T, SWA_WIDTH), BF16),
                   jax.ShapeDtypeStruct((T, KV_WIDTH), BF16),
                   jax.ShapeDtypeStruct((T, KV_WIDTH), BF16),
                   jax.ShapeDtypeStruct((T, CONV_CH), F32)],
        compiler_params=_cparams(1),
        name="proj_odd",
    )(h, g, w, qg, kg, cos, slo, shi)


def _swa_kernel(sink_ref, q_ref, k_ref, v_ref, o_ref, *, seq):
    lo = _lane_lo((1, LANES))
    nt = (((1,), (1,)), ((), ()))
    W = WINDOW
    qi = lax.broadcasted_iota(jnp.int32, (W, 2 * W), 0)
    kj = lax.broadcasted_iota(jnp.int32, (W, 2 * W), 1)
    band = (kj > qi) & (kj <= qi + W)

    def block(n, kstart, mask):
        qrow = pl.ds(pl.multiple_of(n * W, W), W)
        kwin = pl.ds(pl.multiple_of(kstart, W), 2 * W)
        ks = k_ref[kwin, :]
        vs = v_ref[kwin, :]
        for j in range(SWA_WIDTH // LANES):
            cols = slice(j * LANES, (j + 1) * LANES)
            q = q_ref[qrow, cols]
            zero = jnp.zeros_like(q)
            outs = []
            for head, qh in ((j, jnp.where(lo, q, zero)), (4 + j, jnp.where(lo, zero, q))):
                s = jnp.where(mask, lax.dot_general(qh, ks, nt, preferred_element_type=F32), NEG)
                sink = sink_ref[head]
                m = jnp.maximum(jnp.max(s, axis=-1, keepdims=True), sink)
                p = jnp.exp(s - m)
                l = jnp.sum(p, axis=-1, keepdims=True) + jnp.exp(sink - m)
                outs.append(jnp.dot(p.astype(BF16), vs, preferred_element_type=F32) / l)
            o_ref[qrow, cols] = jnp.where(lo, outs[0], outs[1]).astype(BF16)

    block(0, 0, kj <= qi)

    def body(n, carry):
        block(n, (n - 1) * W, band)
        return carry

    lax.fori_loop(1, seq // W, body, 0)


def _swa_attention(sinks, q, k, v, *, seq):
    T = q.shape[0]
    B = T // seq
    return pl.pallas_call(
        functools.partial(_swa_kernel, seq=seq),
        grid=(B,),
        in_specs=[pl.BlockSpec(memory_space=pltpu.SMEM),
                  pl.BlockSpec((seq, SWA_WIDTH), lambda b: (b, 0)),
                  pl.BlockSpec((seq, KV_WIDTH), lambda b: (b, 0)),
                  pl.BlockSpec((seq, KV_WIDTH), lambda b: (b, 0))],
        out_specs=pl.BlockSpec((seq, SWA_WIDTH), lambda b: (b, 0)),
        out_shape=jax.ShapeDtypeStruct((T, SWA_WIDTH), BF16),
        compiler_params=_cparams(1),
        name="swa_attention",
    )(sinks, q, k, v)


def _conv_kernel(prev_ref, cur_ref, w_ref, g_ref, b_ref, o_ref, pad_ref, *, tile, sub):
    r = pl.program_id(1)
    tail = prev_ref[tile - CONV_HALO:, :]
    pad_ref[0:CONV_HALO, :] = jnp.where(r > 0, tail, jnp.zeros_like(tail))
    pad_ref[CONV_HALO:, :] = cur_ref[...]
    w = w_ref[...]
    first = CONV_HALO - (CONV_WIDTH - 1)
    for s in range(tile // sub):
        acc = jnp.zeros((sub, CONV_CH), F32)
        for j in range(CONV_WIDTH):
            start = s * sub + first + j
            acc = acc + pad_ref[start:start + sub, :] * w[j:j + 1, :]
        y = _layernorm(acc, g_ref[...], b_ref[...])
        o_ref[s * sub:(s + 1) * sub, :] = (y * jax.nn.sigmoid(y)).astype(BF16)


def _conv_module(glu, w, g, b, *, seq):
    T = glu.shape[0]
    B = T // seq
    tile = min(CONV_TILE, seq)
    nr = seq // tile
    const = lambda *shape: pl.BlockSpec(shape, lambda bb, r: (0,) * len(shape))
    return pl.pallas_call(
        functools.partial(_conv_kernel, tile=tile, sub=64),
        grid=(B, nr),
        in_specs=[pl.BlockSpec((tile, CONV_CH), lambda bb, r: (bb * nr + jnp.maximum(r - 1, 0), 0)),
                  pl.BlockSpec((tile, CONV_CH), lambda bb, r: (bb * nr + r, 0)),
                  const(CONV_WIDTH, CONV_CH), const(1, CONV_CH), const(1, CONV_CH)],
        out_specs=pl.BlockSpec((tile, CONV_CH), lambda bb, r: (bb * nr + r, 0)),
        out_shape=jax.ShapeDtypeStruct((T, CONV_CH), BF16),
        scratch_shapes=[pltpu.VMEM((CONV_HALO + tile, CONV_CH), F32)],
        compiler_params=_cparams(2),
        name="conv_module",
    )(glu, glu, w, g, b)


def _routing_tables(route, counts, n_tiles):
    e = route[:, 0:2].astype(jnp.int32)
    rank = route[:, 2:4].astype(jnp.int32)
    cnt = counts[0, :N_EXPERTS].astype(jnp.int32)
    tiles = (cnt + EXPERT_TILE - 1) // EXPERT_TILE
    tile_end = jnp.cumsum(tiles)
    offset = (tile_end - tiles) * EXPERT_TILE
    onehot = e[:, :, None] == jnp.arange(N_EXPERTS, dtype=jnp.int32)
    pos = (rank + jnp.sum(jnp.where(onehot, offset, 0), axis=-1)).reshape(-1)
    n_used = tile_end[-1]
    tile_id = jnp.minimum(jnp.arange(n_tiles, dtype=jnp.int32), n_used - 1)
    tile_expert = jnp.sum((tile_end[None, :] <= tile_id[:, None]).astype(jnp.int32), axis=1)
    return pos, tile_expert, n_used.reshape(1).astype(jnp.int32)


def _moe_ple(h, ya, yb, wo, layer, norm_ffn, wr, br, wg, wu, wd, p, wp, ple_norm, wgate):
    T = h.shape[0]
    n_tiles = (2 * T) // EXPERT_TILE + N_EXPERTS
    h1, m, route, counts = _outproj_router(h, ya, yb, wo, norm_ffn, wr, br)
    pos, tile_expert, n_used = _routing_tables(route, counts, n_tiles)
    xs = _dispatch(pos, m, n_tiles * EXPERT_TILE)
    ys = _experts(layer, tile_expert, n_used, xs, wg, wu, wd)
    return _combine_ple(layer, pos, route, h1, ys, p, wp, ple_norm, wgate)


def _router_weights(w_coarse, b_coarse, w_fine, b_fine):
    wf = w_fine.transpose(1, 0, 2).reshape(D_MODEL, N_EXPERTS)
    wr = jnp.concatenate([wf, w_coarse, jnp.zeros((D_MODEL, LANES - N_EXPERTS - N_GROUPS), F32)], axis=1)
    br = jnp.concatenate([b_fine.reshape(-1), b_coarse, jnp.zeros((LANES - N_EXPERTS - N_GROUPS,), F32)])
    w_hi = wr.astype(BF16)
    w_lo = (wr - w_hi.astype(F32)).astype(BF16)
    return jnp.concatenate([w_hi, w_lo], axis=1), br.reshape(1, LANES)


def _rope_tables(seq):
    half = ROT_DIM // 2
    inv_freq = ROPE_THETA ** (-jnp.arange(half, dtype=F32) * 2.0 / ROT_DIM)
    ang = jnp.arange(seq, dtype=F32)[:, None] * inv_freq[None, :]
    cos, sin = jnp.cos(ang), jnp.sin(ang)
    zeros = jnp.zeros((seq, HEAD_DIM - ROT_DIM), F32)
    z8 = jnp.zeros((seq, half), F32)
    cos_h = jnp.concatenate([cos, cos, zeros + 1.0], axis=1)
    slo_h = jnp.concatenate([-sin, z8, zeros], axis=1)
    shi_h = jnp.concatenate([z8, sin, zeros], axis=1)
    two = lambda t: jnp.concatenate([t, t], axis=1)
    return two(cos_h), two(slo_h), two(shi_h)


def kernel(x, p, norm_mix, even_w_in, fox_b_f, gmlp_ln_g, gmlp_ln_b, gmlp_w_s, gmlp_b_s, fox_q_norm, fox_k_norm, even_w_out, odd_w_in, swa_q_norm, swa_k_norm, swa_sinks, conv_w, conv_ln_g, conv_ln_b, odd_w_out, norm_ffn, moe_w_coarse, moe_b_coarse, moe_w_fine, moe_b_fine, moe_w_gate, moe_w_up, moe_w_down, ple_w_proj, ple_norm, ple_w_gate):
    B, S, D = x.shape
    T = B * S
    h = x.reshape(T, D)
    p = p.reshape(p.shape[0], T, D_PLE)
    row = lambda v: v.reshape(1, -1)

    def moe_args(i):
        wr, br = _router_weights(moe_w_coarse[i], moe_b_coarse[i], moe_w_fine[i], moe_b_fine[i])
        return (i, row(norm_ffn[i]), wr, br, moe_w_gate, moe_w_up, moe_w_down, p,
                ple_w_proj[i].astype(BF16), row(ple_norm[i]), ple_w_gate[i].astype(BF16))

    n_in = even_w_in.shape[2]
    w_in = jnp.pad(even_w_in[0], ((0, 0), (0, 2 * GMLP_WIDTH + 3 * FOX_WIDTH + LANES - n_in))).astype(BF16)
    b_f = jnp.pad(fox_b_f[0], (0, LANES - FOX_HEADS)).reshape(1, LANES)
    bs_full = jnp.repeat(gmlp_b_s[0].T, HEAD_DIM, axis=1)
    ya, q, k, v, xq, xk = _proj_even(
        h, row(norm_mix[0]), w_in, b_f, row(gmlp_ln_g[0]), row(gmlp_ln_b[0]), gmlp_w_s[0], bs_full,
        row(jnp.tile(fox_q_norm[0], FOX_HEADS)), row(jnp.tile(fox_k_norm[0], FOX_HEADS)), seq=S)
    yb = _fox_attention(q, xq, k, xk, v, seq=S)
    h = _moe_ple(h, ya, yb, even_w_out[0].astype(BF16), *moe_args(0))

    order = jnp.array([0, 4, 1, 5, 2, 6, 3, 7])
    cols = (order[:, None] * HEAD_DIM + jnp.arange(HEAD_DIM)[None, :]).reshape(-1)
    w_odd = jnp.concatenate([odd_w_in[0][:, :SWA_WIDTH][:, cols], odd_w_in[0][:, SWA_WIDTH:]], axis=1).astype(BF16)
    w_out_odd = jnp.concatenate([odd_w_out[0][:SWA_WIDTH][cols], odd_w_out[0][SWA_WIDTH:]], axis=0).astype(BF16)
    cos, slo, shi = _rope_tables(S)
    q, k, v, glu = _proj_odd(h, row(norm_mix[1]), w_odd, row(jnp.tile(swa_q_norm[0], 8)),
                             row(jnp.tile(swa_k_norm[0], 2)), cos, slo, shi, seq=S)
    yc = _swa_attention(swa_sinks[0], q, k, v, seq=S)
    yd = _conv_module(glu, conv_w[0], row(conv_ln_g[0]), row(conv_ln_b[0]), seq=S)
    h = _moe_ple(h, yc, yd, w_out_odd, *moe_args(1))
    return h.reshape(B, S, D)
```

```python
import functools

import jax
import jax.numpy as jnp
import numpy as np
from jax import lax
from jax.experimental import pallas as pl
from jax.experimental.pallas import tpu as pltpu

F32 = jnp.float32
BF16 = jnp.bfloat16
HIGHEST = lax.Precision.HIGHEST

D_MODEL = 1024
HEAD_DIM = 64
LANES = 128
GMLP_WIDTH = 512
CHUNK = 128
FOX_WIDTH = 512
FOX_HEADS = 8
SWA_WIDTH = 512
KV_WIDTH = 128
WINDOW = 128
CONV_CH = 512
CONV_WIDTH = 31
CONV_HALO = 32
ROPE_THETA = 500000.0
ROT_DIM = 16
N_GROUPS = 4
EXPERTS_PER_GROUP = 8
N_EXPERTS = 32
D_EXPERT = 256
D_PLE = 256
EPS = 1e-6
NEG = -1e30
LOG2E = 1.4426950408889634
BIAS_PIECES = 3

EXPERT_TILE = 256
PAD_BITS = 8
SUBLANES = 8
SUBLANE_BITS = 3
ROW_TILE = 512
GATHER_TILE = 256
DISPATCH_TILE = 1024
DMA_UNROLL = 8
ATTN_TILE = 256
CONV_TILE = 256
VMEM_LIMIT = 56 * 1024 * 1024


def _cparams(n_axes=1, flags=None):
    return pltpu.CompilerParams(dimension_semantics=("arbitrary",) * n_axes,
                                vmem_limit_bytes=VMEM_LIMIT, flags=flags)


def _rms(x, gain):
    return x * lax.rsqrt(jnp.mean(x * x, axis=-1, keepdims=True) + EPS) * gain


def _layernorm(x, g, b):
    mu = jnp.mean(x, axis=-1, keepdims=True)
    xc = x - mu
    var = jnp.mean(xc * xc, axis=-1, keepdims=True)
    return xc * lax.rsqrt(var + EPS) * g + b


def _head_rms(z, gain):
    r = (lax.broadcasted_iota(jnp.int32, (2 * LANES, LANES), 0) % LANES) // HEAD_DIM
    c = lax.broadcasted_iota(jnp.int32, (2 * LANES, LANES), 1) // HEAD_DIM
    bd = jnp.where(r == c, 1.0 / HEAD_DIM, 0.0).astype(BF16)
    outs = []
    for j in range(z.shape[1] // LANES):
        zj = z[:, j * LANES:(j + 1) * LANES]
        sq = zj * zj
        sq_hi = sq.astype(BF16)
        sq_lo = (sq - sq_hi.astype(F32)).astype(BF16)
        ms = jnp.dot(jnp.concatenate([sq_hi, sq_lo], axis=1), bd, preferred_element_type=F32)
        outs.append(zj * lax.rsqrt(ms + EPS))
    zn = outs[0] if len(outs) == 1 else jnp.concatenate(outs, axis=1)
    return zn * gain


def _lane_lo(shape):
    return (lax.broadcasted_iota(jnp.int32, shape, len(shape) - 1) % LANES) < HEAD_DIM


def _proj_even_kernel(h_ref, g_ref, w_ref, bf_ref, lng_ref, lnb_ref, ws_ref, bs_ref, qg_ref, kg_ref,
                      pq_ref, pk_ref, oneq_ref, onek_ref,
                      ya_ref, q_ref, k_ref, v_ref, xq_ref, xk_ref, carry_ref, *, tm, tiles_per_seq):
    i = pl.program_id(0)
    a = _rms(h_ref[...], g_ref[...]).astype(BF16)

    q0 = 2 * GMLP_WIDTH
    za = jnp.dot(a, w_ref[:, 0:q0], preferred_element_type=F32)
    zq = jnp.dot(a, w_ref[:, q0:q0 + FOX_WIDTH], preferred_element_type=F32)
    zk = jnp.dot(a, w_ref[:, q0 + FOX_WIDTH:q0 + 2 * FOX_WIDTH], preferred_element_type=F32)
    zv = jnp.dot(a, w_ref[:, q0 + 2 * FOX_WIDTH:q0 + 3 * FOX_WIDTH], preferred_element_type=F32)
    zf = jnp.dot(a, w_ref[:, q0 + 3 * FOX_WIDTH:], preferred_element_type=F32) + bf_ref[...]

    q_ref[...] = (_head_rms(zq, qg_ref[...]) * (LOG2E * HEAD_DIM ** -0.5)).astype(BF16)
    k_ref[...] = _head_rms(zk, kg_ref[...]).astype(BF16)
    v_ref[...] = zv.astype(BF16)

    rr = lax.broadcasted_iota(jnp.int32, (CHUNK, CHUNK), 0)
    cc = lax.broadcasted_iota(jnp.int32, (CHUNK, CHUNK), 1)

    ls = jnp.minimum(zf, 0.0) - jnp.log(1.0 + jnp.exp(-jnp.abs(zf)))

    @pl.when(i % tiles_per_seq == 0)
    def _():
        carry_ref[...] = jnp.zeros_like(carry_ref)

    def split3(x):
        hi = x.astype(BF16)
        r1 = x - hi.astype(F32)
        mid = r1.astype(BF16)
        return hi, mid, (r1 - mid.astype(F32)).astype(BF16)

    tri = jnp.where(rr >= cc, 1.0, 0.0).astype(BF16)
    pieces = jnp.concatenate(split3(ls), axis=1)
    running = carry_ref[...]
    blocks = []
    for b in range(tm // CHUNK):
        d = jnp.dot(tri, pieces[b * CHUNK:(b + 1) * CHUNK, :], preferred_element_type=F32)
        blk = (d[:, :LANES] + d[:, LANES:2 * LANES]) + d[:, 2 * LANES:] + running
        running = blk[CHUNK - 1:CHUNK, :]
        blocks.append(blk)
    carry_ref[...] = running
    c = jnp.concatenate(blocks, axis=0)

    za = jax.nn.gelu(za)
    u = za[:, :GMLP_WIDTH]
    vln = _layernorm(za[:, GMLP_WIDTH:], lng_ref[...], lnb_ref[...]).astype(BF16)
    lo = _lane_lo((CHUNK, LANES))
    for j in range(GMLP_WIDTH // LANES):
        w_a = jnp.where(rr >= cc, ws_ref[2 * j], 0.0).astype(BF16)
        w_b = jnp.where(rr >= cc, ws_ref[2 * j + 1], 0.0).astype(BF16)
        cols = slice(j * LANES, (j + 1) * LANES)
        for blk in range(tm // CHUNK):
            rows = slice(blk * CHUNK, (blk + 1) * CHUNK)
            vp = vln[rows, cols]
            mixed = jnp.where(lo, jnp.dot(w_a, vp, preferred_element_type=F32),
                              jnp.dot(w_b, vp, preferred_element_type=F32)) + bs_ref[:, cols]
            ya_ref[rows, cols] = (u[rows, cols] * mixed).astype(BF16)

    hi, mid, low = split3(c * LOG2E)
    group = lax.broadcasted_iota(jnp.int32, (tm, LANES), 1) // FOX_HEADS
    sel = jnp.where(group == 0, hi, jnp.where(group == 1, mid, low))
    xq_ref[...] = (jnp.dot(sel, pq_ref[...], preferred_element_type=F32) + oneq_ref[...]).astype(BF16)
    xk_ref[...] = (jnp.dot(sel, pk_ref[...], preferred_element_type=F32) + onek_ref[...]).astype(BF16)


def _proj_even(h, g, w, bf, lng, lnb, ws, bs_full, qg, kg, *, seq):
    T = h.shape[0]
    tm = min(ROW_TILE, seq)
    n_in = w.shape[1]
    const = lambda *shape: pl.BlockSpec(shape, lambda i: (0,) * len(shape))
    row = lambda width: pl.BlockSpec((tm, width), lambda i: (i, 0))
    tps = seq // tm
    return pl.pallas_call(
        functools.partial(_proj_even_kernel, tm=tm, tiles_per_seq=tps),
        grid=(T // tm,),
        in_specs=[row(D_MODEL), const(1, D_MODEL), const(D_MODEL, n_in), const(1, LANES),
                  const(1, GMLP_WIDTH), const(1, GMLP_WIDTH), const(8, CHUNK, CHUNK),
                  const(CHUNK, GMLP_WIDTH), const(1, FOX_WIDTH), const(1, FOX_WIDTH),
                  const(LANES, FOX_WIDTH), const(LANES, FOX_WIDTH),
                  const(1, FOX_WIDTH), const(1, FOX_WIDTH)],
        out_specs=[row(GMLP_WIDTH)] + [row(FOX_WIDTH)] * 5,
        out_shape=[jax.ShapeDtypeStruct((T, GMLP_WIDTH), BF16)]
                  + [jax.ShapeDtypeStruct((T, FOX_WIDTH), BF16)] * 5,
        scratch_shapes=[pltpu.VMEM((1, LANES), F32)],
        compiler_params=_cparams(1),
        name="proj_even",
    )(h, g, w, bf, lng, lnb, ws, bs_full, qg, kg, *_bias_placement())


def _bias_placement():
    pq = np.zeros((LANES, FOX_WIDTH), np.float32)
    pk = np.zeros((LANES, FOX_WIDTH), np.float32)
    oneq = np.zeros((1, FOX_WIDTH), np.float32)
    onek = np.zeros((1, FOX_WIDTH), np.float32)
    for head in range(FOX_HEADS):
        base = (head // 2) * LANES + (HEAD_DIM if head % 2 == 0 else 0)
        for piece in range(BIAS_PIECES):
            pq[piece * FOX_HEADS + head, base + piece] = 1.0
            onek[0, base + piece] = 1.0
            pk[piece * FOX_HEADS + head, base + BIAS_PIECES + piece] = -1.0
            oneq[0, base + BIAS_PIECES + piece] = 1.0
    return (jnp.asarray(pq, BF16), jnp.asarray(pk, BF16), jnp.asarray(oneq), jnp.asarray(onek))


def _fox_tile(n_tiles, q_ref, xq_ref, k_ref, xk_ref, v_ref, o_ref, *, tq):
    half = tq // 2
    lo = _lane_lo((1, LANES))
    rr = lax.broadcasted_iota(jnp.int32, (half, tq), 0)
    cc = lax.broadcasted_iota(jnp.int32, (half, tq), 1)
    nt = (((1,), (1,)), ((), ()))
    chains = [(hf, head) for hf in range(2) for head in range(2)]
    q_aug = []
    for hf in range(2):
        rows = slice(hf * half, (hf + 1) * half)
        q, xq = q_ref[rows, :], xq_ref[rows, :]
        q_aug.append((jnp.where(lo, q, xq), jnp.where(lo, xq, q)))

    def score_products(j):
        rows = slice(j * tq, (j + 1) * tq)
        ks, xk = k_ref[rows, :], xk_ref[rows, :]
        k_aug = (jnp.where(lo, ks, xk), jnp.where(lo, xk, ks))
        return [lax.dot_general(q_aug[hf][head], k_aug[head], nt, preferred_element_type=F32)
                for hf, head in chains]

    maxes = [jnp.full((half, 1), NEG, F32) for _ in chains]
    accs = [jnp.zeros((half, LANES), F32) for _ in chains]
    scores = score_products(0)
    for j in range(n_tiles):
        next_scores = score_products(j + 1) if j + 1 < n_tiles else None
        vs = v_ref[j * tq:(j + 1) * tq, :]
        one = jnp.ones_like(vs)
        v_aug = (jnp.where(lo, vs, one), jnp.where(lo, one, vs))
        for c, (hf, head) in enumerate(chains):
            s = scores[c]
            if j == n_tiles - 1:
                s = jnp.where(cc <= rr + hf * half, s, NEG)
            n = jnp.maximum(maxes[c], jnp.max(s, axis=-1, keepdims=True))
            p = jnp.exp2(s - n).astype(BF16)
            accs[c] = accs[c] * jnp.exp2(maxes[c] - n) + jnp.dot(p, v_aug[head], preferred_element_type=F32)
            maxes[c] = n
        scores = next_scores
    for hf in range(2):
        norm = [acc / pltpu.roll(acc, HEAD_DIM, 1) for acc in accs[2 * hf:2 * hf + 2]]
        o_ref[hf * half:(hf + 1) * half, :] = jnp.where(lo, norm[0], norm[1]).astype(BF16)


def _fox_kernel(q_ref, xq_ref, k_ref, xk_ref, v_ref, o_ref, *, tq, nq):
    i = pl.program_id(2)
    for c in range(nq):
        pl.when(i == c)(functools.partial(_fox_tile, c + 1, q_ref, xq_ref, k_ref, xk_ref, v_ref, o_ref, tq=tq))


def _fox_attention(q, xq, k, xk, v, *, seq):
    T = q.shape[0]
    B = T // seq
    tq = min(ATTN_TILE, seq)
    nq = seq // tq
    tile = pl.BlockSpec((tq, LANES), lambda b, hp, i: (b * nq + i, hp))
    whole = pl.BlockSpec((seq, LANES), lambda b, hp, i: (b, hp))
    return pl.pallas_call(
        functools.partial(_fox_kernel, tq=tq, nq=nq),
        grid=(B, FOX_WIDTH // LANES, nq),
        in_specs=[tile, tile, whole, whole, whole],
        out_specs=tile,
        out_shape=jax.ShapeDtypeStruct((T, FOX_WIDTH), BF16),
        compiler_params=_cparams(3),
        name="fox_attention",
    )(q, xq, k, xk, v)


def _outproj_router_kernel(h_ref, ya_ref, yb_ref, wo_ref, g_ref, wr_ref, br_ref,
                           h1_ref, m_ref, route_ref, cnt_ref, carry_ref, *, tm):
    i = pl.program_id(0)
    half = wo_ref.shape[0] // 2
    mix = (jnp.dot(ya_ref[...], wo_ref[0:half, :], preferred_element_type=F32)
           + jnp.dot(yb_ref[...], wo_ref[half:, :], preferred_element_type=F32))
    h1 = h_ref[...] + mix
    h1_ref[...] = h1
    m = _rms(h1, g_ref[...])
    m_hi = m.astype(BF16)
    m_ref[...] = m

    m_lo = (m - m_hi.astype(F32)).astype(BF16)
    hh = jnp.dot(m_hi, wr_ref[...], preferred_element_type=F32)
    lh = jnp.dot(m_lo, wr_ref[:, :LANES], preferred_element_type=F32)
    logits = hh[:, :LANES] + (hh[:, LANES:] + lh) + br_ref[...]
    lane_i = lax.broadcasted_iota(jnp.int32, (tm, LANES), 1)
    lane = lane_i.astype(F32)
    group_of_lane = (lane_i // EXPERTS_PER_GROUP).astype(F32)
    is_coarse = (lane_i >= N_EXPERTS) & (lane_i < N_EXPERTS + N_GROUPS)
    coarse = jnp.where(is_coarse, logits, NEG)
    cmax = jnp.max(coarse, axis=-1, keepdims=True)
    gidx = jnp.min(jnp.where(coarse == cmax, lane - N_EXPERTS, float(LANES)), axis=-1, keepdims=True)
    p_g = 1.0 / jnp.sum(jnp.where(is_coarse, jnp.exp(coarse - cmax), 0.0), axis=-1, keepdims=True)
    in_group = (lane_i < N_EXPERTS) & (group_of_lane == gidx)
    fine = jnp.where(in_group, logits, NEG)
    v1 = jnp.max(fine, axis=-1, keepdims=True)
    i1 = jnp.min(jnp.where(fine == v1, lane, float(LANES)), axis=-1, keepdims=True)
    fine2 = jnp.where(lane == i1, NEG, fine)
    v2 = jnp.max(fine2, axis=-1, keepdims=True)
    i2 = jnp.min(jnp.where(fine2 == v2, lane, float(LANES)), axis=-1, keepdims=True)
    e2 = jnp.exp(v2 - v1)
    w1 = p_g / (1.0 + e2)
    w2 = p_g * e2 / (1.0 + e2)

    @pl.when(i == 0)
    def _():
        carry_ref[...] = jnp.zeros_like(carry_ref)

    hit1 = lane == i1
    hit2 = lane == i2
    onehot = jnp.where(hit1 | hit2, 1.0, 0.0).astype(F32)
    tr = lax.broadcasted_iota(jnp.int32, (tm, tm), 0)
    tc = lax.broadcasted_iota(jnp.int32, (tm, tm), 1)
    strict = jnp.where(tr > tc, 1.0, 0.0).astype(BF16)
    before = jnp.dot(strict, onehot.astype(BF16), preferred_element_type=F32) + carry_ref[...]
    r1 = jnp.sum(jnp.where(hit1, before, 0.0), axis=-1, keepdims=True)
    r2 = jnp.sum(jnp.where(hit2, before, 0.0), axis=-1, keepdims=True)
    total = carry_ref[...] + jnp.sum(onehot, axis=0, keepdims=True)
    carry_ref[...] = total
    cnt_ref[...] = jnp.broadcast_to(total, cnt_ref.shape)

    route = jnp.where(lane == 0, i1, 0.0)
    route = jnp.where(lane == 1, i2, route)
    route = jnp.where(lane == 2, r1, route)
    route = jnp.where(lane == 3, r2, route)
    route = jnp.where(lane == 4, w1, route)
    route = jnp.where(lane == 5, w2, route)
    route_ref[...] = route


def _outproj_router(h, ya, yb, wo, g, wr, br):
    T = h.shape[0]
    tm = min(ROW_TILE, T)
    const = lambda *shape: pl.BlockSpec(shape, lambda i: (0,) * len(shape))
    row = lambda width: pl.BlockSpec((tm, width), lambda i: (i, 0))
    return pl.pallas_call(
        functools.partial(_outproj_router_kernel, tm=tm),
        grid=(T // tm,),
        in_specs=[row(D_MODEL), row(ya.shape[1]), row(yb.shape[1]), const(*wo.shape),
                  const(1, D_MODEL), const(D_MODEL, 2 * LANES), const(1, LANES)],
        out_specs=[row(D_MODEL), row(D_MODEL), row(LANES), const(8, LANES)],
        out_shape=[jax.ShapeDtypeStruct((T, D_MODEL), F32),
                   jax.ShapeDtypeStruct((T, D_MODEL), F32),
                   jax.ShapeDtypeStruct((T, LANES), F32),
                   jax.ShapeDtypeStruct((8, LANES), F32)],
        scratch_shapes=[pltpu.VMEM((1, LANES), F32)],
        compiler_params=_cparams(1),
        name="outproj_router",
    )(h, ya, yb, wo, g, wr, br)


def _dispatch_kernel(pad_ref, pos_ref, m_ref, xs_hbm, zeros_ref, sem, pad_sem, *, tile):
    @pl.when(pl.program_id(0) == 0)
    def _():
        zeros_ref[...] = jnp.zeros_like(zeros_ref)

        def pad_copies(e, wait):
            first, n_single, n_block = pad_ref[0, e], pad_ref[1, e], pad_ref[2, e]
            for r in range(SUBLANES - 1):
                copy = pltpu.make_async_copy(zeros_ref.at[pl.ds(0, 1)], xs_hbm.at[pl.ds(first + r, 1)], pad_sem)
                pl.when(r < n_single)(copy.wait if wait else copy.start)
            done = first + n_single
            for bit in reversed(range(SUBLANE_BITS, PAD_BITS)):
                size = 1 << bit
                taken = (n_block & size) != 0
                copy = pltpu.make_async_copy(zeros_ref.at[pl.ds(0, size)],
                                             xs_hbm.at[pl.ds(pl.multiple_of(done, SUBLANES), size)], pad_sem)
                pl.when(taken)(copy.wait if wait else copy.start)
                done = done + jnp.where(taken, size, 0)

        def start(e, carry):
            pad_copies(e, False)
            return carry

        def finish(e, carry):
            pad_copies(e, True)
            return carry

        def tile_copy(j):
            return pltpu.make_async_copy(
                zeros_ref, xs_hbm.at[pl.ds(pl.multiple_of(j * EXPERT_TILE, EXPERT_TILE), EXPERT_TILE)], pad_sem)

        def start_tile(j, carry):
            tile_copy(j).start()
            return carry

        def finish_tile(j, carry):
            tile_copy(j).wait()
            return carry

        n_tiles = xs_hbm.shape[0] // EXPERT_TILE
        lax.fori_loop(0, N_EXPERTS, start, 0)
        lax.fori_loop(pad_ref[3, 0], n_tiles, start_tile, 0)
        lax.fori_loop(0, N_EXPERTS, finish, 0)
        lax.fori_loop(pad_ref[3, 0], n_tiles, finish_tile, 0)

    def row_copy(t, k):
        return pltpu.make_async_copy(m_ref.at[pl.ds(t, 1)],
                                     xs_hbm.at[pl.ds(pos_ref[0, 0, 2 * t + k], 1)], sem)

    def issue(c, carry):
        base = pl.multiple_of(c * DMA_UNROLL, DMA_UNROLL)
        for u in range(DMA_UNROLL):
            row_copy(base + u, 0).start(priority=0)
            row_copy(base + u, 1).start(priority=1)
        return carry

    lax.fori_loop(0, tile // DMA_UNROLL, issue, 0)
    for _ in range(2):
        pltpu.make_async_copy(m_ref, xs_hbm.at[pl.ds(0, tile)], sem).wait()


def _dispatch(pads, pos, m, n_rows):
    T, width = m.shape
    tile = min(DISPATCH_TILE, T)
    pos3 = pos.reshape(T // tile, 1, 2 * tile)
    grid_spec = pltpu.PrefetchScalarGridSpec(
        num_scalar_prefetch=1,
        grid=(T // tile,),
        in_specs=[pl.BlockSpec((1, 1, 2 * tile), lambda i, pads: (i, 0, 0), memory_space=pltpu.SMEM),
                  pl.BlockSpec((tile, width), lambda i, pads: (i, 0))],
        out_specs=pl.BlockSpec(memory_space=pl.ANY),
        scratch_shapes=[pltpu.VMEM((EXPERT_TILE, width), m.dtype),
                        pltpu.SemaphoreType.DMA(()), pltpu.SemaphoreType.DMA(())],
    )
    return pl.pallas_call(
        functools.partial(_dispatch_kernel, tile=tile),
        grid_spec=grid_spec,
        out_shape=jax.ShapeDtypeStruct((n_rows, width), m.dtype),
        compiler_params=pltpu.CompilerParams(dimension_semantics=("arbitrary",),
                                             has_side_effects=True),
        name="moe_dispatch",
    )(pads, pos3, m)


def _experts_kernel(te_ref, nused_ref, xs_ref, wg_ref, wu_ref, wd_ref, ys_ref, wgu_b, wd_b):
    j = pl.program_id(0)
    prev = te_ref[jnp.maximum(j - 1, 0)]

    @pl.when((j == 0) | (te_ref[j] != prev))
    def _():
        wgu_b[:, :D_EXPERT] = wg_ref[...].astype(BF16)
        wgu_b[:, D_EXPERT:] = wu_ref[...].astype(BF16)
        wd_b[...] = wd_ref[...].astype(BF16)

    @pl.when(j < nused_ref[0])
    def _():
        gu = jnp.dot(xs_ref[...].astype(BF16), wgu_b[...], preferred_element_type=F32)
        g = gu[:, :D_EXPERT]
        act = g * jax.nn.sigmoid(g) * gu[:, D_EXPERT:]
        ys_ref[...] = jnp.dot(act.astype(BF16), wd_b[...], preferred_element_type=F32)

    @pl.when(j >= nused_ref[0])
    def _():
        ys_ref[...] = jnp.zeros_like(ys_ref)


def _experts(layer, tile_expert, n_used, xs, wg, wu, wd):
    n_rows = xs.shape[0]
    nt = n_rows // EXPERT_TILE
    grid_spec = pltpu.PrefetchScalarGridSpec(
        num_scalar_prefetch=2,
        grid=(nt,),
        in_specs=[pl.BlockSpec((EXPERT_TILE, D_MODEL), lambda j, te, nu: (j, 0)),
                  pl.BlockSpec((None, None, D_MODEL, D_EXPERT), lambda j, te, nu: (layer, te[j], 0, 0)),
                  pl.BlockSpec((None, None, D_MODEL, D_EXPERT), lambda j, te, nu: (layer, te[j], 0, 0)),
                  pl.BlockSpec((None, None, D_EXPERT, D_MODEL), lambda j, te, nu: (layer, te[j], 0, 0))],
        out_specs=pl.BlockSpec((EXPERT_TILE, D_MODEL), lambda j, te, nu: (j, 0)),
        scratch_shapes=[pltpu.VMEM((D_MODEL, 2 * D_EXPERT), BF16),
                        pltpu.VMEM((D_EXPERT, D_MODEL), BF16)],
    )
    return pl.pallas_call(
        _experts_kernel,
        grid_spec=grid_spec,
        out_shape=jax.ShapeDtypeStruct((n_rows, D_MODEL), F32),
        compiler_params=_cparams(1),
        name="moe_experts",
    )(tile_expert, n_used, xs, wg, wu, wd)


def _combine_ple_kernel(pos_ref, next_pos_ref, route_ref, h1_ref, ys_hbm, p_ref, wp_ref, g_ref, wgate_ref,
                        o_ref, ybuf, sems, *, tile):
    i = pl.program_id(0)
    n = pl.num_programs(0)
    slot = i % 2

    def gather(table, s):
        def issue(c, carry):
            base = pl.multiple_of(c * DMA_UNROLL, DMA_UNROLL)
            for u in range(DMA_UNROLL):
                t = base + u
                for k in range(2):
                    pltpu.make_async_copy(ys_hbm.at[pl.ds(table[0, 0, 2 * t + k], 1)],
                                          ybuf.at[s, k, pl.ds(t, 1)], sems.at[s]).start(priority=k)
            return carry

        lax.fori_loop(0, tile // DMA_UNROLL, issue, 0)

    @pl.when(i == 0)
    def _():
        gather(pos_ref, 0)

    @pl.when(i + 1 < n)
    def _():
        gather(next_pos_ref, 1 - slot)

    ple = jnp.dot(p_ref[...].astype(BF16), wp_ref[...], preferred_element_type=F32)
    for k in range(2):
        pltpu.make_async_copy(ys_hbm.at[pl.ds(0, tile)], ybuf.at[slot, k], sems.at[slot]).wait()

    route = route_ref[...]
    h2 = h1_ref[...] + route[:, 4:5] * ybuf[slot, 0] + route[:, 5:6] * ybuf[slot, 1]
    gate = jax.nn.sigmoid(jnp.dot(_rms(h2, g_ref[...]).astype(BF16), wgate_ref[...],
                                  preferred_element_type=F32))
    o_ref[...] = h2 + gate * ple


def _combine_ple(layer, pos, route, h1, ys, p, wp, g, wgate):
    T = h1.shape[0]
    tile = min(GATHER_TILE, T)
    n = T // tile
    pos3 = pos.reshape(n, 1, 2 * tile)
    const = lambda *shape: pl.BlockSpec(shape, lambda i: (0,) * len(shape))
    row = lambda width: pl.BlockSpec((tile, width), lambda i: (i, 0))
    return pl.pallas_call(
        functools.partial(_combine_ple_kernel, tile=tile),
        grid=(n,),
        in_specs=[pl.BlockSpec((1, 1, 2 * tile), lambda i: (i, 0, 0), memory_space=pltpu.SMEM),
                  pl.BlockSpec((1, 1, 2 * tile), lambda i: (jnp.minimum(i + 1, n - 1), 0, 0),
                               memory_space=pltpu.SMEM),
                  row(LANES), row(D_MODEL), pl.BlockSpec(memory_space=pl.ANY),
                  pl.BlockSpec((None, tile, D_PLE), lambda i: (layer, i, 0)),
                  const(D_PLE, D_MODEL), const(1, D_MODEL), const(D_MODEL, D_MODEL)],
        out_specs=row(D_MODEL),
        out_shape=jax.ShapeDtypeStruct((T, D_MODEL), F32),
        scratch_shapes=[pltpu.VMEM((2, 2, tile, D_MODEL), F32), pltpu.SemaphoreType.DMA((2,))],
        compiler_params=_cparams(1),
        name="combine_ple",
    )(pos3, pos3, route, h1, ys, p, wp, g, wgate)


def _rope(z, cos, sin_lo, sin_hi):
    half = ROT_DIM // 2
    outs = []
    for j in range(z.shape[1] // LANES):
        zj = z[:, j * LANES:(j + 1) * LANES]
        outs.append(zj * cos + pltpu.roll(zj, LANES - half, 1) * sin_lo + pltpu.roll(zj, half, 1) * sin_hi)
    return outs[0] if len(outs) == 1 else jnp.concatenate(outs, axis=1)


def _proj_odd_kernel(h_ref, g_ref, w_ref, qg_ref, kg_ref, cos_ref, slo_ref, shi_ref,
                     q_ref, k_ref, v_ref, glu_ref):
    a = _rms(h_ref[...], g_ref[...]).astype(BF16)
    cos, slo, shi = cos_ref[...], slo_ref[...], shi_ref[...]
    v0 = SWA_WIDTH + KV_WIDTH
    d0 = v0 + KV_WIDTH
    zq = jnp.dot(a, w_ref[:, 0:SWA_WIDTH], preferred_element_type=F32)
    zk = jnp.dot(a, w_ref[:, SWA_WIDTH:v0], preferred_element_type=F32)
    zv = jnp.dot(a, w_ref[:, v0:d0], preferred_element_type=F32)
    zd = jnp.dot(a, w_ref[:, d0:d0 + 2 * CONV_CH], preferred_element_type=F32)
    q_ref[...] = (_rope(_head_rms(zq, qg_ref[...]), cos, slo, shi) * (HEAD_DIM ** -0.5)).astype(BF16)
    k_ref[...] = _rope(_head_rms(zk, kg_ref[...]), cos, slo, shi).astype(BF16)
    v_ref[...] = zv.astype(BF16)
    glu_ref[...] = zd[:, :CONV_CH] * jax.nn.sigmoid(zd[:, CONV_CH:])


def _proj_odd(h, g, w, qg, kg, cos, slo, shi, *, seq):
    T = h.shape[0]
    tm = min(ROW_TILE, seq)
    tps = seq // tm
    const = lambda *shape: pl.BlockSpec(shape, lambda i: (0,) * len(shape))
    row = lambda width: pl.BlockSpec((tm, width), lambda i: (i, 0))
    tab = pl.BlockSpec((tm, LANES), lambda i: (i % tps, 0))
    return pl.pallas_call(
        _proj_odd_kernel,
        grid=(T // tm,),
        in_specs=[row(D_MODEL), const(1, D_MODEL), const(*w.shape), const(1, SWA_WIDTH),
                  const(1, KV_WIDTH), tab, tab, tab],
        out_specs=[row(SWA_WIDTH), row(KV_WIDTH), row(KV_WIDTH), row(CONV_CH)],
        out_shape=[jax.ShapeDtypeStruct((T, SWA_WIDTH), BF16),
                   jax.ShapeDtypeStruct((T, KV_WIDTH), BF16),
                   jax.ShapeDtypeStruct((T, KV_WIDTH), BF16),
                   jax.ShapeDtypeStruct((T, CONV_CH), F32)],
        compiler_params=_cparams(1),
        name="proj_odd",
    )(h, g, w, qg, kg, cos, slo, shi)


def _swa_kernel(sink_ref, q_ref, k_ref, v_ref, o_ref, *, seq):
    lo = _lane_lo((1, LANES))
    nt = (((1,), (1,)), ((), ()))
    W = WINDOW
    qi = lax.broadcasted_iota(jnp.int32, (W, 2 * W), 0)
    kj = lax.broadcasted_iota(jnp.int32, (W, 2 * W), 1)
    band = (kj > qi) & (kj <= qi + W)

    def block(n, kstart, mask):
        qrow = pl.ds(pl.multiple_of(n * W, W), W)
        kwin = pl.ds(pl.multiple_of(kstart, W), 2 * W)
        ks = k_ref[kwin, :]
        vs = v_ref[kwin, :]
        n_tiles = SWA_WIDTH // LANES
        heads, scores = [], []
        for j in range(n_tiles):
            q = q_ref[qrow, j * LANES:(j + 1) * LANES]
            zero = jnp.zeros_like(q)
            for head, qh in ((j, jnp.where(lo, q, zero)), (n_tiles + j, jnp.where(lo, zero, q))):
                heads.append(head)
                scores.append(lax.dot_general(qh, ks, nt, preferred_element_type=F32))
        probs, sums = [], []
        for head, s in zip(heads, scores):
            s = jnp.where(mask, s, NEG)
            sink = sink_ref[head]
            m = jnp.maximum(jnp.max(s, axis=-1, keepdims=True), sink)
            p = jnp.exp(s - m)
            sums.append(jnp.sum(p, axis=-1, keepdims=True) + jnp.exp(sink - m))
            probs.append(p.astype(BF16))
        outs = [jnp.dot(p, vs, preferred_element_type=F32) / l for p, l in zip(probs, sums)]
        for j in range(n_tiles):
            o_ref[qrow, j * LANES:(j + 1) * LANES] = jnp.where(lo, outs[2 * j], outs[2 * j + 1]).astype(BF16)

    block(0, 0, kj <= qi)

    def body(n, carry):
        block(n, (n - 1) * W, band)
        return carry

    lax.fori_loop(1, seq // W, body, 0)


def _swa_attention(sinks, q, k, v, *, seq):
    T = q.shape[0]
    B = T // seq
    return pl.pallas_call(
        functools.partial(_swa_kernel, seq=seq),
        grid=(B,),
        in_specs=[pl.BlockSpec(memory_space=pltpu.SMEM),
                  pl.BlockSpec((seq, SWA_WIDTH), lambda b: (b, 0)),
                  pl.BlockSpec((seq, KV_WIDTH), lambda b: (b, 0)),
                  pl.BlockSpec((seq, KV_WIDTH), lambda b: (b, 0))],
        out_specs=pl.BlockSpec((seq, SWA_WIDTH), lambda b: (b, 0)),
        out_shape=jax.ShapeDtypeStruct((T, SWA_WIDTH), BF16),
        compiler_params=_cparams(1),
        name="swa_attention",
    )(sinks, q, k, v)


def _conv_kernel(prev_ref, cur_ref, w_ref, g_ref, b_ref, o_ref, pad_ref, *, tile, sub):
    r = pl.program_id(1)
    tail = prev_ref[tile - CONV_HALO:, :]
    pad_ref[0:CONV_HALO, :] = jnp.where(r > 0, tail, jnp.zeros_like(tail))
    pad_ref[CONV_HALO:, :] = cur_ref[...]
    w = w_ref[...]
    first = CONV_HALO - (CONV_WIDTH - 1)
    for s in range(tile // sub):
        acc = jnp.zeros((sub, CONV_CH), F32)
        for j in range(CONV_WIDTH):
            start = s * sub + first + j
            acc = acc + pad_ref[start:start + sub, :] * w[j:j + 1, :]
        y = _layernorm(acc, g_ref[...], b_ref[...])
        o_ref[s * sub:(s + 1) * sub, :] = (y * jax.nn.sigmoid(y)).astype(BF16)


def _conv_module(glu, w, g, b, *, seq):
    T = glu.shape[0]
    B = T // seq
    tile = min(CONV_TILE, seq)
    nr = seq // tile
    const = lambda *shape: pl.BlockSpec(shape, lambda bb, r: (0,) * len(shape))
    return pl.pallas_call(
        functools.partial(_conv_kernel, tile=tile, sub=64),
        grid=(B, nr),
        in_specs=[pl.BlockSpec((tile, CONV_CH), lambda bb, r: (bb * nr + jnp.maximum(r - 1, 0), 0)),
                  pl.BlockSpec((tile, CONV_CH), lambda bb, r: (bb * nr + r, 0)),
                  const(CONV_WIDTH, CONV_CH), const(1, CONV_CH), const(1, CONV_CH)],
        out_specs=pl.BlockSpec((tile, CONV_CH), lambda bb, r: (bb * nr + r, 0)),
        out_shape=jax.ShapeDtypeStruct((T, CONV_CH), BF16),
        scratch_shapes=[pltpu.VMEM((CONV_HALO + tile, CONV_CH), F32)],
        compiler_params=_cparams(2),
        name="conv_module",
    )(glu, glu, w, g, b)


def _routing_tables(route, counts, n_tiles):
    e = route[:, 0:2].astype(jnp.int32)
    rank = route[:, 2:4].astype(jnp.int32)
    cnt = counts[0, :N_EXPERTS].astype(jnp.int32)
    tiles = (cnt + EXPERT_TILE - 1) // EXPERT_TILE
    tile_end = jnp.cumsum(tiles)
    offset = (tile_end - tiles) * EXPERT_TILE
    onehot = e[:, :, None] == jnp.arange(N_EXPERTS, dtype=jnp.int32)
    pos = (rank + jnp.sum(jnp.where(onehot, offset, 0), axis=-1)).reshape(-1)
    n_used = tile_end[-1]
    tile_id = jnp.minimum(jnp.arange(n_tiles, dtype=jnp.int32), n_used - 1)
    tile_expert = jnp.sum((tile_end[None, :] <= tile_id[:, None]).astype(jnp.int32), axis=1)
    first_pad = offset + cnt
    n_single = (-first_pad) % SUBLANES
    pads = jnp.stack([first_pad, n_single, tiles * EXPERT_TILE - cnt - n_single,
                      jnp.broadcast_to(tile_end[-1], cnt.shape)])
    return pos, pads, tile_expert, n_used.reshape(1).astype(jnp.int32)


def _moe_ple(h, ya, yb, wo, layer, norm_ffn, wr, br, wg, wu, wd, p, wp, ple_norm, wgate):
    T = h.shape[0]
    n_tiles = (2 * T) // EXPERT_TILE + N_EXPERTS
    h1, m, route, counts = _outproj_router(h, ya, yb, wo, norm_ffn, wr, br)
    pos, pads, tile_expert, n_used = _routing_tables(route, counts, n_tiles)
    xs = _dispatch(pads, pos, m, n_tiles * EXPERT_TILE)
    ys = _experts(layer, tile_expert, n_used, xs, wg, wu, wd)
    return _combine_ple(layer, pos, route, h1, ys, p, wp, ple_norm, wgate)


def _router_weights(w_coarse, b_coarse, w_fine, b_fine):
    wf = w_fine.transpose(1, 0, 2).reshape(D_MODEL, N_EXPERTS)
    wr = jnp.concatenate([wf, w_coarse, jnp.zeros((D_MODEL, LANES - N_EXPERTS - N_GROUPS), F32)], axis=1)
    br = jnp.concatenate([b_fine.reshape(-1), b_coarse, jnp.zeros((LANES - N_EXPERTS - N_GROUPS,), F32)])
    w_hi = wr.astype(BF16)
    w_lo = (wr - w_hi.astype(F32)).astype(BF16)
    return jnp.concatenate([w_hi, w_lo], axis=1), br.reshape(1, LANES)


def _rope_tables(seq):
    half = ROT_DIM // 2
    inv_freq = ROPE_THETA ** (-jnp.arange(half, dtype=F32) * 2.0 / ROT_DIM)
    ang = jnp.arange(seq, dtype=F32)[:, None] * inv_freq[None, :]
    cos, sin = jnp.cos(ang), jnp.sin(ang)
    zeros = jnp.zeros((seq, HEAD_DIM - ROT_DIM), F32)
    z8 = jnp.zeros((seq, half), F32)
    cos_h = jnp.concatenate([cos, cos, zeros + 1.0], axis=1)
    slo_h = jnp.concatenate([-sin, z8, zeros], axis=1)
    shi_h = jnp.concatenate([z8, sin, zeros], axis=1)
    two = lambda t: jnp.concatenate([t, t], axis=1)
    return two(cos_h), two(slo_h), two(shi_h)


def kernel(x, p, norm_mix, even_w_in, fox_b_f, gmlp_ln_g, gmlp_ln_b, gmlp_w_s, gmlp_b_s, fox_q_norm, fox_k_norm, even_w_out, odd_w_in, swa_q_norm, swa_k_norm, swa_sinks, conv_w, conv_ln_g, conv_ln_b, odd_w_out, norm_ffn, moe_w_coarse, moe_b_coarse, moe_w_fine, moe_b_fine, moe_w_gate, moe_w_up, moe_w_down, ple_w_proj, ple_norm, ple_w_gate):
    B, S, D = x.shape
    T = B * S
    h = x.reshape(T, D)
    p = p.reshape(p.shape[0], T, D_PLE)
    row = lambda v: v.reshape(1, -1)

    def moe_args(i):
        wr, br = _router_weights(moe_w_coarse[i], moe_b_coarse[i], moe_w_fine[i], moe_b_fine[i])
        return (i, row(norm_ffn[i]), wr, br, moe_w_gate, moe_w_up, moe_w_down, p,
                ple_w_proj[i].astype(BF16), row(ple_norm[i]), ple_w_gate[i].astype(BF16))

    n_main = 2 * GMLP_WIDTH + 3 * FOX_WIDTH
    w_f = jnp.pad(jnp.tile(even_w_in[0][:, n_main:], (1, BIAS_PIECES)),
                  ((0, 0), (0, LANES - BIAS_PIECES * FOX_HEADS)))
    w_in = jnp.concatenate([even_w_in[0][:, :n_main], w_f], axis=1).astype(BF16)
    b_f = jnp.pad(jnp.tile(fox_b_f[0], BIAS_PIECES), (0, LANES - BIAS_PIECES * FOX_HEADS)).reshape(1, LANES)
    bs_full = jnp.repeat(gmlp_b_s[0].T, HEAD_DIM, axis=1)
    ya, q, k, v, xq, xk = _proj_even(
        h, row(norm_mix[0]), w_in, b_f, row(gmlp_ln_g[0]), row(gmlp_ln_b[0]), gmlp_w_s[0], bs_full,
        row(jnp.tile(fox_q_norm[0], FOX_HEADS)), row(jnp.tile(fox_k_norm[0], FOX_HEADS)), seq=S)
    yb = _fox_attention(q, xq, k, xk, v, seq=S)
    h = _moe_ple(h, ya, yb, even_w_out[0].astype(BF16), *moe_args(0))

    order = jnp.array([0, 4, 1, 5, 2, 6, 3, 7])
    cols = (order[:, None] * HEAD_DIM + jnp.arange(HEAD_DIM)[None, :]).reshape(-1)
    w_odd = jnp.concatenate([odd_w_in[0][:, :SWA_WIDTH][:, cols], odd_w_in[0][:, SWA_WIDTH:]], axis=1).astype(BF16)
    w_out_odd = jnp.concatenate([odd_w_out[0][:SWA_WIDTH][cols], odd_w_out[0][SWA_WIDTH:]], axis=0).astype(BF16)
    cos, slo, shi = _rope_tables(S)
    q, k, v, glu = _proj_odd(h, row(norm_mix[1]), w_odd, row(jnp.tile(swa_q_norm[0], 8)),
                             row(jnp.tile(swa_k_norm[0], 2)), cos, slo, shi, seq=S)
    yc = _swa_attention(swa_sinks[0], q, k, v, seq=S)
    yd = _conv_module(glu, conv_w[0], row(conv_ln_g[0]), row(conv_ln_b[0]), seq=S)
    h = _moe_ple(h, yc, yd, w_out_odd, *moe_args(1))
    return h.reshape(B, S, D)
```

```python
import functools

import jax
import jax.numpy as jnp
import numpy as np
from jax import lax
from jax.experimental import pallas as pl
from jax.experimental.pallas import tpu as pltpu

F32 = jnp.float32
BF16 = jnp.bfloat16
HIGHEST = lax.Precision.HIGHEST

D_MODEL = 1024
HEAD_DIM = 64
LANES = 128
GMLP_WIDTH = 512
CHUNK = 128
FOX_WIDTH = 512
FOX_HEADS = 8
SWA_WIDTH = 512
KV_WIDTH = 128
WINDOW = 128
CONV_CH = 512
CONV_WIDTH = 31
CONV_HALO = 32
ROPE_THETA = 500000.0
ROT_DIM = 16
N_GROUPS = 4
EXPERTS_PER_GROUP = 8
N_EXPERTS = 32
D_EXPERT = 256
D_PLE = 256
EPS = 1e-6
NEG = -1e30
LOG2E = 1.4426950408889634
BIAS_PIECES = 3

EXPERT_TILE = 256
PAD_BITS = 8
SUBLANES = 8
SUBLANE_BITS = 3
ROW_TILE = 512
GATHER_TILE = 256
DISPATCH_TILE = 1024
DMA_UNROLL = 8
ATTN_TILE = 256
FOX_PAIRS_PER_STEP = 2
CONV_TILE = 256
VMEM_LIMIT = 56 * 1024 * 1024


def _cparams(n_axes=1, flags=None):
    return pltpu.CompilerParams(dimension_semantics=("arbitrary",) * n_axes,
                                vmem_limit_bytes=VMEM_LIMIT, flags=flags)


def _rms(x, gain):
    return x * lax.rsqrt(jnp.mean(x * x, axis=-1, keepdims=True) + EPS) * gain


def _layernorm(x, g, b):
    mu = jnp.mean(x, axis=-1, keepdims=True)
    xc = x - mu
    var = jnp.mean(xc * xc, axis=-1, keepdims=True)
    return xc * lax.rsqrt(var + EPS) * g + b


def _head_rms(z, gain):
    r = (lax.broadcasted_iota(jnp.int32, (2 * LANES, LANES), 0) % LANES) // HEAD_DIM
    c = lax.broadcasted_iota(jnp.int32, (2 * LANES, LANES), 1) // HEAD_DIM
    bd = jnp.where(r == c, 1.0 / HEAD_DIM, 0.0).astype(BF16)
    outs = []
    for j in range(z.shape[1] // LANES):
        zj = z[:, j * LANES:(j + 1) * LANES]
        sq = zj * zj
        sq_hi = sq.astype(BF16)
        sq_lo = (sq - sq_hi.astype(F32)).astype(BF16)
        ms = jnp.dot(jnp.concatenate([sq_hi, sq_lo], axis=1), bd, preferred_element_type=F32)
        outs.append(zj * lax.rsqrt(ms + EPS))
    zn = outs[0] if len(outs) == 1 else jnp.concatenate(outs, axis=1)
    return zn * gain


def _lane_lo(shape):
    return (lax.broadcasted_iota(jnp.int32, shape, len(shape) - 1) % LANES) < HEAD_DIM


def _proj_even_kernel(h_ref, g_ref, w_ref, bf_ref, lng_ref, lnb_ref, ws_ref, bs_ref, qg_ref, kg_ref,
                      pq_ref, pk_ref, oneq_ref, onek_ref,
                      ya_ref, q_ref, k_ref, v_ref, xq_ref, xk_ref, carry_ref, *, tm, tiles_per_seq):
    i = pl.program_id(0)
    a = _rms(h_ref[...], g_ref[...]).astype(BF16)

    q0 = 2 * GMLP_WIDTH
    za = jnp.dot(a, w_ref[:, 0:q0], preferred_element_type=F32)
    zq = jnp.dot(a, w_ref[:, q0:q0 + FOX_WIDTH], preferred_element_type=F32)
    zk = jnp.dot(a, w_ref[:, q0 + FOX_WIDTH:q0 + 2 * FOX_WIDTH], preferred_element_type=F32)
    zv = jnp.dot(a, w_ref[:, q0 + 2 * FOX_WIDTH:q0 + 3 * FOX_WIDTH], preferred_element_type=F32)
    zf = jnp.dot(a, w_ref[:, q0 + 3 * FOX_WIDTH:], preferred_element_type=F32) + bf_ref[...]

    q_ref[...] = (_head_rms(zq, qg_ref[...]) * (LOG2E * HEAD_DIM ** -0.5)).astype(BF16)
    k_ref[...] = _head_rms(zk, kg_ref[...]).astype(BF16)
    v_ref[...] = zv.astype(BF16)

    rr = lax.broadcasted_iota(jnp.int32, (CHUNK, CHUNK), 0)
    cc = lax.broadcasted_iota(jnp.int32, (CHUNK, CHUNK), 1)

    ls = jnp.minimum(zf, 0.0) - jnp.log(1.0 + jnp.exp(-jnp.abs(zf)))

    @pl.when(i % tiles_per_seq == 0)
    def _():
        carry_ref[...] = jnp.zeros_like(carry_ref)

    def split3(x):
        hi = x.astype(BF16)
        r1 = x - hi.astype(F32)
        mid = r1.astype(BF16)
        return hi, mid, (r1 - mid.astype(F32)).astype(BF16)

    tri = jnp.where(rr >= cc, 1.0, 0.0).astype(BF16)
    pieces = jnp.concatenate(split3(ls), axis=1)
    running = carry_ref[...]
    blocks = []
    for b in range(tm // CHUNK):
        d = jnp.dot(tri, pieces[b * CHUNK:(b + 1) * CHUNK, :], preferred_element_type=F32)
        blk = (d[:, :LANES] + d[:, LANES:2 * LANES]) + d[:, 2 * LANES:] + running
        running = blk[CHUNK - 1:CHUNK, :]
        blocks.append(blk)
    carry_ref[...] = running
    c = jnp.concatenate(blocks, axis=0)

    za = jax.nn.gelu(za)
    u = za[:, :GMLP_WIDTH]
    vln = _layernorm(za[:, GMLP_WIDTH:], lng_ref[...], lnb_ref[...]).astype(BF16)
    lo = _lane_lo((CHUNK, LANES))
    for j in range(GMLP_WIDTH // LANES):
        w_a = jnp.where(rr >= cc, ws_ref[2 * j], 0.0).astype(BF16)
        w_b = jnp.where(rr >= cc, ws_ref[2 * j + 1], 0.0).astype(BF16)
        cols = slice(j * LANES, (j + 1) * LANES)
        for blk in range(tm // CHUNK):
            rows = slice(blk * CHUNK, (blk + 1) * CHUNK)
            vp = vln[rows, cols]
            mixed = jnp.where(lo, jnp.dot(w_a, vp, preferred_element_type=F32),
                              jnp.dot(w_b, vp, preferred_element_type=F32)) + bs_ref[:, cols]
            ya_ref[rows, cols] = (u[rows, cols] * mixed).astype(BF16)

    hi, mid, low = split3(c * LOG2E)
    group = lax.broadcasted_iota(jnp.int32, (tm, LANES), 1) // FOX_HEADS
    sel = jnp.where(group == 0, hi, jnp.where(group == 1, mid, low))
    xq_ref[...] = (jnp.dot(sel, pq_ref[...], preferred_element_type=F32) + oneq_ref[...]).astype(BF16)
    xk_ref[...] = (jnp.dot(sel, pk_ref[...], preferred_element_type=F32) + onek_ref[...]).astype(BF16)


def _proj_even(h, g, w, bf, lng, lnb, ws, bs_full, qg, kg, *, seq):
    T = h.shape[0]
    tm = min(ROW_TILE, seq)
    n_in = w.shape[1]
    const = lambda *shape: pl.BlockSpec(shape, lambda i: (0,) * len(shape))
    row = lambda width: pl.BlockSpec((tm, width), lambda i: (i, 0))
    tps = seq // tm
    return pl.pallas_call(
        functools.partial(_proj_even_kernel, tm=tm, tiles_per_seq=tps),
        grid=(T // tm,),
        in_specs=[row(D_MODEL), const(1, D_MODEL), const(D_MODEL, n_in), const(1, LANES),
                  const(1, GMLP_WIDTH), const(1, GMLP_WIDTH), const(8, CHUNK, CHUNK),
                  const(CHUNK, GMLP_WIDTH), const(1, FOX_WIDTH), const(1, FOX_WIDTH),
                  const(LANES, FOX_WIDTH), const(LANES, FOX_WIDTH),
                  const(1, FOX_WIDTH), const(1, FOX_WIDTH)],
        out_specs=[row(GMLP_WIDTH)] + [row(FOX_WIDTH)] * 5,
        out_shape=[jax.ShapeDtypeStruct((T, GMLP_WIDTH), BF16)]
                  + [jax.ShapeDtypeStruct((T, FOX_WIDTH), BF16)] * 5,
        scratch_shapes=[pltpu.VMEM((1, LANES), F32)],
        compiler_params=_cparams(1),
        name="proj_even",
    )(h, g, w, bf, lng, lnb, ws, bs_full, qg, kg, *_bias_placement())


def _bias_placement():
    pq = np.zeros((LANES, FOX_WIDTH), np.float32)
    pk = np.zeros((LANES, FOX_WIDTH), np.float32)
    oneq = np.zeros((1, FOX_WIDTH), np.float32)
    onek = np.zeros((1, FOX_WIDTH), np.float32)
    for head in range(FOX_HEADS):
        base = (head // 2) * LANES + (HEAD_DIM if head % 2 == 0 else 0)
        for piece in range(BIAS_PIECES):
            pq[piece * FOX_HEADS + head, base + piece] = 1.0
            onek[0, base + piece] = 1.0
            pk[piece * FOX_HEADS + head, base + BIAS_PIECES + piece] = -1.0
            oneq[0, base + BIAS_PIECES + piece] = 1.0
    return (jnp.asarray(pq, BF16), jnp.asarray(pk, BF16), jnp.asarray(oneq), jnp.asarray(onek))


def _fox_tile(n_tiles, q_ref, xq_ref, k_ref, xk_ref, v_ref, o_ref, *, tq):
    half = tq // 2
    lo = _lane_lo((1, LANES))
    rr = lax.broadcasted_iota(jnp.int32, (half, tq), 0)
    cc = lax.broadcasted_iota(jnp.int32, (half, tq), 1)
    nt = (((1,), (1,)), ((), ()))
    pairs = [slice(pp * LANES, (pp + 1) * LANES) for pp in range(FOX_PAIRS_PER_STEP)]
    chains = [(pp, hf, head) for pp in range(len(pairs)) for hf in range(2) for head in range(2)]
    q_aug = {}
    for pp, cols in enumerate(pairs):
        for hf in range(2):
            rows = slice(hf * half, (hf + 1) * half)
            q, xq = q_ref[rows, cols], xq_ref[rows, cols]
            q_aug[pp, hf, 0], q_aug[pp, hf, 1] = jnp.where(lo, q, xq), jnp.where(lo, xq, q)

    def score_products(j):
        rows = slice(j * tq, (j + 1) * tq)
        k_aug = {}
        for pp, cols in enumerate(pairs):
            ks, xk = k_ref[rows, cols], xk_ref[rows, cols]
            k_aug[pp, 0], k_aug[pp, 1] = jnp.where(lo, ks, xk), jnp.where(lo, xk, ks)
        return [lax.dot_general(q_aug[pp, hf, head], k_aug[pp, head], nt, preferred_element_type=F32)
                for pp, hf, head in chains]

    maxes = [jnp.full((half, 1), NEG, F32) for _ in chains]
    accs = [jnp.zeros((half, LANES), F32) for _ in chains]
    scores = score_products(0)
    for j in range(n_tiles):
        next_scores = score_products(j + 1) if j + 1 < n_tiles else None
        v_aug = {}
        for pp, cols in enumerate(pairs):
            vs = v_ref[j * tq:(j + 1) * tq, cols]
            one = jnp.ones_like(vs)
            v_aug[pp, 0], v_aug[pp, 1] = jnp.where(lo, vs, one), jnp.where(lo, one, vs)
        for c, (pp, hf, head) in enumerate(chains):
            s = scores[c]
            if j == n_tiles - 1:
                s = jnp.where(cc <= rr + hf * half, s, NEG)
            n = jnp.maximum(maxes[c], jnp.max(s, axis=-1, keepdims=True))
            p = jnp.exp2(s - n).astype(BF16)
            accs[c] = accs[c] * jnp.exp2(maxes[c] - n) + jnp.dot(p, v_aug[pp, head], preferred_element_type=F32)
            maxes[c] = n
        scores = next_scores
    for c in range(0, len(chains), 2):
        pp, hf, _ = chains[c]
        norm = [acc / pltpu.roll(acc, HEAD_DIM, 1) for acc in accs[c:c + 2]]
        o_ref[hf * half:(hf + 1) * half, pairs[pp]] = jnp.where(lo, norm[0], norm[1]).astype(BF16)


def _fox_kernel(q_ref, xq_ref, k_ref, xk_ref, v_ref, o_ref, *, tq, nq):
    i = pl.program_id(2)
    for c in range(nq):
        pl.when(i == c)(functools.partial(_fox_tile, c + 1, q_ref, xq_ref, k_ref, xk_ref, v_ref, o_ref, tq=tq))


def _fox_attention(q, xq, k, xk, v, *, seq):
    T = q.shape[0]
    B = T // seq
    tq = min(ATTN_TILE, seq)
    nq = seq // tq
    width = FOX_PAIRS_PER_STEP * LANES
    tile = pl.BlockSpec((tq, width), lambda b, hp, i: (b * nq + i, hp))
    whole = pl.BlockSpec((seq, width), lambda b, hp, i: (b, hp))
    return pl.pallas_call(
        functools.partial(_fox_kernel, tq=tq, nq=nq),
        grid=(B, FOX_WIDTH // width, nq),
        in_specs=[tile, tile, whole, whole, whole],
        out_specs=tile,
        out_shape=jax.ShapeDtypeStruct((T, FOX_WIDTH), BF16),
        compiler_params=_cparams(3),
        name="fox_attention",
    )(q, xq, k, xk, v)


def _outproj_router_kernel(h_ref, ya_ref, yb_ref, wo_ref, g_ref, wr_ref, br_ref,
                           h1_ref, m_ref, route_ref, cnt_ref, carry_ref, *, tm):
    i = pl.program_id(0)
    half = wo_ref.shape[0] // 2
    mix = (jnp.dot(ya_ref[...], wo_ref[0:half, :], preferred_element_type=F32)
           + jnp.dot(yb_ref[...], wo_ref[half:, :], preferred_element_type=F32))
    h1 = h_ref[...] + mix
    h1_ref[...] = h1
    m = _rms(h1, g_ref[...])
    m_hi = m.astype(BF16)
    m_ref[...] = m

    m_lo = (m - m_hi.astype(F32)).astype(BF16)
    hh = jnp.dot(m_hi, wr_ref[...], preferred_element_type=F32)
    lh = jnp.dot(m_lo, wr_ref[:, :LANES], preferred_element_type=F32)
    logits = hh[:, :LANES] + (hh[:, LANES:] + lh) + br_ref[...]
    lane_i = lax.broadcasted_iota(jnp.int32, (tm, LANES), 1)
    lane = lane_i.astype(F32)
    group_of_lane = (lane_i // EXPERTS_PER_GROUP).astype(F32)
    is_coarse = (lane_i >= N_EXPERTS) & (lane_i < N_EXPERTS + N_GROUPS)
    coarse = jnp.where(is_coarse, logits, NEG)
    cmax = jnp.max(coarse, axis=-1, keepdims=True)
    gidx = jnp.min(jnp.where(coarse == cmax, lane - N_EXPERTS, float(LANES)), axis=-1, keepdims=True)
    p_g = 1.0 / jnp.sum(jnp.where(is_coarse, jnp.exp(coarse - cmax), 0.0), axis=-1, keepdims=True)
    in_group = (lane_i < N_EXPERTS) & (group_of_lane == gidx)
    fine = jnp.where(in_group, logits, NEG)
    v1 = jnp.max(fine, axis=-1, keepdims=True)
    i1 = jnp.min(jnp.where(fine == v1, lane, float(LANES)), axis=-1, keepdims=True)
    fine2 = jnp.where(lane == i1, NEG, fine)
    v2 = jnp.max(fine2, axis=-1, keepdims=True)
    i2 = jnp.min(jnp.where(fine2 == v2, lane, float(LANES)), axis=-1, keepdims=True)
    e2 = jnp.exp(v2 - v1)
    w1 = p_g / (1.0 + e2)
    w2 = p_g * e2 / (1.0 + e2)

    @pl.when(i == 0)
    def _():
        carry_ref[...] = jnp.zeros_like(carry_ref)

    hit1 = lane == i1
    hit2 = lane == i2
    onehot = jnp.where(hit1 | hit2, 1.0, 0.0).astype(F32)
    tr = lax.broadcasted_iota(jnp.int32, (tm, tm), 0)
    tc = lax.broadcasted_iota(jnp.int32, (tm, tm), 1)
    strict = jnp.where(tr > tc, 1.0, 0.0).astype(BF16)
    before = jnp.dot(strict, onehot.astype(BF16), preferred_element_type=F32) + carry_ref[...]
    r1 = jnp.sum(jnp.where(hit1, before, 0.0), axis=-1, keepdims=True)
    r2 = jnp.sum(jnp.where(hit2, before, 0.0), axis=-1, keepdims=True)
    total = carry_ref[...] + jnp.sum(onehot, axis=0, keepdims=True)
    carry_ref[...] = total
    cnt_ref[...] = jnp.broadcast_to(total, cnt_ref.shape)

    route = jnp.where(lane == 0, i1, 0.0)
    route = jnp.where(lane == 1, i2, route)
    route = jnp.where(lane == 2, r1, route)
    route = jnp.where(lane == 3, r2, route)
    route = jnp.where(lane == 4, w1, route)
    route = jnp.where(lane == 5, w2, route)
    route_ref[...] = route


def _outproj_router(h, ya, yb, wo, g, wr, br):
    T = h.shape[0]
    tm = min(ROW_TILE, T)
    const = lambda *shape: pl.BlockSpec(shape, lambda i: (0,) * len(shape))
    row = lambda width: pl.BlockSpec((tm, width), lambda i: (i, 0))
    return pl.pallas_call(
        functools.partial(_outproj_router_kernel, tm=tm),
        grid=(T // tm,),
        in_specs=[row(D_MODEL), row(ya.shape[1]), row(yb.shape[1]), const(*wo.shape),
                  const(1, D_MODEL), const(D_MODEL, 2 * LANES), const(1, LANES)],
        out_specs=[row(D_MODEL), row(D_MODEL), row(LANES), const(8, LANES)],
        out_shape=[jax.ShapeDtypeStruct((T, D_MODEL), F32),
                   jax.ShapeDtypeStruct((T, D_MODEL), F32),
                   jax.ShapeDtypeStruct((T, LANES), F32),
                   jax.ShapeDtypeStruct((8, LANES), F32)],
        scratch_shapes=[pltpu.VMEM((1, LANES), F32)],
        compiler_params=_cparams(1),
        name="outproj_router",
    )(h, ya, yb, wo, g, wr, br)


def _dispatch_kernel(pad_ref, pos_ref, m_ref, xs_hbm, zeros_ref, sem, pad_sem, *, tile):
    @pl.when(pl.program_id(0) == 0)
    def _():
        zeros_ref[...] = jnp.zeros_like(zeros_ref)

        def pad_copies(e, wait):
            first, n_single, n_block = pad_ref[0, e], pad_ref[1, e], pad_ref[2, e]
            for r in range(SUBLANES - 1):
                copy = pltpu.make_async_copy(zeros_ref.at[pl.ds(0, 1)], xs_hbm.at[pl.ds(first + r, 1)], pad_sem)
                pl.when(r < n_single)(copy.wait if wait else copy.start)
            done = first + n_single
            for bit in reversed(range(SUBLANE_BITS, PAD_BITS)):
                size = 1 << bit
                taken = (n_block & size) != 0
                copy = pltpu.make_async_copy(zeros_ref.at[pl.ds(0, size)],
                                             xs_hbm.at[pl.ds(pl.multiple_of(done, SUBLANES), size)], pad_sem)
                pl.when(taken)(copy.wait if wait else copy.start)
                done = done + jnp.where(taken, size, 0)

        def start(e, carry):
            pad_copies(e, False)
            return carry

        def finish(e, carry):
            pad_copies(e, True)
            return carry

        def tile_copy(j):
            return pltpu.make_async_copy(
                zeros_ref, xs_hbm.at[pl.ds(pl.multiple_of(j * EXPERT_TILE, EXPERT_TILE), EXPERT_TILE)], pad_sem)

        def start_tile(j, carry):
            tile_copy(j).start()
            return carry

        def finish_tile(j, carry):
            tile_copy(j).wait()
            return carry

        n_tiles = xs_hbm.shape[0] // EXPERT_TILE
        lax.fori_loop(0, N_EXPERTS, start, 0)
        lax.fori_loop(pad_ref[3, 0], n_tiles, start_tile, 0)
        lax.fori_loop(0, N_EXPERTS, finish, 0)
        lax.fori_loop(pad_ref[3, 0], n_tiles, finish_tile, 0)

    def row_copy(t, k):
        return pltpu.make_async_copy(m_ref.at[pl.ds(t, 1)],
                                     xs_hbm.at[pl.ds(pos_ref[0, 0, 2 * t + k], 1)], sem)

    def issue(c, carry):
        base = pl.multiple_of(c * DMA_UNROLL, DMA_UNROLL)
        for u in range(DMA_UNROLL):
            row_copy(base + u, 0).start(priority=0)
            row_copy(base + u, 1).start(priority=1)
        return carry

    lax.fori_loop(0, tile // DMA_UNROLL, issue, 0)
    for _ in range(2):
        pltpu.make_async_copy(m_ref, xs_hbm.at[pl.ds(0, tile)], sem).wait()


def _dispatch(pads, pos, m, n_rows):
    T, width = m.shape
    tile = min(DISPATCH_TILE, T)
    pos3 = pos.reshape(T // tile, 1, 2 * tile)
    grid_spec = pltpu.PrefetchScalarGridSpec(
        num_scalar_prefetch=1,
        grid=(T // tile,),
        in_specs=[pl.BlockSpec((1, 1, 2 * tile), lambda i, pads: (i, 0, 0), memory_space=pltpu.SMEM),
                  pl.BlockSpec((tile, width), lambda i, pads: (i, 0))],
        out_specs=pl.BlockSpec(memory_space=pl.ANY),
        scratch_shapes=[pltpu.VMEM((EXPERT_TILE, width), m.dtype),
                        pltpu.SemaphoreType.DMA(()), pltpu.SemaphoreType.DMA(())],
    )
    return pl.pallas_call(
        functools.partial(_dispatch_kernel, tile=tile),
        grid_spec=grid_spec,
        out_shape=jax.ShapeDtypeStruct((n_rows, width), m.dtype),
        compiler_params=pltpu.CompilerParams(dimension_semantics=("arbitrary",),
                                             has_side_effects=True),
        name="moe_dispatch",
    )(pads, pos3, m)


def _experts_kernel(te_ref, nused_ref, xs_ref, wg_ref, wu_ref, wd_ref, ys_ref, wgu_b, wd_b):
    j = pl.program_id(0)
    prev = te_ref[jnp.maximum(j - 1, 0)]

    @pl.when((j == 0) | (te_ref[j] != prev))
    def _():
        wgu_b[:, :D_EXPERT] = wg_ref[...].astype(BF16)
        wgu_b[:, D_EXPERT:] = wu_ref[...].astype(BF16)
        wd_b[...] = wd_ref[...].astype(BF16)

    @pl.when(j < nused_ref[0])
    def _():
        gu = jnp.dot(xs_ref[...].astype(BF16), wgu_b[...], preferred_element_type=F32)
        g = gu[:, :D_EXPERT]
        act = g * jax.nn.sigmoid(g) * gu[:, D_EXPERT:]
        ys_ref[...] = jnp.dot(act.astype(BF16), wd_b[...], preferred_element_type=F32)

    @pl.when(j >= nused_ref[0])
    def _():
        ys_ref[...] = jnp.zeros_like(ys_ref)


def _experts(layer, tile_expert, n_used, xs, wg, wu, wd):
    n_rows = xs.shape[0]
    nt = n_rows // EXPERT_TILE
    grid_spec = pltpu.PrefetchScalarGridSpec(
        num_scalar_prefetch=2,
        grid=(nt,),
        in_specs=[pl.BlockSpec((EXPERT_TILE, D_MODEL), lambda j, te, nu: (j, 0)),
                  pl.BlockSpec((None, None, D_MODEL, D_EXPERT), lambda j, te, nu: (layer, te[j], 0, 0)),
                  pl.BlockSpec((None, None, D_MODEL, D_EXPERT), lambda j, te, nu: (layer, te[j], 0, 0)),
                  pl.BlockSpec((None, None, D_EXPERT, D_MODEL), lambda j, te, nu: (layer, te[j], 0, 0))],
        out_specs=pl.BlockSpec((EXPERT_TILE, D_MODEL), lambda j, te, nu: (j, 0)),
        scratch_shapes=[pltpu.VMEM((D_MODEL, 2 * D_EXPERT), BF16),
                        pltpu.VMEM((D_EXPERT, D_MODEL), BF16)],
    )
    return pl.pallas_call(
        _experts_kernel,
        grid_spec=grid_spec,
        out_shape=jax.ShapeDtypeStruct((n_rows, D_MODEL), F32),
        compiler_params=_cparams(1),
        name="moe_experts",
    )(tile_expert, n_used, xs, wg, wu, wd)


def _combine_ple_kernel(pos_ref, next_pos_ref, route_ref, h1_ref, ys_hbm, p_ref, wp_ref, g_ref, wgate_ref,
                        o_ref, ybuf, sems, *, tile):
    i = pl.program_id(0)
    n = pl.num_programs(0)
    slot = i % 2

    def gather(table, s):
        def issue(c, carry):
            base = pl.multiple_of(c * DMA_UNROLL, DMA_UNROLL)
            for u in range(DMA_UNROLL):
                t = base + u
                for k in range(2):
                    pltpu.make_async_copy(ys_hbm.at[pl.ds(table[0, 0, 2 * t + k], 1)],
                                          ybuf.at[s, k, pl.ds(t, 1)], sems.at[s]).start(priority=k)
            return carry

        lax.fori_loop(0, tile // DMA_UNROLL, issue, 0)

    @pl.when(i == 0)
    def _():
        gather(pos_ref, 0)

    @pl.when(i + 1 < n)
    def _():
        gather(next_pos_ref, 1 - slot)

    halves = [slice(0, tile // 2), slice(tile // 2, tile)]
    p_b = p_ref[...].astype(BF16)
    ples = [jnp.dot(p_b[rows], wp_ref[...], preferred_element_type=F32) for rows in halves]
    for k in range(2):
        pltpu.make_async_copy(ys_hbm.at[pl.ds(0, tile)], ybuf.at[slot, k], sems.at[slot]).wait()

    route = route_ref[...]
    h2s = [h1_ref[rows, :] + route[rows, 4:5] * ybuf[slot, 0, rows, :] + route[rows, 5:6] * ybuf[slot, 1, rows, :]
           for rows in halves]
    normed = [_rms(h2, g_ref[...]).astype(BF16) for h2 in h2s]
    gates = [jnp.dot(x, wgate_ref[...], preferred_element_type=F32) for x in normed]
    for rows, h2, gate, ple in zip(halves, h2s, gates, ples):
        o_ref[rows, :] = h2 + jax.nn.sigmoid(gate) * ple


def _combine_ple(layer, pos, route, h1, ys, p, wp, g, wgate):
    T = h1.shape[0]
    tile = min(GATHER_TILE, T)
    n = T // tile
    pos3 = pos.reshape(n, 1, 2 * tile)
    const = lambda *shape: pl.BlockSpec(shape, lambda i: (0,) * len(shape))
    row = lambda width: pl.BlockSpec((tile, width), lambda i: (i, 0))
    return pl.pallas_call(
        functools.partial(_combine_ple_kernel, tile=tile),
        grid=(n,),
        in_specs=[pl.BlockSpec((1, 1, 2 * tile), lambda i: (i, 0, 0), memory_space=pltpu.SMEM),
                  pl.BlockSpec((1, 1, 2 * tile), lambda i: (jnp.minimum(i + 1, n - 1), 0, 0),
                               memory_space=pltpu.SMEM),
                  row(LANES), row(D_MODEL), pl.BlockSpec(memory_space=pl.ANY),
                  pl.BlockSpec((None, tile, D_PLE), lambda i: (layer, i, 0)),
                  const(D_PLE, D_MODEL), const(1, D_MODEL), const(D_MODEL, D_MODEL)],
        out_specs=row(D_MODEL),
        out_shape=jax.ShapeDtypeStruct((T, D_MODEL), F32),
        scratch_shapes=[pltpu.VMEM((2, 2, tile, D_MODEL), F32), pltpu.SemaphoreType.DMA((2,))],
        compiler_params=_cparams(1),
        name="combine_ple",
    )(pos3, pos3, route, h1, ys, p, wp, g, wgate)


def _rope(z, cos, sin_lo, sin_hi):
    half = ROT_DIM // 2
    outs = []
    for j in range(z.shape[1] // LANES):
        zj = z[:, j * LANES:(j + 1) * LANES]
        outs.append(zj * cos + pltpu.roll(zj, LANES - half, 1) * sin_lo + pltpu.roll(zj, half, 1) * sin_hi)
    return outs[0] if len(outs) == 1 else jnp.concatenate(outs, axis=1)


def _proj_odd_kernel(h_ref, g_ref, w_ref, qg_ref, kg_ref, cos_ref, slo_ref, shi_ref,
                     q_ref, k_ref, v_ref, glu_ref):
    a = _rms(h_ref[...], g_ref[...]).astype(BF16)
    cos, slo, shi = cos_ref[...], slo_ref[...], shi_ref[...]
    v0 = SWA_WIDTH + KV_WIDTH
    d0 = v0 + KV_WIDTH
    zq = jnp.dot(a, w_ref[:, 0:SWA_WIDTH], preferred_element_type=F32)
    zk = jnp.dot(a, w_ref[:, SWA_WIDTH:v0], preferred_element_type=F32)
    zv = jnp.dot(a, w_ref[:, v0:d0], preferred_element_type=F32)
    zd = jnp.dot(a, w_ref[:, d0:d0 + 2 * CONV_CH], preferred_element_type=F32)
    q_ref[...] = (_rope(_head_rms(zq, qg_ref[...]), cos, slo, shi) * (HEAD_DIM ** -0.5)).astype(BF16)
    k_ref[...] = _rope(_head_rms(zk, kg_ref[...]), cos, slo, shi).astype(BF16)
    v_ref[...] = zv.astype(BF16)
    glu_ref[...] = zd[:, :CONV_CH] * jax.nn.sigmoid(zd[:, CONV_CH:])


def _proj_odd(h, g, w, qg, kg, cos, slo, shi, *, seq):
    T = h.shape[0]
    tm = min(ROW_TILE, seq)
    tps = seq // tm
    const = lambda *shape: pl.BlockSpec(shape, lambda i: (0,) * len(shape))
    row = lambda width: pl.BlockSpec((tm, width), lambda i: (i, 0))
    tab = pl.BlockSpec((tm, LANES), lambda i: (i % tps, 0))
    return pl.pallas_call(
        _proj_odd_kernel,
        grid=(T // tm,),
        in_specs=[row(D_MODEL), const(1, D_MODEL), const(*w.shape), const(1, SWA_WIDTH),
                  const(1, KV_WIDTH), tab, tab, tab],
        out_specs=[row(SWA_WIDTH), row(KV_WIDTH), row(KV_WIDTH), row(CONV_CH)],
        out_shape=[jax.ShapeDtypeStruct((T, SWA_WIDTH), BF16),
                   jax.ShapeDtypeStruct((T, KV_WIDTH), BF16),
                   jax.ShapeDtypeStruct((T, KV_WIDTH), BF16),
                   jax.ShapeDtypeStruct((T, CONV_CH), F32)],
        compiler_params=_cparams(1),
        name="proj_odd",
    )(h, g, w, qg, kg, cos, slo, shi)


def _swa_kernel(sink_ref, q_ref, k_ref, v_ref, o_ref, *, seq):
    lo = _lane_lo((1, LANES))
    nt = (((1,), (1,)), ((), ()))
    W = WINDOW
    qi = lax.broadcasted_iota(jnp.int32, (W, 2 * W), 0)
    kj = lax.broadcasted_iota(jnp.int32, (W, 2 * W), 1)
    band = (kj > qi) & (kj <= qi + W)

    def block(n, kstart, mask):
        qrow = pl.ds(pl.multiple_of(n * W, W), W)
        kwin = pl.ds(pl.multiple_of(kstart, W), 2 * W)
        ks = k_ref[kwin, :]
        vs = v_ref[kwin, :]
        n_tiles = SWA_WIDTH // LANES
        heads, scores = [], []
        for j in range(n_tiles):
            q = q_ref[qrow, j * LANES:(j + 1) * LANES]
            zero = jnp.zeros_like(q)
            for head, qh in ((j, jnp.where(lo, q, zero)), (n_tiles + j, jnp.where(lo, zero, q))):
                heads.append(head)
                scores.append(lax.dot_general(qh, ks, nt, preferred_element_type=F32))
        probs, sums = [], []
        for head, s in zip(heads, scores):
            s = jnp.where(mask, s, NEG)
            sink = sink_ref[head]
            m = jnp.maximum(jnp.max(s, axis=-1, keepdims=True), sink)
            p = jnp.exp(s - m)
            sums.append(jnp.sum(p, axis=-1, keepdims=True) + jnp.exp(sink - m))
            probs.append(p.astype(BF16))
        outs = [jnp.dot(p, vs, preferred_element_type=F32) / l for p, l in zip(probs, sums)]
        for j in range(n_tiles):
            o_ref[qrow, j * LANES:(j + 1) * LANES] = jnp.where(lo, outs[2 * j], outs[2 * j + 1]).astype(BF16)

    block(0, 0, kj <= qi)

    def body(n, carry):
        block(n, (n - 1) * W, band)
        return carry

    lax.fori_loop(1, seq // W, body, 0)


def _swa_attention(sinks, q, k, v, *, seq):
    T = q.shape[0]
    B = T // seq
    return pl.pallas_call(
        functools.partial(_swa_kernel, seq=seq),
        grid=(B,),
        in_specs=[pl.BlockSpec(memory_space=pltpu.SMEM),
                  pl.BlockSpec((seq, SWA_WIDTH), lambda b: (b, 0)),
                  pl.BlockSpec((seq, KV_WIDTH), lambda b: (b, 0)),
                  pl.BlockSpec((seq, KV_WIDTH), lambda b: (b, 0))],
        out_specs=pl.BlockSpec((seq, SWA_WIDTH), lambda b: (b, 0)),
        out_shape=jax.ShapeDtypeStruct((T, SWA_WIDTH), BF16),
        compiler_params=_cparams(1),
        name="swa_attention",
    )(sinks, q, k, v)


def _conv_kernel(prev_ref, cur_ref, w_ref, g_ref, b_ref, o_ref, shift_ref, *, tile, sub):
    r = pl.program_id(1)
    rows = CONV_HALO + tile
    tail = prev_ref[tile - CONV_HALO:, :]
    shift_ref[0, 0:CONV_HALO, :] = jnp.where(r > 0, tail, jnp.zeros_like(tail))
    shift_ref[0, CONV_HALO:rows, :] = cur_ref[...]
    shift_ref[0, rows:rows + SUBLANES, :] = jnp.zeros((SUBLANES, CONV_CH), F32)
    for o in range(1, SUBLANES):
        shift_ref[o, 0:rows, :] = shift_ref[0, o:o + rows, :]
    w = w_ref[...]
    first = CONV_HALO - (CONV_WIDTH - 1)
    for s in range(tile // sub):
        acc = jnp.zeros((sub, CONV_CH), F32)
        for j in range(CONV_WIDTH):
            start = s * sub + first + j
            o = start % SUBLANES
            acc = acc + shift_ref[o, start - o:start - o + sub, :] * w[j:j + 1, :]
        y = _layernorm(acc, g_ref[...], b_ref[...])
        o_ref[s * sub:(s + 1) * sub, :] = (y * jax.nn.sigmoid(y)).astype(BF16)


def _conv_module(glu, w, g, b, *, seq):
    T = glu.shape[0]
    B = T // seq
    tile = min(CONV_TILE, seq)
    nr = seq // tile
    const = lambda *shape: pl.BlockSpec(shape, lambda bb, r: (0,) * len(shape))
    return pl.pallas_call(
        functools.partial(_conv_kernel, tile=tile, sub=64),
        grid=(B, nr),
        in_specs=[pl.BlockSpec((tile, CONV_CH), lambda bb, r: (bb * nr + jnp.maximum(r - 1, 0), 0)),
                  pl.BlockSpec((tile, CONV_CH), lambda bb, r: (bb * nr + r, 0)),
                  const(CONV_WIDTH, CONV_CH), const(1, CONV_CH), const(1, CONV_CH)],
        out_specs=pl.BlockSpec((tile, CONV_CH), lambda bb, r: (bb * nr + r, 0)),
        out_shape=jax.ShapeDtypeStruct((T, CONV_CH), BF16),
        scratch_shapes=[pltpu.VMEM((SUBLANES, CONV_HALO + tile + SUBLANES, CONV_CH), F32)],
        compiler_params=_cparams(2),
        name="conv_module",
    )(glu, glu, w, g, b)


def _routing_tables(route, counts, n_tiles):
    e = route[:, 0:2].astype(jnp.int32)
    rank = route[:, 2:4].astype(jnp.int32)
    cnt = counts[0, :N_EXPERTS].astype(jnp.int32)
    tiles = (cnt + EXPERT_TILE - 1) // EXPERT_TILE
    tile_end = jnp.cumsum(tiles)
    offset = (tile_end - tiles) * EXPERT_TILE
    onehot = e[:, :, None] == jnp.arange(N_EXPERTS, dtype=jnp.int32)
    pos = (rank + jnp.sum(jnp.where(onehot, offset, 0), axis=-1)).reshape(-1)
    n_used = tile_end[-1]
    tile_id = jnp.minimum(jnp.arange(n_tiles, dtype=jnp.int32), n_used - 1)
    tile_expert = jnp.sum((tile_end[None, :] <= tile_id[:, None]).astype(jnp.int32), axis=1)
    first_pad = offset + cnt
    n_single = (-first_pad) % SUBLANES
    pads = jnp.stack([first_pad, n_single, tiles * EXPERT_TILE - cnt - n_single,
                      jnp.broadcast_to(tile_end[-1], cnt.shape)])
    return pos, pads, tile_expert, n_used.reshape(1).astype(jnp.int32)


def _moe_ple(h, ya, yb, wo, layer, norm_ffn, wr, br, wg, wu, wd, p, wp, ple_norm, wgate):
    T = h.shape[0]
    n_tiles = (2 * T) // EXPERT_TILE + N_EXPERTS
    h1, m, route, counts = _outproj_router(h, ya, yb, wo, norm_ffn, wr, br)
    pos, pads, tile_expert, n_used = _routing_tables(route, counts, n_tiles)
    xs = _dispatch(pads, pos, m, n_tiles * EXPERT_TILE)
    ys = _experts(layer, tile_expert, n_used, xs, wg, wu, wd)
    return _combine_ple(layer, pos, route, h1, ys, p, wp, ple_norm, wgate)


def _router_weights(w_coarse, b_coarse, w_fine, b_fine):
    wf = w_fine.transpose(1, 0, 2).reshape(D_MODEL, N_EXPERTS)
    wr = jnp.concatenate([wf, w_coarse, jnp.zeros((D_MODEL, LANES - N_EXPERTS - N_GROUPS), F32)], axis=1)
    br = jnp.concatenate([b_fine.reshape(-1), b_coarse, jnp.zeros((LANES - N_EXPERTS - N_GROUPS,), F32)])
    w_hi = wr.astype(BF16)
    w_lo = (wr - w_hi.astype(F32)).astype(BF16)
    return jnp.concatenate([w_hi, w_lo], axis=1), br.reshape(1, LANES)


def _rope_tables(seq):
    half = ROT_DIM // 2
    inv_freq = ROPE_THETA ** (-jnp.arange(half, dtype=F32) * 2.0 / ROT_DIM)
    ang = jnp.arange(seq, dtype=F32)[:, None] * inv_freq[None, :]
    cos, sin = jnp.cos(ang), jnp.sin(ang)
    zeros = jnp.zeros((seq, HEAD_DIM - ROT_DIM), F32)
    z8 = jnp.zeros((seq, half), F32)
    cos_h = jnp.concatenate([cos, cos, zeros + 1.0], axis=1)
    slo_h = jnp.concatenate([-sin, z8, zeros], axis=1)
    shi_h = jnp.concatenate([z8, sin, zeros], axis=1)
    two = lambda t: jnp.concatenate([t, t], axis=1)
    return two(cos_h), two(slo_h), two(shi_h)


def kernel(x, p, norm_mix, even_w_in, fox_b_f, gmlp_ln_g, gmlp_ln_b, gmlp_w_s, gmlp_b_s, fox_q_norm, fox_k_norm, even_w_out, odd_w_in, swa_q_norm, swa_k_norm, swa_sinks, conv_w, conv_ln_g, conv_ln_b, odd_w_out, norm_ffn, moe_w_coarse, moe_b_coarse, moe_w_fine, moe_b_fine, moe_w_gate, moe_w_up, moe_w_down, ple_w_proj, ple_norm, ple_w_gate):
    B, S, D = x.shape
    T = B * S
    h = x.reshape(T, D)
    p = p.reshape(p.shape[0], T, D_PLE)
    row = lambda v: v.reshape(1, -1)

    def moe_args(i):
        wr, br = _router_weights(moe_w_coarse[i], moe_b_coarse[i], moe_w_fine[i], moe_b_fine[i])
        return (i, row(norm_ffn[i]), wr, br, moe_w_gate, moe_w_up, moe_w_down, p,
                ple_w_proj[i].astype(BF16), row(ple_norm[i]), ple_w_gate[i].astype(BF16))

    n_main = 2 * GMLP_WIDTH + 3 * FOX_WIDTH
    w_f = jnp.pad(jnp.tile(even_w_in[0][:, n_main:], (1, BIAS_PIECES)),
                  ((0, 0), (0, LANES - BIAS_PIECES * FOX_HEADS)))
    w_in = jnp.concatenate([even_w_in[0][:, :n_main], w_f], axis=1).astype(BF16)
    b_f = jnp.pad(jnp.tile(fox_b_f[0], BIAS_PIECES), (0, LANES - BIAS_PIECES * FOX_HEADS)).reshape(1, LANES)
    bs_full = jnp.repeat(gmlp_b_s[0].T, HEAD_DIM, axis=1)
    ya, q, k, v, xq, xk = _proj_even(
        h, row(norm_mix[0]), w_in, b_f, row(gmlp_ln_g[0]), row(gmlp_ln_b[0]), gmlp_w_s[0], bs_full,
        row(jnp.tile(fox_q_norm[0], FOX_HEADS)), row(jnp.tile(fox_k_norm[0], FOX_HEADS)), seq=S)
    yb = _fox_attention(q, xq, k, xk, v, seq=S)
    h = _moe_ple(h, ya, yb, even_w_out[0].astype(BF16), *moe_args(0))

    order = jnp.array([0, 4, 1, 5, 2, 6, 3, 7])
    cols = (order[:, None] * HEAD_DIM + jnp.arange(HEAD_DIM)[None, :]).reshape(-1)
    w_odd = jnp.concatenate([odd_w_in[0][:, :SWA_WIDTH][:, cols], odd_w_in[0][:, SWA_WIDTH:]], axis=1).astype(BF16)
    w_out_odd = jnp.concatenate([odd_w_out[0][:SWA_WIDTH][cols], odd_w_out[0][SWA_WIDTH:]], axis=0).astype(BF16)
    cos, slo, shi = _rope_tables(S)
    q, k, v, glu = _proj_odd(h, row(norm_mix[1]), w_odd, row(jnp.tile(swa_q_norm[0], 8)),
                             row(jnp.tile(swa_k_norm[0], 2)), cos, slo, shi, seq=S)
    yc = _swa_attention(swa_sinks[0], q, k, v, seq=S)
    yd = _conv_module(glu, conv_w[0], row(conv_ln_g[0]), row(conv_ln_b[0]), seq=S)
    h = _moe_ple(h, yc, yd, w_out_odd, *moe_args(1))
    return h.reshape(B, S, D)
```

```python
import functools

import jax
import jax.numpy as jnp
import numpy as np
from jax import lax
from jax.experimental import pallas as pl
from jax.experimental.pallas import tpu as pltpu

F32 = jnp.float32
BF16 = jnp.bfloat16
HIGHEST = lax.Precision.HIGHEST

D_MODEL = 1024
HEAD_DIM = 64
LANES = 128
GMLP_WIDTH = 512
CHUNK = 128
FOX_WIDTH = 512
FOX_HEADS = 8
SWA_WIDTH = 512
KV_WIDTH = 128
WINDOW = 128
CONV_CH = 512
CONV_WIDTH = 31
CONV_HALO = 32
ROPE_THETA = 500000.0
ROT_DIM = 16
N_GROUPS = 4
EXPERTS_PER_GROUP = 8
N_EXPERTS = 32
D_EXPERT = 256
D_PLE = 256
EPS = 1e-6
NEG = -1e30
LOG2E = 1.4426950408889634
BIAS_PIECES = 3

EXPERT_TILE = 512
PAD_BITS = 9
SUBLANES = 8
SUBLANE_BITS = 3
ROW_TILE = 512
ROUTER_BLOCKS = 4
GATHER_TILE = 256
DISPATCH_TILE = 1024
DMA_UNROLL = 8
ATTN_TILE = 256
FOX_PAIRS_PER_STEP = 2
CONV_TILE = 256
VMEM_LIMIT = 56 * 1024 * 1024


def _cparams(n_axes=1, flags=None):
    return pltpu.CompilerParams(dimension_semantics=("arbitrary",) * n_axes,
                                vmem_limit_bytes=VMEM_LIMIT, flags=flags)


def _rms(x, gain):
    return x * lax.rsqrt(jnp.mean(x * x, axis=-1, keepdims=True) + EPS) * gain


def _layernorm(x, g, b):
    mu = jnp.mean(x, axis=-1, keepdims=True)
    xc = x - mu
    var = jnp.mean(xc * xc, axis=-1, keepdims=True)
    return xc * lax.rsqrt(var + EPS) * g + b


def _head_rms(z, gain):
    r = (lax.broadcasted_iota(jnp.int32, (2 * LANES, LANES), 0) % LANES) // HEAD_DIM
    c = lax.broadcasted_iota(jnp.int32, (2 * LANES, LANES), 1) // HEAD_DIM
    bd = jnp.where(r == c, 1.0 / HEAD_DIM, 0.0).astype(BF16)
    outs = []
    for j in range(z.shape[1] // LANES):
        zj = z[:, j * LANES:(j + 1) * LANES]
        sq = zj * zj
        sq_hi = sq.astype(BF16)
        sq_lo = (sq - sq_hi.astype(F32)).astype(BF16)
        ms = jnp.dot(jnp.concatenate([sq_hi, sq_lo], axis=1), bd, preferred_element_type=F32)
        outs.append(zj * lax.rsqrt(ms + EPS))
    zn = outs[0] if len(outs) == 1 else jnp.concatenate(outs, axis=1)
    return zn * gain


def _lane_lo(shape):
    return (lax.broadcasted_iota(jnp.int32, shape, len(shape) - 1) % LANES) < HEAD_DIM


def _proj_even_kernel(h_ref, g_ref, w_ref, bf_ref, lng_ref, lnb_ref, ws_ref, bs_ref, qg_ref, kg_ref,
                      pq_ref, pk_ref, oneq_ref, onek_ref,
                      ya_ref, q_ref, k_ref, v_ref, xq_ref, xk_ref, carry_ref, *, tm, tiles_per_seq):
    i = pl.program_id(0)
    a = _rms(h_ref[...], g_ref[...]).astype(BF16)

    q0 = 2 * GMLP_WIDTH
    za = jnp.dot(a, w_ref[:, 0:q0], preferred_element_type=F32)
    zq = jnp.dot(a, w_ref[:, q0:q0 + FOX_WIDTH], preferred_element_type=F32)
    zk = jnp.dot(a, w_ref[:, q0 + FOX_WIDTH:q0 + 2 * FOX_WIDTH], preferred_element_type=F32)
    zv = jnp.dot(a, w_ref[:, q0 + 2 * FOX_WIDTH:q0 + 3 * FOX_WIDTH], preferred_element_type=F32)
    zf = jnp.dot(a, w_ref[:, q0 + 3 * FOX_WIDTH:], preferred_element_type=F32) + bf_ref[...]

    q_ref[...] = (_head_rms(zq, qg_ref[...]) * (LOG2E * HEAD_DIM ** -0.5)).astype(BF16)
    k_ref[...] = _head_rms(zk, kg_ref[...]).astype(BF16)
    v_ref[...] = zv.astype(BF16)

    rr = lax.broadcasted_iota(jnp.int32, (CHUNK, CHUNK), 0)
    cc = lax.broadcasted_iota(jnp.int32, (CHUNK, CHUNK), 1)

    ls = jnp.minimum(zf, 0.0) - jnp.log(1.0 + jnp.exp(-jnp.abs(zf)))

    @pl.when(i % tiles_per_seq == 0)
    def _():
        carry_ref[...] = jnp.zeros_like(carry_ref)

    def split3(x):
        hi = x.astype(BF16)
        r1 = x - hi.astype(F32)
        mid = r1.astype(BF16)
        return hi, mid, (r1 - mid.astype(F32)).astype(BF16)

    tri = jnp.where(rr >= cc, 1.0, 0.0).astype(BF16)
    pieces = jnp.concatenate(split3(ls), axis=1)
    running = carry_ref[...]
    blocks = []
    for b in range(tm // CHUNK):
        d = jnp.dot(tri, pieces[b * CHUNK:(b + 1) * CHUNK, :], preferred_element_type=F32)
        blk = (d[:, :LANES] + d[:, LANES:2 * LANES]) + d[:, 2 * LANES:] + running
        running = blk[CHUNK - 1:CHUNK, :]
        blocks.append(blk)
    carry_ref[...] = running
    c = jnp.concatenate(blocks, axis=0)

    za = jax.nn.gelu(za)
    u = za[:, :GMLP_WIDTH]
    vln = _layernorm(za[:, GMLP_WIDTH:], lng_ref[...], lnb_ref[...]).astype(BF16)
    lo = _lane_lo((CHUNK, LANES))
    for j in range(GMLP_WIDTH // LANES):
        w_a = jnp.where(rr >= cc, ws_ref[2 * j], 0.0).astype(BF16)
        w_b = jnp.where(rr >= cc, ws_ref[2 * j + 1], 0.0).astype(BF16)
        cols = slice(j * LANES, (j + 1) * LANES)
        for blk in range(tm // CHUNK):
            rows = slice(blk * CHUNK, (blk + 1) * CHUNK)
            vp = vln[rows, cols]
            mixed = jnp.where(lo, jnp.dot(w_a, vp, preferred_element_type=F32),
                              jnp.dot(w_b, vp, preferred_element_type=F32)) + bs_ref[:, cols]
            ya_ref[rows, cols] = (u[rows, cols] * mixed).astype(BF16)

    hi, mid, low = split3(c * LOG2E)
    group = lax.broadcasted_iota(jnp.int32, (tm, LANES), 1) // FOX_HEADS
    sel = jnp.where(group == 0, hi, jnp.where(group == 1, mid, low))
    xq_ref[...] = (jnp.dot(sel, pq_ref[...], preferred_element_type=F32) + oneq_ref[...]).astype(BF16)
    xk_ref[...] = (jnp.dot(sel, pk_ref[...], preferred_element_type=F32) + onek_ref[...]).astype(BF16)


def _proj_even(h, g, w, bf, lng, lnb, ws, bs_full, qg, kg, *, seq):
    T = h.shape[0]
    tm = min(ROW_TILE, seq)
    n_in = w.shape[1]
    const = lambda *shape: pl.BlockSpec(shape, lambda i: (0,) * len(shape))
    row = lambda width: pl.BlockSpec((tm, width), lambda i: (i, 0))
    tps = seq // tm
    return pl.pallas_call(
        functools.partial(_proj_even_kernel, tm=tm, tiles_per_seq=tps),
        grid=(T // tm,),
        in_specs=[row(D_MODEL), const(1, D_MODEL), const(D_MODEL, n_in), const(1, LANES),
                  const(1, GMLP_WIDTH), const(1, GMLP_WIDTH), const(8, CHUNK, CHUNK),
                  const(CHUNK, GMLP_WIDTH), const(1, FOX_WIDTH), const(1, FOX_WIDTH),
                  const(LANES, FOX_WIDTH), const(LANES, FOX_WIDTH),
                  const(1, FOX_WIDTH), const(1, FOX_WIDTH)],
        out_specs=[row(GMLP_WIDTH)] + [row(FOX_WIDTH)] * 5,
        out_shape=[jax.ShapeDtypeStruct((T, GMLP_WIDTH), BF16)]
                  + [jax.ShapeDtypeStruct((T, FOX_WIDTH), BF16)] * 5,
        scratch_shapes=[pltpu.VMEM((1, LANES), F32)],
        compiler_params=_cparams(1),
        name="proj_even",
    )(h, g, w, bf, lng, lnb, ws, bs_full, qg, kg, *_bias_placement())


def _bias_placement():
    pq = np.zeros((LANES, FOX_WIDTH), np.float32)
    pk = np.zeros((LANES, FOX_WIDTH), np.float32)
    oneq = np.zeros((1, FOX_WIDTH), np.float32)
    onek = np.zeros((1, FOX_WIDTH), np.float32)
    for head in range(FOX_HEADS):
        base = (head // 2) * LANES + (HEAD_DIM if head % 2 == 0 else 0)
        for piece in range(BIAS_PIECES):
            pq[piece * FOX_HEADS + head, base + piece] = 1.0
            onek[0, base + piece] = 1.0
            pk[piece * FOX_HEADS + head, base + BIAS_PIECES + piece] = -1.0
            oneq[0, base + BIAS_PIECES + piece] = 1.0
    return (jnp.asarray(pq, BF16), jnp.asarray(pk, BF16), jnp.asarray(oneq), jnp.asarray(onek))


def _fox_tile(n_tiles, q_ref, xq_ref, k_ref, xk_ref, v_ref, o_ref, *, tq):
    half = tq // 2
    lo = _lane_lo((1, LANES))
    rr = lax.broadcasted_iota(jnp.int32, (half, tq), 0)
    cc = lax.broadcasted_iota(jnp.int32, (half, tq), 1)
    nt = (((1,), (1,)), ((), ()))
    pairs = [slice(pp * LANES, (pp + 1) * LANES) for pp in range(FOX_PAIRS_PER_STEP)]
    chains = [(pp, hf, head) for pp in range(len(pairs)) for hf in range(2) for head in range(2)]
    q_aug = {}
    for pp, cols in enumerate(pairs):
        for hf in range(2):
            rows = slice(hf * half, (hf + 1) * half)
            q, xq = q_ref[rows, cols], xq_ref[rows, cols]
            q_aug[pp, hf, 0], q_aug[pp, hf, 1] = jnp.where(lo, q, xq), jnp.where(lo, xq, q)

    def score_products(j):
        rows = slice(j * tq, (j + 1) * tq)
        k_aug = {}
        for pp, cols in enumerate(pairs):
            ks, xk = k_ref[rows, cols], xk_ref[rows, cols]
            k_aug[pp, 0], k_aug[pp, 1] = jnp.where(lo, ks, xk), jnp.where(lo, xk, ks)
        return [lax.dot_general(q_aug[pp, hf, head], k_aug[pp, head], nt, preferred_element_type=F32)
                for pp, hf, head in chains]

    maxes = [jnp.full((half, 1), NEG, F32) for _ in chains]
    accs = [jnp.zeros((half, LANES), F32) for _ in chains]
    scores = score_products(0)
    for j in range(n_tiles):
        next_scores = score_products(j + 1) if j + 1 < n_tiles else None
        v_aug = {}
        for pp, cols in enumerate(pairs):
            vs = v_ref[j * tq:(j + 1) * tq, cols]
            one = jnp.ones_like(vs)
            v_aug[pp, 0], v_aug[pp, 1] = jnp.where(lo, vs, one), jnp.where(lo, one, vs)
        for c, (pp, hf, head) in enumerate(chains):
            s = scores[c]
            if j == n_tiles - 1:
                s = jnp.where(cc <= rr + hf * half, s, NEG)
            n = jnp.maximum(maxes[c], jnp.max(s, axis=-1, keepdims=True))
            p = jnp.exp2(s - n).astype(BF16)
            accs[c] = accs[c] * jnp.exp2(maxes[c] - n) + jnp.dot(p, v_aug[pp, head], preferred_element_type=F32)
            maxes[c] = n
        scores = next_scores
    for c in range(0, len(chains), 2):
        pp, hf, _ = chains[c]
        norm = [acc / pltpu.roll(acc, HEAD_DIM, 1) for acc in accs[c:c + 2]]
        o_ref[hf * half:(hf + 1) * half, pairs[pp]] = jnp.where(lo, norm[0], norm[1]).astype(BF16)


def _fox_kernel(q_ref, xq_ref, k_ref, xk_ref, v_ref, o_ref, *, tq, nq):
    i = pl.program_id(2)
    for c in range(nq):
        pl.when(i == c)(functools.partial(_fox_tile, c + 1, q_ref, xq_ref, k_ref, xk_ref, v_ref, o_ref, tq=tq))


def _fox_attention(q, xq, k, xk, v, *, seq):
    T = q.shape[0]
    B = T // seq
    tq = min(ATTN_TILE, seq)
    nq = seq // tq
    width = FOX_PAIRS_PER_STEP * LANES
    tile = pl.BlockSpec((tq, width), lambda b, hp, i: (b * nq + i, hp))
    whole = pl.BlockSpec((seq, width), lambda b, hp, i: (b, hp))
    return pl.pallas_call(
        functools.partial(_fox_kernel, tq=tq, nq=nq),
        grid=(B, FOX_WIDTH // width, nq),
        in_specs=[tile, tile, whole, whole, whole],
        out_specs=tile,
        out_shape=jax.ShapeDtypeStruct((T, FOX_WIDTH), BF16),
        compiler_params=_cparams(3),
        name="fox_attention",
    )(q, xq, k, xk, v)


def _outproj_router_kernel(h_ref, ya_ref, yb_ref, wo_ref, g_ref, wr_ref, br_ref,
                           h1_ref, m_ref, route_ref, cnt_ref, carry_ref, *, tm):
    i = pl.program_id(0)
    half = wo_ref.shape[0] // 2
    nb = ROUTER_BLOCKS
    rb = tm // nb
    blocks = [slice(b * rb, (b + 1) * rb) for b in range(nb)]

    @pl.when(i == 0)
    def _():
        carry_ref[...] = jnp.zeros_like(carry_ref)

    mixes = [jnp.dot(ya_ref[rows, :], wo_ref[0:half, :], preferred_element_type=F32)
             + jnp.dot(yb_ref[rows, :], wo_ref[half:, :], preferred_element_type=F32) for rows in blocks]
    ms = []
    for rows, mix in zip(blocks, mixes):
        h1 = h_ref[rows, :] + mix
        h1_ref[rows, :] = h1
        m = _rms(h1, g_ref[...])
        m_ref[rows, :] = m
        ms.append(m)

    logits = []
    for m in ms:
        m_hi = m.astype(BF16)
        m_lo = (m - m_hi.astype(F32)).astype(BF16)
        hh = jnp.dot(m_hi, wr_ref[...], preferred_element_type=F32)
        lh = jnp.dot(m_lo, wr_ref[:, :LANES], preferred_element_type=F32)
        logits.append(hh[:, :LANES] + (hh[:, LANES:] + lh) + br_ref[...])

    lane_i = lax.broadcasted_iota(jnp.int32, (rb, LANES), 1)
    lane = lane_i.astype(F32)
    group_of_lane = (lane_i // EXPERTS_PER_GROUP).astype(F32)
    is_coarse = (lane_i >= N_EXPERTS) & (lane_i < N_EXPERTS + N_GROUPS)
    picks = []
    for lg in logits:
        coarse = jnp.where(is_coarse, lg, NEG)
        cmax = jnp.max(coarse, axis=-1, keepdims=True)
        gidx = jnp.min(jnp.where(coarse == cmax, lane - N_EXPERTS, float(LANES)), axis=-1, keepdims=True)
        p_g = 1.0 / jnp.sum(jnp.where(is_coarse, jnp.exp(coarse - cmax), 0.0), axis=-1, keepdims=True)
        in_group = (lane_i < N_EXPERTS) & (group_of_lane == gidx)
        fine = jnp.where(in_group, lg, NEG)
        v1 = jnp.max(fine, axis=-1, keepdims=True)
        i1 = jnp.min(jnp.where(fine == v1, lane, float(LANES)), axis=-1, keepdims=True)
        fine2 = jnp.where(lane == i1, NEG, fine)
        v2 = jnp.max(fine2, axis=-1, keepdims=True)
        i2 = jnp.min(jnp.where(fine2 == v2, lane, float(LANES)), axis=-1, keepdims=True)
        e2 = jnp.exp(v2 - v1)
        picks.append((i1, i2, p_g / (1.0 + e2), p_g * e2 / (1.0 + e2)))

    tr = lax.broadcasted_iota(jnp.int32, (rb, rb), 0)
    tc = lax.broadcasted_iota(jnp.int32, (rb, rb), 1)
    strict = jnp.where(tr > tc, 1.0, 0.0).astype(BF16)
    onehots = [jnp.where((lane == i1) | (lane == i2), 1.0, 0.0).astype(F32) for i1, i2, _, _ in picks]
    befores = [jnp.dot(strict, oh.astype(BF16), preferred_element_type=F32) for oh in onehots]
    total = carry_ref[...]
    for rows, (i1, i2, w1, w2), oh, before in zip(blocks, picks, onehots, befores):
        before = before + total
        r1 = jnp.sum(jnp.where(lane == i1, before, 0.0), axis=-1, keepdims=True)
        r2 = jnp.sum(jnp.where(lane == i2, before, 0.0), axis=-1, keepdims=True)
        total = total + jnp.sum(oh, axis=0, keepdims=True)
        route = jnp.where(lane == 0, i1, 0.0)
        route = jnp.where(lane == 1, i2, route)
        route = jnp.where(lane == 2, r1, route)
        route = jnp.where(lane == 3, r2, route)
        route = jnp.where(lane == 4, w1, route)
        route = jnp.where(lane == 5, w2, route)
        route_ref[rows, :] = route
    carry_ref[...] = total
    cnt_ref[...] = jnp.broadcast_to(total, cnt_ref.shape)


def _outproj_router(h, ya, yb, wo, g, wr, br):
    T = h.shape[0]
    tm = min(ROW_TILE, T)
    const = lambda *shape: pl.BlockSpec(shape, lambda i: (0,) * len(shape))
    row = lambda width: pl.BlockSpec((tm, width), lambda i: (i, 0))
    return pl.pallas_call(
        functools.partial(_outproj_router_kernel, tm=tm),
        grid=(T // tm,),
        in_specs=[row(D_MODEL), row(ya.shape[1]), row(yb.shape[1]), const(*wo.shape),
                  const(1, D_MODEL), const(D_MODEL, 2 * LANES), const(1, LANES)],
        out_specs=[row(D_MODEL), row(D_MODEL), row(LANES), const(8, LANES)],
        out_shape=[jax.ShapeDtypeStruct((T, D_MODEL), F32),
                   jax.ShapeDtypeStruct((T, D_MODEL), F32),
                   jax.ShapeDtypeStruct((T, LANES), F32),
                   jax.ShapeDtypeStruct((8, LANES), F32)],
        scratch_shapes=[pltpu.VMEM((1, LANES), F32)],
        compiler_params=_cparams(1),
        name="outproj_router",
    )(h, ya, yb, wo, g, wr, br)


def _dispatch_kernel(pad_ref, pos_ref, m_ref, xs_hbm, zeros_ref, sem, pad_sem, *, tile):
    @pl.when(pl.program_id(0) == 0)
    def _():
        zeros_ref[...] = jnp.zeros_like(zeros_ref)

        def pad_copies(e, wait):
            first, n_single, n_block = pad_ref[0, e], pad_ref[1, e], pad_ref[2, e]
            for r in range(SUBLANES - 1):
                copy = pltpu.make_async_copy(zeros_ref.at[pl.ds(0, 1)], xs_hbm.at[pl.ds(first + r, 1)], pad_sem)
                pl.when(r < n_single)(copy.wait if wait else copy.start)
            done = first + n_single
            for bit in reversed(range(SUBLANE_BITS, PAD_BITS)):
                size = 1 << bit
                taken = (n_block & size) != 0
                copy = pltpu.make_async_copy(zeros_ref.at[pl.ds(0, size)],
                                             xs_hbm.at[pl.ds(pl.multiple_of(done, SUBLANES), size)], pad_sem)
                pl.when(taken)(copy.wait if wait else copy.start)
                done = done + jnp.where(taken, size, 0)

        def start(e, carry):
            pad_copies(e, False)
            return carry

        def finish(e, carry):
            pad_copies(e, True)
            return carry

        def tile_copy(j):
            return pltpu.make_async_copy(
                zeros_ref, xs_hbm.at[pl.ds(pl.multiple_of(j * EXPERT_TILE, EXPERT_TILE), EXPERT_TILE)], pad_sem)

        def start_tile(j, carry):
            tile_copy(j).start()
            return carry

        def finish_tile(j, carry):
            tile_copy(j).wait()
            return carry

        n_tiles = xs_hbm.shape[0] // EXPERT_TILE
        lax.fori_loop(0, N_EXPERTS, start, 0)
        lax.fori_loop(pad_ref[3, 0], n_tiles, start_tile, 0)
        lax.fori_loop(0, N_EXPERTS, finish, 0)
        lax.fori_loop(pad_ref[3, 0], n_tiles, finish_tile, 0)

    def issue(c, carry):
        base = pl.multiple_of(c * DMA_UNROLL, DMA_UNROLL)
        group = m_ref.at[pl.ds(base, DMA_UNROLL)]
        for u in range(DMA_UNROLL):
            for k in range(2):
                dst = xs_hbm.at[pl.ds(pos_ref[0, 0, 2 * (base + u) + k], 1)]
                pltpu.make_async_copy(group.at[pl.ds(u, 1)], dst, sem).start(priority=k)
        return carry

    lax.fori_loop(0, tile // DMA_UNROLL, issue, 0)
    for _ in range(2):
        pltpu.make_async_copy(m_ref, xs_hbm.at[pl.ds(0, tile)], sem).wait()


def _dispatch(pads, pos, m, n_rows):
    T, width = m.shape
    tile = min(DISPATCH_TILE, T)
    pos3 = pos.reshape(T // tile, 1, 2 * tile)
    grid_spec = pltpu.PrefetchScalarGridSpec(
        num_scalar_prefetch=1,
        grid=(T // tile,),
        in_specs=[pl.BlockSpec((1, 1, 2 * tile), lambda i, pads: (i, 0, 0), memory_space=pltpu.SMEM),
                  pl.BlockSpec((tile, width), lambda i, pads: (i, 0))],
        out_specs=pl.BlockSpec(memory_space=pl.ANY),
        scratch_shapes=[pltpu.VMEM((EXPERT_TILE, width), m.dtype),
                        pltpu.SemaphoreType.DMA(()), pltpu.SemaphoreType.DMA(())],
    )
    return pl.pallas_call(
        functools.partial(_dispatch_kernel, tile=tile),
        grid_spec=grid_spec,
        out_shape=jax.ShapeDtypeStruct((n_rows, width), m.dtype),
        compiler_params=pltpu.CompilerParams(dimension_semantics=("arbitrary",),
                                             has_side_effects=True),
        name="moe_dispatch",
    )(pads, pos3, m)


def _experts_kernel(te_ref, nused_ref, xs_ref, wg_ref, wu_ref, wd_ref, ys_ref, wgu_b, wd_b):
    j = pl.program_id(0)
    prev = te_ref[jnp.maximum(j - 1, 0)]

    @pl.when((j == 0) | (te_ref[j] != prev))
    def _():
        wgu_b[:, :D_EXPERT] = wg_ref[...].astype(BF16)
        wgu_b[:, D_EXPERT:] = wu_ref[...].astype(BF16)
        wd_b[...] = wd_ref[...].astype(BF16)

    @pl.when(j < nused_ref[0])
    def _():
        gu = jnp.dot(xs_ref[...].astype(BF16), wgu_b[...], preferred_element_type=F32)
        g = gu[:, :D_EXPERT]
        act = g * jax.nn.sigmoid(g) * gu[:, D_EXPERT:]
        ys_ref[...] = jnp.dot(act.astype(BF16), wd_b[...], preferred_element_type=F32)

    @pl.when(j >= nused_ref[0])
    def _():
        ys_ref[...] = jnp.zeros_like(ys_ref)


def _experts(layer, tile_expert, n_used, xs, wg, wu, wd):
    n_rows = xs.shape[0]
    nt = n_rows // EXPERT_TILE
    grid_spec = pltpu.PrefetchScalarGridSpec(
        num_scalar_prefetch=2,
        grid=(nt,),
        in_specs=[pl.BlockSpec((EXPERT_TILE, D_MODEL), lambda j, te, nu: (jnp.minimum(j, nu[0] - 1), 0)),
                  pl.BlockSpec((None, None, D_MODEL, D_EXPERT), lambda j, te, nu: (layer, te[j], 0, 0)),
                  pl.BlockSpec((None, None, D_MODEL, D_EXPERT), lambda j, te, nu: (layer, te[j], 0, 0)),
                  pl.BlockSpec((None, None, D_EXPERT, D_MODEL), lambda j, te, nu: (layer, te[j], 0, 0))],
        out_specs=pl.BlockSpec((EXPERT_TILE, D_MODEL), lambda j, te, nu: (j, 0)),
        scratch_shapes=[pltpu.VMEM((D_MODEL, 2 * D_EXPERT), BF16),
                        pltpu.VMEM((D_EXPERT, D_MODEL), BF16)],
    )
    return pl.pallas_call(
        _experts_kernel,
        grid_spec=grid_spec,
        out_shape=jax.ShapeDtypeStruct((n_rows, D_MODEL), F32),
        compiler_params=_cparams(1),
        name="moe_experts",
    )(tile_expert, n_used, xs, wg, wu, wd)


def _combine_ple_kernel(pos_ref, next_pos_ref, route_ref, h1_ref, ys_hbm, p_ref, wp_ref, g_ref, wgate_ref,
                        o_ref, ybuf, sems, *, tile):
    i = pl.program_id(0)
    n = pl.num_programs(0)
    slot = i % 2

    def gather(table, s):
        def issue(c, carry):
            base = pl.multiple_of(c * DMA_UNROLL, DMA_UNROLL)
            for k in range(2):
                group = ybuf.at[s, k, pl.ds(base, DMA_UNROLL)]
                for u in range(DMA_UNROLL):
                    pltpu.make_async_copy(ys_hbm.at[pl.ds(table[0, 0, 2 * (base + u) + k], 1)],
                                          group.at[pl.ds(u, 1)], sems.at[s]).start(priority=k)
            return carry

        lax.fori_loop(0, tile // DMA_UNROLL, issue, 0)

    @pl.when(i == 0)
    def _():
        gather(pos_ref, 0)

    @pl.when(i + 1 < n)
    def _():
        gather(next_pos_ref, 1 - slot)

    halves = [slice(0, tile // 2), slice(tile // 2, tile)]
    p_b = p_ref[...].astype(BF16)
    ples = [jnp.dot(p_b[rows], wp_ref[...], preferred_element_type=F32) for rows in halves]
    for k in range(2):
        pltpu.make_async_copy(ys_hbm.at[pl.ds(0, tile)], ybuf.at[slot, k], sems.at[slot]).wait()

    route = route_ref[...]
    h2s = [h1_ref[rows, :] + route[rows, 4:5] * ybuf[slot, 0, rows, :] + route[rows, 5:6] * ybuf[slot, 1, rows, :]
           for rows in halves]
    normed = [_rms(h2, g_ref[...]).astype(BF16) for h2 in h2s]
    gates = [jnp.dot(x, wgate_ref[...], preferred_element_type=F32) for x in normed]
    for rows, h2, gate, ple in zip(halves, h2s, gates, ples):
        o_ref[rows, :] = h2 + jax.nn.sigmoid(gate) * ple


def _combine_ple(layer, pos, route, h1, ys, p, wp, g, wgate):
    T = h1.shape[0]
    tile = min(GATHER_TILE, T)
    n = T // tile
    pos3 = pos.reshape(n, 1, 2 * tile)
    const = lambda *shape: pl.BlockSpec(shape, lambda i: (0,) * len(shape))
    row = lambda width: pl.BlockSpec((tile, width), lambda i: (i, 0))
    return pl.pallas_call(
        functools.partial(_combine_ple_kernel, tile=tile),
        grid=(n,),
        in_specs=[pl.BlockSpec((1, 1, 2 * tile), lambda i: (i, 0, 0), memory_space=pltpu.SMEM),
                  pl.BlockSpec((1, 1, 2 * tile), lambda i: (jnp.minimum(i + 1, n - 1), 0, 0),
                               memory_space=pltpu.SMEM),
                  row(LANES), row(D_MODEL), pl.BlockSpec(memory_space=pl.ANY),
                  pl.BlockSpec((None, tile, D_PLE), lambda i: (layer, i, 0)),
                  const(D_PLE, D_MODEL), const(1, D_MODEL), const(D_MODEL, D_MODEL)],
        out_specs=row(D_MODEL),
        out_shape=jax.ShapeDtypeStruct((T, D_MODEL), F32),
        scratch_shapes=[pltpu.VMEM((2, 2, tile, D_MODEL), F32), pltpu.SemaphoreType.DMA((2,))],
        compiler_params=_cparams(1),
        name="combine_ple",
    )(pos3, pos3, route, h1, ys, p, wp, g, wgate)


def _rope(z, cos, sin_lo, sin_hi):
    half = ROT_DIM // 2
    outs = []
    for j in range(z.shape[1] // LANES):
        zj = z[:, j * LANES:(j + 1) * LANES]
        outs.append(zj * cos + pltpu.roll(zj, LANES - half, 1) * sin_lo + pltpu.roll(zj, half, 1) * sin_hi)
    return outs[0] if len(outs) == 1 else jnp.concatenate(outs, axis=1)


def _proj_odd_kernel(h_ref, g_ref, w_ref, qg_ref, kg_ref, cos_ref, slo_ref, shi_ref,
                     q_ref, k_ref, v_ref, glu_ref):
    a = _rms(h_ref[...], g_ref[...]).astype(BF16)
    cos, slo, shi = cos_ref[...], slo_ref[...], shi_ref[...]
    v0 = SWA_WIDTH + KV_WIDTH
    d0 = v0 + KV_WIDTH
    zq = jnp.dot(a, w_ref[:, 0:SWA_WIDTH], preferred_element_type=F32)
    zk = jnp.dot(a, w_ref[:, SWA_WIDTH:v0], preferred_element_type=F32)
    zv = jnp.dot(a, w_ref[:, v0:d0], preferred_element_type=F32)
    zd = jnp.dot(a, w_ref[:, d0:d0 + 2 * CONV_CH], preferred_element_type=F32)
    q_ref[...] = (_rope(_head_rms(zq, qg_ref[...]), cos, slo, shi) * (HEAD_DIM ** -0.5)).astype(BF16)
    k_ref[...] = _rope(_head_rms(zk, kg_ref[...]), cos, slo, shi).astype(BF16)
    v_ref[...] = zv.astype(BF16)
    glu_ref[...] = zd[:, :CONV_CH] * jax.nn.sigmoid(zd[:, CONV_CH:])


def _proj_odd(h, g, w, qg, kg, cos, slo, shi, *, seq):
    T = h.shape[0]
    tm = min(ROW_TILE, seq)
    tps = seq // tm
    const = lambda *shape: pl.BlockSpec(shape, lambda i: (0,) * len(shape))
    row = lambda width: pl.BlockSpec((tm, width), lambda i: (i, 0))
    tab = pl.BlockSpec((tm, LANES), lambda i: (i % tps, 0))
    return pl.pallas_call(
        _proj_odd_kernel,
        grid=(T // tm,),
        in_specs=[row(D_MODEL), const(1, D_MODEL), const(*w.shape), const(1, SWA_WIDTH),
                  const(1, KV_WIDTH), tab, tab, tab],
        out_specs=[row(SWA_WIDTH), row(KV_WIDTH), row(KV_WIDTH), row(CONV_CH)],
        out_shape=[jax.ShapeDtypeStruct((T, SWA_WIDTH), BF16),
                   jax.ShapeDtypeStruct((T, KV_WIDTH), BF16),
                   jax.ShapeDtypeStruct((T, KV_WIDTH), BF16),
                   jax.ShapeDtypeStruct((T, CONV_CH), F32)],
        compiler_params=_cparams(1),
        name="proj_odd",
    )(h, g, w, qg, kg, cos, slo, shi)


def _swa_kernel(sink_ref, q_ref, k_ref, v_ref, o_ref, *, seq):
    lo = _lane_lo((1, LANES))
    nt = (((1,), (1,)), ((), ()))
    W = WINDOW
    qi = lax.broadcasted_iota(jnp.int32, (W, 2 * W), 0)
    kj = lax.broadcasted_iota(jnp.int32, (W, 2 * W), 1)
    band = (kj > qi) & (kj <= qi + W)

    def block(n, kstart, mask):
        qrow = pl.ds(pl.multiple_of(n * W, W), W)
        kwin = pl.ds(pl.multiple_of(kstart, W), 2 * W)
        ks = k_ref[kwin, :]
        vs = v_ref[kwin, :]
        n_tiles = SWA_WIDTH // LANES
        heads, scores = [], []
        for j in range(n_tiles):
            q = q_ref[qrow, j * LANES:(j + 1) * LANES]
            zero = jnp.zeros_like(q)
            for head, qh in ((j, jnp.where(lo, q, zero)), (n_tiles + j, jnp.where(lo, zero, q))):
                heads.append(head)
                scores.append(lax.dot_general(qh, ks, nt, preferred_element_type=F32))
        probs, sums = [], []
        for head, s in zip(heads, scores):
            s = jnp.where(mask, s, NEG)
            sink = sink_ref[head]
            m = jnp.maximum(jnp.max(s, axis=-1, keepdims=True), sink)
            p = jnp.exp(s - m)
            sums.append(jnp.sum(p, axis=-1, keepdims=True) + jnp.exp(sink - m))
            probs.append(p.astype(BF16))
        outs = [jnp.dot(p, vs, preferred_element_type=F32) / l for p, l in zip(probs, sums)]
        for j in range(n_tiles):
            o_ref[qrow, j * LANES:(j + 1) * LANES] = jnp.where(lo, outs[2 * j], outs[2 * j + 1]).astype(BF16)

    block(0, 0, kj <= qi)

    def body(n, carry):
        block(n, (n - 1) * W, band)
        return carry

    lax.fori_loop(1, seq // W, body, 0)


def _swa_attention(sinks, q, k, v, *, seq):
    T = q.shape[0]
    B = T // seq
    return pl.pallas_call(
        functools.partial(_swa_kernel, seq=seq),
        grid=(B,),
        in_specs=[pl.BlockSpec(memory_space=pltpu.SMEM),
                  pl.BlockSpec((seq, SWA_WIDTH), lambda b: (b, 0)),
                  pl.BlockSpec((seq, KV_WIDTH), lambda b: (b, 0)),
                  pl.BlockSpec((seq, KV_WIDTH), lambda b: (b, 0))],
        out_specs=pl.BlockSpec((seq, SWA_WIDTH), lambda b: (b, 0)),
        out_shape=jax.ShapeDtypeStruct((T, SWA_WIDTH), BF16),
        compiler_params=_cparams(1),
        name="swa_attention",
    )(sinks, q, k, v)


def _conv_kernel(prev_ref, cur_ref, w_ref, g_ref, b_ref, o_ref, shift_ref, *, tile, sub):
    r = pl.program_id(1)
    rows = CONV_HALO + tile
    tail = prev_ref[tile - CONV_HALO:, :]
    shift_ref[0, 0:CONV_HALO, :] = jnp.where(r > 0, tail, jnp.zeros_like(tail))
    shift_ref[0, CONV_HALO:rows, :] = cur_ref[...]
    shift_ref[0, rows:rows + SUBLANES, :] = jnp.zeros((SUBLANES, CONV_CH), F32)
    for o in range(1, SUBLANES):
        shift_ref[o, 0:rows, :] = shift_ref[0, o:o + rows, :]
    w = w_ref[...]
    first = CONV_HALO - (CONV_WIDTH - 1)
    for s in range(tile // sub):
        acc = jnp.zeros((sub, CONV_CH), F32)
        for j in range(CONV_WIDTH):
            start = s * sub + first + j
            o = start % SUBLANES
            acc = acc + shift_ref[o, start - o:start - o + sub, :] * w[j:j + 1, :]
        y = _layernorm(acc, g_ref[...], b_ref[...])
        o_ref[s * sub:(s + 1) * sub, :] = (y * jax.nn.sigmoid(y)).astype(BF16)


def _conv_module(glu, w, g, b, *, seq):
    T = glu.shape[0]
    B = T // seq
    tile = min(CONV_TILE, seq)
    nr = seq // tile
    const = lambda *shape: pl.BlockSpec(shape, lambda bb, r: (0,) * len(shape))
    return pl.pallas_call(
        functools.partial(_conv_kernel, tile=tile, sub=64),
        grid=(B, nr),
        in_specs=[pl.BlockSpec((tile, CONV_CH), lambda bb, r: (bb * nr + jnp.maximum(r - 1, 0), 0)),
                  pl.BlockSpec((tile, CONV_CH), lambda bb, r: (bb * nr + r, 0)),
                  const(CONV_WIDTH, CONV_CH), const(1, CONV_CH), const(1, CONV_CH)],
        out_specs=pl.BlockSpec((tile, CONV_CH), lambda bb, r: (bb * nr + r, 0)),
        out_shape=jax.ShapeDtypeStruct((T, CONV_CH), BF16),
        scratch_shapes=[pltpu.VMEM((SUBLANES, CONV_HALO + tile + SUBLANES, CONV_CH), F32)],
        compiler_params=_cparams(2),
        name="conv_module",
    )(glu, glu, w, g, b)


def _routing_tables(route, counts, n_tiles):
    e = route[:, 0:2].astype(jnp.int32)
    rank = route[:, 2:4].astype(jnp.int32)
    cnt = counts[0, :N_EXPERTS].astype(jnp.int32)
    tiles = (cnt + EXPERT_TILE - 1) // EXPERT_TILE
    tile_end = jnp.cumsum(tiles)
    offset = (tile_end - tiles) * EXPERT_TILE
    onehot = e[:, :, None] == jnp.arange(N_EXPERTS, dtype=jnp.int32)
    pos = (rank + jnp.sum(jnp.where(onehot, offset, 0), axis=-1)).reshape(-1)
    n_used = tile_end[-1]
    tile_id = jnp.minimum(jnp.arange(n_tiles, dtype=jnp.int32), n_used - 1)
    tile_expert = jnp.sum((tile_end[None, :] <= tile_id[:, None]).astype(jnp.int32), axis=1)
    first_pad = offset + cnt
    n_single = (-first_pad) % SUBLANES
    pads = jnp.stack([first_pad, n_single, tiles * EXPERT_TILE - cnt - n_single,
                      jnp.broadcast_to(tile_end[-1], cnt.shape)])
    return pos, pads, tile_expert, n_used.reshape(1).astype(jnp.int32)


def _moe_ple(h, ya, yb, wo, layer, norm_ffn, wr, br, wg, wu, wd, p, wp, ple_norm, wgate):
    T = h.shape[0]
    n_tiles = (2 * T) // EXPERT_TILE + N_EXPERTS
    h1, m, route, counts = _outproj_router(h, ya, yb, wo, norm_ffn, wr, br)
    pos, pads, tile_expert, n_used = _routing_tables(route, counts, n_tiles)
    xs = _dispatch(pads, pos, m, n_tiles * EXPERT_TILE)
    ys = _experts(layer, tile_expert, n_used, xs, wg, wu, wd)
    return _combine_ple(layer, pos, route, h1, ys, p, wp, ple_norm, wgate)


def _router_weights(w_coarse, b_coarse, w_fine, b_fine):
    wf = w_fine.transpose(1, 0, 2).reshape(D_MODEL, N_EXPERTS)
    wr = jnp.concatenate([wf, w_coarse, jnp.zeros((D_MODEL, LANES - N_EXPERTS - N_GROUPS), F32)], axis=1)
    br = jnp.concatenate([b_fine.reshape(-1), b_coarse, jnp.zeros((LANES - N_EXPERTS - N_GROUPS,), F32)])
    w_hi = wr.astype(BF16)
    w_lo = (wr - w_hi.astype(F32)).astype(BF16)
    return jnp.concatenate([w_hi, w_lo], axis=1), br.reshape(1, LANES)


def _rope_tables(seq):
    half = ROT_DIM // 2
    inv_freq = ROPE_THETA ** (-jnp.arange(half, dtype=F32) * 2.0 / ROT_DIM)
    ang = jnp.arange(seq, dtype=F32)[:, None] * inv_freq[None, :]
    cos, sin = jnp.cos(ang), jnp.sin(ang)
    zeros = jnp.zeros((seq, HEAD_DIM - ROT_DIM), F32)
    z8 = jnp.zeros((seq, half), F32)
    cos_h = jnp.concatenate([cos, cos, zeros + 1.0], axis=1)
    slo_h = jnp.concatenate([-sin, z8, zeros], axis=1)
    shi_h = jnp.concatenate([z8, sin, zeros], axis=1)
    two = lambda t: jnp.concatenate([t, t], axis=1)
    return two(cos_h), two(slo_h), two(shi_h)


def kernel(x, p, norm_mix, even_w_in, fox_b_f, gmlp_ln_g, gmlp_ln_b, gmlp_w_s, gmlp_b_s, fox_q_norm, fox_k_norm, even_w_out, odd_w_in, swa_q_norm, swa_k_norm, swa_sinks, conv_w, conv_ln_g, conv_ln_b, odd_w_out, norm_ffn, moe_w_coarse, moe_b_coarse, moe_w_fine, moe_b_fine, moe_w_gate, moe_w_up, moe_w_down, ple_w_proj, ple_norm, ple_w_gate):
    B, S, D = x.shape
    T = B * S
    h = x.reshape(T, D)
    p = p.reshape(p.shape[0], T, D_PLE)
    row = lambda v: v.reshape(1, -1)

    def moe_args(i):
        wr, br = _router_weights(moe_w_coarse[i], moe_b_coarse[i], moe_w_fine[i], moe_b_fine[i])
        return (i, row(norm_ffn[i]), wr, br, moe_w_gate, moe_w_up, moe_w_down, p,
                ple_w_proj[i].astype(BF16), row(ple_norm[i]), ple_w_gate[i].astype(BF16))

    n_main = 2 * GMLP_WIDTH + 3 * FOX_WIDTH
    w_f = jnp.pad(jnp.tile(even_w_in[0][:, n_main:], (1, BIAS_PIECES)),
                  ((0, 0), (0, LANES - BIAS_PIECES * FOX_HEADS)))
    w_in = jnp.concatenate([even_w_in[0][:, :n_main], w_f], axis=1).astype(BF16)
    b_f = jnp.pad(jnp.tile(fox_b_f[0], BIAS_PIECES), (0, LANES - BIAS_PIECES * FOX_HEADS)).reshape(1, LANES)
    bs_full = jnp.repeat(gmlp_b_s[0].T, HEAD_DIM, axis=1)
    ya, q, k, v, xq, xk = _proj_even(
        h, row(norm_mix[0]), w_in, b_f, row(gmlp_ln_g[0]), row(gmlp_ln_b[0]), gmlp_w_s[0], bs_full,
        row(jnp.tile(fox_q_norm[0], FOX_HEADS)), row(jnp.tile(fox_k_norm[0], FOX_HEADS)), seq=S)
    yb = _fox_attention(q, xq, k, xk, v, seq=S)
    h = _moe_ple(h, ya, yb, even_w_out[0].astype(BF16), *moe_args(0))

    order = jnp.array([0, 4, 1, 5, 2, 6, 3, 7])
    cols = (order[:, None] * HEAD_DIM + jnp.arange(HEAD_DIM)[None, :]).reshape(-1)
    w_odd = jnp.concatenate([odd_w_in[0][:, :SWA_WIDTH][:, cols], odd_w_in[0][:, SWA_WIDTH:]], axis=1).astype(BF16)
    w_out_odd = jnp.concatenate([odd_w_out[0][:SWA_WIDTH][cols], odd_w_out[0][SWA_WIDTH:]], axis=0).astype(BF16)
    cos, slo, shi = _rope_tables(S)
    q, k, v, glu = _proj_odd(h, row(norm_mix[1]), w_odd, row(jnp.tile(swa_q_norm[0], 8)),
                             row(jnp.tile(swa_k_norm[0], 2)), cos, slo, shi, seq=S)
    yc = _swa_attention(swa_sinks[0], q, k, v, seq=S)
    yd = _conv_module(glu, conv_w[0], row(conv_ln_g[0]), row(conv_ln_b[0]), seq=S)
    h = _moe_ple(h, yc, yd, w_out_odd, *moe_args(1))
    return h.reshape(B, S, D)
```

```python
import functools

import jax
import jax.numpy as jnp
import numpy as np
from jax import lax
from jax.experimental import pallas as pl
from jax.experimental.pallas import tpu as pltpu

F32 = jnp.float32
BF16 = jnp.bfloat16
HIGHEST = lax.Precision.HIGHEST

D_MODEL = 1024
HEAD_DIM = 64
LANES = 128
GMLP_WIDTH = 512
CHUNK = 128
FOX_WIDTH = 512
FOX_HEADS = 8
SWA_WIDTH = 512
KV_WIDTH = 128
WINDOW = 128
CONV_CH = 512
CONV_WIDTH = 31
CONV_HALO = 32
ROPE_THETA = 500000.0
ROT_DIM = 16
N_GROUPS = 4
EXPERTS_PER_GROUP = 8
N_EXPERTS = 32
D_EXPERT = 256
D_PLE = 256
EPS = 1e-6
NEG = -1e30
LOG2E = 1.4426950408889634
BIAS_PIECES = 3

EXPERT_TILE = 512
PAD_BITS = 9
SUBLANES = 8
SUBLANE_BITS = 3
ROW_TILE = 512
ROUTER_BLOCKS = 4
GATHER_TILE = 256
DMA_UNROLL = 8
ATTN_TILE = 256
FOX_PAIRS_PER_STEP = 2
CONV_TILE = 256
VMEM_LIMIT = 56 * 1024 * 1024


def _cparams(n_axes=1, flags=None):
    return pltpu.CompilerParams(dimension_semantics=("arbitrary",) * n_axes,
                                vmem_limit_bytes=VMEM_LIMIT, flags=flags)


def _rms(x, gain):
    return x * lax.rsqrt(jnp.mean(x * x, axis=-1, keepdims=True) + EPS) * gain


def _layernorm(x, g, b):
    mu = jnp.mean(x, axis=-1, keepdims=True)
    xc = x - mu
    var = jnp.mean(xc * xc, axis=-1, keepdims=True)
    return xc * lax.rsqrt(var + EPS) * g + b


def _head_rms(z, gain):
    r = (lax.broadcasted_iota(jnp.int32, (2 * LANES, LANES), 0) % LANES) // HEAD_DIM
    c = lax.broadcasted_iota(jnp.int32, (2 * LANES, LANES), 1) // HEAD_DIM
    bd = jnp.where(r == c, 1.0 / HEAD_DIM, 0.0).astype(BF16)
    outs = []
    for j in range(z.shape[1] // LANES):
        zj = z[:, j * LANES:(j + 1) * LANES]
        sq = zj * zj
        sq_hi = sq.astype(BF16)
        sq_lo = (sq - sq_hi.astype(F32)).astype(BF16)
        ms = jnp.dot(jnp.concatenate([sq_hi, sq_lo], axis=1), bd, preferred_element_type=F32)
        outs.append(zj * lax.rsqrt(ms + EPS))
    zn = outs[0] if len(outs) == 1 else jnp.concatenate(outs, axis=1)
    return zn * gain


def _lane_lo(shape):
    return (lax.broadcasted_iota(jnp.int32, shape, len(shape) - 1) % LANES) < HEAD_DIM


def _proj_even_kernel(h_ref, g_ref, w_ref, bf_ref, lng_ref, lnb_ref, ws_ref, bs_ref, qg_ref, kg_ref,
                      pq_ref, pk_ref, oneq_ref, onek_ref,
                      ya_ref, q_ref, k_ref, v_ref, xq_ref, xk_ref, carry_ref, *, tm, tiles_per_seq):
    i = pl.program_id(0)
    a = _rms(h_ref[...], g_ref[...]).astype(BF16)

    q0 = 2 * GMLP_WIDTH
    za = jnp.dot(a, w_ref[:, 0:q0], preferred_element_type=F32)
    zq = jnp.dot(a, w_ref[:, q0:q0 + FOX_WIDTH], preferred_element_type=F32)
    zk = jnp.dot(a, w_ref[:, q0 + FOX_WIDTH:q0 + 2 * FOX_WIDTH], preferred_element_type=F32)
    zv = jnp.dot(a, w_ref[:, q0 + 2 * FOX_WIDTH:q0 + 3 * FOX_WIDTH], preferred_element_type=F32)
    zf = jnp.dot(a, w_ref[:, q0 + 3 * FOX_WIDTH:], preferred_element_type=F32) + bf_ref[...]

    q_ref[...] = (_head_rms(zq, qg_ref[...]) * (LOG2E * HEAD_DIM ** -0.5)).astype(BF16)
    k_ref[...] = _head_rms(zk, kg_ref[...]).astype(BF16)
    v_ref[...] = zv.astype(BF16)

    rr = lax.broadcasted_iota(jnp.int32, (CHUNK, CHUNK), 0)
    cc = lax.broadcasted_iota(jnp.int32, (CHUNK, CHUNK), 1)

    ls = jnp.minimum(zf, 0.0) - jnp.log(1.0 + jnp.exp(-jnp.abs(zf)))

    @pl.when(i % tiles_per_seq == 0)
    def _():
        carry_ref[...] = jnp.zeros_like(carry_ref)

    def split3(x):
        hi = x.astype(BF16)
        r1 = x - hi.astype(F32)
        mid = r1.astype(BF16)
        return hi, mid, (r1 - mid.astype(F32)).astype(BF16)

    tri = jnp.where(rr >= cc, 1.0, 0.0).astype(BF16)
    pieces = jnp.concatenate(split3(ls), axis=1)
    running = carry_ref[...]
    blocks = []
    for b in range(tm // CHUNK):
        d = jnp.dot(tri, pieces[b * CHUNK:(b + 1) * CHUNK, :], preferred_element_type=F32)
        blk = (d[:, :LANES] + d[:, LANES:2 * LANES]) + d[:, 2 * LANES:] + running
        running = blk[CHUNK - 1:CHUNK, :]
        blocks.append(blk)
    carry_ref[...] = running
    c = jnp.concatenate(blocks, axis=0)

    za = jax.nn.gelu(za)
    u = za[:, :GMLP_WIDTH]
    vln = _layernorm(za[:, GMLP_WIDTH:], lng_ref[...], lnb_ref[...]).astype(BF16)
    lo = _lane_lo((CHUNK, LANES))
    for j in range(GMLP_WIDTH // LANES):
        w_a = jnp.where(rr >= cc, ws_ref[2 * j], 0.0).astype(BF16)
        w_b = jnp.where(rr >= cc, ws_ref[2 * j + 1], 0.0).astype(BF16)
        cols = slice(j * LANES, (j + 1) * LANES)
        for blk in range(tm // CHUNK):
            rows = slice(blk * CHUNK, (blk + 1) * CHUNK)
            vp = vln[rows, cols]
            mixed = jnp.where(lo, jnp.dot(w_a, vp, preferred_element_type=F32),
                              jnp.dot(w_b, vp, preferred_element_type=F32)) + bs_ref[:, cols]
            ya_ref[rows, cols] = (u[rows, cols] * mixed).astype(BF16)

    hi, mid, low = split3(c * LOG2E)
    group = lax.broadcasted_iota(jnp.int32, (tm, LANES), 1) // FOX_HEADS
    sel = jnp.where(group == 0, hi, jnp.where(group == 1, mid, low))
    xq_ref[...] = (jnp.dot(sel, pq_ref[...], preferred_element_type=F32) + oneq_ref[...]).astype(BF16)
    xk_ref[...] = (jnp.dot(sel, pk_ref[...], preferred_element_type=F32) + onek_ref[...]).astype(BF16)


def _proj_even(h, g, w, bf, lng, lnb, ws, bs_full, qg, kg, *, seq):
    T = h.shape[0]
    tm = min(ROW_TILE, seq)
    n_in = w.shape[1]
    const = lambda *shape: pl.BlockSpec(shape, lambda i: (0,) * len(shape))
    row = lambda width: pl.BlockSpec((tm, width), lambda i: (i, 0))
    tps = seq // tm
    return pl.pallas_call(
        functools.partial(_proj_even_kernel, tm=tm, tiles_per_seq=tps),
        grid=(T // tm,),
        in_specs=[row(D_MODEL), const(1, D_MODEL), const(D_MODEL, n_in), const(1, LANES),
                  const(1, GMLP_WIDTH), const(1, GMLP_WIDTH), const(8, CHUNK, CHUNK),
                  const(CHUNK, GMLP_WIDTH), const(1, FOX_WIDTH), const(1, FOX_WIDTH),
                  const(LANES, FOX_WIDTH), const(LANES, FOX_WIDTH),
                  const(1, FOX_WIDTH), const(1, FOX_WIDTH)],
        out_specs=[row(GMLP_WIDTH)] + [row(FOX_WIDTH)] * 5,
        out_shape=[jax.ShapeDtypeStruct((T, GMLP_WIDTH), BF16)]
                  + [jax.ShapeDtypeStruct((T, FOX_WIDTH), BF16)] * 5,
        scratch_shapes=[pltpu.VMEM((1, LANES), F32)],
        compiler_params=_cparams(1),
        name="proj_even",
    )(h, g, w, bf, lng, lnb, ws, bs_full, qg, kg, *_bias_placement())


def _bias_placement():
    pq = np.zeros((LANES, FOX_WIDTH), np.float32)
    pk = np.zeros((LANES, FOX_WIDTH), np.float32)
    oneq = np.zeros((1, FOX_WIDTH), np.float32)
    onek = np.zeros((1, FOX_WIDTH), np.float32)
    for head in range(FOX_HEADS):
        base = (head // 2) * LANES + (HEAD_DIM if head % 2 == 0 else 0)
        for piece in range(BIAS_PIECES):
            pq[piece * FOX_HEADS + head, base + piece] = 1.0
            onek[0, base + piece] = 1.0
            pk[piece * FOX_HEADS + head, base + BIAS_PIECES + piece] = -1.0
            oneq[0, base + BIAS_PIECES + piece] = 1.0
    return (jnp.asarray(pq, BF16), jnp.asarray(pk, BF16), jnp.asarray(oneq), jnp.asarray(onek))


def _fox_tile(n_tiles, q_ref, xq_ref, k_ref, xk_ref, v_ref, o_ref, *, tq):
    half = tq // 2
    lo = _lane_lo((1, LANES))
    rr = lax.broadcasted_iota(jnp.int32, (half, tq), 0)
    cc = lax.broadcasted_iota(jnp.int32, (half, tq), 1)
    nt = (((1,), (1,)), ((), ()))
    pairs = [slice(pp * LANES, (pp + 1) * LANES) for pp in range(FOX_PAIRS_PER_STEP)]
    chains = [(pp, hf, head) for pp in range(len(pairs)) for hf in range(2) for head in range(2)]
    q_aug = {}
    for pp, cols in enumerate(pairs):
        for hf in range(2):
            rows = slice(hf * half, (hf + 1) * half)
            q, xq = q_ref[rows, cols], xq_ref[rows, cols]
            q_aug[pp, hf, 0], q_aug[pp, hf, 1] = jnp.where(lo, q, xq), jnp.where(lo, xq, q)

    def score_products(j):
        rows = slice(j * tq, (j + 1) * tq)
        k_aug = {}
        for pp, cols in enumerate(pairs):
            ks, xk = k_ref[rows, cols], xk_ref[rows, cols]
            k_aug[pp, 0], k_aug[pp, 1] = jnp.where(lo, ks, xk), jnp.where(lo, xk, ks)
        return [lax.dot_general(q_aug[pp, hf, head], k_aug[pp, head], nt, preferred_element_type=F32)
                for pp, hf, head in chains]

    maxes = [jnp.full((half, 1), NEG, F32) for _ in chains]
    accs = [jnp.zeros((half, LANES), F32) for _ in chains]
    scores = score_products(0)
    for j in range(n_tiles):
        next_scores = score_products(j + 1) if j + 1 < n_tiles else None
        v_aug = {}
        for pp, cols in enumerate(pairs):
            vs = v_ref[j * tq:(j + 1) * tq, cols]
            one = jnp.ones_like(vs)
            v_aug[pp, 0], v_aug[pp, 1] = jnp.where(lo, vs, one), jnp.where(lo, one, vs)
        for c, (pp, hf, head) in enumerate(chains):
            s = scores[c]
            if j == n_tiles - 1:
                s = jnp.where(cc <= rr + hf * half, s, NEG)
            n = jnp.maximum(maxes[c], jnp.max(s, axis=-1, keepdims=True))
            p = jnp.exp2(s - n).astype(BF16)
            accs[c] = accs[c] * jnp.exp2(maxes[c] - n) + jnp.dot(p, v_aug[pp, head], preferred_element_type=F32)
            maxes[c] = n
        scores = next_scores
    for c in range(0, len(chains), 2):
        pp, hf, _ = chains[c]
        norm = [acc / pltpu.roll(acc, HEAD_DIM, 1) for acc in accs[c:c + 2]]
        o_ref[hf * half:(hf + 1) * half, pairs[pp]] = jnp.where(lo, norm[0], norm[1]).astype(BF16)


def _fox_kernel(q_ref, xq_ref, k_ref, xk_ref, v_ref, o_ref, *, tq, nq):
    i = pl.program_id(2)
    for c in range(nq):
        pl.when(i == c)(functools.partial(_fox_tile, c + 1, q_ref, xq_ref, k_ref, xk_ref, v_ref, o_ref, tq=tq))


def _fox_attention(q, xq, k, xk, v, *, seq):
    T = q.shape[0]
    B = T // seq
    tq = min(ATTN_TILE, seq)
    nq = seq // tq
    width = FOX_PAIRS_PER_STEP * LANES
    tile = pl.BlockSpec((tq, width), lambda b, hp, i: (b * nq + i, hp))
    whole = pl.BlockSpec((seq, width), lambda b, hp, i: (b, hp))
    return pl.pallas_call(
        functools.partial(_fox_kernel, tq=tq, nq=nq),
        grid=(B, FOX_WIDTH // width, nq),
        in_specs=[tile, tile, whole, whole, whole],
        out_specs=tile,
        out_shape=jax.ShapeDtypeStruct((T, FOX_WIDTH), BF16),
        compiler_params=_cparams(3),
        name="fox_attention",
    )(q, xq, k, xk, v)


def _outproj_router_kernel(h_ref, ya_ref, yb_ref, wo_ref, g_ref, wr_ref, br_ref,
                           h1_ref, m_ref, route_ref, cnt_ref, carry_ref, *, tm):
    i = pl.program_id(0)
    half = wo_ref.shape[0] // 2
    nb = ROUTER_BLOCKS
    rb = tm // nb
    blocks = [slice(b * rb, (b + 1) * rb) for b in range(nb)]

    @pl.when(i == 0)
    def _():
        carry_ref[...] = jnp.zeros_like(carry_ref)

    mixes = [jnp.dot(ya_ref[rows, :], wo_ref[0:half, :], preferred_element_type=F32)
             + jnp.dot(yb_ref[rows, :], wo_ref[half:, :], preferred_element_type=F32) for rows in blocks]
    ms = []
    for rows, mix in zip(blocks, mixes):
        h1 = h_ref[rows, :] + mix
        h1_ref[rows, :] = h1
        m = _rms(h1, g_ref[...])
        m_ref[rows, :] = m
        ms.append(m)

    logits = []
    for m in ms:
        m_hi = m.astype(BF16)
        m_lo = (m - m_hi.astype(F32)).astype(BF16)
        hh = jnp.dot(m_hi, wr_ref[...], preferred_element_type=F32)
        lh = jnp.dot(m_lo, wr_ref[:, :LANES], preferred_element_type=F32)
        logits.append(hh[:, :LANES] + (hh[:, LANES:] + lh) + br_ref[...])

    lane_i = lax.broadcasted_iota(jnp.int32, (rb, LANES), 1)
    lane = lane_i.astype(F32)
    group_of_lane = (lane_i // EXPERTS_PER_GROUP).astype(F32)
    is_coarse = (lane_i >= N_EXPERTS) & (lane_i < N_EXPERTS + N_GROUPS)
    picks = []
    for lg in logits:
        coarse = jnp.where(is_coarse, lg, NEG)
        cmax = jnp.max(coarse, axis=-1, keepdims=True)
        gidx = jnp.min(jnp.where(coarse == cmax, lane - N_EXPERTS, float(LANES)), axis=-1, keepdims=True)
        p_g = 1.0 / jnp.sum(jnp.where(is_coarse, jnp.exp(coarse - cmax), 0.0), axis=-1, keepdims=True)
        in_group = (lane_i < N_EXPERTS) & (group_of_lane == gidx)
        fine = jnp.where(in_group, lg, NEG)
        v1 = jnp.max(fine, axis=-1, keepdims=True)
        i1 = jnp.min(jnp.where(fine == v1, lane, float(LANES)), axis=-1, keepdims=True)
        fine2 = jnp.where(lane == i1, NEG, fine)
        v2 = jnp.max(fine2, axis=-1, keepdims=True)
        i2 = jnp.min(jnp.where(fine2 == v2, lane, float(LANES)), axis=-1, keepdims=True)
        e2 = jnp.exp(v2 - v1)
        picks.append((i1, i2, p_g / (1.0 + e2), p_g * e2 / (1.0 + e2)))

    tr = lax.broadcasted_iota(jnp.int32, (rb, rb), 0)
    tc = lax.broadcasted_iota(jnp.int32, (rb, rb), 1)
    strict = jnp.where(tr > tc, 1.0, 0.0).astype(BF16)
    onehots = [jnp.where((lane == i1) | (lane == i2), 1.0, 0.0).astype(F32) for i1, i2, _, _ in picks]
    befores = [jnp.dot(strict, oh.astype(BF16), preferred_element_type=F32) for oh in onehots]
    total = carry_ref[...]
    for rows, (i1, i2, w1, w2), oh, before in zip(blocks, picks, onehots, befores):
        before = before + total
        r1 = jnp.sum(jnp.where(lane == i1, before, 0.0), axis=-1, keepdims=True)
        r2 = jnp.sum(jnp.where(lane == i2, before, 0.0), axis=-1, keepdims=True)
        total = total + jnp.sum(oh, axis=0, keepdims=True)
        route = jnp.where(lane == 0, i1, 0.0)
        route = jnp.where(lane == 1, i2, route)
        route = jnp.where(lane == 2, r1, route)
        route = jnp.where(lane == 3, r2, route)
        route = jnp.where(lane == 4, w1, route)
        route = jnp.where(lane == 5, w2, route)
        route_ref[rows, :] = route
    carry_ref[...] = total
    cnt_ref[...] = jnp.broadcast_to(total, cnt_ref.shape)


def _outproj_router(h, ya, yb, wo, g, wr, br):
    T = h.shape[0]
    tm = min(ROW_TILE, T)
    const = lambda *shape: pl.BlockSpec(shape, lambda i: (0,) * len(shape))
    row = lambda width: pl.BlockSpec((tm, width), lambda i: (i, 0))
    return pl.pallas_call(
        functools.partial(_outproj_router_kernel, tm=tm),
        grid=(T // tm,),
        in_specs=[row(D_MODEL), row(ya.shape[1]), row(yb.shape[1]), const(*wo.shape),
                  const(1, D_MODEL), const(D_MODEL, 2 * LANES), const(1, LANES)],
        out_specs=[row(D_MODEL), row(D_MODEL), row(LANES), const(8, LANES)],
        out_shape=[jax.ShapeDtypeStruct((T, D_MODEL), F32),
                   jax.ShapeDtypeStruct((T, D_MODEL), F32),
                   jax.ShapeDtypeStruct((T, LANES), F32),
                   jax.ShapeDtypeStruct((8, LANES), F32)],
        scratch_shapes=[pltpu.VMEM((1, LANES), F32)],
        compiler_params=_cparams(1),
        name="outproj_router",
    )(h, ya, yb, wo, g, wr, br)


def _slot_tokens_kernel(order_ref, tok_ref, *, n_assign):
    def body(c, carry):
        for u in range(DMA_UNROLL):
            a = c * DMA_UNROLL + u
            tok_ref[order_ref[a]] = a // 2
        return carry

    lax.fori_loop(0, n_assign // DMA_UNROLL, body, 0)


def _slot_tokens(order):
    smem = pl.BlockSpec(memory_space=pltpu.SMEM)
    return pl.pallas_call(
        functools.partial(_slot_tokens_kernel, n_assign=order.shape[0]),
        in_specs=[smem],
        out_specs=smem,
        out_shape=jax.ShapeDtypeStruct(order.shape, jnp.int32),
        name="moe_slot_tokens",
    )(order)


def _experts_kernel(te_ref, rows_ref, first_ref, nused_ref, tok_ref, m_hbm, wg_ref, wu_ref, wd_ref, ys_ref,
                    xbuf, sems, wgu_b, wd_b):
    j = pl.program_id(0)
    n = pl.num_programs(0)
    slot = j % 2

    def row_copy(tile, s, group, u):
        base = pl.multiple_of(group * DMA_UNROLL, DMA_UNROLL)
        src = m_hbm.at[pl.ds(tok_ref[first_ref[tile] + base + u], 1)]
        return pltpu.make_async_copy(src, xbuf.at[s, pl.ds(base, DMA_UNROLL)].at[pl.ds(u, 1)], sems.at[s])

    def gather(tile, s):
        groups = rows_ref[tile] // DMA_UNROLL

        def issue(c, carry):
            for u in range(DMA_UNROLL):
                row_copy(tile, s, c, u).start(priority=u % 2)
            return carry

        lax.fori_loop(0, groups, issue, 0)
        for u in range(DMA_UNROLL - 1):
            pl.when(u < rows_ref[tile] % DMA_UNROLL)(lambda u=u: row_copy(tile, s, groups, u).start())

    def wait(tile, s):
        for bit in range(SUBLANE_BITS, PAD_BITS + 1):
            size = 1 << bit
            copy = pltpu.make_async_copy(m_hbm.at[pl.ds(0, size)], xbuf.at[s, pl.ds(0, size)], sems.at[s])
            pl.when((rows_ref[tile] & size) != 0)(copy.wait)
        for u in range(DMA_UNROLL - 1):
            copy = pltpu.make_async_copy(m_hbm.at[pl.ds(0, 1)], xbuf.at[s, pl.ds(0, 1)], sems.at[s])
            pl.when(u < rows_ref[tile] % DMA_UNROLL)(copy.wait)

    @pl.when(j == 0)
    def _():
        xbuf[...] = jnp.zeros_like(xbuf)
        gather(0, 0)

    @pl.when(j + 1 < n)
    def _():
        gather(j + 1, 1 - slot)

    prev = te_ref[jnp.maximum(j - 1, 0)]

    @pl.when((j == 0) | (te_ref[j] != prev))
    def _():
        wgu_b[:, :D_EXPERT] = wg_ref[...].astype(BF16)
        wgu_b[:, D_EXPERT:] = wu_ref[...].astype(BF16)
        wd_b[...] = wd_ref[...].astype(BF16)

    @pl.when(j < nused_ref[0])
    def _():
        wait(j, slot)
        gu = jnp.dot(xbuf[slot].astype(BF16), wgu_b[...], preferred_element_type=F32)
        g = gu[:, :D_EXPERT]
        act = g * jax.nn.sigmoid(g) * gu[:, D_EXPERT:]
        ys_ref[...] = jnp.dot(act.astype(BF16), wd_b[...], preferred_element_type=F32)

    @pl.when(j >= nused_ref[0])
    def _():
        ys_ref[...] = jnp.zeros_like(ys_ref)


def _experts(layer, tile_expert, tile_rows, tile_first, n_used, slot_token, m, wg, wu, wd):
    assert DMA_UNROLL == 1 << SUBLANE_BITS
    nt = tile_expert.shape[0]
    grid_spec = pltpu.PrefetchScalarGridSpec(
        num_scalar_prefetch=5,
        grid=(nt,),
        in_specs=[pl.BlockSpec(memory_space=pl.ANY),
                  pl.BlockSpec((None, None, D_MODEL, D_EXPERT), lambda j, te, *_: (layer, te[j], 0, 0)),
                  pl.BlockSpec((None, None, D_MODEL, D_EXPERT), lambda j, te, *_: (layer, te[j], 0, 0)),
                  pl.BlockSpec((None, None, D_EXPERT, D_MODEL), lambda j, te, *_: (layer, te[j], 0, 0))],
        out_specs=pl.BlockSpec((EXPERT_TILE, D_MODEL), lambda j, *_: (j, 0)),
        scratch_shapes=[pltpu.VMEM((2, EXPERT_TILE, D_MODEL), F32),
                        pltpu.SemaphoreType.DMA((2,)),
                        pltpu.VMEM((D_MODEL, 2 * D_EXPERT), BF16),
                        pltpu.VMEM((D_EXPERT, D_MODEL), BF16)],
    )
    return pl.pallas_call(
        _experts_kernel,
        grid_spec=grid_spec,
        out_shape=jax.ShapeDtypeStruct((nt * EXPERT_TILE, D_MODEL), F32),
        compiler_params=_cparams(1),
        name="moe_experts",
    )(tile_expert, tile_rows, tile_first, n_used, slot_token, m, wg, wu, wd)


def _combine_ple_kernel(pos_ref, next_pos_ref, route_ref, h1_ref, ys_hbm, p_ref, wp_ref, g_ref, wgate_ref,
                        o_ref, ybuf, sems, *, tile):
    i = pl.program_id(0)
    n = pl.num_programs(0)
    slot = i % 2

    def gather(table, s):
        def issue(c, carry):
            base = pl.multiple_of(c * DMA_UNROLL, DMA_UNROLL)
            for k in range(2):
                group = ybuf.at[s, k, pl.ds(base, DMA_UNROLL)]
                for u in range(DMA_UNROLL):
                    pltpu.make_async_copy(ys_hbm.at[pl.ds(table[0, 0, 2 * (base + u) + k], 1)],
                                          group.at[pl.ds(u, 1)], sems.at[s]).start(priority=k)
            return carry

        lax.fori_loop(0, tile // DMA_UNROLL, issue, 0)

    @pl.when(i == 0)
    def _():
        gather(pos_ref, 0)

    @pl.when(i + 1 < n)
    def _():
        gather(next_pos_ref, 1 - slot)

    halves = [slice(0, tile // 2), slice(tile // 2, tile)]
    p_b = p_ref[...].astype(BF16)
    ples = [jnp.dot(p_b[rows], wp_ref[...], preferred_element_type=F32) for rows in halves]
    for k in range(2):
        pltpu.make_async_copy(ys_hbm.at[pl.ds(0, tile)], ybuf.at[slot, k], sems.at[slot]).wait()

    route = route_ref[...]
    h2s = [h1_ref[rows, :] + route[rows, 4:5] * ybuf[slot, 0, rows, :] + route[rows, 5:6] * ybuf[slot, 1, rows, :]
           for rows in halves]
    normed = [_rms(h2, g_ref[...]).astype(BF16) for h2 in h2s]
    gates = [jnp.dot(x, wgate_ref[...], preferred_element_type=F32) for x in normed]
    for rows, h2, gate, ple in zip(halves, h2s, gates, ples):
        o_ref[rows, :] = h2 + jax.nn.sigmoid(gate) * ple


def _combine_ple(layer, pos, route, h1, ys, p, wp, g, wgate):
    T = h1.shape[0]
    tile = min(GATHER_TILE, T)
    n = T // tile
    pos3 = pos.reshape(n, 1, 2 * tile)
    const = lambda *shape: pl.BlockSpec(shape, lambda i: (0,) * len(shape))
    row = lambda width: pl.BlockSpec((tile, width), lambda i: (i, 0))
    return pl.pallas_call(
        functools.partial(_combine_ple_kernel, tile=tile),
        grid=(n,),
        in_specs=[pl.BlockSpec((1, 1, 2 * tile), lambda i: (i, 0, 0), memory_space=pltpu.SMEM),
                  pl.BlockSpec((1, 1, 2 * tile), lambda i: (jnp.minimum(i + 1, n - 1), 0, 0),
                               memory_space=pltpu.SMEM),
                  row(LANES), row(D_MODEL), pl.BlockSpec(memory_space=pl.ANY),
                  pl.BlockSpec((None, tile, D_PLE), lambda i: (layer, i, 0)),
                  const(D_PLE, D_MODEL), const(1, D_MODEL), const(D_MODEL, D_MODEL)],
        out_specs=row(D_MODEL),
        out_shape=jax.ShapeDtypeStruct((T, D_MODEL), F32),
        scratch_shapes=[pltpu.VMEM((2, 2, tile, D_MODEL), F32), pltpu.SemaphoreType.DMA((2,))],
        compiler_params=_cparams(1),
        name="combine_ple",
    )(pos3, pos3, route, h1, ys, p, wp, g, wgate)


def _rope(z, cos, sin_lo, sin_hi):
    half = ROT_DIM // 2
    outs = []
    for j in range(z.shape[1] // LANES):
        zj = z[:, j * LANES:(j + 1) * LANES]
        outs.append(zj * cos + pltpu.roll(zj, LANES - half, 1) * sin_lo + pltpu.roll(zj, half, 1) * sin_hi)
    return outs[0] if len(outs) == 1 else jnp.concatenate(outs, axis=1)


def _proj_odd_kernel(h_ref, g_ref, w_ref, qg_ref, kg_ref, cos_ref, slo_ref, shi_ref,
                     q_ref, k_ref, v_ref, glu_ref):
    a = _rms(h_ref[...], g_ref[...]).astype(BF16)
    cos, slo, shi = cos_ref[...], slo_ref[...], shi_ref[...]
    v0 = SWA_WIDTH + KV_WIDTH
    d0 = v0 + KV_WIDTH
    zq = jnp.dot(a, w_ref[:, 0:SWA_WIDTH], preferred_element_type=F32)
    zk = jnp.dot(a, w_ref[:, SWA_WIDTH:v0], preferred_element_type=F32)
    zv = jnp.dot(a, w_ref[:, v0:d0], preferred_element_type=F32)
    zd = jnp.dot(a, w_ref[:, d0:d0 + 2 * CONV_CH], preferred_element_type=F32)
    q_ref[...] = (_rope(_head_rms(zq, qg_ref[...]), cos, slo, shi) * (HEAD_DIM ** -0.5)).astype(BF16)
    k_ref[...] = _rope(_head_rms(zk, kg_ref[...]), cos, slo, shi).astype(BF16)
    v_ref[...] = zv.astype(BF16)
    glu_ref[...] = zd[:, :CONV_CH] * jax.nn.sigmoid(zd[:, CONV_CH:])


def _proj_odd(h, g, w, qg, kg, cos, slo, shi, *, seq):
    T = h.shape[0]
    tm = min(ROW_TILE, seq)
    tps = seq // tm
    const = lambda *shape: pl.BlockSpec(shape, lambda i: (0,) * len(shape))
    row = lambda width: pl.BlockSpec((tm, width), lambda i: (i, 0))
    tab = pl.BlockSpec((tm, LANES), lambda i: (i % tps, 0))
    return pl.pallas_call(
        _proj_odd_kernel,
        grid=(T // tm,),
        in_specs=[row(D_MODEL), const(1, D_MODEL), const(*w.shape), const(1, SWA_WIDTH),
                  const(1, KV_WIDTH), tab, tab, tab],
        out_specs=[row(SWA_WIDTH), row(KV_WIDTH), row(KV_WIDTH), row(CONV_CH)],
        out_shape=[jax.ShapeDtypeStruct((T, SWA_WIDTH), BF16),
                   jax.ShapeDtypeStruct((T, KV_WIDTH), BF16),
                   jax.ShapeDtypeStruct((T, KV_WIDTH), BF16),
                   jax.ShapeDtypeStruct((T, CONV_CH), F32)],
        compiler_params=_cparams(1),
        name="proj_odd",
    )(h, g, w, qg, kg, cos, slo, shi)


def _swa_kernel(sink_ref, q_ref, k_ref, v_ref, o_ref, *, seq):
    lo = _lane_lo((1, LANES))
    nt = (((1,), (1,)), ((), ()))
    W = WINDOW
    qi = lax.broadcasted_iota(jnp.int32, (W, 2 * W), 0)
    kj = lax.broadcasted_iota(jnp.int32, (W, 2 * W), 1)
    band = (kj > qi) & (kj <= qi + W)

    def block(n, kstart, mask):
        qrow = pl.ds(pl.multiple_of(n * W, W), W)
        kwin = pl.ds(pl.multiple_of(kstart, W), 2 * W)
        ks = k_ref[kwin, :]
        vs = v_ref[kwin, :]
        n_tiles = SWA_WIDTH // LANES
        heads, scores = [], []
        for j in range(n_tiles):
            q = q_ref[qrow, j * LANES:(j + 1) * LANES]
            zero = jnp.zeros_like(q)
            for head, qh in ((j, jnp.where(lo, q, zero)), (n_tiles + j, jnp.where(lo, zero, q))):
                heads.append(head)
                scores.append(lax.dot_general(qh, ks, nt, preferred_element_type=F32))
        probs, sums = [], []
        for head, s in zip(heads, scores):
            s = jnp.where(mask, s, NEG)
            sink = sink_ref[head]
            m = jnp.maximum(jnp.max(s, axis=-1, keepdims=True), sink)
            p = jnp.exp(s - m)
            sums.append(jnp.sum(p, axis=-1, keepdims=True) + jnp.exp(sink - m))
            probs.append(p.astype(BF16))
        outs = [jnp.dot(p, vs, preferred_element_type=F32) / l for p, l in zip(probs, sums)]
        for j in range(n_tiles):
            o_ref[qrow, j * LANES:(j + 1) * LANES] = jnp.where(lo, outs[2 * j], outs[2 * j + 1]).astype(BF16)

    block(0, 0, kj <= qi)

    def body(n, carry):
        block(n, (n - 1) * W, band)
        return carry

    lax.fori_loop(1, seq // W, body, 0)


def _swa_attention(sinks, q, k, v, *, seq):
    T = q.shape[0]
    B = T // seq
    return pl.pallas_call(
        functools.partial(_swa_kernel, seq=seq),
        grid=(B,),
        in_specs=[pl.BlockSpec(memory_space=pltpu.SMEM),
                  pl.BlockSpec((seq, SWA_WIDTH), lambda b: (b, 0)),
                  pl.BlockSpec((seq, KV_WIDTH), lambda b: (b, 0)),
                  pl.BlockSpec((seq, KV_WIDTH), lambda b: (b, 0))],
        out_specs=pl.BlockSpec((seq, SWA_WIDTH), lambda b: (b, 0)),
        out_shape=jax.ShapeDtypeStruct((T, SWA_WIDTH), BF16),
        compiler_params=_cparams(1),
        name="swa_attention",
    )(sinks, q, k, v)


def _conv_kernel(prev_ref, cur_ref, w_ref, g_ref, b_ref, o_ref, shift_ref, *, tile, sub):
    r = pl.program_id(1)
    rows = CONV_HALO + tile
    tail = prev_ref[tile - CONV_HALO:, :]
    shift_ref[0, 0:CONV_HALO, :] = jnp.where(r > 0, tail, jnp.zeros_like(tail))
    shift_ref[0, CONV_HALO:rows, :] = cur_ref[...]
    shift_ref[0, rows:rows + SUBLANES, :] = jnp.zeros((SUBLANES, CONV_CH), F32)
    for o in range(1, SUBLANES):
        shift_ref[o, 0:rows, :] = shift_ref[0, o:o + rows, :]
    w = w_ref[...]
    first = CONV_HALO - (CONV_WIDTH - 1)
    for s in range(tile // sub):
        acc = jnp.zeros((sub, CONV_CH), F32)
        for j in range(CONV_WIDTH):
            start = s * sub + first + j
            o = start % SUBLANES
            acc = acc + shift_ref[o, start - o:start - o + sub, :] * w[j:j + 1, :]
        y = _layernorm(acc, g_ref[...], b_ref[...])
        o_ref[s * sub:(s + 1) * sub, :] = (y * jax.nn.sigmoid(y)).astype(BF16)


def _conv_module(glu, w, g, b, *, seq):
    T = glu.shape[0]
    B = T // seq
    tile = min(CONV_TILE, seq)
    nr = seq // tile
    const = lambda *shape: pl.BlockSpec(shape, lambda bb, r: (0,) * len(shape))
    return pl.pallas_call(
        functools.partial(_conv_kernel, tile=tile, sub=64),
        grid=(B, nr),
        in_specs=[pl.BlockSpec((tile, CONV_CH), lambda bb, r: (bb * nr + jnp.maximum(r - 1, 0), 0)),
                  pl.BlockSpec((tile, CONV_CH), lambda bb, r: (bb * nr + r, 0)),
                  const(CONV_WIDTH, CONV_CH), const(1, CONV_CH), const(1, CONV_CH)],
        out_specs=pl.BlockSpec((tile, CONV_CH), lambda bb, r: (bb * nr + r, 0)),
        out_shape=jax.ShapeDtypeStruct((T, CONV_CH), BF16),
        scratch_shapes=[pltpu.VMEM((SUBLANES, CONV_HALO + tile + SUBLANES, CONV_CH), F32)],
        compiler_params=_cparams(2),
        name="conv_module",
    )(glu, glu, w, g, b)


def _routing_tables(route, counts, n_tiles):
    e = route[:, 0:2].astype(jnp.int32)
    rank = route[:, 2:4].astype(jnp.int32)
    cnt = counts[0, :N_EXPERTS].astype(jnp.int32)
    tiles = (cnt + EXPERT_TILE - 1) // EXPERT_TILE
    tile_end = jnp.cumsum(tiles)
    offset = (tile_end - tiles) * EXPERT_TILE
    onehot = e[:, :, None] == jnp.arange(N_EXPERTS, dtype=jnp.int32)
    pos = (rank + jnp.sum(jnp.where(onehot, offset, 0), axis=-1)).reshape(-1)
    n_used = tile_end[-1]
    tile_id = jnp.minimum(jnp.arange(n_tiles, dtype=jnp.int32), n_used - 1)
    tile_expert = jnp.sum((tile_end[None, :] <= tile_id[:, None]).astype(jnp.int32), axis=1)
    dense_offset = jnp.cumsum(cnt) - cnt
    order = (rank + jnp.sum(jnp.where(onehot, dense_offset, 0), axis=-1)).reshape(-1)
    tiles_j = jnp.arange(n_tiles, dtype=jnp.int32)
    mine = tile_expert[:, None] == jnp.arange(N_EXPERTS, dtype=jnp.int32)
    pick = lambda table: jnp.sum(jnp.where(mine, table, 0), axis=1)
    row_in_expert = tiles_j * EXPERT_TILE - pick(offset)
    tile_rows = jnp.where(tiles_j < n_used, jnp.clip(pick(cnt) - row_in_expert, 0, EXPERT_TILE), 0)
    tile_first = jnp.clip(pick(dense_offset) + row_in_expert, 0, order.shape[0] - 1)
    as_i32 = lambda v: v.astype(jnp.int32)
    return pos, order, tile_expert, as_i32(tile_rows), as_i32(tile_first), as_i32(n_used.reshape(1))


def _moe_ple(h, ya, yb, wo, layer, norm_ffn, wr, br, wg, wu, wd, p, wp, ple_norm, wgate):
    T = h.shape[0]
    n_tiles = (2 * T) // EXPERT_TILE + N_EXPERTS
    h1, m, route, counts = _outproj_router(h, ya, yb, wo, norm_ffn, wr, br)
    pos, order, tile_expert, tile_rows, tile_first, n_used = _routing_tables(route, counts, n_tiles)
    ys = _experts(layer, tile_expert, tile_rows, tile_first, n_used, _slot_tokens(order), m, wg, wu, wd)
    return _combine_ple(layer, pos, route, h1, ys, p, wp, ple_norm, wgate)


def _router_weights(w_coarse, b_coarse, w_fine, b_fine):
    wf = w_fine.transpose(1, 0, 2).reshape(D_MODEL, N_EXPERTS)
    wr = jnp.concatenate([wf, w_coarse, jnp.zeros((D_MODEL, LANES - N_EXPERTS - N_GROUPS), F32)], axis=1)
    br = jnp.concatenate([b_fine.reshape(-1), b_coarse, jnp.zeros((LANES - N_EXPERTS - N_GROUPS,), F32)])
    w_hi = wr.astype(BF16)
    w_lo = (wr - w_hi.astype(F32)).astype(BF16)
    return jnp.concatenate([w_hi, w_lo], axis=1), br.reshape(1, LANES)


def _rope_tables(seq):
    half = ROT_DIM // 2
    inv_freq = ROPE_THETA ** (-jnp.arange(half, dtype=F32) * 2.0 / ROT_DIM)
    ang = jnp.arange(seq, dtype=F32)[:, None] * inv_freq[None, :]
    cos, sin = jnp.cos(ang), jnp.sin(ang)
    zeros = jnp.zeros((seq, HEAD_DIM - ROT_DIM), F32)
    z8 = jnp.zeros((seq, half), F32)
    cos_h = jnp.concatenate([cos, cos, zeros + 1.0], axis=1)
    slo_h = jnp.concatenate([-sin, z8, zeros], axis=1)
    shi_h = jnp.concatenate([z8, sin, zeros], axis=1)
    two = lambda t: jnp.concatenate([t, t], axis=1)
    return two(cos_h), two(slo_h), two(shi_h)


def kernel(x, p, norm_mix, even_w_in, fox_b_f, gmlp_ln_g, gmlp_ln_b, gmlp_w_s, gmlp_b_s, fox_q_norm, fox_k_norm, even_w_out, odd_w_in, swa_q_norm, swa_k_norm, swa_sinks, conv_w, conv_ln_g, conv_ln_b, odd_w_out, norm_ffn, moe_w_coarse, moe_b_coarse, moe_w_fine, moe_b_fine, moe_w_gate, moe_w_up, moe_w_down, ple_w_proj, ple_norm, ple_w_gate):
    B, S, D = x.shape
    T = B * S
    h = x.reshape(T, D)
    p = p.reshape(p.shape[0], T, D_PLE)
    row = lambda v: v.reshape(1, -1)

    def moe_args(i):
        wr, br = _router_weights(moe_w_coarse[i], moe_b_coarse[i], moe_w_fine[i], moe_b_fine[i])
        return (i, row(norm_ffn[i]), wr, br, moe_w_gate, moe_w_up, moe_w_down, p,
                ple_w_proj[i].astype(BF16), row(ple_norm[i]), ple_w_gate[i].astype(BF16))

    n_main = 2 * GMLP_WIDTH + 3 * FOX_WIDTH
    w_f = jnp.pad(jnp.tile(even_w_in[0][:, n_main:], (1, BIAS_PIECES)),
                  ((0, 0), (0, LANES - BIAS_PIECES * FOX_HEADS)))
    w_in = jnp.concatenate([even_w_in[0][:, :n_main], w_f], axis=1).astype(BF16)
    b_f = jnp.pad(jnp.tile(fox_b_f[0], BIAS_PIECES), (0, LANES - BIAS_PIECES * FOX_HEADS)).reshape(1, LANES)
    bs_full = jnp.repeat(gmlp_b_s[0].T, HEAD_DIM, axis=1)
    ya, q, k, v, xq, xk = _proj_even(
        h, row(norm_mix[0]), w_in, b_f, row(gmlp_ln_g[0]), row(gmlp_ln_b[0]), gmlp_w_s[0], bs_full,
        row(jnp.tile(fox_q_norm[0], FOX_HEADS)), row(jnp.tile(fox_k_norm[0], FOX_HEADS)), seq=S)
    yb = _fox_attention(q, xq, k, xk, v, seq=S)
    h = _moe_ple(h, ya, yb, even_w_out[0].astype(BF16), *moe_args(0))

    order = jnp.array([0, 4, 1, 5, 2, 6, 3, 7])
    cols = (order[:, None] * HEAD_DIM + jnp.arange(HEAD_DIM)[None, :]).reshape(-1)
    w_odd = jnp.concatenate([odd_w_in[0][:, :SWA_WIDTH][:, cols], odd_w_in[0][:, SWA_WIDTH:]], axis=1).astype(BF16)
    w_out_odd = jnp.concatenate([odd_w_out[0][:SWA_WIDTH][cols], odd_w_out[0][SWA_WIDTH:]], axis=0).astype(BF16)
    cos, slo, shi = _rope_tables(S)
    q, k, v, glu = _proj_odd(h, row(norm_mix[1]), w_odd, row(jnp.tile(swa_q_norm[0], 8)),
                             row(jnp.tile(swa_k_norm[0], 2)), cos, slo, shi, seq=S)
    yc = _swa_attention(swa_sinks[0], q, k, v, seq=S)
    yd = _conv_module(glu, conv_w[0], row(conv_ln_g[0]), row(conv_ln_b[0]), seq=S)
    h = _moe_ple(h, yc, yd, w_out_odd, *moe_args(1))
    return h.reshape(B, S, D)
```

```python
import functools

import jax
import jax.numpy as jnp
import numpy as np
from jax import lax
from jax.experimental import pallas as pl
from jax.experimental.pallas import tpu as pltpu

F32 = jnp.float32
BF16 = jnp.bfloat16
HIGHEST = lax.Precision.HIGHEST

D_MODEL = 1024
HEAD_DIM = 64
LANES = 128
GMLP_WIDTH = 512
CHUNK = 128
FOX_WIDTH = 512
FOX_HEADS = 8
SWA_WIDTH = 512
KV_WIDTH = 128
WINDOW = 128
CONV_CH = 512
CONV_WIDTH = 31
CONV_HALO = 32
ROPE_THETA = 500000.0
ROT_DIM = 16
N_GROUPS = 4
EXPERTS_PER_GROUP = 8
N_EXPERTS = 32
D_EXPERT = 256
D_PLE = 256
EPS = 1e-6
NEG = -1e30
LOG2E = 1.4426950408889634
BIAS_PIECES = 3

EXPERT_TILE = 512
PAD_BITS = 9
SUBLANES = 8
SUBLANE_BITS = 3
ROW_TILE = 512
ROUTER_BLOCKS = 4
GATHER_TILE = 512
DISPATCH_TILE = 2048
DMA_UNROLL = 8
ATTN_TILE = 256
FOX_PAIRS_PER_STEP = 2
CONV_TILE = 256
VMEM_LIMIT = 56 * 1024 * 1024


def _cparams(n_axes=1, flags=None):
    return pltpu.CompilerParams(dimension_semantics=("arbitrary",) * n_axes,
                                vmem_limit_bytes=VMEM_LIMIT, flags=flags)


def _rms(x, gain):
    return x * lax.rsqrt(jnp.mean(x * x, axis=-1, keepdims=True) + EPS) * gain


def _layernorm(x, g, b):
    mu = jnp.mean(x, axis=-1, keepdims=True)
    xc = x - mu
    var = jnp.mean(xc * xc, axis=-1, keepdims=True)
    return xc * lax.rsqrt(var + EPS) * g + b


def _head_rms(z, gain):
    r = (lax.broadcasted_iota(jnp.int32, (2 * LANES, LANES), 0) % LANES) // HEAD_DIM
    c = lax.broadcasted_iota(jnp.int32, (2 * LANES, LANES), 1) // HEAD_DIM
    bd = jnp.where(r == c, 1.0 / HEAD_DIM, 0.0).astype(BF16)
    outs = []
    for j in range(z.shape[1] // LANES):
        zj = z[:, j * LANES:(j + 1) * LANES]
        sq = zj * zj
        sq_hi = sq.astype(BF16)
        sq_lo = (sq - sq_hi.astype(F32)).astype(BF16)
        ms = jnp.dot(jnp.concatenate([sq_hi, sq_lo], axis=1), bd, preferred_element_type=F32)
        outs.append(zj * lax.rsqrt(ms + EPS))
    zn = outs[0] if len(outs) == 1 else jnp.concatenate(outs, axis=1)
    return zn * gain


def _lane_lo(shape):
    return (lax.broadcasted_iota(jnp.int32, shape, len(shape) - 1) % LANES) < HEAD_DIM


def _proj_even_kernel(h_ref, g_ref, w_ref, bf_ref, lng_ref, lnb_ref, ws_ref, bs_ref, qg_ref, kg_ref,
                      pq_ref, pk_ref, oneq_ref, onek_ref,
                      ya_ref, q_ref, k_ref, v_ref, xq_ref, xk_ref, carry_ref, *, tm, tiles_per_seq):
    i = pl.program_id(0)
    a = _rms(h_ref[...], g_ref[...]).astype(BF16)

    q0 = 2 * GMLP_WIDTH
    za = jnp.dot(a, w_ref[:, 0:q0], preferred_element_type=F32)
    zq = jnp.dot(a, w_ref[:, q0:q0 + FOX_WIDTH], preferred_element_type=F32)
    zk = jnp.dot(a, w_ref[:, q0 + FOX_WIDTH:q0 + 2 * FOX_WIDTH], preferred_element_type=F32)
    zv = jnp.dot(a, w_ref[:, q0 + 2 * FOX_WIDTH:q0 + 3 * FOX_WIDTH], preferred_element_type=F32)
    zf = jnp.dot(a, w_ref[:, q0 + 3 * FOX_WIDTH:], preferred_element_type=F32) + bf_ref[...]

    q_ref[...] = (_head_rms(zq, qg_ref[...]) * (LOG2E * HEAD_DIM ** -0.5)).astype(BF16)
    k_ref[...] = _head_rms(zk, kg_ref[...]).astype(BF16)
    v_ref[...] = zv.astype(BF16)

    rr = lax.broadcasted_iota(jnp.int32, (CHUNK, CHUNK), 0)
    cc = lax.broadcasted_iota(jnp.int32, (CHUNK, CHUNK), 1)

    ls = jnp.minimum(zf, 0.0) - jnp.log(1.0 + jnp.exp(-jnp.abs(zf)))

    @pl.when(i % tiles_per_seq == 0)
    def _():
        carry_ref[...] = jnp.zeros_like(carry_ref)

    def split3(x):
        hi = x.astype(BF16)
        r1 = x - hi.astype(F32)
        mid = r1.astype(BF16)
        return hi, mid, (r1 - mid.astype(F32)).astype(BF16)

    tri = jnp.where(rr >= cc, 1.0, 0.0).astype(BF16)
    pieces = jnp.concatenate(split3(ls), axis=1)
    running = carry_ref[...]
    blocks = []
    for b in range(tm // CHUNK):
        d = jnp.dot(tri, pieces[b * CHUNK:(b + 1) * CHUNK, :], preferred_element_type=F32)
        blk = (d[:, :LANES] + d[:, LANES:2 * LANES]) + d[:, 2 * LANES:] + running
        running = blk[CHUNK - 1:CHUNK, :]
        blocks.append(blk)
    carry_ref[...] = running
    c = jnp.concatenate(blocks, axis=0)

    za = jax.nn.gelu(za)
    u = za[:, :GMLP_WIDTH]
    vln = _layernorm(za[:, GMLP_WIDTH:], lng_ref[...], lnb_ref[...]).astype(BF16)
    lo = _lane_lo((CHUNK, LANES))
    for j in range(GMLP_WIDTH // LANES):
        w_a = jnp.where(rr >= cc, ws_ref[2 * j], 0.0).astype(BF16)
        w_b = jnp.where(rr >= cc, ws_ref[2 * j + 1], 0.0).astype(BF16)
        cols = slice(j * LANES, (j + 1) * LANES)
        for blk in range(tm // CHUNK):
            rows = slice(blk * CHUNK, (blk + 1) * CHUNK)
            vp = vln[rows, cols]
            mixed = jnp.where(lo, jnp.dot(w_a, vp, preferred_element_type=F32),
                              jnp.dot(w_b, vp, preferred_element_type=F32)) + bs_ref[:, cols]
            ya_ref[rows, cols] = (u[rows, cols] * mixed).astype(BF16)

    hi, mid, low = split3(c * LOG2E)
    group = lax.broadcasted_iota(jnp.int32, (tm, LANES), 1) // FOX_HEADS
    sel = jnp.where(group == 0, hi, jnp.where(group == 1, mid, low))
    xq_ref[...] = (jnp.dot(sel, pq_ref[...], preferred_element_type=F32) + oneq_ref[...]).astype(BF16)
    xk_ref[...] = (jnp.dot(sel, pk_ref[...], preferred_element_type=F32) + onek_ref[...]).astype(BF16)


def _proj_even(h, g, w, bf, lng, lnb, ws, bs_full, qg, kg, *, seq):
    T = h.shape[0]
    tm = min(ROW_TILE, seq)
    n_in = w.shape[1]
    const = lambda *shape: pl.BlockSpec(shape, lambda i: (0,) * len(shape))
    row = lambda width: pl.BlockSpec((tm, width), lambda i: (i, 0))
    tps = seq // tm
    return pl.pallas_call(
        functools.partial(_proj_even_kernel, tm=tm, tiles_per_seq=tps),
        grid=(T // tm,),
        in_specs=[row(D_MODEL), const(1, D_MODEL), const(D_MODEL, n_in), const(1, LANES),
                  const(1, GMLP_WIDTH), const(1, GMLP_WIDTH), const(8, CHUNK, CHUNK),
                  const(CHUNK, GMLP_WIDTH), const(1, FOX_WIDTH), const(1, FOX_WIDTH),
                  const(LANES, FOX_WIDTH), const(LANES, FOX_WIDTH),
                  const(1, FOX_WIDTH), const(1, FOX_WIDTH)],
        out_specs=[row(GMLP_WIDTH)] + [row(FOX_WIDTH)] * 5,
        out_shape=[jax.ShapeDtypeStruct((T, GMLP_WIDTH), BF16)]
                  + [jax.ShapeDtypeStruct((T, FOX_WIDTH), BF16)] * 5,
        scratch_shapes=[pltpu.VMEM((1, LANES), F32)],
        compiler_params=_cparams(1),
        name="proj_even",
    )(h, g, w, bf, lng, lnb, ws, bs_full, qg, kg, *_bias_placement())


def _bias_placement():
    pq = np.zeros((LANES, FOX_WIDTH), np.float32)
    pk = np.zeros((LANES, FOX_WIDTH), np.float32)
    oneq = np.zeros((1, FOX_WIDTH), np.float32)
    onek = np.zeros((1, FOX_WIDTH), np.float32)
    for head in range(FOX_HEADS):
        base = (head // 2) * LANES + (HEAD_DIM if head % 2 == 0 else 0)
        for piece in range(BIAS_PIECES):
            pq[piece * FOX_HEADS + head, base + piece] = 1.0
            onek[0, base + piece] = 1.0
            pk[piece * FOX_HEADS + head, base + BIAS_PIECES + piece] = -1.0
            oneq[0, base + BIAS_PIECES + piece] = 1.0
    return (jnp.asarray(pq, BF16), jnp.asarray(pk, BF16), jnp.asarray(oneq), jnp.asarray(onek))


def _fox_tile(n_tiles, q_ref, xq_ref, k_ref, xk_ref, v_ref, o_ref, *, tq):
    half = tq // 2
    lo = _lane_lo((1, LANES))
    rr = lax.broadcasted_iota(jnp.int32, (half, tq), 0)
    cc = lax.broadcasted_iota(jnp.int32, (half, tq), 1)
    nt = (((1,), (1,)), ((), ()))
    pairs = [slice(pp * LANES, (pp + 1) * LANES) for pp in range(FOX_PAIRS_PER_STEP)]
    chains = [(pp, hf, head) for pp in range(len(pairs)) for hf in range(2) for head in range(2)]
    q_aug = {}
    for pp, cols in enumerate(pairs):
        for hf in range(2):
            rows = slice(hf * half, (hf + 1) * half)
            q, xq = q_ref[rows, cols], xq_ref[rows, cols]
            q_aug[pp, hf, 0], q_aug[pp, hf, 1] = jnp.where(lo, q, xq), jnp.where(lo, xq, q)

    def score_products(j):
        rows = slice(j * tq, (j + 1) * tq)
        k_aug = {}
        for pp, cols in enumerate(pairs):
            ks, xk = k_ref[rows, cols], xk_ref[rows, cols]
            k_aug[pp, 0], k_aug[pp, 1] = jnp.where(lo, ks, xk), jnp.where(lo, xk, ks)
        return [lax.dot_general(q_aug[pp, hf, head], k_aug[pp, head], nt, preferred_element_type=F32)
                for pp, hf, head in chains]

    maxes = [jnp.full((half, 1), NEG, F32) for _ in chains]
    accs = [jnp.zeros((half, LANES), F32) for _ in chains]
    scores = score_products(0)
    for j in range(n_tiles):
        next_scores = score_products(j + 1) if j + 1 < n_tiles else None
        v_aug = {}
        for pp, cols in enumerate(pairs):
            vs = v_ref[j * tq:(j + 1) * tq, cols]
            one = jnp.ones_like(vs)
            v_aug[pp, 0], v_aug[pp, 1] = jnp.where(lo, vs, one), jnp.where(lo, one, vs)
        for c, (pp, hf, head) in enumerate(chains):
            s = scores[c]
            if j == n_tiles - 1:
                s = jnp.where(cc <= rr + hf * half, s, NEG)
            n = jnp.maximum(maxes[c], jnp.max(s, axis=-1, keepdims=True))
            p = jnp.exp2(s - n).astype(BF16)
            accs[c] = accs[c] * jnp.exp2(maxes[c] - n) + jnp.dot(p, v_aug[pp, head], preferred_element_type=F32)
            maxes[c] = n
        scores = next_scores
    for c in range(0, len(chains), 2):
        pp, hf, _ = chains[c]
        norm = [acc / pltpu.roll(acc, HEAD_DIM, 1) for acc in accs[c:c + 2]]
        o_ref[hf * half:(hf + 1) * half, pairs[pp]] = jnp.where(lo, norm[0], norm[1]).astype(BF16)


def _fox_kernel(q_ref, xq_ref, k_ref, xk_ref, v_ref, o_ref, *, tq, nq):
    i = pl.program_id(2)
    for c in range(nq):
        pl.when(i == c)(functools.partial(_fox_tile, c + 1, q_ref, xq_ref, k_ref, xk_ref, v_ref, o_ref, tq=tq))


def _fox_attention(q, xq, k, xk, v, *, seq):
    T = q.shape[0]
    B = T // seq
    tq = min(ATTN_TILE, seq)
    nq = seq // tq
    width = FOX_PAIRS_PER_STEP * LANES
    tile = pl.BlockSpec((tq, width), lambda b, hp, i: (b * nq + i, hp))
    whole = pl.BlockSpec((seq, width), lambda b, hp, i: (b, hp))
    return pl.pallas_call(
        functools.partial(_fox_kernel, tq=tq, nq=nq),
        grid=(B, FOX_WIDTH // width, nq),
        in_specs=[tile, tile, whole, whole, whole],
        out_specs=tile,
        out_shape=jax.ShapeDtypeStruct((T, FOX_WIDTH), BF16),
        compiler_params=_cparams(3),
        name="fox_attention",
    )(q, xq, k, xk, v)


def _outproj_router_kernel(h_ref, ya_ref, yb_ref, wo_ref, g_ref, wr_ref, br_ref,
                           h1_ref, m_ref, route_ref, cnt_ref, carry_ref, *, tm):
    i = pl.program_id(0)
    half = wo_ref.shape[0] // 2
    nb = ROUTER_BLOCKS
    rb = tm // nb
    blocks = [slice(b * rb, (b + 1) * rb) for b in range(nb)]

    @pl.when(i == 0)
    def _():
        carry_ref[...] = jnp.zeros_like(carry_ref)

    mixes = [jnp.dot(ya_ref[rows, :], wo_ref[0:half, :], preferred_element_type=F32)
             + jnp.dot(yb_ref[rows, :], wo_ref[half:, :], preferred_element_type=F32) for rows in blocks]
    ms = []
    for rows, mix in zip(blocks, mixes):
        h1 = h_ref[rows, :] + mix
        h1_ref[rows, :] = h1
        m = _rms(h1, g_ref[...])
        m_ref[rows, :] = m
        ms.append(m)

    logits = []
    for m in ms:
        m_hi = m.astype(BF16)
        m_lo = (m - m_hi.astype(F32)).astype(BF16)
        hh = jnp.dot(m_hi, wr_ref[...], preferred_element_type=F32)
        lh = jnp.dot(m_lo, wr_ref[:, :LANES], preferred_element_type=F32)
        logits.append(hh[:, :LANES] + (hh[:, LANES:] + lh) + br_ref[...])

    lane_i = lax.broadcasted_iota(jnp.int32, (rb, LANES), 1)
    lane = lane_i.astype(F32)
    group_of_lane = (lane_i // EXPERTS_PER_GROUP).astype(F32)
    is_coarse = (lane_i >= N_EXPERTS) & (lane_i < N_EXPERTS + N_GROUPS)
    picks = []
    for lg in logits:
        coarse = jnp.where(is_coarse, lg, NEG)
        cmax = jnp.max(coarse, axis=-1, keepdims=True)
        gidx = jnp.min(jnp.where(coarse == cmax, lane - N_EXPERTS, float(LANES)), axis=-1, keepdims=True)
        p_g = 1.0 / jnp.sum(jnp.where(is_coarse, jnp.exp(coarse - cmax), 0.0), axis=-1, keepdims=True)
        in_group = (lane_i < N_EXPERTS) & (group_of_lane == gidx)
        fine = jnp.where(in_group, lg, NEG)
        v1 = jnp.max(fine, axis=-1, keepdims=True)
        i1 = jnp.min(jnp.where(fine == v1, lane, float(LANES)), axis=-1, keepdims=True)
        fine2 = jnp.where(lane == i1, NEG, fine)
        v2 = jnp.max(fine2, axis=-1, keepdims=True)
        i2 = jnp.min(jnp.where(fine2 == v2, lane, float(LANES)), axis=-1, keepdims=True)
        e2 = jnp.exp(v2 - v1)
        picks.append((i1, i2, p_g / (1.0 + e2), p_g * e2 / (1.0 + e2)))

    tr = lax.broadcasted_iota(jnp.int32, (rb, rb), 0)
    tc = lax.broadcasted_iota(jnp.int32, (rb, rb), 1)
    strict = jnp.where(tr > tc, 1.0, 0.0).astype(BF16)
    onehots = [jnp.where((lane == i1) | (lane == i2), 1.0, 0.0).astype(F32) for i1, i2, _, _ in picks]
    befores = [jnp.dot(strict, oh.astype(BF16), preferred_element_type=F32) for oh in onehots]
    total = carry_ref[...]
    for rows, (i1, i2, w1, w2), oh, before in zip(blocks, picks, onehots, befores):
        before = before + total
        r1 = jnp.sum(jnp.where(lane == i1, before, 0.0), axis=-1, keepdims=True)
        r2 = jnp.sum(jnp.where(lane == i2, before, 0.0), axis=-1, keepdims=True)
        total = total + jnp.sum(oh, axis=0, keepdims=True)
        route = jnp.where(lane == 0, i1, 0.0)
        route = jnp.where(lane == 1, i2, route)
        route = jnp.where(lane == 2, r1, route)
        route = jnp.where(lane == 3, r2, route)
        route = jnp.where(lane == 4, w1, route)
        route = jnp.where(lane == 5, w2, route)
        route_ref[rows, :] = route
    carry_ref[...] = total
    cnt_ref[...] = jnp.broadcast_to(total, cnt_ref.shape)


def _outproj_router(h, ya, yb, wo, g, wr, br):
    T = h.shape[0]
    tm = min(ROW_TILE, T)
    const = lambda *shape: pl.BlockSpec(shape, lambda i: (0,) * len(shape))
    row = lambda width: pl.BlockSpec((tm, width), lambda i: (i, 0))
    return pl.pallas_call(
        functools.partial(_outproj_router_kernel, tm=tm),
        grid=(T // tm,),
        in_specs=[row(D_MODEL), row(ya.shape[1]), row(yb.shape[1]), const(*wo.shape),
                  const(1, D_MODEL), const(D_MODEL, 2 * LANES), const(1, LANES)],
        out_specs=[row(D_MODEL), row(D_MODEL), row(LANES), const(8, LANES)],
        out_shape=[jax.ShapeDtypeStruct((T, D_MODEL), F32),
                   jax.ShapeDtypeStruct((T, D_MODEL), F32),
                   jax.ShapeDtypeStruct((T, LANES), F32),
                   jax.ShapeDtypeStruct((8, LANES), F32)],
        scratch_shapes=[pltpu.VMEM((1, LANES), F32)],
        compiler_params=_cparams(1),
        name="outproj_router",
    )(h, ya, yb, wo, g, wr, br)


def _dispatch_kernel(pad_ref, pos_ref, m_ref, xs_hbm, zeros_ref, sem, pad_sem, *, tile):
    @pl.when(pl.program_id(0) == 0)
    def _():
        zeros_ref[...] = jnp.zeros_like(zeros_ref)

        def pad_copies(e, wait):
            first, n_single, n_block = pad_ref[0, e], pad_ref[1, e], pad_ref[2, e]
            for r in range(SUBLANES - 1):
                copy = pltpu.make_async_copy(zeros_ref.at[pl.ds(0, 1)], xs_hbm.at[pl.ds(first + r, 1)], pad_sem)
                pl.when(r < n_single)(copy.wait if wait else copy.start)
            done = first + n_single
            for bit in reversed(range(SUBLANE_BITS, PAD_BITS)):
                size = 1 << bit
                taken = (n_block & size) != 0
                copy = pltpu.make_async_copy(zeros_ref.at[pl.ds(0, size)],
                                             xs_hbm.at[pl.ds(pl.multiple_of(done, SUBLANES), size)], pad_sem)
                pl.when(taken)(copy.wait if wait else copy.start)
                done = done + jnp.where(taken, size, 0)

        def start(e, carry):
            pad_copies(e, False)
            return carry

        def finish(e, carry):
            pad_copies(e, True)
            return carry

        def tile_copy(j):
            return pltpu.make_async_copy(
                zeros_ref, xs_hbm.at[pl.ds(pl.multiple_of(j * EXPERT_TILE, EXPERT_TILE), EXPERT_TILE)], pad_sem)

        def start_tile(j, carry):
            tile_copy(j).start()
            return carry

        def finish_tile(j, carry):
            tile_copy(j).wait()
            return carry

        n_tiles = xs_hbm.shape[0] // EXPERT_TILE
        lax.fori_loop(0, N_EXPERTS, start, 0)
        lax.fori_loop(pad_ref[3, 0], n_tiles, start_tile, 0)
        lax.fori_loop(0, N_EXPERTS, finish, 0)
        lax.fori_loop(pad_ref[3, 0], n_tiles, finish_tile, 0)

    def issue(c, carry):
        base = pl.multiple_of(c * DMA_UNROLL, DMA_UNROLL)
        group = m_ref.at[pl.ds(base, DMA_UNROLL)]
        for u in range(DMA_UNROLL):
            for k in range(2):
                dst = xs_hbm.at[pl.ds(pos_ref[0, 0, 2 * (base + u) + k], 1)]
                pltpu.make_async_copy(group.at[pl.ds(u, 1)], dst, sem).start(priority=k)
        return carry

    lax.fori_loop(0, tile // DMA_UNROLL, issue, 0)
    for _ in range(2):
        pltpu.make_async_copy(m_ref, xs_hbm.at[pl.ds(0, tile)], sem).wait()


def _dispatch(pads, pos, m, n_rows):
    T, width = m.shape
    tile = min(DISPATCH_TILE, T)
    pos3 = pos.reshape(T // tile, 1, 2 * tile)
    grid_spec = pltpu.PrefetchScalarGridSpec(
        num_scalar_prefetch=1,
        grid=(T // tile,),
        in_specs=[pl.BlockSpec((1, 1, 2 * tile), lambda i, pads: (i, 0, 0), memory_space=pltpu.SMEM),
                  pl.BlockSpec((tile, width), lambda i, pads: (i, 0))],
        out_specs=pl.BlockSpec(memory_space=pl.ANY),
        scratch_shapes=[pltpu.VMEM((EXPERT_TILE, width), m.dtype),
                        pltpu.SemaphoreType.DMA(()), pltpu.SemaphoreType.DMA(())],
    )
    return pl.pallas_call(
        functools.partial(_dispatch_kernel, tile=tile),
        grid_spec=grid_spec,
        out_shape=jax.ShapeDtypeStruct((n_rows, width), m.dtype),
        compiler_params=pltpu.CompilerParams(dimension_semantics=("arbitrary",),
                                             has_side_effects=True, vmem_limit_bytes=VMEM_LIMIT),
        name="moe_dispatch",
    )(pads, pos3, m)


def _experts_kernel(te_ref, nused_ref, xs_ref, wg_ref, wu_ref, wd_ref, ys_ref, wgu_b, wd_b):
    j = pl.program_id(0)
    prev = te_ref[jnp.maximum(j - 1, 0)]

    @pl.when((j == 0) | (te_ref[j] != prev))
    def _():
        wgu_b[:, :D_EXPERT] = wg_ref[...].astype(BF16)
        wgu_b[:, D_EXPERT:] = wu_ref[...].astype(BF16)
        wd_b[...] = wd_ref[...].astype(BF16)

    @pl.when(j < nused_ref[0])
    def _():
        gu = jnp.dot(xs_ref[...].astype(BF16), wgu_b[...], preferred_element_type=F32)
        g = gu[:, :D_EXPERT]
        act = g * jax.nn.sigmoid(g) * gu[:, D_EXPERT:]
        ys_ref[...] = jnp.dot(act.astype(BF16), wd_b[...], preferred_element_type=F32)

    @pl.when(j >= nused_ref[0])
    def _():
        ys_ref[...] = jnp.zeros_like(ys_ref)


def _experts(layer, tile_expert, n_used, xs, wg, wu, wd):
    n_rows = xs.shape[0]
    nt = n_rows // EXPERT_TILE
    grid_spec = pltpu.PrefetchScalarGridSpec(
        num_scalar_prefetch=2,
        grid=(nt,),
        in_specs=[pl.BlockSpec((EXPERT_TILE, D_MODEL), lambda j, te, nu: (jnp.minimum(j, nu[0] - 1), 0)),
                  pl.BlockSpec((None, None, D_MODEL, D_EXPERT), lambda j, te, nu: (layer, te[j], 0, 0)),
                  pl.BlockSpec((None, None, D_MODEL, D_EXPERT), lambda j, te, nu: (layer, te[j], 0, 0)),
                  pl.BlockSpec((None, None, D_EXPERT, D_MODEL), lambda j, te, nu: (layer, te[j], 0, 0))],
        out_specs=pl.BlockSpec((EXPERT_TILE, D_MODEL), lambda j, te, nu: (j, 0)),
        scratch_shapes=[pltpu.VMEM((D_MODEL, 2 * D_EXPERT), BF16),
                        pltpu.VMEM((D_EXPERT, D_MODEL), BF16)],
    )
    return pl.pallas_call(
        _experts_kernel,
        grid_spec=grid_spec,
        out_shape=jax.ShapeDtypeStruct((n_rows, D_MODEL), F32),
        compiler_params=_cparams(1),
        name="moe_experts",
    )(tile_expert, n_used, xs, wg, wu, wd)


def _combine_ple_kernel(pos_ref, next_pos_ref, route_ref, h1_ref, ys_hbm, p_ref, wp_ref, g_ref, wgate_ref,
                        o_ref, ybuf, sems, *, tile):
    i = pl.program_id(0)
    n = pl.num_programs(0)
    slot = i % 2

    def gather(table, s):
        def issue(c, carry):
            base = pl.multiple_of(c * DMA_UNROLL, DMA_UNROLL)
            for k in range(2):
                group = ybuf.at[s, k, pl.ds(base, DMA_UNROLL)]
                for u in range(DMA_UNROLL):
                    pltpu.make_async_copy(ys_hbm.at[pl.ds(table[0, 0, 2 * (base + u) + k], 1)],
                                          group.at[pl.ds(u, 1)], sems.at[s]).start(priority=k)
            return carry

        lax.fori_loop(0, tile // DMA_UNROLL, issue, 0)

    @pl.when(i == 0)
    def _():
        gather(pos_ref, 0)

    @pl.when(i + 1 < n)
    def _():
        gather(next_pos_ref, 1 - slot)

    halves = [slice(0, tile // 2), slice(tile // 2, tile)]
    p_b = p_ref[...].astype(BF16)
    ples = [jnp.dot(p_b[rows], wp_ref[...], preferred_element_type=F32) for rows in halves]
    for k in range(2):
        pltpu.make_async_copy(ys_hbm.at[pl.ds(0, tile)], ybuf.at[slot, k], sems.at[slot]).wait()

    route = route_ref[...]
    h2s = [h1_ref[rows, :] + route[rows, 4:5] * ybuf[slot, 0, rows, :] + route[rows, 5:6] * ybuf[slot, 1, rows, :]
           for rows in halves]
    normed = [_rms(h2, g_ref[...]).astype(BF16) for h2 in h2s]
    gates = [jnp.dot(x, wgate_ref[...], preferred_element_type=F32) for x in normed]
    for rows, h2, gate, ple in zip(halves, h2s, gates, ples):
        o_ref[rows, :] = h2 + jax.nn.sigmoid(gate) * ple


def _combine_ple(layer, pos, route, h1, ys, p, wp, g, wgate):
    T = h1.shape[0]
    tile = min(GATHER_TILE, T)
    n = T // tile
    pos3 = pos.reshape(n, 1, 2 * tile)
    const = lambda *shape: pl.BlockSpec(shape, lambda i: (0,) * len(shape))
    row = lambda width: pl.BlockSpec((tile, width), lambda i: (i, 0))
    return pl.pallas_call(
        functools.partial(_combine_ple_kernel, tile=tile),
        grid=(n,),
        in_specs=[pl.BlockSpec((1, 1, 2 * tile), lambda i: (0, 0, 0), memory_space=pltpu.SMEM),
                  pl.BlockSpec((1, 1, 2 * tile), lambda i: (jnp.minimum(i + 1, n - 1), 0, 0),
                               memory_space=pltpu.SMEM),
                  row(LANES), row(D_MODEL), pl.BlockSpec(memory_space=pl.ANY),
                  pl.BlockSpec((None, tile, D_PLE), lambda i: (layer, i, 0)),
                  const(D_PLE, D_MODEL), const(1, D_MODEL), const(D_MODEL, D_MODEL)],
        out_specs=row(D_MODEL),
        out_shape=jax.ShapeDtypeStruct((T, D_MODEL), F32),
        scratch_shapes=[pltpu.VMEM((2, 2, tile, D_MODEL), F32), pltpu.SemaphoreType.DMA((2,))],
        compiler_params=_cparams(1),
        name="combine_ple",
    )(pos3, pos3, route, h1, ys, p, wp, g, wgate)


def _rope(z, cos, sin_lo, sin_hi):
    half = ROT_DIM // 2
    outs = []
    for j in range(z.shape[1] // LANES):
        zj = z[:, j * LANES:(j + 1) * LANES]
        outs.append(zj * cos + pltpu.roll(zj, LANES - half, 1) * sin_lo + pltpu.roll(zj, half, 1) * sin_hi)
    return outs[0] if len(outs) == 1 else jnp.concatenate(outs, axis=1)


def _proj_odd_kernel(h_ref, g_ref, w_ref, qg_ref, kg_ref, cos_ref, slo_ref, shi_ref,
                     q_ref, k_ref, v_ref, glu_ref):
    a = _rms(h_ref[...], g_ref[...]).astype(BF16)
    cos, slo, shi = cos_ref[...], slo_ref[...], shi_ref[...]
    v0 = SWA_WIDTH + KV_WIDTH
    d0 = v0 + KV_WIDTH
    zq = jnp.dot(a, w_ref[:, 0:SWA_WIDTH], preferred_element_type=F32)
    zk = jnp.dot(a, w_ref[:, SWA_WIDTH:v0], preferred_element_type=F32)
    zv = jnp.dot(a, w_ref[:, v0:d0], preferred_element_type=F32)
    zd = jnp.dot(a, w_ref[:, d0:d0 + 2 * CONV_CH], preferred_element_type=F32)
    q_ref[...] = (_rope(_head_rms(zq, qg_ref[...]), cos, slo, shi) * (HEAD_DIM ** -0.5)).astype(BF16)
    k_ref[...] = _rope(_head_rms(zk, kg_ref[...]), cos, slo, shi).astype(BF16)
    v_ref[...] = zv.astype(BF16)
    glu_ref[...] = zd[:, :CONV_CH] * jax.nn.sigmoid(zd[:, CONV_CH:])


def _proj_odd(h, g, w, qg, kg, cos, slo, shi, *, seq):
    T = h.shape[0]
    tm = min(ROW_TILE, seq)
    tps = seq // tm
    const = lambda *shape: pl.BlockSpec(shape, lambda i: (0,) * len(shape))
    row = lambda width: pl.BlockSpec((tm, width), lambda i: (i, 0))
    tab = pl.BlockSpec((tm, LANES), lambda i: (i % tps, 0))
    return pl.pallas_call(
        _proj_odd_kernel,
        grid=(T // tm,),
        in_specs=[row(D_MODEL), const(1, D_MODEL), const(*w.shape), const(1, SWA_WIDTH),
                  const(1, KV_WIDTH), tab, tab, tab],
        out_specs=[row(SWA_WIDTH), row(KV_WIDTH), row(KV_WIDTH), row(CONV_CH)],
        out_shape=[jax.ShapeDtypeStruct((T, SWA_WIDTH), BF16),
                   jax.ShapeDtypeStruct((T, KV_WIDTH), BF16),
                   jax.ShapeDtypeStruct((T, KV_WIDTH), BF16),
                   jax.ShapeDtypeStruct((T, CONV_CH), F32)],
        compiler_params=_cparams(1),
        name="proj_odd",
    )(h, g, w, qg, kg, cos, slo, shi)


def _swa_kernel(sink_ref, q_ref, k_ref, v_ref, o_ref, *, seq):
    lo = _lane_lo((1, LANES))
    nt = (((1,), (1,)), ((), ()))
    W = WINDOW
    qi = lax.broadcasted_iota(jnp.int32, (W, 2 * W), 0)
    kj = lax.broadcasted_iota(jnp.int32, (W, 2 * W), 1)
    band = (kj > qi) & (kj <= qi + W)

    def block(n, kstart, mask):
        qrow = pl.ds(pl.multiple_of(n * W, W), W)
        kwin = pl.ds(pl.multiple_of(kstart, W), 2 * W)
        ks = k_ref[kwin, :]
        vs = v_ref[kwin, :]
        n_tiles = SWA_WIDTH // LANES
        heads, scores = [], []
        for j in range(n_tiles):
            q = q_ref[qrow, j * LANES:(j + 1) * LANES]
            zero = jnp.zeros_like(q)
            for head, qh in ((j, jnp.where(lo, q, zero)), (n_tiles + j, jnp.where(lo, zero, q))):
                heads.append(head)
                scores.append(lax.dot_general(qh, ks, nt, preferred_element_type=F32))
        probs, sums = [], []
        for head, s in zip(heads, scores):
            s = jnp.where(mask, s, NEG)
            sink = sink_ref[head]
            m = jnp.maximum(jnp.max(s, axis=-1, keepdims=True), sink)
            p = jnp.exp(s - m)
            sums.append(jnp.sum(p, axis=-1, keepdims=True) + jnp.exp(sink - m))
            probs.append(p.astype(BF16))
        outs = [jnp.dot(p, vs, preferred_element_type=F32) / l for p, l in zip(probs, sums)]
        for j in range(n_tiles):
            o_ref[qrow, j * LANES:(j + 1) * LANES] = jnp.where(lo, outs[2 * j], outs[2 * j + 1]).astype(BF16)

    block(0, 0, kj <= qi)

    def body(n, carry):
        block(n, (n - 1) * W, band)
        return carry

    lax.fori_loop(1, seq // W, body, 0)


def _swa_attention(sinks, q, k, v, *, seq):
    T = q.shape[0]
    B = T // seq
    return pl.pallas_call(
        functools.partial(_swa_kernel, seq=seq),
        grid=(B,),
        in_specs=[pl.BlockSpec(memory_space=pltpu.SMEM),
                  pl.BlockSpec((seq, SWA_WIDTH), lambda b: (b, 0)),
                  pl.BlockSpec((seq, KV_WIDTH), lambda b: (b, 0)),
                  pl.BlockSpec((seq, KV_WIDTH), lambda b: (b, 0))],
        out_specs=pl.BlockSpec((seq, SWA_WIDTH), lambda b: (b, 0)),
        out_shape=jax.ShapeDtypeStruct((T, SWA_WIDTH), BF16),
        compiler_params=_cparams(1),
        name="swa_attention",
    )(sinks, q, k, v)


def _conv_kernel(prev_ref, cur_ref, w_ref, g_ref, b_ref, o_ref, shift_ref, *, tile, sub):
    r = pl.program_id(1)
    rows = CONV_HALO + tile
    tail = prev_ref[tile - CONV_HALO:, :]
    shift_ref[0, 0:CONV_HALO, :] = jnp.where(r > 0, tail, jnp.zeros_like(tail))
    shift_ref[0, CONV_HALO:rows, :] = cur_ref[...]
    shift_ref[0, rows:rows + SUBLANES, :] = jnp.zeros((SUBLANES, CONV_CH), F32)
    for o in range(1, SUBLANES):
        shift_ref[o, 0:rows, :] = shift_ref[0, o:o + rows, :]
    w = w_ref[...]
    first = CONV_HALO - (CONV_WIDTH - 1)
    for s in range(tile // sub):
        acc = jnp.zeros((sub, CONV_CH), F32)
        for j in range(CONV_WIDTH):
            start = s * sub + first + j
            o = start % SUBLANES
            acc = acc + shift_ref[o, start - o:start - o + sub, :] * w[j:j + 1, :]
        y = _layernorm(acc, g_ref[...], b_ref[...])
        o_ref[s * sub:(s + 1) * sub, :] = (y * jax.nn.sigmoid(y)).astype(BF16)


def _conv_module(glu, w, g, b, *, seq):
    T = glu.shape[0]
    B = T // seq
    tile = min(CONV_TILE, seq)
    nr = seq // tile
    const = lambda *shape: pl.BlockSpec(shape, lambda bb, r: (0,) * len(shape))
    return pl.pallas_call(
        functools.partial(_conv_kernel, tile=tile, sub=64),
        grid=(B, nr),
        in_specs=[pl.BlockSpec((tile, CONV_CH), lambda bb, r: (bb * nr + jnp.maximum(r - 1, 0), 0)),
                  pl.BlockSpec((tile, CONV_CH), lambda bb, r: (bb * nr + r, 0)),
                  const(CONV_WIDTH, CONV_CH), const(1, CONV_CH), const(1, CONV_CH)],
        out_specs=pl.BlockSpec((tile, CONV_CH), lambda bb, r: (bb * nr + r, 0)),
        out_shape=jax.ShapeDtypeStruct((T, CONV_CH), BF16),
        scratch_shapes=[pltpu.VMEM((SUBLANES, CONV_HALO + tile + SUBLANES, CONV_CH), F32)],
        compiler_params=_cparams(2),
        name="conv_module",
    )(glu, glu, w, g, b)


def _routing_tables(route, counts, n_tiles):
    e = route[:, 0:2].astype(jnp.int32)
    rank = route[:, 2:4].astype(jnp.int32)
    cnt = counts[0, :N_EXPERTS].astype(jnp.int32)
    tiles = (cnt + EXPERT_TILE - 1) // EXPERT_TILE
    tile_end = jnp.cumsum(tiles)
    offset = (tile_end - tiles) * EXPERT_TILE
    onehot = e[:, :, None] == jnp.arange(N_EXPERTS, dtype=jnp.int32)
    pos = (rank + jnp.sum(jnp.where(onehot, offset, 0), axis=-1)).reshape(-1)
    n_used = tile_end[-1]
    tile_id = jnp.minimum(jnp.arange(n_tiles, dtype=jnp.int32), n_used - 1)
    tile_expert = jnp.sum((tile_end[None, :] <= tile_id[:, None]).astype(jnp.int32), axis=1)
    first_pad = offset + cnt
    n_single = (-first_pad) % SUBLANES
    pads = jnp.stack([first_pad, n_single, tiles * EXPERT_TILE - cnt - n_single,
                      jnp.broadcast_to(tile_end[-1], cnt.shape)])
    return pos, pads, tile_expert, n_used.reshape(1).astype(jnp.int32)


def _moe_ple(h, ya, yb, wo, layer, norm_ffn, wr, br, wg, wu, wd, p, wp, ple_norm, wgate):
    T = h.shape[0]
    n_tiles = (2 * T) // EXPERT_TILE + N_EXPERTS
    h1, m, route, counts = _outproj_router(h, ya, yb, wo, norm_ffn, wr, br)
    pos, pads, tile_expert, n_used = _routing_tables(route, counts, n_tiles)
    xs = _dispatch(pads, pos, m, n_tiles * EXPERT_TILE)
    ys = _experts(layer, tile_expert, n_used, xs, wg, wu, wd)
    return _combine_ple(layer, pos, route, h1, ys, p, wp, ple_norm, wgate)


def _router_weights(w_coarse, b_coarse, w_fine, b_fine):
    wf = w_fine.transpose(1, 0, 2).reshape(D_MODEL, N_EXPERTS)
    wr = jnp.concatenate([wf, w_coarse, jnp.zeros((D_MODEL, LANES - N_EXPERTS - N_GROUPS), F32)], axis=1)
    br = jnp.concatenate([b_fine.reshape(-1), b_coarse, jnp.zeros((LANES - N_EXPERTS - N_GROUPS,), F32)])
    w_hi = wr.astype(BF16)
    w_lo = (wr - w_hi.astype(F32)).astype(BF16)
    return jnp.concatenate([w_hi, w_lo], axis=1), br.reshape(1, LANES)


def _rope_tables(seq):
    half = ROT_DIM // 2
    inv_freq = ROPE_THETA ** (-jnp.arange(half, dtype=F32) * 2.0 / ROT_DIM)
    ang = jnp.arange(seq, dtype=F32)[:, None] * inv_freq[None, :]
    cos, sin = jnp.cos(ang), jnp.sin(ang)
    zeros = jnp.zeros((seq, HEAD_DIM - ROT_DIM), F32)
    z8 = jnp.zeros((seq, half), F32)
    cos_h = jnp.concatenate([cos, cos, zeros + 1.0], axis=1)
    slo_h = jnp.concatenate([-sin, z8, zeros], axis=1)
    shi_h = jnp.concatenate([z8, sin, zeros], axis=1)
    two = lambda t: jnp.concatenate([t, t], axis=1)
    return two(cos_h), two(slo_h), two(shi_h)


def kernel(x, p, norm_mix, even_w_in, fox_b_f, gmlp_ln_g, gmlp_ln_b, gmlp_w_s, gmlp_b_s, fox_q_norm, fox_k_norm, even_w_out, odd_w_in, swa_q_norm, swa_k_norm, swa_sinks, conv_w, conv_ln_g, conv_ln_b, odd_w_out, norm_ffn, moe_w_coarse, moe_b_coarse, moe_w_fine, moe_b_fine, moe_w_gate, moe_w_up, moe_w_down, ple_w_proj, ple_norm, ple_w_gate):
    B, S, D = x.shape
    T = B * S
    h = x.reshape(T, D)
    p = p.reshape(p.shape[0], T, D_PLE)
    row = lambda v: v.reshape(1, -1)

    def moe_args(i):
        wr, br = _router_weights(moe_w_coarse[i], moe_b_coarse[i], moe_w_fine[i], moe_b_fine[i])
        return (i, row(norm_ffn[i]), wr, br, moe_w_gate, moe_w_up, moe_w_down, p,
                ple_w_proj[i].astype(BF16), row(ple_norm[i]), ple_w_gate[i].astype(BF16))

    n_main = 2 * GMLP_WIDTH + 3 * FOX_WIDTH
    w_f = jnp.pad(jnp.tile(even_w_in[0][:, n_main:], (1, BIAS_PIECES)),
                  ((0, 0), (0, LANES - BIAS_PIECES * FOX_HEADS)))
    w_in = jnp.concatenate([even_w_in[0][:, :n_main], w_f], axis=1).astype(BF16)
    b_f = jnp.pad(jnp.tile(fox_b_f[0], BIAS_PIECES), (0, LANES - BIAS_PIECES * FOX_HEADS)).reshape(1, LANES)
    bs_full = jnp.repeat(gmlp_b_s[0].T, HEAD_DIM, axis=1)
    ya, q, k, v, xq, xk = _proj_even(
        h, row(norm_mix[0]), w_in, b_f, row(gmlp_ln_g[0]), row(gmlp_ln_b[0]), gmlp_w_s[0], bs_full,
        row(jnp.tile(fox_q_norm[0], FOX_HEADS)), row(jnp.tile(fox_k_norm[0], FOX_HEADS)), seq=S)
    yb = _fox_attention(q, xq, k, xk, v, seq=S)
    h = _moe_ple(h, ya, yb, even_w_out[0].astype(BF16), *moe_args(0))

    order = jnp.array([0, 4, 1, 5, 2, 6, 3, 7])
    cols = (order[:, None] * HEAD_DIM + jnp.arange(HEAD_DIM)[None, :]).reshape(-1)
    w_odd = jnp.concatenate([odd_w_in[0][:, :SWA_WIDTH][:, cols], odd_w_in[0][:, SWA_WIDTH:]], axis=1).astype(BF16)
    w_out_odd = jnp.concatenate([odd_w_out[0][:SWA_WIDTH][cols], odd_w_out[0][SWA_WIDTH:]], axis=0).astype(BF16)
    cos, slo, shi = _rope_tables(S)
    q, k, v, glu = _proj_odd(h, row(norm_mix[1]), w_odd, row(jnp.tile(swa_q_norm[0], 8)),
                             row(jnp.tile(swa_k_norm[0], 2)), cos, slo, shi, seq=S)
    yc = _swa_attention(swa_sinks[0], q, k, v, seq=S)
    yd = _conv_module(glu, conv_w[0], row(conv_ln_g[0]), row(conv_ln_b[0]), seq=S)
    h = _moe_ple(h, yc, yd, w_out_odd, *moe_args(1))
    return h.reshape(B, S, D)
```

```python
import functools

import jax
import jax.numpy as jnp
import numpy as np
from jax import lax
from jax.experimental import pallas as pl
from jax.experimental.pallas import tpu as pltpu

F32 = jnp.float32
BF16 = jnp.bfloat16
HIGHEST = lax.Precision.HIGHEST

D_MODEL = 1024
HEAD_DIM = 64
LANES = 128
GMLP_WIDTH = 512
CHUNK = 128
FOX_WIDTH = 512
FOX_HEADS = 8
SWA_WIDTH = 512
KV_WIDTH = 128
WINDOW = 128
CONV_CH = 512
CONV_WIDTH = 31
CONV_HALO = 32
ROPE_THETA = 500000.0
ROT_DIM = 16
N_GROUPS = 4
EXPERTS_PER_GROUP = 8
N_EXPERTS = 32
D_EXPERT = 256
D_PLE = 256
EPS = 1e-6
NEG = -1e30
LOG2E = 1.4426950408889634
BIAS_PIECES = 3

EXPERT_TILE = 512
PAD_BITS = 9
SUBLANES = 8
SUBLANE_BITS = 3
ROW_TILE = 1024
ROUTER_BLOCKS = 8
GATHER_TILE = 512
DISPATCH_TILE = 2048
DMA_UNROLL = 8
ATTN_TILE = 256
FOX_PAIRS_PER_STEP = 2
CONV_TILE = 256
VMEM_LIMIT = 56 * 1024 * 1024


def _cparams(n_axes=1, flags=None):
    return pltpu.CompilerParams(dimension_semantics=("arbitrary",) * n_axes,
                                vmem_limit_bytes=VMEM_LIMIT, flags=flags)


def _rms(x, gain):
    return x * lax.rsqrt(jnp.mean(x * x, axis=-1, keepdims=True) + EPS) * gain


def _layernorm(x, g, b):
    mu = jnp.mean(x, axis=-1, keepdims=True)
    xc = x - mu
    var = jnp.mean(xc * xc, axis=-1, keepdims=True)
    return xc * lax.rsqrt(var + EPS) * g + b


def _head_rms(z, gain):
    r = (lax.broadcasted_iota(jnp.int32, (2 * LANES, LANES), 0) % LANES) // HEAD_DIM
    c = lax.broadcasted_iota(jnp.int32, (2 * LANES, LANES), 1) // HEAD_DIM
    bd = jnp.where(r == c, 1.0 / HEAD_DIM, 0.0).astype(BF16)
    outs = []
    for j in range(z.shape[1] // LANES):
        zj = z[:, j * LANES:(j + 1) * LANES]
        sq = zj * zj
        sq_hi = sq.astype(BF16)
        sq_lo = (sq - sq_hi.astype(F32)).astype(BF16)
        ms = jnp.dot(jnp.concatenate([sq_hi, sq_lo], axis=1), bd, preferred_element_type=F32)
        outs.append(zj * lax.rsqrt(ms + EPS))
    zn = outs[0] if len(outs) == 1 else jnp.concatenate(outs, axis=1)
    return zn * gain


def _lane_lo(shape):
    return (lax.broadcasted_iota(jnp.int32, shape, len(shape) - 1) % LANES) < HEAD_DIM


def _proj_even_kernel(h_ref, g_ref, w_ref, bf_ref, lng_ref, lnb_ref, ws_ref, bs_ref, qg_ref, kg_ref,
                      pq_ref, pk_ref, oneq_ref, onek_ref,
                      ya_ref, q_ref, k_ref, v_ref, xq_ref, xk_ref, carry_ref, *, tm, tiles_per_seq):
    i = pl.program_id(0)
    a = _rms(h_ref[...], g_ref[...]).astype(BF16)

    q0 = 2 * GMLP_WIDTH
    za = jnp.dot(a, w_ref[:, 0:q0], preferred_element_type=F32)
    zq = jnp.dot(a, w_ref[:, q0:q0 + FOX_WIDTH], preferred_element_type=F32)
    zk = jnp.dot(a, w_ref[:, q0 + FOX_WIDTH:q0 + 2 * FOX_WIDTH], preferred_element_type=F32)
    zv = jnp.dot(a, w_ref[:, q0 + 2 * FOX_WIDTH:q0 + 3 * FOX_WIDTH], preferred_element_type=F32)
    zf = jnp.dot(a, w_ref[:, q0 + 3 * FOX_WIDTH:], preferred_element_type=F32) + bf_ref[...]

    q_ref[...] = (_head_rms(zq, qg_ref[...]) * (LOG2E * HEAD_DIM ** -0.5)).astype(BF16)
    k_ref[...] = _head_rms(zk, kg_ref[...]).astype(BF16)
    v_ref[...] = zv.astype(BF16)

    rr = lax.broadcasted_iota(jnp.int32, (CHUNK, CHUNK), 0)
    cc = lax.broadcasted_iota(jnp.int32, (CHUNK, CHUNK), 1)

    ls = jnp.minimum(zf, 0.0) - jnp.log(1.0 + jnp.exp(-jnp.abs(zf)))

    @pl.when(i % tiles_per_seq == 0)
    def _():
        carry_ref[...] = jnp.zeros_like(carry_ref)

    def split3(x):
        hi = x.astype(BF16)
        r1 = x - hi.astype(F32)
        mid = r1.astype(BF16)
        return hi, mid, (r1 - mid.astype(F32)).astype(BF16)

    tri = jnp.where(rr >= cc, 1.0, 0.0).astype(BF16)
    pieces = jnp.concatenate(split3(ls), axis=1)
    running = carry_ref[...]
    blocks = []
    for b in range(tm // CHUNK):
        d = jnp.dot(tri, pieces[b * CHUNK:(b + 1) * CHUNK, :], preferred_element_type=F32)
        blk = (d[:, :LANES] + d[:, LANES:2 * LANES]) + d[:, 2 * LANES:] + running
        running = blk[CHUNK - 1:CHUNK, :]
        blocks.append(blk)
    carry_ref[...] = running
    c = jnp.concatenate(blocks, axis=0)

    za = jax.nn.gelu(za)
    u = za[:, :GMLP_WIDTH]
    vln = _layernorm(za[:, GMLP_WIDTH:], lng_ref[...], lnb_ref[...]).astype(BF16)
    lo = _lane_lo((CHUNK, LANES))
    for j in range(GMLP_WIDTH // LANES):
        w_a = jnp.where(rr >= cc, ws_ref[2 * j], 0.0).astype(BF16)
        w_b = jnp.where(rr >= cc, ws_ref[2 * j + 1], 0.0).astype(BF16)
        cols = slice(j * LANES, (j + 1) * LANES)
        for blk in range(tm // CHUNK):
            rows = slice(blk * CHUNK, (blk + 1) * CHUNK)
            vp = vln[rows, cols]
            mixed = jnp.where(lo, jnp.dot(w_a, vp, preferred_element_type=F32),
                              jnp.dot(w_b, vp, preferred_element_type=F32)) + bs_ref[:, cols]
            ya_ref[rows, cols] = (u[rows, cols] * mixed).astype(BF16)

    hi, mid, low = split3(c * LOG2E)
    group = lax.broadcasted_iota(jnp.int32, (tm, LANES), 1) // FOX_HEADS
    sel = jnp.where(group == 0, hi, jnp.where(group == 1, mid, low))
    xq_ref[...] = (jnp.dot(sel, pq_ref[...], preferred_element_type=F32) + oneq_ref[...]).astype(BF16)
    xk_ref[...] = (jnp.dot(sel, pk_ref[...], preferred_element_type=F32) + onek_ref[...]).astype(BF16)


def _proj_even(h, g, w, bf, lng, lnb, ws, bs_full, qg, kg, *, seq):
    T = h.shape[0]
    tm = min(ROW_TILE, seq)
    n_in = w.shape[1]
    const = lambda *shape: pl.BlockSpec(shape, lambda i: (0,) * len(shape))
    row = lambda width: pl.BlockSpec((tm, width), lambda i: (i, 0))
    tps = seq // tm
    return pl.pallas_call(
        functools.partial(_proj_even_kernel, tm=tm, tiles_per_seq=tps),
        grid=(T // tm,),
        in_specs=[row(D_MODEL), const(1, D_MODEL), const(D_MODEL, n_in), const(1, LANES),
                  const(1, GMLP_WIDTH), const(1, GMLP_WIDTH), const(8, CHUNK, CHUNK),
                  const(CHUNK, GMLP_WIDTH), const(1, FOX_WIDTH), const(1, FOX_WIDTH),
                  const(LANES, FOX_WIDTH), const(LANES, FOX_WIDTH),
                  const(1, FOX_WIDTH), const(1, FOX_WIDTH)],
        out_specs=[row(GMLP_WIDTH)] + [row(FOX_WIDTH)] * 5,
        out_shape=[jax.ShapeDtypeStruct((T, GMLP_WIDTH), BF16)]
                  + [jax.ShapeDtypeStruct((T, FOX_WIDTH), BF16)] * 5,
        scratch_shapes=[pltpu.VMEM((1, LANES), F32)],
        compiler_params=_cparams(1),
        name="proj_even",
    )(h, g, w, bf, lng, lnb, ws, bs_full, qg, kg, *_bias_placement())


def _bias_placement():
    pq = np.zeros((LANES, FOX_WIDTH), np.float32)
    pk = np.zeros((LANES, FOX_WIDTH), np.float32)
    oneq = np.zeros((1, FOX_WIDTH), np.float32)
    onek = np.zeros((1, FOX_WIDTH), np.float32)
    for head in range(FOX_HEADS):
        base = (head // 2) * LANES + (HEAD_DIM if head % 2 == 0 else 0)
        for piece in range(BIAS_PIECES):
            pq[piece * FOX_HEADS + head, base + piece] = 1.0
            onek[0, base + piece] = 1.0
            pk[piece * FOX_HEADS + head, base + BIAS_PIECES + piece] = -1.0
            oneq[0, base + BIAS_PIECES + piece] = 1.0
    return (jnp.asarray(pq, BF16), jnp.asarray(pk, BF16), jnp.asarray(oneq), jnp.asarray(onek))


def _fox_tile(n_tiles, q_ref, xq_ref, k_ref, xk_ref, v_ref, o_ref, *, tq):
    half = tq // 2
    lo = _lane_lo((1, LANES))
    rr = lax.broadcasted_iota(jnp.int32, (half, tq), 0)
    cc = lax.broadcasted_iota(jnp.int32, (half, tq), 1)
    nt = (((1,), (1,)), ((), ()))
    pairs = [slice(pp * LANES, (pp + 1) * LANES) for pp in range(FOX_PAIRS_PER_STEP)]
    chains = [(pp, hf, head) for pp in range(len(pairs)) for hf in range(2) for head in range(2)]
    q_aug = {}
    for pp, cols in enumerate(pairs):
        for hf in range(2):
            rows = slice(hf * half, (hf + 1) * half)
            q, xq = q_ref[rows, cols], xq_ref[rows, cols]
            q_aug[pp, hf, 0], q_aug[pp, hf, 1] = jnp.where(lo, q, xq), jnp.where(lo, xq, q)

    def score_products(j):
        rows = slice(j * tq, (j + 1) * tq)
        k_aug = {}
        for pp, cols in enumerate(pairs):
            ks, xk = k_ref[rows, cols], xk_ref[rows, cols]
            k_aug[pp, 0], k_aug[pp, 1] = jnp.where(lo, ks, xk), jnp.where(lo, xk, ks)
        return [lax.dot_general(q_aug[pp, hf, head], k_aug[pp, head], nt, preferred_element_type=F32)
                for pp, hf, head in chains]

    maxes = [jnp.full((half, 1), NEG, F32) for _ in chains]
    accs = [jnp.zeros((half, LANES), F32) for _ in chains]
    scores = score_products(0)
    for j in range(n_tiles):
        next_scores = score_products(j + 1) if j + 1 < n_tiles else None
        v_aug = {}
        for pp, cols in enumerate(pairs):
            vs = v_ref[j * tq:(j + 1) * tq, cols]
            one = jnp.ones_like(vs)
            v_aug[pp, 0], v_aug[pp, 1] = jnp.where(lo, vs, one), jnp.where(lo, one, vs)
        for c, (pp, hf, head) in enumerate(chains):
            s = scores[c]
            if j == n_tiles - 1:
                s = jnp.where(cc <= rr + hf * half, s, NEG)
            n = jnp.maximum(maxes[c], jnp.max(s, axis=-1, keepdims=True))
            p = jnp.exp2(s - n).astype(BF16)
            accs[c] = accs[c] * jnp.exp2(maxes[c] - n) + jnp.dot(p, v_aug[pp, head], preferred_element_type=F32)
            maxes[c] = n
        scores = next_scores
    for c in range(0, len(chains), 2):
        pp, hf, _ = chains[c]
        norm = [acc / pltpu.roll(acc, HEAD_DIM, 1) for acc in accs[c:c + 2]]
        o_ref[hf * half:(hf + 1) * half, pairs[pp]] = jnp.where(lo, norm[0], norm[1]).astype(BF16)


def _fox_kernel(q_ref, xq_ref, k_ref, xk_ref, v_ref, o_ref, *, tq, nq):
    i = pl.program_id(2)
    for c in range(nq):
        pl.when(i == c)(functools.partial(_fox_tile, c + 1, q_ref, xq_ref, k_ref, xk_ref, v_ref, o_ref, tq=tq))


def _fox_attention(q, xq, k, xk, v, *, seq):
    T = q.shape[0]
    B = T // seq
    tq = min(ATTN_TILE, seq)
    nq = seq // tq
    width = FOX_PAIRS_PER_STEP * LANES
    tile = pl.BlockSpec((tq, width), lambda b, hp, i: (b * nq + i, hp))
    whole = pl.BlockSpec((seq, width), lambda b, hp, i: (b, hp))
    return pl.pallas_call(
        functools.partial(_fox_kernel, tq=tq, nq=nq),
        grid=(B, FOX_WIDTH // width, nq),
        in_specs=[tile, tile, whole, whole, whole],
        out_specs=tile,
        out_shape=jax.ShapeDtypeStruct((T, FOX_WIDTH), BF16),
        compiler_params=_cparams(3),
        name="fox_attention",
    )(q, xq, k, xk, v)


def _outproj_router_kernel(h_ref, ya_ref, yb_ref, wo_ref, g_ref, wr_ref, br_ref,
                           h1_ref, m_ref, route_ref, cnt_ref, carry_ref, *, tm):
    i = pl.program_id(0)
    half = wo_ref.shape[0] // 2
    nb = ROUTER_BLOCKS
    rb = tm // nb
    blocks = [slice(b * rb, (b + 1) * rb) for b in range(nb)]

    @pl.when(i == 0)
    def _():
        carry_ref[...] = jnp.zeros_like(carry_ref)

    mixes = [jnp.dot(ya_ref[rows, :], wo_ref[0:half, :], preferred_element_type=F32)
             + jnp.dot(yb_ref[rows, :], wo_ref[half:, :], preferred_element_type=F32) for rows in blocks]
    ms = []
    for rows, mix in zip(blocks, mixes):
        h1 = h_ref[rows, :] + mix
        h1_ref[rows, :] = h1
        m = _rms(h1, g_ref[...])
        m_ref[rows, :] = m
        ms.append(m)

    logits = []
    for m in ms:
        m_hi = m.astype(BF16)
        m_lo = (m - m_hi.astype(F32)).astype(BF16)
        hh = jnp.dot(m_hi, wr_ref[...], preferred_element_type=F32)
        lh = jnp.dot(m_lo, wr_ref[:, :LANES], preferred_element_type=F32)
        logits.append(hh[:, :LANES] + (hh[:, LANES:] + lh) + br_ref[...])

    lane_i = lax.broadcasted_iota(jnp.int32, (rb, LANES), 1)
    lane = lane_i.astype(F32)
    group_of_lane = (lane_i // EXPERTS_PER_GROUP).astype(F32)
    is_coarse = (lane_i >= N_EXPERTS) & (lane_i < N_EXPERTS + N_GROUPS)
    picks = []
    for lg in logits:
        coarse = jnp.where(is_coarse, lg, NEG)
        cmax = jnp.max(coarse, axis=-1, keepdims=True)
        gidx = jnp.min(jnp.where(coarse == cmax, lane - N_EXPERTS, float(LANES)), axis=-1, keepdims=True)
        p_g = 1.0 / jnp.sum(jnp.where(is_coarse, jnp.exp(coarse - cmax), 0.0), axis=-1, keepdims=True)
        in_group = (lane_i < N_EXPERTS) & (group_of_lane == gidx)
        fine = jnp.where(in_group, lg, NEG)
        v1 = jnp.max(fine, axis=-1, keepdims=True)
        i1 = jnp.min(jnp.where(fine == v1, lane, float(LANES)), axis=-1, keepdims=True)
        fine2 = jnp.where(lane == i1, NEG, fine)
        v2 = jnp.max(fine2, axis=-1, keepdims=True)
        i2 = jnp.min(jnp.where(fine2 == v2, lane, float(LANES)), axis=-1, keepdims=True)
        e2 = jnp.exp(v2 - v1)
        picks.append((i1, i2, p_g / (1.0 + e2), p_g * e2 / (1.0 + e2)))

    tr = lax.broadcasted_iota(jnp.int32, (rb, rb), 0)
    tc = lax.broadcasted_iota(jnp.int32, (rb, rb), 1)
    strict = jnp.where(tr > tc, 1.0, 0.0).astype(BF16)
    onehots = [jnp.where((lane == i1) | (lane == i2), 1.0, 0.0).astype(F32) for i1, i2, _, _ in picks]
    befores = [jnp.dot(strict, oh.astype(BF16), preferred_element_type=F32) for oh in onehots]
    total = carry_ref[...]
    for rows, (i1, i2, w1, w2), oh, before in zip(blocks, picks, onehots, befores):
        before = before + total
        r1 = jnp.sum(jnp.where(lane == i1, before, 0.0), axis=-1, keepdims=True)
        r2 = jnp.sum(jnp.where(lane == i2, before, 0.0), axis=-1, keepdims=True)
        total = total + jnp.sum(oh, axis=0, keepdims=True)
        route = jnp.where(lane == 0, i1, 0.0)
        route = jnp.where(lane == 1, i2, route)
        route = jnp.where(lane == 2, r1, route)
        route = jnp.where(lane == 3, r2, route)
        route = jnp.where(lane == 4, w1, route)
        route = jnp.where(lane == 5, w2, route)
        route_ref[rows, :] = route
    carry_ref[...] = total
    cnt_ref[...] = jnp.broadcast_to(total, cnt_ref.shape)


def _outproj_router(h, ya, yb, wo, g, wr, br):
    T = h.shape[0]
    tm = min(ROW_TILE, T)
    const = lambda *shape: pl.BlockSpec(shape, lambda i: (0,) * len(shape))
    row = lambda width: pl.BlockSpec((tm, width), lambda i: (i, 0))
    return pl.pallas_call(
        functools.partial(_outproj_router_kernel, tm=tm),
        grid=(T // tm,),
        in_specs=[row(D_MODEL), row(ya.shape[1]), row(yb.shape[1]), const(*wo.shape),
                  const(1, D_MODEL), const(D_MODEL, 2 * LANES), const(1, LANES)],
        out_specs=[row(D_MODEL), row(D_MODEL), row(LANES), const(8, LANES)],
        out_shape=[jax.ShapeDtypeStruct((T, D_MODEL), F32),
                   jax.ShapeDtypeStruct((T, D_MODEL), F32),
                   jax.ShapeDtypeStruct((T, LANES), F32),
                   jax.ShapeDtypeStruct((8, LANES), F32)],
        scratch_shapes=[pltpu.VMEM((1, LANES), F32)],
        compiler_params=_cparams(1),
        name="outproj_router",
    )(h, ya, yb, wo, g, wr, br)


def _dispatch_kernel(pad_ref, pos_ref, m_ref, xs_hbm, zeros_ref, sem, pad_sem, *, tile, table_tokens):
    @pl.when(pl.program_id(0) == 0)
    def _():
        zeros_ref[...] = jnp.zeros_like(zeros_ref)

        def pad_copies(e, wait):
            first, n_single, n_block = pad_ref[0, e], pad_ref[1, e], pad_ref[2, e]
            for r in range(SUBLANES - 1):
                copy = pltpu.make_async_copy(zeros_ref.at[pl.ds(0, 1)], xs_hbm.at[pl.ds(first + r, 1)], pad_sem)
                pl.when(r < n_single)(copy.wait if wait else copy.start)
            done = first + n_single
            for bit in reversed(range(SUBLANE_BITS, PAD_BITS)):
                size = 1 << bit
                taken = (n_block & size) != 0
                copy = pltpu.make_async_copy(zeros_ref.at[pl.ds(0, size)],
                                             xs_hbm.at[pl.ds(pl.multiple_of(done, SUBLANES), size)], pad_sem)
                pl.when(taken)(copy.wait if wait else copy.start)
                done = done + jnp.where(taken, size, 0)

        def start(e, carry):
            pad_copies(e, False)
            return carry

        def finish(e, carry):
            pad_copies(e, True)
            return carry

        def tile_copy(j):
            return pltpu.make_async_copy(
                zeros_ref, xs_hbm.at[pl.ds(pl.multiple_of(j * EXPERT_TILE, EXPERT_TILE), EXPERT_TILE)], pad_sem)

        def start_tile(j, carry):
            tile_copy(j).start()
            return carry

        def finish_tile(j, carry):
            tile_copy(j).wait()
            return carry

        n_tiles = xs_hbm.shape[0] // EXPERT_TILE
        lax.fori_loop(0, N_EXPERTS, start, 0)
        lax.fori_loop(pad_ref[3, 0], n_tiles, start_tile, 0)
        lax.fori_loop(0, N_EXPERTS, finish, 0)
        lax.fori_loop(pad_ref[3, 0], n_tiles, finish_tile, 0)

    def issue(c, carry):
        base = pl.multiple_of(c * DMA_UNROLL, DMA_UNROLL)
        group = m_ref.at[pl.ds(base, DMA_UNROLL)]
        table_row = base // table_tokens
        table_col = 2 * (base % table_tokens)
        for u in range(DMA_UNROLL):
            for k in range(2):
                dst = xs_hbm.at[pl.ds(pos_ref[table_row, 0, table_col + 2 * u + k], 1)]
                pltpu.make_async_copy(group.at[pl.ds(u, 1)], dst, sem).start(priority=k)
        return carry

    lax.fori_loop(0, tile // DMA_UNROLL, issue, 0)
    for _ in range(2):
        pltpu.make_async_copy(m_ref, xs_hbm.at[pl.ds(0, tile)], sem).wait()


def _dispatch(pads, pos3, m, n_rows):
    T, width = m.shape
    tile = min(DISPATCH_TILE, T)
    table_rows = tile // (pos3.shape[2] // 2)
    grid_spec = pltpu.PrefetchScalarGridSpec(
        num_scalar_prefetch=1,
        grid=(T // tile,),
        in_specs=[pl.BlockSpec((table_rows, 1, pos3.shape[2]), lambda i, pads: (i, 0, 0), memory_space=pltpu.SMEM),
                  pl.BlockSpec((tile, width), lambda i, pads: (i, 0))],
        out_specs=pl.BlockSpec(memory_space=pl.ANY),
        scratch_shapes=[pltpu.VMEM((EXPERT_TILE, width), m.dtype),
                        pltpu.SemaphoreType.DMA(()), pltpu.SemaphoreType.DMA(())],
    )
    return pl.pallas_call(
        functools.partial(_dispatch_kernel, tile=tile, table_tokens=pos3.shape[2] // 2),
        grid_spec=grid_spec,
        out_shape=jax.ShapeDtypeStruct((n_rows, width), m.dtype),
        compiler_params=pltpu.CompilerParams(dimension_semantics=("arbitrary",),
                                             has_side_effects=True, vmem_limit_bytes=VMEM_LIMIT),
        name="moe_dispatch",
    )(pads, pos3, m)


def _experts_kernel(te_ref, nused_ref, xs_ref, wg_ref, wu_ref, wd_ref, ys_ref, wgu_b, wd_b):
    j = pl.program_id(0)
    prev = te_ref[jnp.maximum(j - 1, 0)]

    @pl.when((j == 0) | (te_ref[j] != prev))
    def _():
        wgu_b[:, :D_EXPERT] = wg_ref[...].astype(BF16)
        wgu_b[:, D_EXPERT:] = wu_ref[...].astype(BF16)
        wd_b[...] = wd_ref[...].astype(BF16)

    @pl.when(j < nused_ref[0])
    def _():
        gu = jnp.dot(xs_ref[...].astype(BF16), wgu_b[...], preferred_element_type=F32)
        g = gu[:, :D_EXPERT]
        act = g * jax.nn.sigmoid(g) * gu[:, D_EXPERT:]
        ys_ref[...] = jnp.dot(act.astype(BF16), wd_b[...], preferred_element_type=F32)

    @pl.when(j >= nused_ref[0])
    def _():
        ys_ref[...] = jnp.zeros_like(ys_ref)


def _experts(layer, tile_expert, n_used, xs, wg, wu, wd):
    n_rows = xs.shape[0]
    nt = n_rows // EXPERT_TILE
    grid_spec = pltpu.PrefetchScalarGridSpec(
        num_scalar_prefetch=2,
        grid=(nt,),
        in_specs=[pl.BlockSpec((EXPERT_TILE, D_MODEL), lambda j, te, nu: (jnp.minimum(j, nu[0] - 1), 0)),
                  pl.BlockSpec((None, None, D_MODEL, D_EXPERT), lambda j, te, nu: (layer, te[j], 0, 0)),
                  pl.BlockSpec((None, None, D_MODEL, D_EXPERT), lambda j, te, nu: (layer, te[j], 0, 0)),
                  pl.BlockSpec((None, None, D_EXPERT, D_MODEL), lambda j, te, nu: (layer, te[j], 0, 0))],
        out_specs=pl.BlockSpec((EXPERT_TILE, D_MODEL), lambda j, te, nu: (j, 0)),
        scratch_shapes=[pltpu.VMEM((D_MODEL, 2 * D_EXPERT), BF16),
                        pltpu.VMEM((D_EXPERT, D_MODEL), BF16)],
    )
    return pl.pallas_call(
        _experts_kernel,
        grid_spec=grid_spec,
        out_shape=jax.ShapeDtypeStruct((n_rows, D_MODEL), F32),
        compiler_params=_cparams(1),
        name="moe_experts",
    )(tile_expert, n_used, xs, wg, wu, wd)


def _combine_ple_kernel(pos_ref, next_pos_ref, route_ref, h1_ref, ys_hbm, p_ref, wp_ref, g_ref, wgate_ref,
                        o_ref, ybuf, sems, *, tile):
    i = pl.program_id(0)
    n = pl.num_programs(0)
    slot = i % 2

    def gather(table, s):
        def issue(c, carry):
            base = pl.multiple_of(c * DMA_UNROLL, DMA_UNROLL)
            for k in range(2):
                group = ybuf.at[s, k, pl.ds(base, DMA_UNROLL)]
                for u in range(DMA_UNROLL):
                    pltpu.make_async_copy(ys_hbm.at[pl.ds(table[0, 0, 2 * (base + u) + k], 1)],
                                          group.at[pl.ds(u, 1)], sems.at[s]).start(priority=k)
            return carry

        lax.fori_loop(0, tile // DMA_UNROLL, issue, 0)

    @pl.when(i == 0)
    def _():
        gather(pos_ref, 0)

    @pl.when(i + 1 < n)
    def _():
        gather(next_pos_ref, 1 - slot)

    halves = [slice(0, tile // 2), slice(tile // 2, tile)]
    p_b = p_ref[...].astype(BF16)
    ples = [jnp.dot(p_b[rows], wp_ref[...], preferred_element_type=F32) for rows in halves]
    for k in range(2):
        pltpu.make_async_copy(ys_hbm.at[pl.ds(0, tile)], ybuf.at[slot, k], sems.at[slot]).wait()

    route = route_ref[...]
    h2s = [h1_ref[rows, :] + route[rows, 4:5] * ybuf[slot, 0, rows, :] + route[rows, 5:6] * ybuf[slot, 1, rows, :]
           for rows in halves]
    normed = [_rms(h2, g_ref[...]).astype(BF16) for h2 in h2s]
    gates = [jnp.dot(x, wgate_ref[...], preferred_element_type=F32) for x in normed]
    for rows, h2, gate, ple in zip(halves, h2s, gates, ples):
        o_ref[rows, :] = h2 + jax.nn.sigmoid(gate) * ple


def _combine_ple(layer, pos3, route, h1, ys, p, wp, g, wgate):
    T = h1.shape[0]
    n, _, width = pos3.shape
    tile = width // 2
    const = lambda *shape: pl.BlockSpec(shape, lambda i: (0,) * len(shape))
    row = lambda width: pl.BlockSpec((tile, width), lambda i: (i, 0))
    return pl.pallas_call(
        functools.partial(_combine_ple_kernel, tile=tile),
        grid=(n,),
        in_specs=[pl.BlockSpec((1, 1, 2 * tile), lambda i: (0, 0, 0), memory_space=pltpu.SMEM),
                  pl.BlockSpec((1, 1, 2 * tile), lambda i: (jnp.minimum(i + 1, n - 1), 0, 0),
                               memory_space=pltpu.SMEM),
                  row(LANES), row(D_MODEL), pl.BlockSpec(memory_space=pl.ANY),
                  pl.BlockSpec((None, tile, D_PLE), lambda i: (layer, i, 0)),
                  const(D_PLE, D_MODEL), const(1, D_MODEL), const(D_MODEL, D_MODEL)],
        out_specs=row(D_MODEL),
        out_shape=jax.ShapeDtypeStruct((T, D_MODEL), F32),
        scratch_shapes=[pltpu.VMEM((2, 2, tile, D_MODEL), F32), pltpu.SemaphoreType.DMA((2,))],
        compiler_params=_cparams(1),
        name="combine_ple",
    )(pos3, pos3, route, h1, ys, p, wp, g, wgate)


def _rope(z, cos, sin_lo, sin_hi):
    half = ROT_DIM // 2
    outs = []
    for j in range(z.shape[1] // LANES):
        zj = z[:, j * LANES:(j + 1) * LANES]
        outs.append(zj * cos + pltpu.roll(zj, LANES - half, 1) * sin_lo + pltpu.roll(zj, half, 1) * sin_hi)
    return outs[0] if len(outs) == 1 else jnp.concatenate(outs, axis=1)


def _proj_odd_kernel(h_ref, g_ref, w_ref, qg_ref, kg_ref, cos_ref, slo_ref, shi_ref,
                     q_ref, k_ref, v_ref, glu_ref):
    a = _rms(h_ref[...], g_ref[...]).astype(BF16)
    cos, slo, shi = cos_ref[...], slo_ref[...], shi_ref[...]
    v0 = SWA_WIDTH + KV_WIDTH
    d0 = v0 + KV_WIDTH
    zq = jnp.dot(a, w_ref[:, 0:SWA_WIDTH], preferred_element_type=F32)
    zk = jnp.dot(a, w_ref[:, SWA_WIDTH:v0], preferred_element_type=F32)
    zv = jnp.dot(a, w_ref[:, v0:d0], preferred_element_type=F32)
    zd = jnp.dot(a, w_ref[:, d0:d0 + 2 * CONV_CH], preferred_element_type=F32)
    q_ref[...] = (_rope(_head_rms(zq, qg_ref[...]), cos, slo, shi) * (HEAD_DIM ** -0.5)).astype(BF16)
    k_ref[...] = _rope(_head_rms(zk, kg_ref[...]), cos, slo, shi).astype(BF16)
    v_ref[...] = zv.astype(BF16)
    glu_ref[...] = zd[:, :CONV_CH] * jax.nn.sigmoid(zd[:, CONV_CH:])


def _proj_odd(h, g, w, qg, kg, cos, slo, shi, *, seq):
    T = h.shape[0]
    tm = min(ROW_TILE, seq)
    tps = seq // tm
    const = lambda *shape: pl.BlockSpec(shape, lambda i: (0,) * len(shape))
    row = lambda width: pl.BlockSpec((tm, width), lambda i: (i, 0))
    tab = pl.BlockSpec((tm, LANES), lambda i: (i % tps, 0))
    return pl.pallas_call(
        _proj_odd_kernel,
        grid=(T // tm,),
        in_specs=[row(D_MODEL), const(1, D_MODEL), const(*w.shape), const(1, SWA_WIDTH),
                  const(1, KV_WIDTH), tab, tab, tab],
        out_specs=[row(SWA_WIDTH), row(KV_WIDTH), row(KV_WIDTH), row(CONV_CH)],
        out_shape=[jax.ShapeDtypeStruct((T, SWA_WIDTH), BF16),
                   jax.ShapeDtypeStruct((T, KV_WIDTH), BF16),
                   jax.ShapeDtypeStruct((T, KV_WIDTH), BF16),
                   jax.ShapeDtypeStruct((T, CONV_CH), F32)],
        compiler_params=_cparams(1),
        name="proj_odd",
    )(h, g, w, qg, kg, cos, slo, shi)


def _swa_kernel(sink_ref, q_ref, k_ref, v_ref, o_ref, *, seq):
    lo = _lane_lo((1, LANES))
    nt = (((1,), (1,)), ((), ()))
    W = WINDOW
    qi = lax.broadcasted_iota(jnp.int32, (W, 2 * W), 0)
    kj = lax.broadcasted_iota(jnp.int32, (W, 2 * W), 1)
    band = (kj > qi) & (kj <= qi + W)

    n_tiles = SWA_WIDTH // LANES
    heads = [head for j in range(n_tiles) for head in (j, n_tiles + j)]

    def window(n):
        start = max(n - 1, 0) * W
        return slice(start, start + 2 * W), (band if n > 0 else kj <= qi)

    def score_products(n):
        kwin, _ = window(n)
        ks = k_ref[kwin, :]
        scores = []
        for j in range(n_tiles):
            q = q_ref[n * W:(n + 1) * W, j * LANES:(j + 1) * LANES]
            zero = jnp.zeros_like(q)
            scores += [lax.dot_general(qh, ks, nt, preferred_element_type=F32)
                       for qh in (jnp.where(lo, q, zero), jnp.where(lo, zero, q))]
        return scores

    scores = score_products(0)
    for n in range(seq // W):
        next_scores = score_products(n + 1) if (n + 1) * W < seq else None
        kwin, mask = window(n)
        vs = v_ref[kwin, :]
        probs, sums = [], []
        for head, s in zip(heads, scores):
            s = jnp.where(mask, s, NEG)
            sink = sink_ref[head]
            m = jnp.maximum(jnp.max(s, axis=-1, keepdims=True), sink)
            p = jnp.exp(s - m)
            sums.append(jnp.sum(p, axis=-1, keepdims=True) + jnp.exp(sink - m))
            probs.append(p.astype(BF16))
        outs = [jnp.dot(p, vs, preferred_element_type=F32) / l for p, l in zip(probs, sums)]
        for j in range(n_tiles):
            o_ref[n * W:(n + 1) * W, j * LANES:(j + 1) * LANES] = (
                jnp.where(lo, outs[2 * j], outs[2 * j + 1]).astype(BF16))
        scores = next_scores


def _swa_attention(sinks, q, k, v, *, seq):
    T = q.shape[0]
    B = T // seq
    return pl.pallas_call(
        functools.partial(_swa_kernel, seq=seq),
        grid=(B,),
        in_specs=[pl.BlockSpec(memory_space=pltpu.SMEM),
                  pl.BlockSpec((seq, SWA_WIDTH), lambda b: (b, 0)),
                  pl.BlockSpec((seq, KV_WIDTH), lambda b: (b, 0)),
                  pl.BlockSpec((seq, KV_WIDTH), lambda b: (b, 0))],
        out_specs=pl.BlockSpec((seq, SWA_WIDTH), lambda b: (b, 0)),
        out_shape=jax.ShapeDtypeStruct((T, SWA_WIDTH), BF16),
        compiler_params=_cparams(1),
        name="swa_attention",
    )(sinks, q, k, v)


def _conv_kernel(prev_ref, cur_ref, w_ref, g_ref, b_ref, o_ref, shift_ref, *, tile, sub):
    r = pl.program_id(1)
    rows = CONV_HALO + tile
    tail = prev_ref[tile - CONV_HALO:, :]
    shift_ref[0, 0:CONV_HALO, :] = jnp.where(r > 0, tail, jnp.zeros_like(tail))
    shift_ref[0, CONV_HALO:rows, :] = cur_ref[...]
    shift_ref[0, rows:rows + SUBLANES, :] = jnp.zeros((SUBLANES, CONV_CH), F32)
    for o in range(1, SUBLANES):
        shift_ref[o, 0:rows, :] = shift_ref[0, o:o + rows, :]
    w = w_ref[...]
    first = CONV_HALO - (CONV_WIDTH - 1)
    for s in range(tile // sub):
        acc = jnp.zeros((sub, CONV_CH), F32)
        for j in range(CONV_WIDTH):
            start = s * sub + first + j
            o = start % SUBLANES
            acc = acc + shift_ref[o, start - o:start - o + sub, :] * w[j:j + 1, :]
        y = _layernorm(acc, g_ref[...], b_ref[...])
        o_ref[s * sub:(s + 1) * sub, :] = (y * jax.nn.sigmoid(y)).astype(BF16)


def _conv_module(glu, w, g, b, *, seq):
    T = glu.shape[0]
    B = T // seq
    tile = min(CONV_TILE, seq)
    nr = seq // tile
    const = lambda *shape: pl.BlockSpec(shape, lambda bb, r: (0,) * len(shape))
    return pl.pallas_call(
        functools.partial(_conv_kernel, tile=tile, sub=64),
        grid=(B, nr),
        in_specs=[pl.BlockSpec((tile, CONV_CH), lambda bb, r: (bb * nr + jnp.maximum(r - 1, 0), 0)),
                  pl.BlockSpec((tile, CONV_CH), lambda bb, r: (bb * nr + r, 0)),
                  const(CONV_WIDTH, CONV_CH), const(1, CONV_CH), const(1, CONV_CH)],
        out_specs=pl.BlockSpec((tile, CONV_CH), lambda bb, r: (bb * nr + r, 0)),
        out_shape=jax.ShapeDtypeStruct((T, CONV_CH), BF16),
        scratch_shapes=[pltpu.VMEM((SUBLANES, CONV_HALO + tile + SUBLANES, CONV_CH), F32)],
        compiler_params=_cparams(2),
        name="conv_module",
    )(glu, glu, w, g, b)


def _routing_tables(route, counts, n_tiles):
    e = route[:, 0:2].astype(jnp.int32)
    rank = route[:, 2:4].astype(jnp.int32)
    cnt = counts[0, :N_EXPERTS].astype(jnp.int32)
    tiles = (cnt + EXPERT_TILE - 1) // EXPERT_TILE
    tile_end = jnp.cumsum(tiles)
    offset = (tile_end - tiles) * EXPERT_TILE
    onehot = e[:, :, None] == jnp.arange(N_EXPERTS, dtype=jnp.int32)
    pos = (rank + jnp.sum(jnp.where(onehot, offset, 0), axis=-1)).reshape(-1)
    n_used = tile_end[-1]
    tile_id = jnp.minimum(jnp.arange(n_tiles, dtype=jnp.int32), n_used - 1)
    tile_expert = jnp.sum((tile_end[None, :] <= tile_id[:, None]).astype(jnp.int32), axis=1)
    first_pad = offset + cnt
    n_single = (-first_pad) % SUBLANES
    pads = jnp.stack([first_pad, n_single, tiles * EXPERT_TILE - cnt - n_single,
                      jnp.broadcast_to(tile_end[-1], cnt.shape)])
    return pos, pads, tile_expert, n_used.reshape(1).astype(jnp.int32)


def _moe_ple(h, ya, yb, wo, layer, norm_ffn, wr, br, wg, wu, wd, p, wp, ple_norm, wgate):
    T = h.shape[0]
    n_tiles = (2 * T) // EXPERT_TILE + N_EXPERTS
    h1, m, route, counts = _outproj_router(h, ya, yb, wo, norm_ffn, wr, br)
    pos, pads, tile_expert, n_used = _routing_tables(route, counts, n_tiles)
    pos3 = pos.reshape(-1, 1, 2 * min(GATHER_TILE, T))
    xs = _dispatch(pads, pos3, m, n_tiles * EXPERT_TILE)
    ys = _experts(layer, tile_expert, n_used, xs, wg, wu, wd)
    return _combine_ple(layer, pos3, route, h1, ys, p, wp, ple_norm, wgate)


def _router_weights(w_coarse, b_coarse, w_fine, b_fine):
    wf = w_fine.transpose(1, 0, 2).reshape(D_MODEL, N_EXPERTS)
    wr = jnp.concatenate([wf, w_coarse, jnp.zeros((D_MODEL, LANES - N_EXPERTS - N_GROUPS), F32)], axis=1)
    br = jnp.concatenate([b_fine.reshape(-1), b_coarse, jnp.zeros((LANES - N_EXPERTS - N_GROUPS,), F32)])
    w_hi = wr.astype(BF16)
    w_lo = (wr - w_hi.astype(F32)).astype(BF16)
    return jnp.concatenate([w_hi, w_lo], axis=1), br.reshape(1, LANES)


def _rope_tables(seq):
    half = ROT_DIM // 2
    inv_freq = ROPE_THETA ** (-jnp.arange(half, dtype=F32) * 2.0 / ROT_DIM)
    ang = jnp.arange(seq, dtype=F32)[:, None] * inv_freq[None, :]
    cos, sin = jnp.cos(ang), jnp.sin(ang)
    zeros = jnp.zeros((seq, HEAD_DIM - ROT_DIM), F32)
    z8 = jnp.zeros((seq, half), F32)
    cos_h = jnp.concatenate([cos, cos, zeros + 1.0], axis=1)
    slo_h = jnp.concatenate([-sin, z8, zeros], axis=1)
    shi_h = jnp.concatenate([z8, sin, zeros], axis=1)
    two = lambda t: jnp.concatenate([t, t], axis=1)
    return two(cos_h), two(slo_h), two(shi_h)


def kernel(x, p, norm_mix, even_w_in, fox_b_f, gmlp_ln_g, gmlp_ln_b, gmlp_w_s, gmlp_b_s, fox_q_norm, fox_k_norm, even_w_out, odd_w_in, swa_q_norm, swa_k_norm, swa_sinks, conv_w, conv_ln_g, conv_ln_b, odd_w_out, norm_ffn, moe_w_coarse, moe_b_coarse, moe_w_fine, moe_b_fine, moe_w_gate, moe_w_up, moe_w_down, ple_w_proj, ple_norm, ple_w_gate):
    B, S, D = x.shape
    T = B * S
    h = x.reshape(T, D)
    p = p.reshape(p.shape[0], T, D_PLE)
    row = lambda v: v.reshape(1, -1)

    def moe_args(i):
        wr, br = _router_weights(moe_w_coarse[i], moe_b_coarse[i], moe_w_fine[i], moe_b_fine[i])
        return (i, row(norm_ffn[i]), wr, br, moe_w_gate, moe_w_up, moe_w_down, p,
                ple_w_proj[i].astype(BF16), row(ple_norm[i]), ple_w_gate[i].astype(BF16))

    n_main = 2 * GMLP_WIDTH + 3 * FOX_WIDTH
    w_f = jnp.pad(jnp.tile(even_w_in[0][:, n_main:], (1, BIAS_PIECES)),
                  ((0, 0), (0, LANES - BIAS_PIECES * FOX_HEADS)))
    w_in = jnp.concatenate([even_w_in[0][:, :n_main], w_f], axis=1).astype(BF16)
    b_f = jnp.pad(jnp.tile(fox_b_f[0], BIAS_PIECES), (0, LANES - BIAS_PIECES * FOX_HEADS)).reshape(1, LANES)
    bs_full = jnp.repeat(gmlp_b_s[0].T, HEAD_DIM, axis=1)
    ya, q, k, v, xq, xk = _proj_even(
        h, row(norm_mix[0]), w_in, b_f, row(gmlp_ln_g[0]), row(gmlp_ln_b[0]), gmlp_w_s[0], bs_full,
        row(jnp.tile(fox_q_norm[0], FOX_HEADS)), row(jnp.tile(fox_k_norm[0], FOX_HEADS)), seq=S)
    yb = _fox_attention(q, xq, k, xk, v, seq=S)
    h = _moe_ple(h, ya, yb, even_w_out[0].astype(BF16), *moe_args(0))

    order = jnp.array([0, 4, 1, 5, 2, 6, 3, 7])
    cols = (order[:, None] * HEAD_DIM + jnp.arange(HEAD_DIM)[None, :]).reshape(-1)
    w_odd = jnp.concatenate([odd_w_in[0][:, :SWA_WIDTH][:, cols], odd_w_in[0][:, SWA_WIDTH:]], axis=1).astype(BF16)
    w_out_odd = jnp.concatenate([odd_w_out[0][:SWA_WIDTH][cols], odd_w_out[0][SWA_WIDTH:]], axis=0).astype(BF16)
    cos, slo, shi = _rope_tables(S)
    q, k, v, glu = _proj_odd(h, row(norm_mix[1]), w_odd, row(jnp.tile(swa_q_norm[0], 8)),
                             row(jnp.tile(swa_k_norm[0], 2)), cos, slo, shi, seq=S)
    yc = _swa_attention(swa_sinks[0], q, k, v, seq=S)
    yd = _conv_module(glu, conv_w[0], row(conv_ln_g[0]), row(conv_ln_b[0]), seq=S)
    h = _moe_ple(h, yc, yd, w_out_odd, *moe_args(1))
    return h.reshape(B, S, D)
```

```python
import functools

import jax
import jax.numpy as jnp
import numpy as np
from jax import lax
from jax.experimental import pallas as pl
from jax.experimental.pallas import tpu as pltpu

F32 = jnp.float32
BF16 = jnp.bfloat16
HIGHEST = lax.Precision.HIGHEST

D_MODEL = 1024
HEAD_DIM = 64
LANES = 128
GMLP_WIDTH = 512
CHUNK = 128
FOX_WIDTH = 512
FOX_HEADS = 8
SWA_WIDTH = 512
KV_WIDTH = 128
WINDOW = 128
CONV_CH = 512
CONV_WIDTH = 31
CONV_HALO = 32
ROPE_THETA = 500000.0
ROT_DIM = 16
N_GROUPS = 4
EXPERTS_PER_GROUP = 8
N_EXPERTS = 32
D_EXPERT = 256
D_PLE = 256
EPS = 1e-6
NEG = -1e30
LOG2E = 1.4426950408889634
BIAS_PIECES = 3

EXPERT_TILE = 512
PAD_BITS = 9
SUBLANES = 8
SUBLANE_BITS = 3
ROW_TILE = 1024
ROUTER_BLOCKS = 8
GATHER_TILE = 512
DISPATCH_TILE = 2048
DMA_UNROLL = 8
ATTN_TILE = 256
FOX_PAIRS_PER_STEP = 2
CONV_TILE = 256
VMEM_LIMIT = 56 * 1024 * 1024


def _cparams(n_axes=1, flags=None):
    return pltpu.CompilerParams(dimension_semantics=("arbitrary",) * n_axes,
                                vmem_limit_bytes=VMEM_LIMIT, flags=flags)


def _rms(x, gain):
    return x * lax.rsqrt(jnp.mean(x * x, axis=-1, keepdims=True) + EPS) * gain


def _layernorm(x, g, b):
    mu = jnp.mean(x, axis=-1, keepdims=True)
    xc = x - mu
    var = jnp.mean(xc * xc, axis=-1, keepdims=True)
    return xc * lax.rsqrt(var + EPS) * g + b


def _head_rms(z, gain):
    r = (lax.broadcasted_iota(jnp.int32, (2 * LANES, LANES), 0) % LANES) // HEAD_DIM
    c = lax.broadcasted_iota(jnp.int32, (2 * LANES, LANES), 1) // HEAD_DIM
    bd = jnp.where(r == c, 1.0 / HEAD_DIM, 0.0).astype(BF16)
    outs = []
    for j in range(z.shape[1] // LANES):
        zj = z[:, j * LANES:(j + 1) * LANES]
        sq = zj * zj
        sq_hi = sq.astype(BF16)
        sq_lo = (sq - sq_hi.astype(F32)).astype(BF16)
        ms = jnp.dot(jnp.concatenate([sq_hi, sq_lo], axis=1), bd, preferred_element_type=F32)
        outs.append(zj * lax.rsqrt(ms + EPS))
    zn = outs[0] if len(outs) == 1 else jnp.concatenate(outs, axis=1)
    return zn * gain


def _lane_lo(shape):
    return (lax.broadcasted_iota(jnp.int32, shape, len(shape) - 1) % LANES) < HEAD_DIM


def _proj_even_kernel(h_ref, g_ref, w_ref, bf_ref, lng_ref, lnb_ref, ws_ref, bs_ref, qg_ref, kg_ref,
                      pq_ref, pk_ref, oneq_ref, onek_ref,
                      ya_ref, q_ref, k_ref, v_ref, xq_ref, xk_ref, carry_ref, *, tm, tiles_per_seq):
    i = pl.program_id(0)
    a = _rms(h_ref[...], g_ref[...]).astype(BF16)

    q0 = 2 * GMLP_WIDTH
    za = jnp.dot(a, w_ref[:, 0:q0], preferred_element_type=F32)
    zq = jnp.dot(a, w_ref[:, q0:q0 + FOX_WIDTH], preferred_element_type=F32)
    zk = jnp.dot(a, w_ref[:, q0 + FOX_WIDTH:q0 + 2 * FOX_WIDTH], preferred_element_type=F32)
    zv = jnp.dot(a, w_ref[:, q0 + 2 * FOX_WIDTH:q0 + 3 * FOX_WIDTH], preferred_element_type=F32)
    zf = jnp.dot(a, w_ref[:, q0 + 3 * FOX_WIDTH:], preferred_element_type=F32) + bf_ref[...]

    q_ref[...] = (_head_rms(zq, qg_ref[...]) * (LOG2E * HEAD_DIM ** -0.5)).astype(BF16)
    k_ref[...] = _head_rms(zk, kg_ref[...]).astype(BF16)
    v_ref[...] = zv.astype(BF16)

    rr = lax.broadcasted_iota(jnp.int32, (CHUNK, CHUNK), 0)
    cc = lax.broadcasted_iota(jnp.int32, (CHUNK, CHUNK), 1)

    ls = jnp.minimum(zf, 0.0) - jnp.log(1.0 + jnp.exp(-jnp.abs(zf)))

    @pl.when(i % tiles_per_seq == 0)
    def _():
        carry_ref[...] = jnp.zeros_like(carry_ref)

    def split3(x):
        hi = x.astype(BF16)
        r1 = x - hi.astype(F32)
        mid = r1.astype(BF16)
        return hi, mid, (r1 - mid.astype(F32)).astype(BF16)

    tri = jnp.where(rr >= cc, 1.0, 0.0).astype(BF16)
    pieces = jnp.concatenate(split3(ls), axis=1)
    running = carry_ref[...]
    blocks = []
    for b in range(tm // CHUNK):
        d = jnp.dot(tri, pieces[b * CHUNK:(b + 1) * CHUNK, :], preferred_element_type=F32)
        blk = (d[:, :LANES] + d[:, LANES:2 * LANES]) + d[:, 2 * LANES:] + running
        running = blk[CHUNK - 1:CHUNK, :]
        blocks.append(blk)
    carry_ref[...] = running
    c = jnp.concatenate(blocks, axis=0)

    za = jax.nn.gelu(za)
    u = za[:, :GMLP_WIDTH]
    vln = _layernorm(za[:, GMLP_WIDTH:], lng_ref[...], lnb_ref[...]).astype(BF16)
    lo = _lane_lo((CHUNK, LANES))
    for j in range(GMLP_WIDTH // LANES):
        w_a = jnp.where(rr >= cc, ws_ref[2 * j], 0.0).astype(BF16)
        w_b = jnp.where(rr >= cc, ws_ref[2 * j + 1], 0.0).astype(BF16)
        cols = slice(j * LANES, (j + 1) * LANES)
        for blk in range(tm // CHUNK):
            rows = slice(blk * CHUNK, (blk + 1) * CHUNK)
            vp = vln[rows, cols]
            mixed = jnp.where(lo, jnp.dot(w_a, vp, preferred_element_type=F32),
                              jnp.dot(w_b, vp, preferred_element_type=F32)) + bs_ref[:, cols]
            ya_ref[rows, cols] = (u[rows, cols] * mixed).astype(BF16)

    hi, mid, low = split3(c * LOG2E)
    group = lax.broadcasted_iota(jnp.int32, (tm, LANES), 1) // FOX_HEADS
    sel = jnp.where(group == 0, hi, jnp.where(group == 1, mid, low))
    xq_ref[...] = (jnp.dot(sel, pq_ref[...], preferred_element_type=F32) + oneq_ref[...]).astype(BF16)
    xk_ref[...] = (jnp.dot(sel, pk_ref[...], preferred_element_type=F32) + onek_ref[...]).astype(BF16)


def _proj_even(h, g, w, bf, lng, lnb, ws, bs_full, qg, kg, *, seq):
    T = h.shape[0]
    tm = min(ROW_TILE, seq)
    n_in = w.shape[1]
    const = lambda *shape: pl.BlockSpec(shape, lambda i: (0,) * len(shape))
    row = lambda width: pl.BlockSpec((tm, width), lambda i: (i, 0))
    tps = seq // tm
    return pl.pallas_call(
        functools.partial(_proj_even_kernel, tm=tm, tiles_per_seq=tps),
        grid=(T // tm,),
        in_specs=[row(D_MODEL), const(1, D_MODEL), const(D_MODEL, n_in), const(1, LANES),
                  const(1, GMLP_WIDTH), const(1, GMLP_WIDTH), const(8, CHUNK, CHUNK),
                  const(CHUNK, GMLP_WIDTH), const(1, FOX_WIDTH), const(1, FOX_WIDTH),
                  const(LANES, FOX_WIDTH), const(LANES, FOX_WIDTH),
                  const(1, FOX_WIDTH), const(1, FOX_WIDTH)],
        out_specs=[row(GMLP_WIDTH)] + [row(FOX_WIDTH)] * 5,
        out_shape=[jax.ShapeDtypeStruct((T, GMLP_WIDTH), BF16)]
                  + [jax.ShapeDtypeStruct((T, FOX_WIDTH), BF16)] * 5,
        scratch_shapes=[pltpu.VMEM((1, LANES), F32)],
        compiler_params=_cparams(1),
        name="proj_even",
    )(h, g, w, bf, lng, lnb, ws, bs_full, qg, kg, *_bias_placement())


def _bias_placement():
    pq = np.zeros((LANES, FOX_WIDTH), np.float32)
    pk = np.zeros((LANES, FOX_WIDTH), np.float32)
    oneq = np.zeros((1, FOX_WIDTH), np.float32)
    onek = np.zeros((1, FOX_WIDTH), np.float32)
    for head in range(FOX_HEADS):
        base = (head // 2) * LANES + (HEAD_DIM if head % 2 == 0 else 0)
        for piece in range(BIAS_PIECES):
            pq[piece * FOX_HEADS + head, base + piece] = 1.0
            onek[0, base + piece] = 1.0
            pk[piece * FOX_HEADS + head, base + BIAS_PIECES + piece] = -1.0
            oneq[0, base + BIAS_PIECES + piece] = 1.0
    return (jnp.asarray(pq, BF16), jnp.asarray(pk, BF16), jnp.asarray(oneq), jnp.asarray(onek))


def _fox_tile(n_tiles, q_ref, xq_ref, k_ref, xk_ref, v_ref, o_ref, *, tq):
    half = tq // 2
    lo = _lane_lo((1, LANES))
    rr = lax.broadcasted_iota(jnp.int32, (half, tq), 0)
    cc = lax.broadcasted_iota(jnp.int32, (half, tq), 1)
    nt = (((1,), (1,)), ((), ()))
    pairs = [slice(pp * LANES, (pp + 1) * LANES) for pp in range(FOX_PAIRS_PER_STEP)]
    chains = [(pp, hf, head) for pp in range(len(pairs)) for hf in range(2) for head in range(2)]
    q_aug = {}
    for pp, cols in enumerate(pairs):
        for hf in range(2):
            rows = slice(hf * half, (hf + 1) * half)
            q, xq = q_ref[rows, cols], xq_ref[rows, cols]
            q_aug[pp, hf, 0], q_aug[pp, hf, 1] = jnp.where(lo, q, xq), jnp.where(lo, xq, q)

    def score_products(j):
        rows = slice(j * tq, (j + 1) * tq)
        k_aug = {}
        for pp, cols in enumerate(pairs):
            ks, xk = k_ref[rows, cols], xk_ref[rows, cols]
            k_aug[pp, 0], k_aug[pp, 1] = jnp.where(lo, ks, xk), jnp.where(lo, xk, ks)
        return [lax.dot_general(q_aug[pp, hf, head], k_aug[pp, head], nt, preferred_element_type=F32)
                for pp, hf, head in chains]

    maxes = [jnp.full((half, 1), NEG, F32) for _ in chains]
    accs = [jnp.zeros((half, LANES), F32) for _ in chains]
    scores = score_products(0)
    for j in range(n_tiles):
        next_scores = score_products(j + 1) if j + 1 < n_tiles else None
        v_aug = {}
        for pp, cols in enumerate(pairs):
            vs = v_ref[j * tq:(j + 1) * tq, cols]
            one = jnp.ones_like(vs)
            v_aug[pp, 0], v_aug[pp, 1] = jnp.where(lo, vs, one), jnp.where(lo, one, vs)
        for c, (pp, hf, head) in enumerate(chains):
            s = scores[c]
            if j == n_tiles - 1:
                s = jnp.where(cc <= rr + hf * half, s, NEG)
            n = jnp.maximum(maxes[c], jnp.max(s, axis=-1, keepdims=True))
            p = jnp.exp2(s - n).astype(BF16)
            accs[c] = accs[c] * jnp.exp2(maxes[c] - n) + jnp.dot(p, v_aug[pp, head], preferred_element_type=F32)
            maxes[c] = n
        scores = next_scores
    for c in range(0, len(chains), 2):
        pp, hf, _ = chains[c]
        norm = [acc / pltpu.roll(acc, HEAD_DIM, 1) for acc in accs[c:c + 2]]
        o_ref[hf * half:(hf + 1) * half, pairs[pp]] = jnp.where(lo, norm[0], norm[1]).astype(BF16)


def _fox_kernel(q_ref, xq_ref, k_ref, xk_ref, v_ref, o_ref, *, tq, nq):
    i = pl.program_id(2)
    for c in range(nq):
        pl.when(i == c)(functools.partial(_fox_tile, c + 1, q_ref, xq_ref, k_ref, xk_ref, v_ref, o_ref, tq=tq))


def _fox_attention(q, xq, k, xk, v, *, seq):
    T = q.shape[0]
    B = T // seq
    tq = min(ATTN_TILE, seq)
    nq = seq // tq
    width = FOX_PAIRS_PER_STEP * LANES
    tile = pl.BlockSpec((tq, width), lambda b, hp, i: (b * nq + i, hp))
    whole = pl.BlockSpec((seq, width), lambda b, hp, i: (b, hp))
    return pl.pallas_call(
        functools.partial(_fox_kernel, tq=tq, nq=nq),
        grid=(B, FOX_WIDTH // width, nq),
        in_specs=[tile, tile, whole, whole, whole],
        out_specs=tile,
        out_shape=jax.ShapeDtypeStruct((T, FOX_WIDTH), BF16),
        compiler_params=_cparams(3),
        name="fox_attention",
    )(q, xq, k, xk, v)


def _outproj_router_kernel(h_ref, ya_ref, yb_ref, wo_ref, g_ref, wr_ref, br_ref,
                           h1_ref, m_ref, route_ref, cnt_ref, carry_ref, *, tm):
    i = pl.program_id(0)
    half = wo_ref.shape[0] // 2
    nb = ROUTER_BLOCKS
    rb = tm // nb
    blocks = [slice(b * rb, (b + 1) * rb) for b in range(nb)]

    @pl.when(i == 0)
    def _():
        carry_ref[...] = jnp.zeros_like(carry_ref)

    mixes = [jnp.dot(ya_ref[rows, :], wo_ref[0:half, :], preferred_element_type=F32)
             + jnp.dot(yb_ref[rows, :], wo_ref[half:, :], preferred_element_type=F32) for rows in blocks]
    ms = []
    for rows, mix in zip(blocks, mixes):
        h1 = h_ref[rows, :] + mix
        h1_ref[rows, :] = h1
        m = _rms(h1, g_ref[...])
        m_ref[rows, :] = m
        ms.append(m)

    logits = []
    for m in ms:
        m_hi = m.astype(BF16)
        m_lo = (m - m_hi.astype(F32)).astype(BF16)
        hh = jnp.dot(m_hi, wr_ref[...], preferred_element_type=F32)
        lh = jnp.dot(m_lo, wr_ref[:, :LANES], preferred_element_type=F32)
        logits.append(hh[:, :LANES] + (hh[:, LANES:] + lh) + br_ref[...])

    lane_i = lax.broadcasted_iota(jnp.int32, (rb, LANES), 1)
    lane = lane_i.astype(F32)
    group_of_lane = (lane_i // EXPERTS_PER_GROUP).astype(F32)
    is_coarse = (lane_i >= N_EXPERTS) & (lane_i < N_EXPERTS + N_GROUPS)
    picks = []
    for lg in logits:
        coarse = jnp.where(is_coarse, lg, NEG)
        cmax = jnp.max(coarse, axis=-1, keepdims=True)
        gidx = jnp.min(jnp.where(coarse == cmax, lane - N_EXPERTS, float(LANES)), axis=-1, keepdims=True)
        p_g = 1.0 / jnp.sum(jnp.where(is_coarse, jnp.exp(coarse - cmax), 0.0), axis=-1, keepdims=True)
        in_group = (lane_i < N_EXPERTS) & (group_of_lane == gidx)
        fine = jnp.where(in_group, lg, NEG)
        v1 = jnp.max(fine, axis=-1, keepdims=True)
        i1 = jnp.min(jnp.where(fine == v1, lane, float(LANES)), axis=-1, keepdims=True)
        fine2 = jnp.where(lane == i1, NEG, fine)
        v2 = jnp.max(fine2, axis=-1, keepdims=True)
        i2 = jnp.min(jnp.where(fine2 == v2, lane, float(LANES)), axis=-1, keepdims=True)
        e2 = jnp.exp(v2 - v1)
        picks.append((i1, i2, p_g / (1.0 + e2), p_g * e2 / (1.0 + e2)))

    tr = lax.broadcasted_iota(jnp.int32, (rb, rb), 0)
    tc = lax.broadcasted_iota(jnp.int32, (rb, rb), 1)
    strict = jnp.where(tr > tc, 1.0, 0.0).astype(BF16)
    onehots = [jnp.where((lane == i1) | (lane == i2), 1.0, 0.0).astype(F32) for i1, i2, _, _ in picks]
    befores = [jnp.dot(strict, oh.astype(BF16), preferred_element_type=F32) for oh in onehots]
    total = carry_ref[...]
    for rows, (i1, i2, w1, w2), oh, before in zip(blocks, picks, onehots, befores):
        before = before + total
        r1 = jnp.sum(jnp.where(lane == i1, before, 0.0), axis=-1, keepdims=True)
        r2 = jnp.sum(jnp.where(lane == i2, before, 0.0), axis=-1, keepdims=True)
        total = total + jnp.sum(oh, axis=0, keepdims=True)
        route = jnp.where(lane == 0, i1, 0.0)
        route = jnp.where(lane == 1, i2, route)
        route = jnp.where(lane == 2, r1, route)
        route = jnp.where(lane == 3, r2, route)
        route = jnp.where(lane == 4, w1, route)
        route = jnp.where(lane == 5, w2, route)
        route_ref[rows, :] = route
    carry_ref[...] = total
    cnt_ref[...] = jnp.broadcast_to(total, cnt_ref.shape)


def _outproj_router(h, ya, yb, wo, g, wr, br):
    T = h.shape[0]
    tm = min(ROW_TILE, T)
    const = lambda *shape: pl.BlockSpec(shape, lambda i: (0,) * len(shape))
    row = lambda width: pl.BlockSpec((tm, width), lambda i: (i, 0))
    return pl.pallas_call(
        functools.partial(_outproj_router_kernel, tm=tm),
        grid=(T // tm,),
        in_specs=[row(D_MODEL), row(ya.shape[1]), row(yb.shape[1]), const(*wo.shape),
                  const(1, D_MODEL), const(D_MODEL, 2 * LANES), const(1, LANES)],
        out_specs=[row(D_MODEL), row(D_MODEL), row(LANES), const(8, LANES)],
        out_shape=[jax.ShapeDtypeStruct((T, D_MODEL), F32),
                   jax.ShapeDtypeStruct((T, D_MODEL), F32),
                   jax.ShapeDtypeStruct((T, LANES), F32),
                   jax.ShapeDtypeStruct((8, LANES), F32)],
        scratch_shapes=[pltpu.VMEM((1, LANES), F32)],
        compiler_params=_cparams(1),
        name="outproj_router",
    )(h, ya, yb, wo, g, wr, br)


def _dispatch_kernel(pad_ref, pos_ref, m_ref, xs_hbm, zeros_ref, sem, pad_sem, *, tile):
    @pl.when(pl.program_id(0) == 0)
    def _():
        zeros_ref[...] = jnp.zeros_like(zeros_ref)

        def pad_copies(e, wait):
            first, n_single, n_block = pad_ref[0, e], pad_ref[1, e], pad_ref[2, e]
            for r in range(SUBLANES - 1):
                copy = pltpu.make_async_copy(zeros_ref.at[pl.ds(0, 1)], xs_hbm.at[pl.ds(first + r, 1)], pad_sem)
                pl.when(r < n_single)(copy.wait if wait else copy.start)
            done = first + n_single
            for bit in reversed(range(SUBLANE_BITS, PAD_BITS)):
                size = 1 << bit
                taken = (n_block & size) != 0
                copy = pltpu.make_async_copy(zeros_ref.at[pl.ds(0, size)],
                                             xs_hbm.at[pl.ds(pl.multiple_of(done, SUBLANES), size)], pad_sem)
                pl.when(taken)(copy.wait if wait else copy.start)
                done = done + jnp.where(taken, size, 0)

        def start(e, carry):
            pad_copies(e, False)
            return carry

        def finish(e, carry):
            pad_copies(e, True)
            return carry

        def tile_copy(j):
            return pltpu.make_async_copy(
                zeros_ref, xs_hbm.at[pl.ds(pl.multiple_of(j * EXPERT_TILE, EXPERT_TILE), EXPERT_TILE)], pad_sem)

        def start_tile(j, carry):
            tile_copy(j).start()
            return carry

        def finish_tile(j, carry):
            tile_copy(j).wait()
            return carry

        n_tiles = xs_hbm.shape[0] // EXPERT_TILE
        lax.fori_loop(0, N_EXPERTS, start, 0)
        lax.fori_loop(pad_ref[3, 0], n_tiles, start_tile, 0)
        lax.fori_loop(0, N_EXPERTS, finish, 0)
        lax.fori_loop(pad_ref[3, 0], n_tiles, finish_tile, 0)

    def issue(c, carry):
        base = pl.multiple_of(c * DMA_UNROLL, DMA_UNROLL)
        group = m_ref.at[pl.ds(base, DMA_UNROLL)]
        for u in range(DMA_UNROLL):
            for k in range(2):
                dst = xs_hbm.at[pl.ds(pos_ref[0, 0, 2 * (base + u) + k], 1)]
                pltpu.make_async_copy(group.at[pl.ds(u, 1)], dst, sem).start(priority=k)
        return carry

    lax.fori_loop(0, tile // DMA_UNROLL, issue, 0)
    for _ in range(2):
        pltpu.make_async_copy(m_ref, xs_hbm.at[pl.ds(0, tile)], sem).wait()


def _dispatch(pads, pos, m, n_rows):
    T, width = m.shape
    tile = min(DISPATCH_TILE, T)
    pos3 = pos.reshape(T // tile, 1, 2 * tile)
    grid_spec = pltpu.PrefetchScalarGridSpec(
        num_scalar_prefetch=1,
        grid=(T // tile,),
        in_specs=[pl.BlockSpec((1, 1, 2 * tile), lambda i, pads: (i, 0, 0), memory_space=pltpu.SMEM),
                  pl.BlockSpec((tile, width), lambda i, pads: (i, 0))],
        out_specs=pl.BlockSpec(memory_space=pl.ANY),
        scratch_shapes=[pltpu.VMEM((EXPERT_TILE, width), m.dtype),
                        pltpu.SemaphoreType.DMA(()), pltpu.SemaphoreType.DMA(())],
    )
    return pl.pallas_call(
        functools.partial(_dispatch_kernel, tile=tile),
        grid_spec=grid_spec,
        out_shape=jax.ShapeDtypeStruct((n_rows, width), m.dtype),
        compiler_params=pltpu.CompilerParams(dimension_semantics=("arbitrary",),
                                             has_side_effects=True, vmem_limit_bytes=VMEM_LIMIT),
        name="moe_dispatch",
    )(pads, pos3, m)


def _experts_kernel(te_ref, nused_ref, xs_ref, wg_ref, wu_ref, wd_ref, ys_ref, wgu_b, wd_b):
    j = pl.program_id(0)
    prev = te_ref[jnp.maximum(j - 1, 0)]

    @pl.when((j == 0) | (te_ref[j] != prev))
    def _():
        wgu_b[:, :D_EXPERT] = wg_ref[...].astype(BF16)
        wgu_b[:, D_EXPERT:] = wu_ref[...].astype(BF16)
        wd_b[...] = wd_ref[...].astype(BF16)

    @pl.when(j < nused_ref[0])
    def _():
        gu = jnp.dot(xs_ref[...].astype(BF16), wgu_b[...], preferred_element_type=F32)
        g = gu[:, :D_EXPERT]
        act = g * jax.nn.sigmoid(g) * gu[:, D_EXPERT:]
        ys_ref[...] = jnp.dot(act.astype(BF16), wd_b[...], preferred_element_type=F32)

    @pl.when(j >= nused_ref[0])
    def _():
        ys_ref[...] = jnp.zeros_like(ys_ref)


def _experts(layer, tile_expert, n_used, xs, wg, wu, wd):
    n_rows = xs.shape[0]
    nt = n_rows // EXPERT_TILE
    grid_spec = pltpu.PrefetchScalarGridSpec(
        num_scalar_prefetch=2,
        grid=(nt,),
        in_specs=[pl.BlockSpec((EXPERT_TILE, D_MODEL), lambda j, te, nu: (jnp.minimum(j, nu[0] - 1), 0)),
                  pl.BlockSpec((None, None, D_MODEL, D_EXPERT), lambda j, te, nu: (layer, te[j], 0, 0)),
                  pl.BlockSpec((None, None, D_MODEL, D_EXPERT), lambda j, te, nu: (layer, te[j], 0, 0)),
                  pl.BlockSpec((None, None, D_EXPERT, D_MODEL), lambda j, te, nu: (layer, te[j], 0, 0))],
        out_specs=pl.BlockSpec((EXPERT_TILE, D_MODEL), lambda j, te, nu: (j, 0)),
        scratch_shapes=[pltpu.VMEM((D_MODEL, 2 * D_EXPERT), BF16),
                        pltpu.VMEM((D_EXPERT, D_MODEL), BF16)],
    )
    return pl.pallas_call(
        _experts_kernel,
        grid_spec=grid_spec,
        out_shape=jax.ShapeDtypeStruct((n_rows, D_MODEL), F32),
        compiler_params=_cparams(1),
        name="moe_experts",
    )(tile_expert, n_used, xs, wg, wu, wd)


def _combine_ple_kernel(pos_ref, next_pos_ref, route_ref, h1_ref, ys_hbm, p_ref, wp_ref, g_ref, wgate_ref,
                        o_ref, ybuf, sems, *, tile):
    i = pl.program_id(0)
    n = pl.num_programs(0)
    slot = i % 2

    def gather(table, s):
        def issue(c, carry):
            base = pl.multiple_of(c * DMA_UNROLL, DMA_UNROLL)
            for k in range(2):
                group = ybuf.at[s, k, pl.ds(base, DMA_UNROLL)]
                for u in range(DMA_UNROLL):
                    pltpu.make_async_copy(ys_hbm.at[pl.ds(table[0, 0, 2 * (base + u) + k], 1)],
                                          group.at[pl.ds(u, 1)], sems.at[s]).start(priority=k)
            return carry

        lax.fori_loop(0, tile // DMA_UNROLL, issue, 0)

    @pl.when(i == 0)
    def _():
        gather(pos_ref, 0)

    @pl.when(i + 1 < n)
    def _():
        gather(next_pos_ref, 1 - slot)

    halves = [slice(0, tile // 2), slice(tile // 2, tile)]
    p_b = p_ref[...].astype(BF16)
    ples = [jnp.dot(p_b[rows], wp_ref[...], preferred_element_type=F32) for rows in halves]
    for k in range(2):
        pltpu.make_async_copy(ys_hbm.at[pl.ds(0, tile)], ybuf.at[slot, k], sems.at[slot]).wait()

    route = route_ref[...]
    h2s = [h1_ref[rows, :] + route[rows, 4:5] * ybuf[slot, 0, rows, :] + route[rows, 5:6] * ybuf[slot, 1, rows, :]
           for rows in halves]
    normed = [_rms(h2, g_ref[...]).astype(BF16) for h2 in h2s]
    gates = [jnp.dot(x, wgate_ref[...], preferred_element_type=F32) for x in normed]
    for rows, h2, gate, ple in zip(halves, h2s, gates, ples):
        o_ref[rows, :] = h2 + jax.nn.sigmoid(gate) * ple


def _combine_ple(layer, pos3, route, h1, ys, p, wp, g, wgate):
    T = h1.shape[0]
    n, _, width = pos3.shape
    tile = width // 2
    const = lambda *shape: pl.BlockSpec(shape, lambda i: (0,) * len(shape))
    row = lambda width: pl.BlockSpec((tile, width), lambda i: (i, 0))
    return pl.pallas_call(
        functools.partial(_combine_ple_kernel, tile=tile),
        grid=(n,),
        in_specs=[pl.BlockSpec((1, 1, 2 * tile), lambda i: (0, 0, 0), memory_space=pltpu.SMEM),
                  pl.BlockSpec((1, 1, 2 * tile), lambda i: (jnp.minimum(i + 1, n - 1), 0, 0),
                               memory_space=pltpu.SMEM),
                  row(LANES), row(D_MODEL), pl.BlockSpec(memory_space=pl.ANY),
                  pl.BlockSpec((None, tile, D_PLE), lambda i: (layer, i, 0)),
                  const(D_PLE, D_MODEL), const(1, D_MODEL), const(D_MODEL, D_MODEL)],
        out_specs=row(D_MODEL),
        out_shape=jax.ShapeDtypeStruct((T, D_MODEL), F32),
        scratch_shapes=[pltpu.VMEM((2, 2, tile, D_MODEL), F32), pltpu.SemaphoreType.DMA((2,))],
        compiler_params=_cparams(1),
        name="combine_ple",
    )(pos3, pos3, route, h1, ys, p, wp, g, wgate)


def _rope(z, cos, sin_lo, sin_hi):
    half = ROT_DIM // 2
    outs = []
    for j in range(z.shape[1] // LANES):
        zj = z[:, j * LANES:(j + 1) * LANES]
        outs.append(zj * cos + pltpu.roll(zj, LANES - half, 1) * sin_lo + pltpu.roll(zj, half, 1) * sin_hi)
    return outs[0] if len(outs) == 1 else jnp.concatenate(outs, axis=1)


def _proj_odd_kernel(h_ref, g_ref, w_ref, qg_ref, kg_ref, cos_ref, slo_ref, shi_ref,
                     q_ref, k_ref, v_ref, glu_ref):
    a = _rms(h_ref[...], g_ref[...]).astype(BF16)
    cos, slo, shi = cos_ref[...], slo_ref[...], shi_ref[...]
    v0 = SWA_WIDTH + KV_WIDTH
    d0 = v0 + KV_WIDTH
    zq = jnp.dot(a, w_ref[:, 0:SWA_WIDTH], preferred_element_type=F32)
    zk = jnp.dot(a, w_ref[:, SWA_WIDTH:v0], preferred_element_type=F32)
    zv = jnp.dot(a, w_ref[:, v0:d0], preferred_element_type=F32)
    zd = jnp.dot(a, w_ref[:, d0:d0 + 2 * CONV_CH], preferred_element_type=F32)
    q_ref[...] = (_rope(_head_rms(zq, qg_ref[...]), cos, slo, shi) * (HEAD_DIM ** -0.5)).astype(BF16)
    k_ref[...] = _rope(_head_rms(zk, kg_ref[...]), cos, slo, shi).astype(BF16)
    v_ref[...] = zv.astype(BF16)
    glu_ref[...] = zd[:, :CONV_CH] * jax.nn.sigmoid(zd[:, CONV_CH:])


def _proj_odd(h, g, w, qg, kg, cos, slo, shi, *, seq):
    T = h.shape[0]
    tm = min(ROW_TILE, seq)
    tps = seq // tm
    const = lambda *shape: pl.BlockSpec(shape, lambda i: (0,) * len(shape))
    row = lambda width: pl.BlockSpec((tm, width), lambda i: (i, 0))
    tab = pl.BlockSpec((tm, LANES), lambda i: (i % tps, 0))
    return pl.pallas_call(
        _proj_odd_kernel,
        grid=(T // tm,),
        in_specs=[row(D_MODEL), const(1, D_MODEL), const(*w.shape), const(1, SWA_WIDTH),
                  const(1, KV_WIDTH), tab, tab, tab],
        out_specs=[row(SWA_WIDTH), row(KV_WIDTH), row(KV_WIDTH), row(CONV_CH)],
        out_shape=[jax.ShapeDtypeStruct((T, SWA_WIDTH), BF16),
                   jax.ShapeDtypeStruct((T, KV_WIDTH), BF16),
                   jax.ShapeDtypeStruct((T, KV_WIDTH), BF16),
                   jax.ShapeDtypeStruct((T, CONV_CH), F32)],
        compiler_params=_cparams(1),
        name="proj_odd",
    )(h, g, w, qg, kg, cos, slo, shi)


def _swa_kernel(sink_ref, q_ref, k_ref, v_ref, o_ref, *, seq):
    lo = _lane_lo((1, LANES))
    nt = (((1,), (1,)), ((), ()))
    W = WINDOW
    qi = lax.broadcasted_iota(jnp.int32, (W, 2 * W), 0)
    kj = lax.broadcasted_iota(jnp.int32, (W, 2 * W), 1)
    band = (kj > qi) & (kj <= qi + W)

    n_tiles = SWA_WIDTH // LANES
    heads = [head for j in range(n_tiles) for head in (j, n_tiles + j)]

    def window(n):
        start = max(n - 1, 0) * W
        return slice(start, start + 2 * W), (band if n > 0 else kj <= qi)

    def score_products(n):
        kwin, _ = window(n)
        ks = k_ref[kwin, :]
        scores = []
        for j in range(n_tiles):
            q = q_ref[n * W:(n + 1) * W, j * LANES:(j + 1) * LANES]
            zero = jnp.zeros_like(q)
            scores += [lax.dot_general(qh, ks, nt, preferred_element_type=F32)
                       for qh in (jnp.where(lo, q, zero), jnp.where(lo, zero, q))]
        return scores

    scores = score_products(0)
    for n in range(seq // W):
        next_scores = score_products(n + 1) if (n + 1) * W < seq else None
        kwin, mask = window(n)
        vs = v_ref[kwin, :]
        probs, sums = [], []
        for head, s in zip(heads, scores):
            s = jnp.where(mask, s, NEG)
            sink = sink_ref[head]
            m = jnp.maximum(jnp.max(s, axis=-1, keepdims=True), sink)
            p = jnp.exp(s - m)
            sums.append(jnp.sum(p, axis=-1, keepdims=True) + jnp.exp(sink - m))
            probs.append(p.astype(BF16))
        outs = [jnp.dot(p, vs, preferred_element_type=F32) / l for p, l in zip(probs, sums)]
        for j in range(n_tiles):
            o_ref[n * W:(n + 1) * W, j * LANES:(j + 1) * LANES] = (
                jnp.where(lo, outs[2 * j], outs[2 * j + 1]).astype(BF16))
        scores = next_scores


def _swa_attention(sinks, q, k, v, *, seq):
    T = q.shape[0]
    B = T // seq
    return pl.pallas_call(
        functools.partial(_swa_kernel, seq=seq),
        grid=(B,),
        in_specs=[pl.BlockSpec(memory_space=pltpu.SMEM),
                  pl.BlockSpec((seq, SWA_WIDTH), lambda b: (b, 0)),
                  pl.BlockSpec((seq, KV_WIDTH), lambda b: (b, 0)),
                  pl.BlockSpec((seq, KV_WIDTH), lambda b: (b, 0))],
        out_specs=pl.BlockSpec((seq, SWA_WIDTH), lambda b: (b, 0)),
        out_shape=jax.ShapeDtypeStruct((T, SWA_WIDTH), BF16),
        compiler_params=_cparams(1),
        name="swa_attention",
    )(sinks, q, k, v)


def _conv_kernel(prev_ref, cur_ref, w_ref, g_ref, b_ref, o_ref, shift_ref, *, tile, sub):
    r = pl.program_id(1)
    rows = CONV_HALO + tile
    tail = prev_ref[tile - CONV_HALO:, :]
    shift_ref[0, 0:CONV_HALO, :] = jnp.where(r > 0, tail, jnp.zeros_like(tail))
    shift_ref[0, CONV_HALO:rows, :] = cur_ref[...]
    shift_ref[0, rows:rows + SUBLANES, :] = jnp.zeros((SUBLANES, CONV_CH), F32)
    for o in range(1, SUBLANES):
        shift_ref[o, 0:rows, :] = shift_ref[0, o:o + rows, :]
    w = w_ref[...]
    first = CONV_HALO - (CONV_WIDTH - 1)
    for s in range(tile // sub):
        acc = jnp.zeros((sub, CONV_CH), F32)
        for j in range(CONV_WIDTH):
            start = s * sub + first + j
            o = start % SUBLANES
            acc = acc + shift_ref[o, start - o:start - o + sub, :] * w[j:j + 1, :]
        y = _layernorm(acc, g_ref[...], b_ref[...])
        o_ref[s * sub:(s + 1) * sub, :] = (y * jax.nn.sigmoid(y)).astype(BF16)


def _conv_module(glu, w, g, b, *, seq):
    T = glu.shape[0]
    B = T // seq
    tile = min(CONV_TILE, seq)
    nr = seq // tile
    const = lambda *shape: pl.BlockSpec(shape, lambda bb, r: (0,) * len(shape))
    return pl.pallas_call(
        functools.partial(_conv_kernel, tile=tile, sub=64),
        grid=(B, nr),
        in_specs=[pl.BlockSpec((tile, CONV_CH), lambda bb, r: (bb * nr + jnp.maximum(r - 1, 0), 0)),
                  pl.BlockSpec((tile, CONV_CH), lambda bb, r: (bb * nr + r, 0)),
                  const(CONV_WIDTH, CONV_CH), const(1, CONV_CH), const(1, CONV_CH)],
        out_specs=pl.BlockSpec((tile, CONV_CH), lambda bb, r: (bb * nr + r, 0)),
        out_shape=jax.ShapeDtypeStruct((T, CONV_CH), BF16),
        scratch_shapes=[pltpu.VMEM((SUBLANES, CONV_HALO + tile + SUBLANES, CONV_CH), F32)],
        compiler_params=_cparams(2),
        name="conv_module",
    )(glu, glu, w, g, b)


def _routing_tables(route, counts, n_tiles):
    e = route[:, 0:2].astype(jnp.int32)
    rank = route[:, 2:4].astype(jnp.int32)
    cnt = counts[0, :N_EXPERTS].astype(jnp.int32)
    tiles = (cnt + EXPERT_TILE - 1) // EXPERT_TILE
    tile_end = jnp.cumsum(tiles)
    offset = (tile_end - tiles) * EXPERT_TILE
    onehot = e[:, :, None] == jnp.arange(N_EXPERTS, dtype=jnp.int32)
    pos = (rank + jnp.sum(jnp.where(onehot, offset, 0), axis=-1)).reshape(-1)
    n_used = tile_end[-1]
    tile_id = jnp.minimum(jnp.arange(n_tiles, dtype=jnp.int32), n_used - 1)
    tile_expert = jnp.sum((tile_end[None, :] <= tile_id[:, None]).astype(jnp.int32), axis=1)
    first_pad = offset + cnt
    n_single = (-first_pad) % SUBLANES
    pads = jnp.stack([first_pad, n_single, tiles * EXPERT_TILE - cnt - n_single,
                      jnp.broadcast_to(tile_end[-1], cnt.shape)])
    return pos, pads, tile_expert, n_used.reshape(1).astype(jnp.int32)


def _moe_ple(h, ya, yb, wo, layer, norm_ffn, wr, br, wg, wu, wd, p, wp, ple_norm, wgate):
    T = h.shape[0]
    n_tiles = (2 * T) // EXPERT_TILE + N_EXPERTS
    h1, m, route, counts = _outproj_router(h, ya, yb, wo, norm_ffn, wr, br)
    pos, pads, tile_expert, n_used = _routing_tables(route, counts, n_tiles)
    xs = _dispatch(pads, pos, m, n_tiles * EXPERT_TILE)
    pos3 = pos.reshape(-1, 1, 2 * min(GATHER_TILE, T))
    ys = _experts(layer, tile_expert, n_used, xs, wg, wu, wd)
    return _combine_ple(layer, pos3, route, h1, ys, p, wp, ple_norm, wgate)


def _router_weights(w_coarse, b_coarse, w_fine, b_fine):
    wf = w_fine.transpose(1, 0, 2).reshape(D_MODEL, N_EXPERTS)
    wr = jnp.concatenate([wf, w_coarse, jnp.zeros((D_MODEL, LANES - N_EXPERTS - N_GROUPS), F32)], axis=1)
    br = jnp.concatenate([b_fine.reshape(-1), b_coarse, jnp.zeros((LANES - N_EXPERTS - N_GROUPS,), F32)])
    w_hi = wr.astype(BF16)
    w_lo = (wr - w_hi.astype(F32)).astype(BF16)
    return jnp.concatenate([w_hi, w_lo], axis=1), br.reshape(1, LANES)


def _rope_tables(seq):
    half = ROT_DIM // 2
    inv_freq = ROPE_THETA ** (-jnp.arange(half, dtype=F32) * 2.0 / ROT_DIM)
    ang = jnp.arange(seq, dtype=F32)[:, None] * inv_freq[None, :]
    cos, sin = jnp.cos(ang), jnp.sin(ang)
    zeros = jnp.zeros((seq, HEAD_DIM - ROT_DIM), F32)
    z8 = jnp.zeros((seq, half), F32)
    cos_h = jnp.concatenate([cos, cos, zeros + 1.0], axis=1)
    slo_h = jnp.concatenate([-sin, z8, zeros], axis=1)
    shi_h = jnp.concatenate([z8, sin, zeros], axis=1)
    two = lambda t: jnp.concatenate([t, t], axis=1)
    return two(cos_h), two(slo_h), two(shi_h)


def kernel(x, p, norm_mix, even_w_in, fox_b_f, gmlp_ln_g, gmlp_ln_b, gmlp_w_s, gmlp_b_s, fox_q_norm, fox_k_norm, even_w_out, odd_w_in, swa_q_norm, swa_k_norm, swa_sinks, conv_w, conv_ln_g, conv_ln_b, odd_w_out, norm_ffn, moe_w_coarse, moe_b_coarse, moe_w_fine, moe_b_fine, moe_w_gate, moe_w_up, moe_w_down, ple_w_proj, ple_norm, ple_w_gate):
    B, S, D = x.shape
    T = B * S
    h = x.reshape(T, D)
    p = p.reshape(p.shape[0], T, D_PLE)
    row = lambda v: v.reshape(1, -1)

    def moe_args(i):
        wr, br = _router_weights(moe_w_coarse[i], moe_b_coarse[i], moe_w_fine[i], moe_b_fine[i])
        return (i, row(norm_ffn[i]), wr, br, moe_w_gate, moe_w_up, moe_w_down, p,
                ple_w_proj[i].astype(BF16), row(ple_norm[i]), ple_w_gate[i].astype(BF16))

    n_main = 2 * GMLP_WIDTH + 3 * FOX_WIDTH
    w_f = jnp.pad(jnp.tile(even_w_in[0][:, n_main:], (1, BIAS_PIECES)),
                  ((0, 0), (0, LANES - BIAS_PIECES * FOX_HEADS)))
    w_in = jnp.concatenate([even_w_in[0][:, :n_main], w_f], axis=1).astype(BF16)
    b_f = jnp.pad(jnp.tile(fox_b_f[0], BIAS_PIECES), (0, LANES - BIAS_PIECES * FOX_HEADS)).reshape(1, LANES)
    bs_full = jnp.repeat(gmlp_b_s[0].T, HEAD_DIM, axis=1)
    ya, q, k, v, xq, xk = _proj_even(
        h, row(norm_mix[0]), w_in, b_f, row(gmlp_ln_g[0]), row(gmlp_ln_b[0]), gmlp_w_s[0], bs_full,
        row(jnp.tile(fox_q_norm[0], FOX_HEADS)), row(jnp.tile(fox_k_norm[0], FOX_HEADS)), seq=S)
    yb = _fox_attention(q, xq, k, xk, v, seq=S)
    h = _moe_ple(h, ya, yb, even_w_out[0].astype(BF16), *moe_args(0))

    order = jnp.array([0, 4, 1, 5, 2, 6, 3, 7])
    cols = (order[:, None] * HEAD_DIM + jnp.arange(HEAD_DIM)[None, :]).reshape(-1)
    w_odd = jnp.concatenate([odd_w_in[0][:, :SWA_WIDTH][:, cols], odd_w_in[0][:, SWA_WIDTH:]], axis=1).astype(BF16)
    w_out_odd = jnp.concatenate([odd_w_out[0][:SWA_WIDTH][cols], odd_w_out[0][SWA_WIDTH:]], axis=0).astype(BF16)
    cos, slo, shi = _rope_tables(S)
    q, k, v, glu = _proj_odd(h, row(norm_mix[1]), w_odd, row(jnp.tile(swa_q_norm[0], 8)),
                             row(jnp.tile(swa_k_norm[0], 2)), cos, slo, shi, seq=S)
    yc = _swa_attention(swa_sinks[0], q, k, v, seq=S)
    yd = _conv_module(glu, conv_w[0], row(conv_ln_g[0]), row(conv_ln_b[0]), seq=S)
    h = _moe_ple(h, yc, yd, w_out_odd, *moe_args(1))
    return h.reshape(B, S, D)
```

```python
import functools

import jax
import jax.numpy as jnp
import numpy as np
from jax import lax
from jax.experimental import pallas as pl
from jax.experimental.pallas import tpu as pltpu

F32 = jnp.float32
BF16 = jnp.bfloat16
HIGHEST = lax.Precision.HIGHEST

D_MODEL = 1024
HEAD_DIM = 64
LANES = 128
GMLP_WIDTH = 512
CHUNK = 128
FOX_WIDTH = 512
FOX_HEADS = 8
SWA_WIDTH = 512
KV_WIDTH = 128
WINDOW = 128
CONV_CH = 512
CONV_WIDTH = 31
CONV_HALO = 32
ROPE_THETA = 500000.0
ROT_DIM = 16
N_GROUPS = 4
EXPERTS_PER_GROUP = 8
N_EXPERTS = 32
D_EXPERT = 256
D_PLE = 256
EPS = 1e-6
NEG = -1e30
LOG2E = 1.4426950408889634
BIAS_PIECES = 3

EXPERT_TILE = 512
PAD_BITS = 9
SUBLANES = 8
SUBLANE_BITS = 3
ROW_TILE = 1024
ROUTER_BLOCKS = 8
GATHER_TILE = 512
DISPATCH_TILE = 2048
DMA_UNROLL = 8
ATTN_TILE = 256
FOX_Q_ROWS = 512
FOX_PAIRS_PER_STEP = 2
CONV_TILE = 256
VMEM_LIMIT = 56 * 1024 * 1024


def _cparams(n_axes=1, flags=None):
    return pltpu.CompilerParams(dimension_semantics=("arbitrary",) * n_axes,
                                vmem_limit_bytes=VMEM_LIMIT, flags=flags)


def _rms(x, gain):
    return x * lax.rsqrt(jnp.mean(x * x, axis=-1, keepdims=True) + EPS) * gain


def _layernorm(x, g, b):
    mu = jnp.mean(x, axis=-1, keepdims=True)
    xc = x - mu
    var = jnp.mean(xc * xc, axis=-1, keepdims=True)
    return xc * lax.rsqrt(var + EPS) * g + b


def _head_rms(z, gain):
    r = (lax.broadcasted_iota(jnp.int32, (2 * LANES, LANES), 0) % LANES) // HEAD_DIM
    c = lax.broadcasted_iota(jnp.int32, (2 * LANES, LANES), 1) // HEAD_DIM
    bd = jnp.where(r == c, 1.0 / HEAD_DIM, 0.0).astype(BF16)
    outs = []
    for j in range(z.shape[1] // LANES):
        zj = z[:, j * LANES:(j + 1) * LANES]
        sq = zj * zj
        sq_hi = sq.astype(BF16)
        sq_lo = (sq - sq_hi.astype(F32)).astype(BF16)
        ms = jnp.dot(jnp.concatenate([sq_hi, sq_lo], axis=1), bd, preferred_element_type=F32)
        outs.append(zj * lax.rsqrt(ms + EPS))
    zn = outs[0] if len(outs) == 1 else jnp.concatenate(outs, axis=1)
    return zn * gain


def _lane_lo(shape):
    return (lax.broadcasted_iota(jnp.int32, shape, len(shape) - 1) % LANES) < HEAD_DIM


def _proj_even_kernel(h_ref, g_ref, w_ref, bf_ref, lng_ref, lnb_ref, ws_ref, bs_ref, qg_ref, kg_ref,
                      pq_ref, pk_ref, oneq_ref, onek_ref,
                      ya_ref, q_ref, k_ref, v_ref, xq_ref, xk_ref, carry_ref, *, tm, tiles_per_seq):
    i = pl.program_id(0)
    a = _rms(h_ref[...], g_ref[...]).astype(BF16)

    q0 = 2 * GMLP_WIDTH
    za = jnp.dot(a, w_ref[:, 0:q0], preferred_element_type=F32)
    zq = jnp.dot(a, w_ref[:, q0:q0 + FOX_WIDTH], preferred_element_type=F32)
    zk = jnp.dot(a, w_ref[:, q0 + FOX_WIDTH:q0 + 2 * FOX_WIDTH], preferred_element_type=F32)
    zv = jnp.dot(a, w_ref[:, q0 + 2 * FOX_WIDTH:q0 + 3 * FOX_WIDTH], preferred_element_type=F32)
    zf = jnp.dot(a, w_ref[:, q0 + 3 * FOX_WIDTH:], preferred_element_type=F32) + bf_ref[...]

    q_ref[...] = (_head_rms(zq, qg_ref[...]) * (LOG2E * HEAD_DIM ** -0.5)).astype(BF16)
    k_ref[...] = _head_rms(zk, kg_ref[...]).astype(BF16)
    v_ref[...] = zv.astype(BF16)

    rr = lax.broadcasted_iota(jnp.int32, (CHUNK, CHUNK), 0)
    cc = lax.broadcasted_iota(jnp.int32, (CHUNK, CHUNK), 1)

    ls = jnp.minimum(zf, 0.0) - jnp.log(1.0 + jnp.exp(-jnp.abs(zf)))

    @pl.when(i % tiles_per_seq == 0)
    def _():
        carry_ref[...] = jnp.zeros_like(carry_ref)

    def split3(x):
        hi = x.astype(BF16)
        r1 = x - hi.astype(F32)
        mid = r1.astype(BF16)
        return hi, mid, (r1 - mid.astype(F32)).astype(BF16)

    tri = jnp.where(rr >= cc, 1.0, 0.0).astype(BF16)
    pieces = jnp.concatenate(split3(ls), axis=1)
    running = carry_ref[...]
    blocks = []
    for b in range(tm // CHUNK):
        d = jnp.dot(tri, pieces[b * CHUNK:(b + 1) * CHUNK, :], preferred_element_type=F32)
        blk = (d[:, :LANES] + d[:, LANES:2 * LANES]) + d[:, 2 * LANES:] + running
        running = blk[CHUNK - 1:CHUNK, :]
        blocks.append(blk)
    carry_ref[...] = running
    c = jnp.concatenate(blocks, axis=0)

    za = jax.nn.gelu(za)
    u = za[:, :GMLP_WIDTH]
    vln = _layernorm(za[:, GMLP_WIDTH:], lng_ref[...], lnb_ref[...]).astype(BF16)
    lo = _lane_lo((CHUNK, LANES))
    for j in range(GMLP_WIDTH // LANES):
        w_a = jnp.where(rr >= cc, ws_ref[2 * j], 0.0).astype(BF16)
        w_b = jnp.where(rr >= cc, ws_ref[2 * j + 1], 0.0).astype(BF16)
        cols = slice(j * LANES, (j + 1) * LANES)
        for blk in range(tm // CHUNK):
            rows = slice(blk * CHUNK, (blk + 1) * CHUNK)
            vp = vln[rows, cols]
            mixed = jnp.where(lo, jnp.dot(w_a, vp, preferred_element_type=F32),
                              jnp.dot(w_b, vp, preferred_element_type=F32)) + bs_ref[:, cols]
            ya_ref[rows, cols] = (u[rows, cols] * mixed).astype(BF16)

    hi, mid, low = split3(c * LOG2E)
    group = lax.broadcasted_iota(jnp.int32, (tm, LANES), 1) // FOX_HEADS
    sel = jnp.where(group == 0, hi, jnp.where(group == 1, mid, low))
    xq_ref[...] = (jnp.dot(sel, pq_ref[...], preferred_element_type=F32) + oneq_ref[...]).astype(BF16)
    xk_ref[...] = (jnp.dot(sel, pk_ref[...], preferred_element_type=F32) + onek_ref[...]).astype(BF16)


def _proj_even(h, g, w, bf, lng, lnb, ws, bs_full, qg, kg, *, seq):
    T = h.shape[0]
    tm = min(ROW_TILE, seq)
    n_in = w.shape[1]
    const = lambda *shape: pl.BlockSpec(shape, lambda i: (0,) * len(shape))
    row = lambda width: pl.BlockSpec((tm, width), lambda i: (i, 0))
    tps = seq // tm
    return pl.pallas_call(
        functools.partial(_proj_even_kernel, tm=tm, tiles_per_seq=tps),
        grid=(T // tm,),
        in_specs=[row(D_MODEL), const(1, D_MODEL), const(D_MODEL, n_in), const(1, LANES),
                  const(1, GMLP_WIDTH), const(1, GMLP_WIDTH), const(8, CHUNK, CHUNK),
                  const(CHUNK, GMLP_WIDTH), const(1, FOX_WIDTH), const(1, FOX_WIDTH),
                  const(LANES, FOX_WIDTH), const(LANES, FOX_WIDTH),
                  const(1, FOX_WIDTH), const(1, FOX_WIDTH)],
        out_specs=[row(GMLP_WIDTH)] + [row(FOX_WIDTH)] * 5,
        out_shape=[jax.ShapeDtypeStruct((T, GMLP_WIDTH), BF16)]
                  + [jax.ShapeDtypeStruct((T, FOX_WIDTH), BF16)] * 5,
        scratch_shapes=[pltpu.VMEM((1, LANES), F32)],
        compiler_params=_cparams(1),
        name="proj_even",
    )(h, g, w, bf, lng, lnb, ws, bs_full, qg, kg, *_bias_placement())


def _bias_placement():
    pq = np.zeros((LANES, FOX_WIDTH), np.float32)
    pk = np.zeros((LANES, FOX_WIDTH), np.float32)
    oneq = np.zeros((1, FOX_WIDTH), np.float32)
    onek = np.zeros((1, FOX_WIDTH), np.float32)
    for head in range(FOX_HEADS):
        base = (head // 2) * LANES + (HEAD_DIM if head % 2 == 0 else 0)
        for piece in range(BIAS_PIECES):
            pq[piece * FOX_HEADS + head, base + piece] = 1.0
            onek[0, base + piece] = 1.0
            pk[piece * FOX_HEADS + head, base + BIAS_PIECES + piece] = -1.0
            oneq[0, base + BIAS_PIECES + piece] = 1.0
    return (jnp.asarray(pq, BF16), jnp.asarray(pk, BF16), jnp.asarray(oneq), jnp.asarray(onek))


def _fox_tile(i, q_ref, xq_ref, k_ref, xk_ref, v_ref, o_ref, *, tq, tk):
    part_rows = tk // 2
    lo = _lane_lo((1, LANES))
    rr = lax.broadcasted_iota(jnp.int32, (part_rows, tk), 0)
    cc = lax.broadcasted_iota(jnp.int32, (part_rows, tk), 1)
    nt = (((1,), (1,)), ((), ()))
    pairs = [slice(pp * LANES, (pp + 1) * LANES) for pp in range(FOX_PAIRS_PER_STEP)]
    first_row = [i * tq + part * part_rows for part in range(tq // part_rows)]
    n_tiles = [row // tk + 1 for row in first_row]
    diag_offset = [row % tk for row in first_row]
    chains = [(pp, part, head) for pp in range(len(pairs)) for part in range(len(first_row)) for head in range(2)]
    q_aug = {}
    for pp, cols in enumerate(pairs):
        for part in range(len(first_row)):
            rows = slice(part * part_rows, (part + 1) * part_rows)
            q, xq = q_ref[rows, cols], xq_ref[rows, cols]
            q_aug[pp, part, 0], q_aug[pp, part, 1] = jnp.where(lo, q, xq), jnp.where(lo, xq, q)

    def score_products(j):
        rows = slice(j * tk, (j + 1) * tk)
        k_aug = {}
        for pp, cols in enumerate(pairs):
            ks, xk = k_ref[rows, cols], xk_ref[rows, cols]
            k_aug[pp, 0], k_aug[pp, 1] = jnp.where(lo, ks, xk), jnp.where(lo, xk, ks)
        return {(pp, part, head): lax.dot_general(q_aug[pp, part, head], k_aug[pp, head], nt,
                                                  preferred_element_type=F32)
                for pp, part, head in chains if j < n_tiles[part]}

    maxes = {c: jnp.full((part_rows, 1), NEG, F32) for c in chains}
    accs = {c: jnp.zeros((part_rows, LANES), F32) for c in chains}
    scores = score_products(0)
    for j in range(max(n_tiles)):
        next_scores = score_products(j + 1) if j + 1 < max(n_tiles) else {}
        v_aug = {}
        for pp, cols in enumerate(pairs):
            vs = v_ref[j * tk:(j + 1) * tk, cols]
            one = jnp.ones_like(vs)
            v_aug[pp, 0], v_aug[pp, 1] = jnp.where(lo, vs, one), jnp.where(lo, one, vs)
        for c, s in scores.items():
            pp, part, head = c
            if j == n_tiles[part] - 1:
                s = jnp.where(cc <= rr + diag_offset[part], s, NEG)
            n = jnp.maximum(maxes[c], jnp.max(s, axis=-1, keepdims=True))
            p = jnp.exp2(s - n).astype(BF16)
            accs[c] = accs[c] * jnp.exp2(maxes[c] - n) + jnp.dot(p, v_aug[pp, head], preferred_element_type=F32)
            maxes[c] = n
        scores = next_scores
    for pp, cols in enumerate(pairs):
        for part in range(len(first_row)):
            norm = [accs[pp, part, head] / pltpu.roll(accs[pp, part, head], HEAD_DIM, 1) for head in range(2)]
            o_ref[part * part_rows:(part + 1) * part_rows, cols] = jnp.where(lo, norm[0], norm[1]).astype(BF16)


def _fox_kernel(q_ref, xq_ref, k_ref, xk_ref, v_ref, o_ref, *, tq, tk, nq):
    i = pl.program_id(2)
    for c in range(nq):
        pl.when(i == c)(functools.partial(_fox_tile, c, q_ref, xq_ref, k_ref, xk_ref, v_ref, o_ref, tq=tq, tk=tk))


def _fox_attention(q, xq, k, xk, v, *, seq):
    T = q.shape[0]
    B = T // seq
    tk = min(ATTN_TILE, seq)
    tq = min(FOX_Q_ROWS, seq)
    nq = seq // tq
    width = FOX_PAIRS_PER_STEP * LANES
    tile = pl.BlockSpec((tq, width), lambda b, hp, i: (b * nq + i, hp))
    whole = pl.BlockSpec((seq, width), lambda b, hp, i: (b, hp))
    return pl.pallas_call(
        functools.partial(_fox_kernel, tq=tq, tk=tk, nq=nq),
        grid=(B, FOX_WIDTH // width, nq),
        in_specs=[tile, tile, whole, whole, whole],
        out_specs=tile,
        out_shape=jax.ShapeDtypeStruct((T, FOX_WIDTH), BF16),
        compiler_params=_cparams(3),
        name="fox_attention",
    )(q, xq, k, xk, v)


def _outproj_router_kernel(h_ref, ya_ref, yb_ref, wo_ref, g_ref, wr_ref, br_ref,
                           h1_ref, m_ref, route_ref, cnt_ref, carry_ref, *, tm):
    i = pl.program_id(0)
    half = wo_ref.shape[0] // 2
    nb = ROUTER_BLOCKS
    rb = tm // nb
    blocks = [slice(b * rb, (b + 1) * rb) for b in range(nb)]

    @pl.when(i == 0)
    def _():
        carry_ref[...] = jnp.zeros_like(carry_ref)

    mixes = [jnp.dot(ya_ref[rows, :], wo_ref[0:half, :], preferred_element_type=F32)
             + jnp.dot(yb_ref[rows, :], wo_ref[half:, :], preferred_element_type=F32) for rows in blocks]
    ms = []
    for rows, mix in zip(blocks, mixes):
        h1 = h_ref[rows, :] + mix
        h1_ref[rows, :] = h1
        m = _rms(h1, g_ref[...])
        m_ref[rows, :] = m
        ms.append(m)

    logits = []
    for m in ms:
        m_hi = m.astype(BF16)
        m_lo = (m - m_hi.astype(F32)).astype(BF16)
        hh = jnp.dot(m_hi, wr_ref[...], preferred_element_type=F32)
        lh = jnp.dot(m_lo, wr_ref[:, :LANES], preferred_element_type=F32)
        logits.append(hh[:, :LANES] + (hh[:, LANES:] + lh) + br_ref[...])

    lane_i = lax.broadcasted_iota(jnp.int32, (rb, LANES), 1)
    lane = lane_i.astype(F32)
    group_of_lane = (lane_i // EXPERTS_PER_GROUP).astype(F32)
    is_coarse = (lane_i >= N_EXPERTS) & (lane_i < N_EXPERTS + N_GROUPS)
    picks = []
    for lg in logits:
        coarse = jnp.where(is_coarse, lg, NEG)
        cmax = jnp.max(coarse, axis=-1, keepdims=True)
        gidx = jnp.min(jnp.where(coarse == cmax, lane - N_EXPERTS, float(LANES)), axis=-1, keepdims=True)
        p_g = 1.0 / jnp.sum(jnp.where(is_coarse, jnp.exp(coarse - cmax), 0.0), axis=-1, keepdims=True)
        in_group = (lane_i < N_EXPERTS) & (group_of_lane == gidx)
        fine = jnp.where(in_group, lg, NEG)
        v1 = jnp.max(fine, axis=-1, keepdims=True)
        i1 = jnp.min(jnp.where(fine == v1, lane, float(LANES)), axis=-1, keepdims=True)
        fine2 = jnp.where(lane == i1, NEG, fine)
        v2 = jnp.max(fine2, axis=-1, keepdims=True)
        i2 = jnp.min(jnp.where(fine2 == v2, lane, float(LANES)), axis=-1, keepdims=True)
        e2 = jnp.exp(v2 - v1)
        picks.append((i1, i2, p_g / (1.0 + e2), p_g * e2 / (1.0 + e2)))

    tr = lax.broadcasted_iota(jnp.int32, (rb, rb), 0)
    tc = lax.broadcasted_iota(jnp.int32, (rb, rb), 1)
    strict = jnp.where(tr > tc, 1.0, 0.0).astype(BF16)
    onehots = [jnp.where((lane == i1) | (lane == i2), 1.0, 0.0).astype(F32) for i1, i2, _, _ in picks]
    befores = [jnp.dot(strict, oh.astype(BF16), preferred_element_type=F32) for oh in onehots]
    total = carry_ref[...]
    for rows, (i1, i2, w1, w2), oh, before in zip(blocks, picks, onehots, befores):
        before = before + total
        r1 = jnp.sum(jnp.where(lane == i1, before, 0.0), axis=-1, keepdims=True)
        r2 = jnp.sum(jnp.where(lane == i2, before, 0.0), axis=-1, keepdims=True)
        total = total + jnp.sum(oh, axis=0, keepdims=True)
        route = jnp.where(lane == 0, i1, 0.0)
        route = jnp.where(lane == 1, i2, route)
        route = jnp.where(lane == 2, r1, route)
        route = jnp.where(lane == 3, r2, route)
        route = jnp.where(lane == 4, w1, route)
        route = jnp.where(lane == 5, w2, route)
        route_ref[rows, :] = route
    carry_ref[...] = total
    cnt_ref[...] = jnp.broadcast_to(total, cnt_ref.shape)


def _outproj_router(h, ya, yb, wo, g, wr, br):
    T = h.shape[0]
    tm = min(ROW_TILE, T)
    const = lambda *shape: pl.BlockSpec(shape, lambda i: (0,) * len(shape))
    row = lambda width: pl.BlockSpec((tm, width), lambda i: (i, 0))
    return pl.pallas_call(
        functools.partial(_outproj_router_kernel, tm=tm),
        grid=(T // tm,),
        in_specs=[row(D_MODEL), row(ya.shape[1]), row(yb.shape[1]), const(*wo.shape),
                  const(1, D_MODEL), const(D_MODEL, 2 * LANES), const(1, LANES)],
        out_specs=[row(D_MODEL), row(D_MODEL), row(LANES), const(8, LANES)],
        out_shape=[jax.ShapeDtypeStruct((T, D_MODEL), F32),
                   jax.ShapeDtypeStruct((T, D_MODEL), F32),
                   jax.ShapeDtypeStruct((T, LANES), F32),
                   jax.ShapeDtypeStruct((8, LANES), F32)],
        scratch_shapes=[pltpu.VMEM((1, LANES), F32)],
        compiler_params=_cparams(1),
        name="outproj_router",
    )(h, ya, yb, wo, g, wr, br)


def _dispatch_kernel(pad_ref, pos_ref, m_ref, xs_hbm, zeros_ref, sem, pad_sem, *, tile):
    @pl.when(pl.program_id(0) == 0)
    def _():
        zeros_ref[...] = jnp.zeros_like(zeros_ref)

        def pad_copies(e, wait):
            first, n_single, n_block = pad_ref[0, e], pad_ref[1, e], pad_ref[2, e]
            for r in range(SUBLANES - 1):
                copy = pltpu.make_async_copy(zeros_ref.at[pl.ds(0, 1)], xs_hbm.at[pl.ds(first + r, 1)], pad_sem)
                pl.when(r < n_single)(copy.wait if wait else copy.start)
            done = first + n_single
            for bit in reversed(range(SUBLANE_BITS, PAD_BITS)):
                size = 1 << bit
                taken = (n_block & size) != 0
                copy = pltpu.make_async_copy(zeros_ref.at[pl.ds(0, size)],
                                             xs_hbm.at[pl.ds(pl.multiple_of(done, SUBLANES), size)], pad_sem)
                pl.when(taken)(copy.wait if wait else copy.start)
                done = done + jnp.where(taken, size, 0)

        def start(e, carry):
            pad_copies(e, False)
            return carry

        def finish(e, carry):
            pad_copies(e, True)
            return carry

        def tile_copy(j):
            return pltpu.make_async_copy(
                zeros_ref, xs_hbm.at[pl.ds(pl.multiple_of(j * EXPERT_TILE, EXPERT_TILE), EXPERT_TILE)], pad_sem)

        def start_tile(j, carry):
            tile_copy(j).start()
            return carry

        def finish_tile(j, carry):
            tile_copy(j).wait()
            return carry

        n_tiles = xs_hbm.shape[0] // EXPERT_TILE
        lax.fori_loop(0, N_EXPERTS, start, 0)
        lax.fori_loop(pad_ref[3, 0], n_tiles, start_tile, 0)
        lax.fori_loop(0, N_EXPERTS, finish, 0)
        lax.fori_loop(pad_ref[3, 0], n_tiles, finish_tile, 0)

    def issue(c, carry):
        base = pl.multiple_of(c * DMA_UNROLL, DMA_UNROLL)
        group = m_ref.at[pl.ds(base, DMA_UNROLL)]
        for u in range(DMA_UNROLL):
            for k in range(2):
                dst = xs_hbm.at[pl.ds(pos_ref[0, 0, 2 * (base + u) + k], 1)]
                pltpu.make_async_copy(group.at[pl.ds(u, 1)], dst, sem).start(priority=k)
        return carry

    lax.fori_loop(0, tile // DMA_UNROLL, issue, 0)
    for _ in range(2):
        pltpu.make_async_copy(m_ref, xs_hbm.at[pl.ds(0, tile)], sem).wait()


def _dispatch(pads, pos, m, n_rows):
    T, width = m.shape
    tile = min(DISPATCH_TILE, T)
    pos3 = pos.reshape(T // tile, 1, 2 * tile)
    grid_spec = pltpu.PrefetchScalarGridSpec(
        num_scalar_prefetch=1,
        grid=(T // tile,),
        in_specs=[pl.BlockSpec((1, 1, 2 * tile), lambda i, pads: (i, 0, 0), memory_space=pltpu.SMEM),
                  pl.BlockSpec((tile, width), lambda i, pads: (i, 0))],
        out_specs=pl.BlockSpec(memory_space=pl.ANY),
        scratch_shapes=[pltpu.VMEM((EXPERT_TILE, width), m.dtype),
                        pltpu.SemaphoreType.DMA(()), pltpu.SemaphoreType.DMA(())],
    )
    return pl.pallas_call(
        functools.partial(_dispatch_kernel, tile=tile),
        grid_spec=grid_spec,
        out_shape=jax.ShapeDtypeStruct((n_rows, width), m.dtype),
        compiler_params=pltpu.CompilerParams(dimension_semantics=("arbitrary",),
                                             has_side_effects=True, vmem_limit_bytes=VMEM_LIMIT),
        name="moe_dispatch",
    )(pads, pos3, m)


def _experts_kernel(te_ref, nused_ref, xs_ref, wg_ref, wu_ref, wd_ref, ys_ref, wgu_b, wd_b):
    j = pl.program_id(0)
    prev = te_ref[jnp.maximum(j - 1, 0)]

    @pl.when((j == 0) | (te_ref[j] != prev))
    def _():
        wgu_b[:, :D_EXPERT] = wg_ref[...].astype(BF16)
        wgu_b[:, D_EXPERT:] = wu_ref[...].astype(BF16)
        wd_b[...] = wd_ref[...].astype(BF16)

    @pl.when(j < nused_ref[0])
    def _():
        gu = jnp.dot(xs_ref[...].astype(BF16), wgu_b[...], preferred_element_type=F32)
        g = gu[:, :D_EXPERT]
        act = g * jax.nn.sigmoid(g) * gu[:, D_EXPERT:]
        ys_ref[...] = jnp.dot(act.astype(BF16), wd_b[...], preferred_element_type=F32)

    @pl.when(j >= nused_ref[0])
    def _():
        ys_ref[...] = jnp.zeros_like(ys_ref)


def _experts(layer, tile_expert, n_used, xs, wg, wu, wd):
    n_rows = xs.shape[0]
    nt = n_rows // EXPERT_TILE
    grid_spec = pltpu.PrefetchScalarGridSpec(
        num_scalar_prefetch=2,
        grid=(nt,),
        in_specs=[pl.BlockSpec((EXPERT_TILE, D_MODEL), lambda j, te, nu: (jnp.minimum(j, nu[0] - 1), 0)),
                  pl.BlockSpec((None, None, D_MODEL, D_EXPERT), lambda j, te, nu: (layer, te[j], 0, 0)),
                  pl.BlockSpec((None, None, D_MODEL, D_EXPERT), lambda j, te, nu: (layer, te[j], 0, 0)),
                  pl.BlockSpec((None, None, D_EXPERT, D_MODEL), lambda j, te, nu: (layer, te[j], 0, 0))],
        out_specs=pl.BlockSpec((EXPERT_TILE, D_MODEL), lambda j, te, nu: (j, 0)),
        scratch_shapes=[pltpu.VMEM((D_MODEL, 2 * D_EXPERT), BF16),
                        pltpu.VMEM((D_EXPERT, D_MODEL), BF16)],
    )
    return pl.pallas_call(
        _experts_kernel,
        grid_spec=grid_spec,
        out_shape=jax.ShapeDtypeStruct((n_rows, D_MODEL), F32),
        compiler_params=_cparams(1),
        name="moe_experts",
    )(tile_expert, n_used, xs, wg, wu, wd)


def _combine_ple_kernel(pos_ref, next_pos_ref, route_ref, h1_ref, ys_hbm, p_ref, wp_ref, g_ref, wgate_ref,
                        o_ref, ybuf, sems, *, tile):
    i = pl.program_id(0)
    n = pl.num_programs(0)
    slot = i % 2

    def gather(table, s):
        def issue(c, carry):
            base = pl.multiple_of(c * DMA_UNROLL, DMA_UNROLL)
            for k in range(2):
                group = ybuf.at[s, k, pl.ds(base, DMA_UNROLL)]
                for u in range(DMA_UNROLL):
                    pltpu.make_async_copy(ys_hbm.at[pl.ds(table[0, 0, 2 * (base + u) + k], 1)],
                                          group.at[pl.ds(u, 1)], sems.at[s]).start(priority=k)
            return carry

        lax.fori_loop(0, tile // DMA_UNROLL, issue, 0)

    @pl.when(i == 0)
    def _():
        gather(pos_ref, 0)

    @pl.when(i + 1 < n)
    def _():
        gather(next_pos_ref, 1 - slot)

    halves = [slice(0, tile // 2), slice(tile // 2, tile)]
    p_b = p_ref[...].astype(BF16)
    ples = [jnp.dot(p_b[rows], wp_ref[...], preferred_element_type=F32) for rows in halves]
    for k in range(2):
        pltpu.make_async_copy(ys_hbm.at[pl.ds(0, tile)], ybuf.at[slot, k], sems.at[slot]).wait()

    route = route_ref[...]
    h2s = [h1_ref[rows, :] + route[rows, 4:5] * ybuf[slot, 0, rows, :] + route[rows, 5:6] * ybuf[slot, 1, rows, :]
           for rows in halves]
    normed = [_rms(h2, g_ref[...]).astype(BF16) for h2 in h2s]
    gates = [jnp.dot(x, wgate_ref[...], preferred_element_type=F32) for x in normed]
    for rows, h2, gate, ple in zip(halves, h2s, gates, ples):
        o_ref[rows, :] = h2 + jax.nn.sigmoid(gate) * ple


def _combine_ple(layer, pos3, route, h1, ys, p, wp, g, wgate):
    T = h1.shape[0]
    n, _, width = pos3.shape
    tile = width // 2
    const = lambda *shape: pl.BlockSpec(shape, lambda i: (0,) * len(shape))
    row = lambda width: pl.BlockSpec((tile, width), lambda i: (i, 0))
    return pl.pallas_call(
        functools.partial(_combine_ple_kernel, tile=tile),
        grid=(n,),
        in_specs=[pl.BlockSpec((1, 1, 2 * tile), lambda i: (0, 0, 0), memory_space=pltpu.SMEM),
                  pl.BlockSpec((1, 1, 2 * tile), lambda i: (jnp.minimum(i + 1, n - 1), 0, 0),
                               memory_space=pltpu.SMEM),
                  row(LANES), row(D_MODEL), pl.BlockSpec(memory_space=pl.ANY),
                  pl.BlockSpec((None, tile, D_PLE), lambda i: (layer, i, 0)),
                  const(D_PLE, D_MODEL), const(1, D_MODEL), const(D_MODEL, D_MODEL)],
        out_specs=row(D_MODEL),
        out_shape=jax.ShapeDtypeStruct((T, D_MODEL), F32),
        scratch_shapes=[pltpu.VMEM((2, 2, tile, D_MODEL), F32), pltpu.SemaphoreType.DMA((2,))],
        compiler_params=_cparams(1),
        name="combine_ple",
    )(pos3, pos3, route, h1, ys, p, wp, g, wgate)


def _rope(z, cos, sin_lo, sin_hi):
    half = ROT_DIM // 2
    outs = []
    for j in range(z.shape[1] // LANES):
        zj = z[:, j * LANES:(j + 1) * LANES]
        outs.append(zj * cos + pltpu.roll(zj, LANES - half, 1) * sin_lo + pltpu.roll(zj, half, 1) * sin_hi)
    return outs[0] if len(outs) == 1 else jnp.concatenate(outs, axis=1)


def _proj_odd_kernel(h_ref, g_ref, w_ref, qg_ref, kg_ref, cos_ref, slo_ref, shi_ref,
                     q_ref, k_ref, v_ref, glu_ref):
    a = _rms(h_ref[...], g_ref[...]).astype(BF16)
    cos, slo, shi = cos_ref[...], slo_ref[...], shi_ref[...]
    v0 = SWA_WIDTH + KV_WIDTH
    d0 = v0 + KV_WIDTH
    zq = jnp.dot(a, w_ref[:, 0:SWA_WIDTH], preferred_element_type=F32)
    zk = jnp.dot(a, w_ref[:, SWA_WIDTH:v0], preferred_element_type=F32)
    zv = jnp.dot(a, w_ref[:, v0:d0], preferred_element_type=F32)
    zd = jnp.dot(a, w_ref[:, d0:d0 + 2 * CONV_CH], preferred_element_type=F32)
    q_ref[...] = (_rope(_head_rms(zq, qg_ref[...]), cos, slo, shi) * (HEAD_DIM ** -0.5)).astype(BF16)
    k_ref[...] = _rope(_head_rms(zk, kg_ref[...]), cos, slo, shi).astype(BF16)
    v_ref[...] = zv.astype(BF16)
    glu_ref[...] = zd[:, :CONV_CH] * jax.nn.sigmoid(zd[:, CONV_CH:])


def _proj_odd(h, g, w, qg, kg, cos, slo, shi, *, seq):
    T = h.shape[0]
    tm = min(ROW_TILE, seq)
    tps = seq // tm
    const = lambda *shape: pl.BlockSpec(shape, lambda i: (0,) * len(shape))
    row = lambda width: pl.BlockSpec((tm, width), lambda i: (i, 0))
    tab = pl.BlockSpec((tm, LANES), lambda i: (i % tps, 0))
    return pl.pallas_call(
        _proj_odd_kernel,
        grid=(T // tm,),
        in_specs=[row(D_MODEL), const(1, D_MODEL), const(*w.shape), const(1, SWA_WIDTH),
                  const(1, KV_WIDTH), tab, tab, tab],
        out_specs=[row(SWA_WIDTH), row(KV_WIDTH), row(KV_WIDTH), row(CONV_CH)],
        out_shape=[jax.ShapeDtypeStruct((T, SWA_WIDTH), BF16),
                   jax.ShapeDtypeStruct((T, KV_WIDTH), BF16),
                   jax.ShapeDtypeStruct((T, KV_WIDTH), BF16),
                   jax.ShapeDtypeStruct((T, CONV_CH), F32)],
        compiler_params=_cparams(1),
        name="proj_odd",
    )(h, g, w, qg, kg, cos, slo, shi)


def _swa_kernel(sink_ref, q_ref, k_ref, v_ref, o_ref, *, seq):
    lo = _lane_lo((1, LANES))
    nt = (((1,), (1,)), ((), ()))
    W = WINDOW
    qi = lax.broadcasted_iota(jnp.int32, (W, 2 * W), 0)
    kj = lax.broadcasted_iota(jnp.int32, (W, 2 * W), 1)
    band = (kj > qi) & (kj <= qi + W)

    n_tiles = SWA_WIDTH // LANES
    heads = [head for j in range(n_tiles) for head in (j, n_tiles + j)]

    def window(n):
        start = max(n - 1, 0) * W
        return slice(start, start + 2 * W), (band if n > 0 else kj <= qi)

    def score_products(n):
        kwin, _ = window(n)
        ks = k_ref[kwin, :]
        scores = []
        for j in range(n_tiles):
            q = q_ref[n * W:(n + 1) * W, j * LANES:(j + 1) * LANES]
            zero = jnp.zeros_like(q)
            scores += [lax.dot_general(qh, ks, nt, preferred_element_type=F32)
                       for qh in (jnp.where(lo, q, zero), jnp.where(lo, zero, q))]
        return scores

    scores = score_products(0)
    for n in range(seq // W):
        next_scores = score_products(n + 1) if (n + 1) * W < seq else None
        kwin, mask = window(n)
        vs = v_ref[kwin, :]
        probs, sums = [], []
        for head, s in zip(heads, scores):
            s = jnp.where(mask, s, NEG)
            sink = sink_ref[head]
            m = jnp.maximum(jnp.max(s, axis=-1, keepdims=True), sink)
            p = jnp.exp(s - m)
            sums.append(jnp.sum(p, axis=-1, keepdims=True) + jnp.exp(sink - m))
            probs.append(p.astype(BF16))
        outs = [jnp.dot(p, vs, preferred_element_type=F32) / l for p, l in zip(probs, sums)]
        for j in range(n_tiles):
            o_ref[n * W:(n + 1) * W, j * LANES:(j + 1) * LANES] = (
                jnp.where(lo, outs[2 * j], outs[2 * j + 1]).astype(BF16))
        scores = next_scores


def _swa_attention(sinks, q, k, v, *, seq):
    T = q.shape[0]
    B = T // seq
    return pl.pallas_call(
        functools.partial(_swa_kernel, seq=seq),
        grid=(B,),
        in_specs=[pl.BlockSpec(memory_space=pltpu.SMEM),
                  pl.BlockSpec((seq, SWA_WIDTH), lambda b: (b, 0)),
                  pl.BlockSpec((seq, KV_WIDTH), lambda b: (b, 0)),
                  pl.BlockSpec((seq, KV_WIDTH), lambda b: (b, 0))],
        out_specs=pl.BlockSpec((seq, SWA_WIDTH), lambda b: (b, 0)),
        out_shape=jax.ShapeDtypeStruct((T, SWA_WIDTH), BF16),
        compiler_params=_cparams(1),
        name="swa_attention",
    )(sinks, q, k, v)


def _conv_kernel(prev_ref, cur_ref, w_ref, g_ref, b_ref, o_ref, shift_ref, *, tile, sub):
    r = pl.program_id(1)
    rows = CONV_HALO + tile
    tail = prev_ref[tile - CONV_HALO:, :]
    shift_ref[0, 0:CONV_HALO, :] = jnp.where(r > 0, tail, jnp.zeros_like(tail))
    shift_ref[0, CONV_HALO:rows, :] = cur_ref[...]
    shift_ref[0, rows:rows + SUBLANES, :] = jnp.zeros((SUBLANES, CONV_CH), F32)
    for o in range(1, SUBLANES):
        shift_ref[o, 0:rows, :] = shift_ref[0, o:o + rows, :]
    w = w_ref[...]
    first = CONV_HALO - (CONV_WIDTH - 1)
    for s in range(tile // sub):
        acc = jnp.zeros((sub, CONV_CH), F32)
        for j in range(CONV_WIDTH):
            start = s * sub + first + j
            o = start % SUBLANES
            acc = acc + shift_ref[o, start - o:start - o + sub, :] * w[j:j + 1, :]
        y = _layernorm(acc, g_ref[...], b_ref[...])
        o_ref[s * sub:(s + 1) * sub, :] = (y * jax.nn.sigmoid(y)).astype(BF16)


def _conv_module(glu, w, g, b, *, seq):
    T = glu.shape[0]
    B = T // seq
    tile = min(CONV_TILE, seq)
    nr = seq // tile
    const = lambda *shape: pl.BlockSpec(shape, lambda bb, r: (0,) * len(shape))
    return pl.pallas_call(
        functools.partial(_conv_kernel, tile=tile, sub=64),
        grid=(B, nr),
        in_specs=[pl.BlockSpec((tile, CONV_CH), lambda bb, r: (bb * nr + jnp.maximum(r - 1, 0), 0)),
                  pl.BlockSpec((tile, CONV_CH), lambda bb, r: (bb * nr + r, 0)),
                  const(CONV_WIDTH, CONV_CH), const(1, CONV_CH), const(1, CONV_CH)],
        out_specs=pl.BlockSpec((tile, CONV_CH), lambda bb, r: (bb * nr + r, 0)),
        out_shape=jax.ShapeDtypeStruct((T, CONV_CH), BF16),
        scratch_shapes=[pltpu.VMEM((SUBLANES, CONV_HALO + tile + SUBLANES, CONV_CH), F32)],
        compiler_params=_cparams(2),
        name="conv_module",
    )(glu, glu, w, g, b)


def _routing_tables(route, counts, n_tiles):
    e = route[:, 0:2].astype(jnp.int32)
    rank = route[:, 2:4].astype(jnp.int32)
    cnt = counts[0, :N_EXPERTS].astype(jnp.int32)
    tiles = (cnt + EXPERT_TILE - 1) // EXPERT_TILE
    tile_end = jnp.cumsum(tiles)
    offset = (tile_end - tiles) * EXPERT_TILE
    onehot = e[:, :, None] == jnp.arange(N_EXPERTS, dtype=jnp.int32)
    pos = (rank + jnp.sum(jnp.where(onehot, offset, 0), axis=-1)).reshape(-1)
    n_used = tile_end[-1]
    tile_id = jnp.minimum(jnp.arange(n_tiles, dtype=jnp.int32), n_used - 1)
    tile_expert = jnp.sum((tile_end[None, :] <= tile_id[:, None]).astype(jnp.int32), axis=1)
    first_pad = offset + cnt
    n_single = (-first_pad) % SUBLANES
    pads = jnp.stack([first_pad, n_single, tiles * EXPERT_TILE - cnt - n_single,
                      jnp.broadcast_to(tile_end[-1], cnt.shape)])
    return pos, pads, tile_expert, n_used.reshape(1).astype(jnp.int32)


def _moe_ple(h, ya, yb, wo, layer, norm_ffn, wr, br, wg, wu, wd, p, wp, ple_norm, wgate):
    T = h.shape[0]
    n_tiles = (2 * T) // EXPERT_TILE + N_EXPERTS
    h1, m, route, counts = _outproj_router(h, ya, yb, wo, norm_ffn, wr, br)
    pos, pads, tile_expert, n_used = _routing_tables(route, counts, n_tiles)
    xs = _dispatch(pads, pos, m, n_tiles * EXPERT_TILE)
    pos3 = pos.reshape(-1, 1, 2 * min(GATHER_TILE, T))
    ys = _experts(layer, tile_expert, n_used, xs, wg, wu, wd)
    return _combine_ple(layer, pos3, route, h1, ys, p, wp, ple_norm, wgate)


def _router_weights(w_coarse, b_coarse, w_fine, b_fine):
    wf = w_fine.transpose(1, 0, 2).reshape(D_MODEL, N_EXPERTS)
    wr = jnp.concatenate([wf, w_coarse, jnp.zeros((D_MODEL, LANES - N_EXPERTS - N_GROUPS), F32)], axis=1)
    br = jnp.concatenate([b_fine.reshape(-1), b_coarse, jnp.zeros((LANES - N_EXPERTS - N_GROUPS,), F32)])
    w_hi = wr.astype(BF16)
    w_lo = (wr - w_hi.astype(F32)).astype(BF16)
    return jnp.concatenate([w_hi, w_lo], axis=1), br.reshape(1, LANES)


def _rope_tables(seq):
    half = ROT_DIM // 2
    inv_freq = ROPE_THETA ** (-jnp.arange(half, dtype=F32) * 2.0 / ROT_DIM)
    ang = jnp.arange(seq, dtype=F32)[:, None] * inv_freq[None, :]
    cos, sin = jnp.cos(ang), jnp.sin(ang)
    zeros = jnp.zeros((seq, HEAD_DIM - ROT_DIM), F32)
    z8 = jnp.zeros((seq, half), F32)
    cos_h = jnp.concatenate([cos, cos, zeros + 1.0], axis=1)
    slo_h = jnp.concatenate([-sin, z8, zeros], axis=1)
    shi_h = jnp.concatenate([z8, sin, zeros], axis=1)
    two = lambda t: jnp.concatenate([t, t], axis=1)
    return two(cos_h), two(slo_h), two(shi_h)


def kernel(x, p, norm_mix, even_w_in, fox_b_f, gmlp_ln_g, gmlp_ln_b, gmlp_w_s, gmlp_b_s, fox_q_norm, fox_k_norm, even_w_out, odd_w_in, swa_q_norm, swa_k_norm, swa_sinks, conv_w, conv_ln_g, conv_ln_b, odd_w_out, norm_ffn, moe_w_coarse, moe_b_coarse, moe_w_fine, moe_b_fine, moe_w_gate, moe_w_up, moe_w_down, ple_w_proj, ple_norm, ple_w_gate):
    B, S, D = x.shape
    T = B * S
    h = x.reshape(T, D)
    p = p.reshape(p.shape[0], T, D_PLE)
    row = lambda v: v.reshape(1, -1)

    def moe_args(i):
        wr, br = _router_weights(moe_w_coarse[i], moe_b_coarse[i], moe_w_fine[i], moe_b_fine[i])
        return (i, row(norm_ffn[i]), wr, br, moe_w_gate, moe_w_up, moe_w_down, p,
                ple_w_proj[i].astype(BF16), row(ple_norm[i]), ple_w_gate[i].astype(BF16))

    n_main = 2 * GMLP_WIDTH + 3 * FOX_WIDTH
    w_f = jnp.pad(jnp.tile(even_w_in[0][:, n_main:], (1, BIAS_PIECES)),
                  ((0, 0), (0, LANES - BIAS_PIECES * FOX_HEADS)))
    w_in = jnp.concatenate([even_w_in[0][:, :n_main], w_f], axis=1).astype(BF16)
    b_f = jnp.pad(jnp.tile(fox_b_f[0], BIAS_PIECES), (0, LANES - BIAS_PIECES * FOX_HEADS)).reshape(1, LANES)
    bs_full = jnp.repeat(gmlp_b_s[0].T, HEAD_DIM, axis=1)
    ya, q, k, v, xq, xk = _proj_even(
        h, row(norm_mix[0]), w_in, b_f, row(gmlp_ln_g[0]), row(gmlp_ln_b[0]), gmlp_w_s[0], bs_full,
        row(jnp.tile(fox_q_norm[0], FOX_HEADS)), row(jnp.tile(fox_k_norm[0], FOX_HEADS)), seq=S)
    yb = _fox_attention(q, xq, k, xk, v, seq=S)
    h = _moe_ple(h, ya, yb, even_w_out[0].astype(BF16), *moe_args(0))

    order = jnp.array([0, 4, 1, 5, 2, 6, 3, 7])
    cols = (order[:, None] * HEAD_DIM + jnp.arange(HEAD_DIM)[None, :]).reshape(-1)
    w_odd = jnp.concatenate([odd_w_in[0][:, :SWA_WIDTH][:, cols], odd_w_in[0][:, SWA_WIDTH:]], axis=1).astype(BF16)
    w_out_odd = jnp.concatenate([odd_w_out[0][:SWA_WIDTH][cols], odd_w_out[0][SWA_WIDTH:]], axis=0).astype(BF16)
    cos, slo, shi = _rope_tables(S)
    q, k, v, glu = _proj_odd(h, row(norm_mix[1]), w_odd, row(jnp.tile(swa_q_norm[0], 8)),
                             row(jnp.tile(swa_k_norm[0], 2)), cos, slo, shi, seq=S)
    yc = _swa_attention(swa_sinks[0], q, k, v, seq=S)
    yd = _conv_module(glu, conv_w[0], row(conv_ln_g[0]), row(conv_ln_b[0]), seq=S)
    h = _moe_ple(h, yc, yd, w_out_odd, *moe_args(1))
    return h.reshape(B, S, D)
```

```python
import functools

import jax
import jax.numpy as jnp
import numpy as np
from jax import lax
from jax.experimental import pallas as pl
from jax.experimental.pallas import tpu as pltpu

F32 = jnp.float32
BF16 = jnp.bfloat16
HIGHEST = lax.Precision.HIGHEST

D_MODEL = 1024
HEAD_DIM = 64
LANES = 128
GMLP_WIDTH = 512
CHUNK = 128
FOX_WIDTH = 512
FOX_HEADS = 8
SWA_WIDTH = 512
KV_WIDTH = 128
WINDOW = 128
CONV_CH = 512
CONV_WIDTH = 31
CONV_HALO = 32
ROPE_THETA = 500000.0
ROT_DIM = 16
N_GROUPS = 4
EXPERTS_PER_GROUP = 8
N_EXPERTS = 32
D_EXPERT = 256
D_PLE = 256
EPS = 1e-6
NEG = -1e30
LOG2E = 1.4426950408889634
BIAS_PIECES = 3

EXPERT_TILE = 512
PAD_BITS = 9
SUBLANES = 8
SUBLANE_BITS = 3
ROW_TILE = 1024
ROUTER_BLOCKS = 8
GATHER_TILE = 512
DISPATCH_TILE = 1024
DMA_UNROLL = 8
ATTN_TILE = 256
FOX_Q_ROWS = 512
FOX_PAIRS_PER_STEP = 2
CONV_TILE = 512
VMEM_LIMIT = 56 * 1024 * 1024


def _cparams(n_axes=1, flags=None):
    return pltpu.CompilerParams(dimension_semantics=("arbitrary",) * n_axes,
                                vmem_limit_bytes=VMEM_LIMIT, flags=flags)


def _rms(x, gain):
    return x * lax.rsqrt(jnp.mean(x * x, axis=-1, keepdims=True) + EPS) * gain


def _layernorm(x, g, b):
    mu = jnp.mean(x, axis=-1, keepdims=True)
    xc = x - mu
    var = jnp.mean(xc * xc, axis=-1, keepdims=True)
    return xc * lax.rsqrt(var + EPS) * g + b


def _head_rms(z, gain):
    lo = _lane_lo((1, LANES))
    outs = []
    for j in range(z.shape[1] // LANES):
        zj = z[:, j * LANES:(j + 1) * LANES]
        sq = zj * zj
        sum_lo = jnp.sum(jnp.where(lo, sq, 0.0), axis=-1, keepdims=True)
        sum_hi = jnp.sum(jnp.where(lo, 0.0, sq), axis=-1, keepdims=True)
        ms = jnp.where(lo, sum_lo, sum_hi) * (1.0 / HEAD_DIM)
        outs.append(zj * lax.rsqrt(ms + EPS))
    zn = outs[0] if len(outs) == 1 else jnp.concatenate(outs, axis=1)
    return zn * gain


def _lane_lo(shape):
    return (lax.broadcasted_iota(jnp.int32, shape, len(shape) - 1) % LANES) < HEAD_DIM


def _proj_even_kernel(h_ref, g_ref, w_ref, bf_ref, lng_ref, lnb_ref, ws_ref, bs_ref, qg_ref, kg_ref,
                      pq_ref, pk_ref, oneq_ref, onek_ref,
                      ya_ref, q_ref, k_ref, v_ref, xq_ref, xk_ref, carry_ref, *, tm, tiles_per_seq):
    i = pl.program_id(0)
    a = _rms(h_ref[...], g_ref[...]).astype(BF16)

    q0 = 2 * GMLP_WIDTH
    za = jnp.dot(a, w_ref[:, 0:q0], preferred_element_type=F32)
    zq = jnp.dot(a, w_ref[:, q0:q0 + FOX_WIDTH], preferred_element_type=F32)
    zk = jnp.dot(a, w_ref[:, q0 + FOX_WIDTH:q0 + 2 * FOX_WIDTH], preferred_element_type=F32)
    zv = jnp.dot(a, w_ref[:, q0 + 2 * FOX_WIDTH:q0 + 3 * FOX_WIDTH], preferred_element_type=F32)
    zf = jnp.dot(a, w_ref[:, q0 + 3 * FOX_WIDTH:], preferred_element_type=F32) + bf_ref[...]

    q_ref[...] = (_head_rms(zq, qg_ref[...]) * (LOG2E * HEAD_DIM ** -0.5)).astype(BF16)
    k_ref[...] = _head_rms(zk, kg_ref[...]).astype(BF16)
    v_ref[...] = zv.astype(BF16)

    rr = lax.broadcasted_iota(jnp.int32, (CHUNK, CHUNK), 0)
    cc = lax.broadcasted_iota(jnp.int32, (CHUNK, CHUNK), 1)

    ls = jnp.minimum(zf, 0.0) - jnp.log(1.0 + jnp.exp(-jnp.abs(zf)))

    @pl.when(i % tiles_per_seq == 0)
    def _():
        carry_ref[...] = jnp.zeros_like(carry_ref)

    def split3(x):
        hi = x.astype(BF16)
        r1 = x - hi.astype(F32)
        mid = r1.astype(BF16)
        return hi, mid, (r1 - mid.astype(F32)).astype(BF16)

    tri = jnp.where(rr >= cc, 1.0, 0.0).astype(BF16)
    pieces = jnp.concatenate(split3(ls), axis=1)
    running = carry_ref[...]
    blocks = []
    for b in range(tm // CHUNK):
        d = jnp.dot(tri, pieces[b * CHUNK:(b + 1) * CHUNK, :], preferred_element_type=F32)
        blk = (d[:, :LANES] + d[:, LANES:2 * LANES]) + d[:, 2 * LANES:] + running
        running = blk[CHUNK - 1:CHUNK, :]
        blocks.append(blk)
    carry_ref[...] = running
    c = jnp.concatenate(blocks, axis=0)

    za = jax.nn.gelu(za)
    u = za[:, :GMLP_WIDTH]
    vln = _layernorm(za[:, GMLP_WIDTH:], lng_ref[...], lnb_ref[...]).astype(BF16)
    lo = _lane_lo((CHUNK, LANES))
    for j in range(GMLP_WIDTH // LANES):
        w_a = jnp.where(rr >= cc, ws_ref[2 * j], 0.0).astype(BF16)
        w_b = jnp.where(rr >= cc, ws_ref[2 * j + 1], 0.0).astype(BF16)
        cols = slice(j * LANES, (j + 1) * LANES)
        for blk in range(tm // CHUNK):
            rows = slice(blk * CHUNK, (blk + 1) * CHUNK)
            vp = vln[rows, cols]
            mixed = jnp.where(lo, jnp.dot(w_a, vp, preferred_element_type=F32),
                              jnp.dot(w_b, vp, preferred_element_type=F32)) + bs_ref[:, cols]
            ya_ref[rows, cols] = (u[rows, cols] * mixed).astype(BF16)

    hi, mid, low = split3(c * LOG2E)
    group = lax.broadcasted_iota(jnp.int32, (tm, LANES), 1) // FOX_HEADS
    sel = jnp.where(group == 0, hi, jnp.where(group == 1, mid, low))
    xq_ref[...] = (jnp.dot(sel, pq_ref[...], preferred_element_type=F32) + oneq_ref[...]).astype(BF16)
    xk_ref[...] = (jnp.dot(sel, pk_ref[...], preferred_element_type=F32) + onek_ref[...]).astype(BF16)


def _proj_even(h, g, w, bf, lng, lnb, ws, bs_full, qg, kg, *, seq):
    T = h.shape[0]
    tm = min(ROW_TILE, seq)
    n_in = w.shape[1]
    const = lambda *shape: pl.BlockSpec(shape, lambda i: (0,) * len(shape))
    row = lambda width: pl.BlockSpec((tm, width), lambda i: (i, 0))
    tps = seq // tm
    return pl.pallas_call(
        functools.partial(_proj_even_kernel, tm=tm, tiles_per_seq=tps),
        grid=(T // tm,),
        in_specs=[row(D_MODEL), const(1, D_MODEL), const(D_MODEL, n_in), const(1, LANES),
                  const(1, GMLP_WIDTH), const(1, GMLP_WIDTH), const(8, CHUNK, CHUNK),
                  const(CHUNK, GMLP_WIDTH), const(1, FOX_WIDTH), const(1, FOX_WIDTH),
                  const(LANES, FOX_WIDTH), const(LANES, FOX_WIDTH),
                  const(1, FOX_WIDTH), const(1, FOX_WIDTH)],
        out_specs=[row(GMLP_WIDTH)] + [row(FOX_WIDTH)] * 5,
        out_shape=[jax.ShapeDtypeStruct((T, GMLP_WIDTH), BF16)]
                  + [jax.ShapeDtypeStruct((T, FOX_WIDTH), BF16)] * 5,
        scratch_shapes=[pltpu.VMEM((1, LANES), F32)],
        compiler_params=_cparams(1),
        name="proj_even",
    )(h, g, w, bf, lng, lnb, ws, bs_full, qg, kg, *_bias_placement())


def _bias_placement():
    pq = np.zeros((LANES, FOX_WIDTH), np.float32)
    pk = np.zeros((LANES, FOX_WIDTH), np.float32)
    oneq = np.zeros((1, FOX_WIDTH), np.float32)
    onek = np.zeros((1, FOX_WIDTH), np.float32)
    for head in range(FOX_HEADS):
        base = (head // 2) * LANES + (HEAD_DIM if head % 2 == 0 else 0)
        for piece in range(BIAS_PIECES):
            pq[piece * FOX_HEADS + head, base + piece] = 1.0
            onek[0, base + piece] = 1.0
            pk[piece * FOX_HEADS + head, base + BIAS_PIECES + piece] = -1.0
            oneq[0, base + BIAS_PIECES + piece] = 1.0
    return (jnp.asarray(pq, BF16), jnp.asarray(pk, BF16), jnp.asarray(oneq), jnp.asarray(onek))


def _fox_tile(i, q_ref, xq_ref, k_ref, xk_ref, v_ref, o_ref, *, tq, tk):
    part_rows = tk // 2
    lo = _lane_lo((1, LANES))
    rr = lax.broadcasted_iota(jnp.int32, (part_rows, tk), 0)
    cc = lax.broadcasted_iota(jnp.int32, (part_rows, tk), 1)
    nt = (((1,), (1,)), ((), ()))
    pairs = [slice(pp * LANES, (pp + 1) * LANES) for pp in range(FOX_PAIRS_PER_STEP)]
    first_row = [i * tq + part * part_rows for part in range(tq // part_rows)]
    n_tiles = [row // tk + 1 for row in first_row]
    diag_offset = [row % tk for row in first_row]
    chains = [(pp, part, head) for pp in range(len(pairs)) for part in range(len(first_row)) for head in range(2)]
    q_aug = {}
    for pp, cols in enumerate(pairs):
        for part in range(len(first_row)):
            rows = slice(part * part_rows, (part + 1) * part_rows)
            q, xq = q_ref[rows, cols], xq_ref[rows, cols]
            q_aug[pp, part, 0], q_aug[pp, part, 1] = jnp.where(lo, q, xq), jnp.where(lo, xq, q)

    def score_products(j):
        rows = slice(j * tk, (j + 1) * tk)
        k_aug = {}
        for pp, cols in enumerate(pairs):
            ks, xk = k_ref[rows, cols], xk_ref[rows, cols]
            k_aug[pp, 0], k_aug[pp, 1] = jnp.where(lo, ks, xk), jnp.where(lo, xk, ks)
        return {(pp, part, head): lax.dot_general(q_aug[pp, part, head], k_aug[pp, head], nt,
                                                  preferred_element_type=F32)
                for pp, part, head in chains if j < n_tiles[part]}

    maxes = {c: jnp.full((part_rows, 1), NEG, F32) for c in chains}
    accs = {c: jnp.zeros((part_rows, LANES), F32) for c in chains}
    scores = score_products(0)
    for j in range(max(n_tiles)):
        next_scores = score_products(j + 1) if j + 1 < max(n_tiles) else {}
        v_aug = {}
        for pp, cols in enumerate(pairs):
            vs = v_ref[j * tk:(j + 1) * tk, cols]
            one = jnp.ones_like(vs)
            v_aug[pp, 0], v_aug[pp, 1] = jnp.where(lo, vs, one), jnp.where(lo, one, vs)
        for c, s in scores.items():
            pp, part, head = c
            if j == n_tiles[part] - 1:
                s = jnp.where(cc <= rr + diag_offset[part], s, NEG)
            n = jnp.maximum(maxes[c], jnp.max(s, axis=-1, keepdims=True))
            p = jnp.exp2(s - n).astype(BF16)
            accs[c] = accs[c] * jnp.exp2(maxes[c] - n) + jnp.dot(p, v_aug[pp, head], preferred_element_type=F32)
            maxes[c] = n
        scores = next_scores
    for pp, cols in enumerate(pairs):
        for part in range(len(first_row)):
            norm = [accs[pp, part, head] / pltpu.roll(accs[pp, part, head], HEAD_DIM, 1) for head in range(2)]
            o_ref[part * part_rows:(part + 1) * part_rows, cols] = jnp.where(lo, norm[0], norm[1]).astype(BF16)


def _fox_kernel(q_ref, xq_ref, k_ref, xk_ref, v_ref, o_ref, *, tq, tk, nq):
    i = pl.program_id(2)
    for c in range(nq):
        pl.when(i == c)(functools.partial(_fox_tile, c, q_ref, xq_ref, k_ref, xk_ref, v_ref, o_ref, tq=tq, tk=tk))


def _fox_attention(q, xq, k, xk, v, *, seq):
    T = q.shape[0]
    B = T // seq
    tk = min(ATTN_TILE, seq)
    tq = min(FOX_Q_ROWS, seq)
    nq = seq // tq
    width = FOX_PAIRS_PER_STEP * LANES
    tile = pl.BlockSpec((tq, width), lambda b, hp, i: (b * nq + i, hp))
    whole = pl.BlockSpec((seq, width), lambda b, hp, i: (b, hp))
    return pl.pallas_call(
        functools.partial(_fox_kernel, tq=tq, tk=tk, nq=nq),
        grid=(B, FOX_WIDTH // width, nq),
        in_specs=[tile, tile, whole, whole, whole],
        out_specs=tile,
        out_shape=jax.ShapeDtypeStruct((T, FOX_WIDTH), BF16),
        compiler_params=_cparams(3),
        name="fox_attention",
    )(q, xq, k, xk, v)


def _outproj_router_kernel(h_ref, ya_ref, yb_ref, wo_ref, g_ref, wr_ref, br_ref,
                           h1_ref, m_ref, route_ref, cnt_ref, carry_ref, *, tm):
    i = pl.program_id(0)
    half = wo_ref.shape[0] // 2
    nb = ROUTER_BLOCKS
    rb = tm // nb
    blocks = [slice(b * rb, (b + 1) * rb) for b in range(nb)]

    @pl.when(i == 0)
    def _():
        carry_ref[...] = jnp.zeros_like(carry_ref)

    mixes = [jnp.dot(ya_ref[rows, :], wo_ref[0:half, :], preferred_element_type=F32)
             + jnp.dot(yb_ref[rows, :], wo_ref[half:, :], preferred_element_type=F32) for rows in blocks]
    ms = []
    for rows, mix in zip(blocks, mixes):
        h1 = h_ref[rows, :] + mix
        h1_ref[rows, :] = h1
        m = _rms(h1, g_ref[...])
        m_ref[rows, :] = m
        ms.append(m)

    logits = []
    for m in ms:
        m_hi = m.astype(BF16)
        m_lo = (m - m_hi.astype(F32)).astype(BF16)
        hh = jnp.dot(m_hi, wr_ref[...], preferred_element_type=F32)
        lh = jnp.dot(m_lo, wr_ref[:, :LANES], preferred_element_type=F32)
        logits.append(hh[:, :LANES] + (hh[:, LANES:] + lh) + br_ref[...])

    lane_i = lax.broadcasted_iota(jnp.int32, (rb, LANES), 1)
    lane = lane_i.astype(F32)
    group_of_lane = (lane_i // EXPERTS_PER_GROUP).astype(F32)
    is_coarse = (lane_i >= N_EXPERTS) & (lane_i < N_EXPERTS + N_GROUPS)
    picks = []
    for lg in logits:
        coarse = jnp.where(is_coarse, lg, NEG)
        cmax = jnp.max(coarse, axis=-1, keepdims=True)
        gidx = jnp.min(jnp.where(coarse == cmax, lane - N_EXPERTS, float(LANES)), axis=-1, keepdims=True)
        p_g = 1.0 / jnp.sum(jnp.where(is_coarse, jnp.exp(coarse - cmax), 0.0), axis=-1, keepdims=True)
        in_group = (lane_i < N_EXPERTS) & (group_of_lane == gidx)
        fine = jnp.where(in_group, lg, NEG)
        v1 = jnp.max(fine, axis=-1, keepdims=True)
        i1 = jnp.min(jnp.where(fine == v1, lane, float(LANES)), axis=-1, keepdims=True)
        fine2 = jnp.where(lane == i1, NEG, fine)
        v2 = jnp.max(fine2, axis=-1, keepdims=True)
        i2 = jnp.min(jnp.where(fine2 == v2, lane, float(LANES)), axis=-1, keepdims=True)
        e2 = jnp.exp(v2 - v1)
        picks.append((i1, i2, p_g / (1.0 + e2), p_g * e2 / (1.0 + e2)))

    tr = lax.broadcasted_iota(jnp.int32, (rb, rb), 0)
    tc = lax.broadcasted_iota(jnp.int32, (rb, rb), 1)
    strict = jnp.where(tr > tc, 1.0, 0.0).astype(BF16)
    onehots = [jnp.where((lane == i1) | (lane == i2), 1.0, 0.0).astype(F32) for i1, i2, _, _ in picks]
    befores = [jnp.dot(strict, oh.astype(BF16), preferred_element_type=F32) for oh in onehots]
    total = carry_ref[...]
    for rows, (i1, i2, w1, w2), oh, before in zip(blocks, picks, onehots, befores):
        before = before + total
        r1 = jnp.sum(jnp.where(lane == i1, before, 0.0), axis=-1, keepdims=True)
        r2 = jnp.sum(jnp.where(lane == i2, before, 0.0), axis=-1, keepdims=True)
        total = total + jnp.sum(oh, axis=0, keepdims=True)
        route = jnp.where(lane == 0, i1, 0.0)
        route = jnp.where(lane == 1, i2, route)
        route = jnp.where(lane == 2, r1, route)
        route = jnp.where(lane == 3, r2, route)
        route = jnp.where(lane == 4, w1, route)
        route = jnp.where(lane == 5, w2, route)
        route_ref[rows, :] = route
    carry_ref[...] = total
    cnt_ref[...] = jnp.broadcast_to(total, cnt_ref.shape)


def _outproj_router(h, ya, yb, wo, g, wr, br):
    T = h.shape[0]
    tm = min(ROW_TILE, T)
    const = lambda *shape: pl.BlockSpec(shape, lambda i: (0,) * len(shape))
    row = lambda width: pl.BlockSpec((tm, width), lambda i: (i, 0))
    return pl.pallas_call(
        functools.partial(_outproj_router_kernel, tm=tm),
        grid=(T // tm,),
        in_specs=[row(D_MODEL), row(ya.shape[1]), row(yb.shape[1]), const(*wo.shape),
                  const(1, D_MODEL), const(D_MODEL, 2 * LANES), const(1, LANES)],
        out_specs=[row(D_MODEL), row(D_MODEL), row(LANES), const(8, LANES)],
        out_shape=[jax.ShapeDtypeStruct((T, D_MODEL), F32),
                   jax.ShapeDtypeStruct((T, D_MODEL), F32),
                   jax.ShapeDtypeStruct((T, LANES), F32),
                   jax.ShapeDtypeStruct((8, LANES), F32)],
        scratch_shapes=[pltpu.VMEM((1, LANES), F32)],
        compiler_params=_cparams(1),
        name="outproj_router",
    )(h, ya, yb, wo, g, wr, br)


def _dispatch_kernel(pad_ref, pos_ref, m_ref, xs_hbm, zeros_ref, sem, pad_sem, *, tile):
    @pl.when(pl.program_id(0) == 0)
    def _():
        zeros_ref[...] = jnp.zeros_like(zeros_ref)

        def pad_copies(e, wait):
            first, n_single, n_block = pad_ref[0, e], pad_ref[1, e], pad_ref[2, e]
            for r in range(SUBLANES - 1):
                copy = pltpu.make_async_copy(zeros_ref.at[pl.ds(0, 1)], xs_hbm.at[pl.ds(first + r, 1)], pad_sem)
                pl.when(r < n_single)(copy.wait if wait else copy.start)
            done = first + n_single
            for bit in reversed(range(SUBLANE_BITS, PAD_BITS)):
                size = 1 << bit
                taken = (n_block & size) != 0
                copy = pltpu.make_async_copy(zeros_ref.at[pl.ds(0, size)],
                                             xs_hbm.at[pl.ds(pl.multiple_of(done, SUBLANES), size)], pad_sem)
                pl.when(taken)(copy.wait if wait else copy.start)
                done = done + jnp.where(taken, size, 0)

        def start(e, carry):
            pad_copies(e, False)
            return carry

        def finish(e, carry):
            pad_copies(e, True)
            return carry

        def tile_copy(j):
            return pltpu.make_async_copy(
                zeros_ref, xs_hbm.at[pl.ds(pl.multiple_of(j * EXPERT_TILE, EXPERT_TILE), EXPERT_TILE)], pad_sem)

        def start_tile(j, carry):
            tile_copy(j).start()
            return carry

        def finish_tile(j, carry):
            tile_copy(j).wait()
            return carry

        n_tiles = xs_hbm.shape[0] // EXPERT_TILE
        lax.fori_loop(0, N_EXPERTS, start, 0)
        lax.fori_loop(pad_ref[3, 0], n_tiles, start_tile, 0)
        lax.fori_loop(0, N_EXPERTS, finish, 0)
        lax.fori_loop(pad_ref[3, 0], n_tiles, finish_tile, 0)

    def issue(c, carry):
        base = pl.multiple_of(c * DMA_UNROLL, DMA_UNROLL)
        group = m_ref.at[pl.ds(base, DMA_UNROLL)]
        for u in range(DMA_UNROLL):
            for k in range(2):
                dst = xs_hbm.at[pl.ds(pos_ref[0, 0, k * tile + base + u], 1)]
                pltpu.make_async_copy(group.at[pl.ds(u, 1)], dst, sem).start(priority=k)
        return carry

    lax.fori_loop(0, tile // DMA_UNROLL, issue, 0)
    for _ in range(2):
        pltpu.make_async_copy(m_ref, xs_hbm.at[pl.ds(0, tile)], sem).wait()


def _dispatch(pads, pos3, m, n_rows):
    T, width = m.shape
    tile = pos3.shape[2] // 2
    grid_spec = pltpu.PrefetchScalarGridSpec(
        num_scalar_prefetch=1,
        grid=(T // tile,),
        in_specs=[pl.BlockSpec((1, 1, 2 * tile), lambda i, pads: (i, 0, 0), memory_space=pltpu.SMEM),
                  pl.BlockSpec((tile, width), lambda i, pads: (i, 0))],
        out_specs=pl.BlockSpec(memory_space=pl.ANY),
        scratch_shapes=[pltpu.VMEM((EXPERT_TILE, width), m.dtype),
                        pltpu.SemaphoreType.DMA(()), pltpu.SemaphoreType.DMA(())],
    )
    return pl.pallas_call(
        functools.partial(_dispatch_kernel, tile=tile),
        grid_spec=grid_spec,
        out_shape=jax.ShapeDtypeStruct((n_rows, width), m.dtype),
        compiler_params=pltpu.CompilerParams(dimension_semantics=("arbitrary",),
                                             has_side_effects=True, vmem_limit_bytes=VMEM_LIMIT),
        name="moe_dispatch",
    )(pads, pos3, m)


def _experts_kernel(te_ref, nused_ref, xs_ref, wg_ref, wu_ref, wd_ref, ys_ref, wgu_b, wd_b):
    j = pl.program_id(0)
    prev = te_ref[jnp.maximum(j - 1, 0)]

    @pl.when((j == 0) | (te_ref[j] != prev))
    def _():
        wgu_b[:, :D_EXPERT] = wg_ref[...].astype(BF16)
        wgu_b[:, D_EXPERT:] = wu_ref[...].astype(BF16)
        wd_b[...] = wd_ref[...].astype(BF16)

    @pl.when(j < nused_ref[0])
    def _():
        gu = jnp.dot(xs_ref[...].astype(BF16), wgu_b[...], preferred_element_type=F32)
        g = gu[:, :D_EXPERT]
        act = g * jax.nn.sigmoid(g) * gu[:, D_EXPERT:]
        ys_ref[...] = jnp.dot(act.astype(BF16), wd_b[...], preferred_element_type=F32)

    @pl.when(j >= nused_ref[0])
    def _():
        ys_ref[...] = jnp.zeros_like(ys_ref)


def _experts(layer, tile_expert, n_used, xs, wg, wu, wd):
    n_rows = xs.shape[0]
    nt = n_rows // EXPERT_TILE
    grid_spec = pltpu.PrefetchScalarGridSpec(
        num_scalar_prefetch=2,
        grid=(nt,),
        in_specs=[pl.BlockSpec((EXPERT_TILE, D_MODEL), lambda j, te, nu: (jnp.minimum(j, nu[0] - 1), 0)),
                  pl.BlockSpec((None, None, D_MODEL, D_EXPERT), lambda j, te, nu: (layer, te[j], 0, 0)),
                  pl.BlockSpec((None, None, D_MODEL, D_EXPERT), lambda j, te, nu: (layer, te[j], 0, 0)),
                  pl.BlockSpec((None, None, D_EXPERT, D_MODEL), lambda j, te, nu: (layer, te[j], 0, 0))],
        out_specs=pl.BlockSpec((EXPERT_TILE, D_MODEL), lambda j, te, nu: (j, 0)),
        scratch_shapes=[pltpu.VMEM((D_MODEL, 2 * D_EXPERT), BF16),
                        pltpu.VMEM((D_EXPERT, D_MODEL), BF16)],
    )
    return pl.pallas_call(
        _experts_kernel,
        grid_spec=grid_spec,
        out_shape=jax.ShapeDtypeStruct((n_rows, D_MODEL), F32),
        compiler_params=_cparams(1),
        name="moe_experts",
    )(tile_expert, n_used, xs, wg, wu, wd)


def _combine_ple_kernel(pos_ref, next_pos_ref, route_ref, h1_ref, ys_hbm, p_ref, wp_ref, g_ref, wgate_ref,
                        o_ref, ybuf, sems, *, tile):
    i = pl.program_id(0)
    n = pl.num_programs(0)
    slot = i % 2

    def gather(table, s):
        def issue(c, carry):
            base = pl.multiple_of(c * DMA_UNROLL, DMA_UNROLL)
            for k in range(2):
                group = ybuf.at[s, k, pl.ds(base, DMA_UNROLL)]
                for u in range(DMA_UNROLL):
                    pltpu.make_async_copy(ys_hbm.at[pl.ds(table[0, 0, k * tile + base + u], 1)],
                                          group.at[pl.ds(u, 1)], sems.at[s]).start(priority=k)
            return carry

        lax.fori_loop(0, tile // DMA_UNROLL, issue, 0)

    @pl.when(i == 0)
    def _():
        gather(pos_ref, 0)

    @pl.when(i + 1 < n)
    def _():
        gather(next_pos_ref, 1 - slot)

    halves = [slice(0, tile // 2), slice(tile // 2, tile)]
    p_b = p_ref[...].astype(BF16)
    ples = [jnp.dot(p_b[rows], wp_ref[...], preferred_element_type=F32) for rows in halves]
    for k in range(2):
        pltpu.make_async_copy(ys_hbm.at[pl.ds(0, tile)], ybuf.at[slot, k], sems.at[slot]).wait()

    route = route_ref[...]
    h2s = [h1_ref[rows, :] + route[rows, 4:5] * ybuf[slot, 0, rows, :] + route[rows, 5:6] * ybuf[slot, 1, rows, :]
           for rows in halves]
    normed = [_rms(h2, g_ref[...]).astype(BF16) for h2 in h2s]
    gates = [jnp.dot(x, wgate_ref[...], preferred_element_type=F32) for x in normed]
    for rows, h2, gate, ple in zip(halves, h2s, gates, ples):
        o_ref[rows, :] = h2 + jax.nn.sigmoid(gate) * ple


def _combine_ple(layer, pos3, route, h1, ys, p, wp, g, wgate):
    T = h1.shape[0]
    n, _, width = pos3.shape
    tile = width // 2
    const = lambda *shape: pl.BlockSpec(shape, lambda i: (0,) * len(shape))
    row = lambda width: pl.BlockSpec((tile, width), lambda i: (i, 0))
    return pl.pallas_call(
        functools.partial(_combine_ple_kernel, tile=tile),
        grid=(n,),
        in_specs=[pl.BlockSpec((1, 1, 2 * tile), lambda i: (0, 0, 0), memory_space=pltpu.SMEM),
                  pl.BlockSpec((1, 1, 2 * tile), lambda i: (jnp.minimum(i + 1, n - 1), 0, 0),
                               memory_space=pltpu.SMEM),
                  row(LANES), row(D_MODEL), pl.BlockSpec(memory_space=pl.ANY),
                  pl.BlockSpec((None, tile, D_PLE), lambda i: (layer, i, 0)),
                  const(D_PLE, D_MODEL), const(1, D_MODEL), const(D_MODEL, D_MODEL)],
        out_specs=row(D_MODEL),
        out_shape=jax.ShapeDtypeStruct((T, D_MODEL), F32),
        scratch_shapes=[pltpu.VMEM((2, 2, tile, D_MODEL), F32), pltpu.SemaphoreType.DMA((2,))],
        compiler_params=_cparams(1),
        name="combine_ple",
    )(pos3, pos3, route, h1, ys, p, wp, g, wgate)


def _rope(z, cos, sin_lo, sin_hi):
    half = ROT_DIM // 2
    outs = []
    for j in range(z.shape[1] // LANES):
        zj = z[:, j * LANES:(j + 1) * LANES]
        outs.append(zj * cos + pltpu.roll(zj, LANES - half, 1) * sin_lo + pltpu.roll(zj, half, 1) * sin_hi)
    return outs[0] if len(outs) == 1 else jnp.concatenate(outs, axis=1)


def _proj_odd_kernel(h_ref, g_ref, w_ref, qg_ref, kg_ref, cos_ref, slo_ref, shi_ref,
                     q_ref, k_ref, v_ref, glu_ref):
    a = _rms(h_ref[...], g_ref[...]).astype(BF16)
    cos, slo, shi = cos_ref[...], slo_ref[...], shi_ref[...]
    v0 = SWA_WIDTH + KV_WIDTH
    d0 = v0 + KV_WIDTH
    zq = jnp.dot(a, w_ref[:, 0:SWA_WIDTH], preferred_element_type=F32)
    zk = jnp.dot(a, w_ref[:, SWA_WIDTH:v0], preferred_element_type=F32)
    zv = jnp.dot(a, w_ref[:, v0:d0], preferred_element_type=F32)
    zd = jnp.dot(a, w_ref[:, d0:d0 + 2 * CONV_CH], preferred_element_type=F32)
    q_ref[...] = (_rope(_head_rms(zq, qg_ref[...]), cos, slo, shi) * (HEAD_DIM ** -0.5)).astype(BF16)
    k_ref[...] = _rope(_head_rms(zk, kg_ref[...]), cos, slo, shi).astype(BF16)
    v_ref[...] = zv.astype(BF16)
    glu_ref[...] = zd[:, :CONV_CH] * jax.nn.sigmoid(zd[:, CONV_CH:])


def _proj_odd(h, g, w, qg, kg, cos, slo, shi, *, seq):
    T = h.shape[0]
    tm = min(ROW_TILE, seq)
    tps = seq // tm
    const = lambda *shape: pl.BlockSpec(shape, lambda i: (0,) * len(shape))
    row = lambda width: pl.BlockSpec((tm, width), lambda i: (i, 0))
    tab = pl.BlockSpec((tm, LANES), lambda i: (i % tps, 0))
    return pl.pallas_call(
        _proj_odd_kernel,
        grid=(T // tm,),
        in_specs=[row(D_MODEL), const(1, D_MODEL), const(*w.shape), const(1, SWA_WIDTH),
                  const(1, KV_WIDTH), tab, tab, tab],
        out_specs=[row(SWA_WIDTH), row(KV_WIDTH), row(KV_WIDTH), row(CONV_CH)],
        out_shape=[jax.ShapeDtypeStruct((T, SWA_WIDTH), BF16),
                   jax.ShapeDtypeStruct((T, KV_WIDTH), BF16),
                   jax.ShapeDtypeStruct((T, KV_WIDTH), BF16),
                   jax.ShapeDtypeStruct((T, CONV_CH), F32)],
        compiler_params=_cparams(1),
        name="proj_odd",
    )(h, g, w, qg, kg, cos, slo, shi)


def _swa_kernel(sink_ref, q_ref, k_ref, v_ref, o_ref, *, seq):
    lo = _lane_lo((1, LANES))
    nt = (((1,), (1,)), ((), ()))
    W = WINDOW
    qi = lax.broadcasted_iota(jnp.int32, (W, 2 * W), 0)
    kj = lax.broadcasted_iota(jnp.int32, (W, 2 * W), 1)
    band = (kj > qi) & (kj <= qi + W)

    n_tiles = SWA_WIDTH // LANES
    heads = [head for j in range(n_tiles) for head in (j, n_tiles + j)]

    def window(n):
        start = max(n - 1, 0) * W
        return slice(start, start + 2 * W), (band if n > 0 else kj <= qi)

    def score_products(n):
        kwin, _ = window(n)
        ks = k_ref[kwin, :]
        scores = []
        for j in range(n_tiles):
            q = q_ref[n * W:(n + 1) * W, j * LANES:(j + 1) * LANES]
            zero = jnp.zeros_like(q)
            scores += [lax.dot_general(qh, ks, nt, preferred_element_type=F32)
                       for qh in (jnp.where(lo, q, zero), jnp.where(lo, zero, q))]
        return scores

    scores = score_products(0)
    for n in range(seq // W):
        next_scores = score_products(n + 1) if (n + 1) * W < seq else None
        kwin, mask = window(n)
        vs = v_ref[kwin, :]
        probs, sums = [], []
        for head, s in zip(heads, scores):
            s = jnp.where(mask, s, NEG)
            sink = sink_ref[head]
            m = jnp.maximum(jnp.max(s, axis=-1, keepdims=True), sink)
            p = jnp.exp(s - m)
            sums.append(jnp.sum(p, axis=-1, keepdims=True) + jnp.exp(sink - m))
            probs.append(p.astype(BF16))
        outs = [jnp.dot(p, vs, preferred_element_type=F32) / l for p, l in zip(probs, sums)]
        for j in range(n_tiles):
            o_ref[n * W:(n + 1) * W, j * LANES:(j + 1) * LANES] = (
                jnp.where(lo, outs[2 * j], outs[2 * j + 1]).astype(BF16))
        scores = next_scores


def _swa_attention(sinks, q, k, v, *, seq):
    T = q.shape[0]
    B = T // seq
    return pl.pallas_call(
        functools.partial(_swa_kernel, seq=seq),
        grid=(B,),
        in_specs=[pl.BlockSpec(memory_space=pltpu.SMEM),
                  pl.BlockSpec((seq, SWA_WIDTH), lambda b: (b, 0)),
                  pl.BlockSpec((seq, KV_WIDTH), lambda b: (b, 0)),
                  pl.BlockSpec((seq, KV_WIDTH), lambda b: (b, 0))],
        out_specs=pl.BlockSpec((seq, SWA_WIDTH), lambda b: (b, 0)),
        out_shape=jax.ShapeDtypeStruct((T, SWA_WIDTH), BF16),
        compiler_params=_cparams(1),
        name="swa_attention",
    )(sinks, q, k, v)


def _conv_kernel(prev_ref, cur_ref, w_ref, g_ref, b_ref, o_ref, shift_ref, *, tile, sub):
    r = pl.program_id(1)
    rows = CONV_HALO + tile
    tail = prev_ref[tile - CONV_HALO:, :]
    shift_ref[0, 0:CONV_HALO, :] = jnp.where(r > 0, tail, jnp.zeros_like(tail))
    shift_ref[0, CONV_HALO:rows, :] = cur_ref[...]
    shift_ref[0, rows:rows + SUBLANES, :] = jnp.zeros((SUBLANES, CONV_CH), F32)
    for o in range(1, SUBLANES):
        shift_ref[o, 0:rows, :] = shift_ref[0, o:o + rows, :]
    w = w_ref[...]
    first = CONV_HALO - (CONV_WIDTH - 1)
    for s in range(tile // sub):
        acc = jnp.zeros((sub, CONV_CH), F32)
        for j in range(CONV_WIDTH):
            start = s * sub + first + j
            o = start % SUBLANES
            acc = acc + shift_ref[o, start - o:start - o + sub, :] * w[j:j + 1, :]
        y = _layernorm(acc, g_ref[...], b_ref[...])
        o_ref[s * sub:(s + 1) * sub, :] = (y * jax.nn.sigmoid(y)).astype(BF16)


def _conv_module(glu, w, g, b, *, seq):
    T = glu.shape[0]
    B = T // seq
    tile = min(CONV_TILE, seq)
    nr = seq // tile
    const = lambda *shape: pl.BlockSpec(shape, lambda bb, r: (0,) * len(shape))
    return pl.pallas_call(
        functools.partial(_conv_kernel, tile=tile, sub=64),
        grid=(B, nr),
        in_specs=[pl.BlockSpec((tile, CONV_CH), lambda bb, r: (bb * nr + jnp.maximum(r - 1, 0), 0)),
                  pl.BlockSpec((tile, CONV_CH), lambda bb, r: (bb * nr + r, 0)),
                  const(CONV_WIDTH, CONV_CH), const(1, CONV_CH), const(1, CONV_CH)],
        out_specs=pl.BlockSpec((tile, CONV_CH), lambda bb, r: (bb * nr + r, 0)),
        out_shape=jax.ShapeDtypeStruct((T, CONV_CH), BF16),
        scratch_shapes=[pltpu.VMEM((SUBLANES, CONV_HALO + tile + SUBLANES, CONV_CH), F32)],
        compiler_params=_cparams(2),
        name="conv_module",
    )(glu, glu, w, g, b)


def _slot_tables_kernel(route_ref, off_ref, disp_ref, comb_ref, *, tm, tc):
    rt = route_ref[...].T
    slots = []
    for k in range(2):
        expert = rt[k:k + 1, :]
        first = jnp.zeros_like(expert)
        for e in range(N_EXPERTS):
            first = jnp.where(expert == float(e), off_ref[0:1, e:e + 1], first)
        slots.append((first + rt[2 + k:3 + k, :]).astype(jnp.int32))
    for k in range(2):
        disp_ref[0, :, k * tm:(k + 1) * tm] = slots[k]
        for b in range(tm // tc):
            comb_ref[b, :, k * tc:(k + 1) * tc] = slots[k][:, b * tc:(b + 1) * tc]


def _slot_tables(route, offset, *, tm, tc):
    T = route.shape[0]
    off = jnp.pad(offset.astype(F32), (0, LANES - N_EXPERTS)).reshape(1, LANES)
    return pl.pallas_call(
        functools.partial(_slot_tables_kernel, tm=tm, tc=tc),
        grid=(T // tm,),
        in_specs=[pl.BlockSpec((tm, LANES), lambda i: (i, 0)), pl.BlockSpec((1, LANES), lambda i: (0, 0))],
        out_specs=[pl.BlockSpec((1, 1, 2 * tm), lambda i: (i, 0, 0)),
                   pl.BlockSpec((tm // tc, 1, 2 * tc), lambda i: (i, 0, 0))],
        out_shape=[jax.ShapeDtypeStruct((T // tm, 1, 2 * tm), jnp.int32),
                   jax.ShapeDtypeStruct((T // tc, 1, 2 * tc), jnp.int32)],
        compiler_params=_cparams(1),
        name="moe_slot_tables",
    )(route, off)


def _routing_tables(counts, n_tiles):
    cnt = counts[0, :N_EXPERTS].astype(jnp.int32)
    tiles = (cnt + EXPERT_TILE - 1) // EXPERT_TILE
    tile_end = jnp.cumsum(tiles)
    offset = (tile_end - tiles) * EXPERT_TILE
    n_used = tile_end[-1]
    tile_id = jnp.minimum(jnp.arange(n_tiles, dtype=jnp.int32), n_used - 1)
    tile_expert = jnp.sum((tile_end[None, :] <= tile_id[:, None]).astype(jnp.int32), axis=1)
    first_pad = offset + cnt
    n_single = (-first_pad) % SUBLANES
    pads = jnp.stack([first_pad, n_single, tiles * EXPERT_TILE - cnt - n_single,
                      jnp.broadcast_to(tile_end[-1], cnt.shape)])
    return offset, pads, tile_expert, n_used.reshape(1).astype(jnp.int32)


def _moe_ple(h, ya, yb, wo, layer, norm_ffn, wr, br, wg, wu, wd, p, wp, ple_norm, wgate):
    T = h.shape[0]
    n_tiles = (2 * T) // EXPERT_TILE + N_EXPERTS
    h1, m, route, counts = _outproj_router(h, ya, yb, wo, norm_ffn, wr, br)
    offset, pads, tile_expert, n_used = _routing_tables(counts, n_tiles)
    disp_slots, comb_slots = _slot_tables(route, offset, tm=min(DISPATCH_TILE, T), tc=min(GATHER_TILE, T))
    xs = _dispatch(pads, disp_slots, m, n_tiles * EXPERT_TILE)
    ys = _experts(layer, tile_expert, n_used, xs, wg, wu, wd)
    return _combine_ple(layer, comb_slots, route, h1, ys, p, wp, ple_norm, wgate)


def _router_weights(w_coarse, b_coarse, w_fine, b_fine):
    wf = w_fine.transpose(1, 0, 2).reshape(D_MODEL, N_EXPERTS)
    wr = jnp.concatenate([wf, w_coarse, jnp.zeros((D_MODEL, LANES - N_EXPERTS - N_GROUPS), F32)], axis=1)
    br = jnp.concatenate([b_fine.reshape(-1), b_coarse, jnp.zeros((LANES - N_EXPERTS - N_GROUPS,), F32)])
    w_hi = wr.astype(BF16)
    w_lo = (wr - w_hi.astype(F32)).astype(BF16)
    return jnp.concatenate([w_hi, w_lo], axis=1), br.reshape(1, LANES)


def _rope_tables(seq):
    half = ROT_DIM // 2
    inv_freq = ROPE_THETA ** (-jnp.arange(half, dtype=F32) * 2.0 / ROT_DIM)
    ang = jnp.arange(seq, dtype=F32)[:, None] * inv_freq[None, :]
    cos, sin = jnp.cos(ang), jnp.sin(ang)
    zeros = jnp.zeros((seq, HEAD_DIM - ROT_DIM), F32)
    z8 = jnp.zeros((seq, half), F32)
    cos_h = jnp.concatenate([cos, cos, zeros + 1.0], axis=1)
    slo_h = jnp.concatenate([-sin, z8, zeros], axis=1)
    shi_h = jnp.concatenate([z8, sin, zeros], axis=1)
    two = lambda t: jnp.concatenate([t, t], axis=1)
    return two(cos_h), two(slo_h), two(shi_h)


def kernel(x, p, norm_mix, even_w_in, fox_b_f, gmlp_ln_g, gmlp_ln_b, gmlp_w_s, gmlp_b_s, fox_q_norm, fox_k_norm, even_w_out, odd_w_in, swa_q_norm, swa_k_norm, swa_sinks, conv_w, conv_ln_g, conv_ln_b, odd_w_out, norm_ffn, moe_w_coarse, moe_b_coarse, moe_w_fine, moe_b_fine, moe_w_gate, moe_w_up, moe_w_down, ple_w_proj, ple_norm, ple_w_gate):
    B, S, D = x.shape
    T = B * S
    h = x.reshape(T, D)
    p = p.reshape(p.shape[0], T, D_PLE)
    row = lambda v: v.reshape(1, -1)

    def moe_args(i):
        wr, br = _router_weights(moe_w_coarse[i], moe_b_coarse[i], moe_w_fine[i], moe_b_fine[i])
        return (i, row(norm_ffn[i]), wr, br, moe_w_gate, moe_w_up, moe_w_down, p,
                ple_w_proj[i].astype(BF16), row(ple_norm[i]), ple_w_gate[i].astype(BF16))

    n_main = 2 * GMLP_WIDTH + 3 * FOX_WIDTH
    w_f = jnp.pad(jnp.tile(even_w_in[0][:, n_main:], (1, BIAS_PIECES)),
                  ((0, 0), (0, LANES - BIAS_PIECES * FOX_HEADS)))
    w_in = jnp.concatenate([even_w_in[0][:, :n_main], w_f], axis=1).astype(BF16)
    b_f = jnp.pad(jnp.tile(fox_b_f[0], BIAS_PIECES), (0, LANES - BIAS_PIECES * FOX_HEADS)).reshape(1, LANES)
    bs_full = jnp.repeat(gmlp_b_s[0].T, HEAD_DIM, axis=1)
    ya, q, k, v, xq, xk = _proj_even(
        h, row(norm_mix[0]), w_in, b_f, row(gmlp_ln_g[0]), row(gmlp_ln_b[0]), gmlp_w_s[0], bs_full,
        row(jnp.tile(fox_q_norm[0], FOX_HEADS)), row(jnp.tile(fox_k_norm[0], FOX_HEADS)), seq=S)
    yb = _fox_attention(q, xq, k, xk, v, seq=S)
    h = _moe_ple(h, ya, yb, even_w_out[0].astype(BF16), *moe_args(0))

    order = jnp.array([0, 4, 1, 5, 2, 6, 3, 7])
    cols = (order[:, None] * HEAD_DIM + jnp.arange(HEAD_DIM)[None, :]).reshape(-1)
    w_odd = jnp.concatenate([odd_w_in[0][:, :SWA_WIDTH][:, cols], odd_w_in[0][:, SWA_WIDTH:]], axis=1).astype(BF16)
    w_out_odd = jnp.concatenate([odd_w_out[0][:SWA_WIDTH][cols], odd_w_out[0][SWA_WIDTH:]], axis=0).astype(BF16)
    cos, slo, shi = _rope_tables(S)
    q, k, v, glu = _proj_odd(h, row(norm_mix[1]), w_odd, row(jnp.tile(swa_q_norm[0], 8)),
                             row(jnp.tile(swa_k_norm[0], 2)), cos, slo, shi, seq=S)
    yc = _swa_attention(swa_sinks[0], q, k, v, seq=S)
    yd = _conv_module(glu, conv_w[0], row(conv_ln_g[0]), row(conv_ln_b[0]), seq=S)
    h = _moe_ple(h, yc, yd, w_out_odd, *moe_args(1))
    return h.reshape(B, S, D)
```

```python
import functools

import jax
import jax.numpy as jnp
import numpy as np
from jax import lax
from jax.experimental import pallas as pl
from jax.experimental.pallas import tpu as pltpu

F32 = jnp.float32
BF16 = jnp.bfloat16
HIGHEST = lax.Precision.HIGHEST

D_MODEL = 1024
HEAD_DIM = 64
LANES = 128
GMLP_WIDTH = 512
CHUNK = 128
FOX_WIDTH = 512
FOX_HEADS = 8
SWA_WIDTH = 512
KV_WIDTH = 128
WINDOW = 128
CONV_CH = 512
CONV_WIDTH = 31
CONV_HALO = 32
ROPE_THETA = 500000.0
ROT_DIM = 16
N_GROUPS = 4
EXPERTS_PER_GROUP = 8
N_EXPERTS = 32
D_EXPERT = 256
D_PLE = 256
EPS = 1e-6
NEG = -1e30
LOG2E = 1.4426950408889634
BIAS_PIECES = 3

EXPERT_TILE = 512
PAD_BITS = 9
SUBLANES = 8
SUBLANE_BITS = 3
ROW_TILE = 1024
ROUTER_BLOCKS = 8
GATHER_TILE = 1024
DISPATCH_TILE = 1024
DMA_UNROLL = 16
ATTN_TILE = 256
FOX_Q_ROWS = 512
FOX_PAIRS_PER_STEP = 2
CONV_TILE = 512
VMEM_LIMIT = 56 * 1024 * 1024


def _cparams(n_axes=1, flags=None):
    return pltpu.CompilerParams(dimension_semantics=("arbitrary",) * n_axes,
                                vmem_limit_bytes=VMEM_LIMIT, flags=flags)


def _rms(x, gain):
    return x * lax.rsqrt(jnp.mean(x * x, axis=-1, keepdims=True) + EPS) * gain


def _layernorm(x, g, b):
    mu = jnp.mean(x, axis=-1, keepdims=True)
    xc = x - mu
    var = jnp.mean(xc * xc, axis=-1, keepdims=True)
    return xc * lax.rsqrt(var + EPS) * g + b


def _head_rms(z, gain):
    lo = _lane_lo((1, LANES))
    outs = []
    for j in range(z.shape[1] // LANES):
        zj = z[:, j * LANES:(j + 1) * LANES]
        sq = zj * zj
        sum_lo = jnp.sum(jnp.where(lo, sq, 0.0), axis=-1, keepdims=True)
        sum_hi = jnp.sum(jnp.where(lo, 0.0, sq), axis=-1, keepdims=True)
        ms = jnp.where(lo, sum_lo, sum_hi) * (1.0 / HEAD_DIM)
        outs.append(zj * lax.rsqrt(ms + EPS))
    zn = outs[0] if len(outs) == 1 else jnp.concatenate(outs, axis=1)
    return zn * gain


def _lane_lo(shape):
    return (lax.broadcasted_iota(jnp.int32, shape, len(shape) - 1) % LANES) < HEAD_DIM


def _proj_even_kernel(h_ref, g_ref, w_ref, bf_ref, lng_ref, lnb_ref, ws_ref, bs_ref, qg_ref, kg_ref,
                      pq_ref, pk_ref, oneq_ref, onek_ref,
                      ya_ref, q_ref, k_ref, v_ref, xq_ref, xk_ref, carry_ref, *, tm, tiles_per_seq):
    i = pl.program_id(0)
    a = _rms(h_ref[...], g_ref[...]).astype(BF16)

    q0 = 2 * GMLP_WIDTH
    za = jnp.dot(a, w_ref[:, 0:q0], preferred_element_type=F32)
    zq = jnp.dot(a, w_ref[:, q0:q0 + FOX_WIDTH], preferred_element_type=F32)
    zk = jnp.dot(a, w_ref[:, q0 + FOX_WIDTH:q0 + 2 * FOX_WIDTH], preferred_element_type=F32)
    zv = jnp.dot(a, w_ref[:, q0 + 2 * FOX_WIDTH:q0 + 3 * FOX_WIDTH], preferred_element_type=F32)
    zf = jnp.dot(a, w_ref[:, q0 + 3 * FOX_WIDTH:], preferred_element_type=F32) + bf_ref[...]

    q_ref[...] = (_head_rms(zq, qg_ref[...]) * (LOG2E * HEAD_DIM ** -0.5)).astype(BF16)
    k_ref[...] = _head_rms(zk, kg_ref[...]).astype(BF16)
    v_ref[...] = zv.astype(BF16)

    rr = lax.broadcasted_iota(jnp.int32, (CHUNK, CHUNK), 0)
    cc = lax.broadcasted_iota(jnp.int32, (CHUNK, CHUNK), 1)

    ls = jnp.minimum(zf, 0.0) - jnp.log(1.0 + jnp.exp(-jnp.abs(zf)))

    @pl.when(i % tiles_per_seq == 0)
    def _():
        carry_ref[...] = jnp.zeros_like(carry_ref)

    def split3(x):
        hi = x.astype(BF16)
        r1 = x - hi.astype(F32)
        mid = r1.astype(BF16)
        return hi, mid, (r1 - mid.astype(F32)).astype(BF16)

    tri = jnp.where(rr >= cc, 1.0, 0.0).astype(BF16)
    pieces = jnp.concatenate(split3(ls), axis=1)
    running = carry_ref[...]
    blocks = []
    for b in range(tm // CHUNK):
        d = jnp.dot(tri, pieces[b * CHUNK:(b + 1) * CHUNK, :], preferred_element_type=F32)
        blk = (d[:, :LANES] + d[:, LANES:2 * LANES]) + d[:, 2 * LANES:] + running
        running = blk[CHUNK - 1:CHUNK, :]
        blocks.append(blk)
    carry_ref[...] = running
    c = jnp.concatenate(blocks, axis=0)

    za = jax.nn.gelu(za)
    u = za[:, :GMLP_WIDTH]
    vln = _layernorm(za[:, GMLP_WIDTH:], lng_ref[...], lnb_ref[...]).astype(BF16)
    lo = _lane_lo((CHUNK, LANES))
    for j in range(GMLP_WIDTH // LANES):
        w_a = jnp.where(rr >= cc, ws_ref[2 * j], 0.0).astype(BF16)
        w_b = jnp.where(rr >= cc, ws_ref[2 * j + 1], 0.0).astype(BF16)
        cols = slice(j * LANES, (j + 1) * LANES)
        for blk in range(tm // CHUNK):
            rows = slice(blk * CHUNK, (blk + 1) * CHUNK)
            vp = vln[rows, cols]
            mixed = jnp.where(lo, jnp.dot(w_a, vp, preferred_element_type=F32),
                              jnp.dot(w_b, vp, preferred_element_type=F32)) + bs_ref[:, cols]
            ya_ref[rows, cols] = (u[rows, cols] * mixed).astype(BF16)

    hi, mid, low = split3(c * LOG2E)
    group = lax.broadcasted_iota(jnp.int32, (tm, LANES), 1) // FOX_HEADS
    sel = jnp.where(group == 0, hi, jnp.where(group == 1, mid, low))
    xq_ref[...] = (jnp.dot(sel, pq_ref[...], preferred_element_type=F32) + oneq_ref[...]).astype(BF16)
    xk_ref[...] = (jnp.dot(sel, pk_ref[...], preferred_element_type=F32) + onek_ref[...]).astype(BF16)


def _proj_even(h, g, w, bf, lng, lnb, ws, bs_full, qg, kg, *, seq):
    T = h.shape[0]
    tm = min(ROW_TILE, seq)
    n_in = w.shape[1]
    const = lambda *shape: pl.BlockSpec(shape, lambda i: (0,) * len(shape))
    row = lambda width: pl.BlockSpec((tm, width), lambda i: (i, 0))
    tps = seq // tm
    return pl.pallas_call(
        functools.partial(_proj_even_kernel, tm=tm, tiles_per_seq=tps),
        grid=(T // tm,),
        in_specs=[row(D_MODEL), const(1, D_MODEL), const(D_MODEL, n_in), const(1, LANES),
                  const(1, GMLP_WIDTH), const(1, GMLP_WIDTH), const(8, CHUNK, CHUNK),
                  const(CHUNK, GMLP_WIDTH), const(1, FOX_WIDTH), const(1, FOX_WIDTH),
                  const(LANES, FOX_WIDTH), const(LANES, FOX_WIDTH),
                  const(1, FOX_WIDTH), const(1, FOX_WIDTH)],
        out_specs=[row(GMLP_WIDTH)] + [row(FOX_WIDTH)] * 5,
        out_shape=[jax.ShapeDtypeStruct((T, GMLP_WIDTH), BF16)]
                  + [jax.ShapeDtypeStruct((T, FOX_WIDTH), BF16)] * 5,
        scratch_shapes=[pltpu.VMEM((1, LANES), F32)],
        compiler_params=_cparams(1),
        name="proj_even",
    )(h, g, w, bf, lng, lnb, ws, bs_full, qg, kg, *_bias_placement())


def _bias_placement():
    pq = np.zeros((LANES, FOX_WIDTH), np.float32)
    pk = np.zeros((LANES, FOX_WIDTH), np.float32)
    oneq = np.zeros((1, FOX_WIDTH), np.float32)
    onek = np.zeros((1, FOX_WIDTH), np.float32)
    for head in range(FOX_HEADS):
        base = (head // 2) * LANES + (HEAD_DIM if head % 2 == 0 else 0)
        for piece in range(BIAS_PIECES):
            pq[piece * FOX_HEADS + head, base + piece] = 1.0
            onek[0, base + piece] = 1.0
            pk[piece * FOX_HEADS + head, base + BIAS_PIECES + piece] = -1.0
            oneq[0, base + BIAS_PIECES + piece] = 1.0
    return (jnp.asarray(pq, BF16), jnp.asarray(pk, BF16), jnp.asarray(oneq), jnp.asarray(onek))


def _fox_tile(i, q_ref, xq_ref, k_ref, xk_ref, v_ref, o_ref, *, tq, tk):
    part_rows = tk // 2
    lo = _lane_lo((1, LANES))
    rr = lax.broadcasted_iota(jnp.int32, (part_rows, tk), 0)
    cc = lax.broadcasted_iota(jnp.int32, (part_rows, tk), 1)
    nt = (((1,), (1,)), ((), ()))
    pairs = [slice(pp * LANES, (pp + 1) * LANES) for pp in range(FOX_PAIRS_PER_STEP)]
    first_row = [i * tq + part * part_rows for part in range(tq // part_rows)]
    n_tiles = [row // tk + 1 for row in first_row]
    diag_offset = [row % tk for row in first_row]
    chains = [(pp, part, head) for pp in range(len(pairs)) for part in range(len(first_row)) for head in range(2)]
    q_aug = {}
    for pp, cols in enumerate(pairs):
        for part in range(len(first_row)):
            rows = slice(part * part_rows, (part + 1) * part_rows)
            q, xq = q_ref[rows, cols], xq_ref[rows, cols]
            q_aug[pp, part, 0], q_aug[pp, part, 1] = jnp.where(lo, q, xq), jnp.where(lo, xq, q)

    def score_products(j):
        rows = slice(j * tk, (j + 1) * tk)
        k_aug = {}
        for pp, cols in enumerate(pairs):
            ks, xk = k_ref[rows, cols], xk_ref[rows, cols]
            k_aug[pp, 0], k_aug[pp, 1] = jnp.where(lo, ks, xk), jnp.where(lo, xk, ks)
        return {(pp, part, head): lax.dot_general(q_aug[pp, part, head], k_aug[pp, head], nt,
                                                  preferred_element_type=F32)
                for pp, part, head in chains if j < n_tiles[part]}

    maxes = {c: jnp.full((part_rows, 1), NEG, F32) for c in chains}
    accs = {c: jnp.zeros((part_rows, LANES), F32) for c in chains}
    scores = score_products(0)
    for j in range(max(n_tiles)):
        next_scores = score_products(j + 1) if j + 1 < max(n_tiles) else {}
        v_aug = {}
        for pp, cols in enumerate(pairs):
            vs = v_ref[j * tk:(j + 1) * tk, cols]
            one = jnp.ones_like(vs)
            v_aug[pp, 0], v_aug[pp, 1] = jnp.where(lo, vs, one), jnp.where(lo, one, vs)
        for c, s in scores.items():
            pp, part, head = c
            if j == n_tiles[part] - 1:
                s = jnp.where(cc <= rr + diag_offset[part], s, NEG)
            n = jnp.maximum(maxes[c], jnp.max(s, axis=-1, keepdims=True))
            p = jnp.exp2(s - n).astype(BF16)
            accs[c] = accs[c] * jnp.exp2(maxes[c] - n) + jnp.dot(p, v_aug[pp, head], preferred_element_type=F32)
            maxes[c] = n
        scores = next_scores
    for pp, cols in enumerate(pairs):
        for part in range(len(first_row)):
            norm = [accs[pp, part, head] / pltpu.roll(accs[pp, part, head], HEAD_DIM, 1) for head in range(2)]
            o_ref[part * part_rows:(part + 1) * part_rows, cols] = jnp.where(lo, norm[0], norm[1]).astype(BF16)


def _fox_kernel(q_ref, xq_ref, k_ref, xk_ref, v_ref, o_ref, *, tq, tk, nq):
    i = pl.program_id(2)
    for c in range(nq):
        pl.when(i == c)(functools.partial(_fox_tile, c, q_ref, xq_ref, k_ref, xk_ref, v_ref, o_ref, tq=tq, tk=tk))


def _fox_attention(q, xq, k, xk, v, *, seq):
    T = q.shape[0]
    B = T // seq
    tk = min(ATTN_TILE, seq)
    tq = min(FOX_Q_ROWS, seq)
    nq = seq // tq
    width = FOX_PAIRS_PER_STEP * LANES
    tile = pl.BlockSpec((tq, width), lambda b, hp, i: (b * nq + i, hp))
    whole = pl.BlockSpec((seq, width), lambda b, hp, i: (b, hp))
    return pl.pallas_call(
        functools.partial(_fox_kernel, tq=tq, tk=tk, nq=nq),
        grid=(B, FOX_WIDTH // width, nq),
        in_specs=[tile, tile, whole, whole, whole],
        out_specs=tile,
        out_shape=jax.ShapeDtypeStruct((T, FOX_WIDTH), BF16),
        compiler_params=_cparams(3),
        name="fox_attention",
    )(q, xq, k, xk, v)


def _outproj_router_kernel(h_ref, ya_ref, yb_ref, wo_ref, g_ref, wr_ref, br_ref,
                           h1_ref, m_ref, route_ref, cnt_ref, carry_ref, *, tm):
    i = pl.program_id(0)
    half = wo_ref.shape[0] // 2
    nb = ROUTER_BLOCKS
    rb = tm // nb
    blocks = [slice(b * rb, (b + 1) * rb) for b in range(nb)]

    @pl.when(i == 0)
    def _():
        carry_ref[...] = jnp.zeros_like(carry_ref)

    mixes = [jnp.dot(ya_ref[rows, :], wo_ref[0:half, :], preferred_element_type=F32)
             + jnp.dot(yb_ref[rows, :], wo_ref[half:, :], preferred_element_type=F32) for rows in blocks]
    ms = []
    for rows, mix in zip(blocks, mixes):
        h1 = h_ref[rows, :] + mix
        h1_ref[rows, :] = h1
        m = _rms(h1, g_ref[...])
        m_ref[rows, :] = m
        ms.append(m)

    logits = []
    for m in ms:
        m_hi = m.astype(BF16)
        m_lo = (m - m_hi.astype(F32)).astype(BF16)
        hh = jnp.dot(m_hi, wr_ref[...], preferred_element_type=F32)
        lh = jnp.dot(m_lo, wr_ref[:, :LANES], preferred_element_type=F32)
        logits.append(hh[:, :LANES] + (hh[:, LANES:] + lh) + br_ref[...])

    lane_i = lax.broadcasted_iota(jnp.int32, (rb, LANES), 1)
    lane = lane_i.astype(F32)
    group_of_lane = (lane_i // EXPERTS_PER_GROUP).astype(F32)
    is_coarse = (lane_i >= N_EXPERTS) & (lane_i < N_EXPERTS + N_GROUPS)
    picks = []
    for lg in logits:
        coarse = jnp.where(is_coarse, lg, NEG)
        cmax = jnp.max(coarse, axis=-1, keepdims=True)
        gidx = jnp.min(jnp.where(coarse == cmax, lane - N_EXPERTS, float(LANES)), axis=-1, keepdims=True)
        p_g = 1.0 / jnp.sum(jnp.where(is_coarse, jnp.exp(coarse - cmax), 0.0), axis=-1, keepdims=True)
        in_group = (lane_i < N_EXPERTS) & (group_of_lane == gidx)
        fine = jnp.where(in_group, lg, NEG)
        v1 = jnp.max(fine, axis=-1, keepdims=True)
        i1 = jnp.min(jnp.where(fine == v1, lane, float(LANES)), axis=-1, keepdims=True)
        fine2 = jnp.where(lane == i1, NEG, fine)
        v2 = jnp.max(fine2, axis=-1, keepdims=True)
        i2 = jnp.min(jnp.where(fine2 == v2, lane, float(LANES)), axis=-1, keepdims=True)
        e2 = jnp.exp(v2 - v1)
        picks.append((i1, i2, p_g / (1.0 + e2), p_g * e2 / (1.0 + e2)))

    tr = lax.broadcasted_iota(jnp.int32, (rb, rb), 0)
    tc = lax.broadcasted_iota(jnp.int32, (rb, rb), 1)
    strict = jnp.where(tr > tc, 1.0, 0.0).astype(BF16)
    onehots = [jnp.where((lane == i1) | (lane == i2), 1.0, 0.0).astype(F32) for i1, i2, _, _ in picks]
    befores = [jnp.dot(strict, oh.astype(BF16), preferred_element_type=F32) for oh in onehots]
    total = carry_ref[...]
    for rows, (i1, i2, w1, w2), oh, before in zip(blocks, picks, onehots, befores):
        before = before + total
        r1 = jnp.sum(jnp.where(lane == i1, before, 0.0), axis=-1, keepdims=True)
        r2 = jnp.sum(jnp.where(lane == i2, before, 0.0), axis=-1, keepdims=True)
        total = total + jnp.sum(oh, axis=0, keepdims=True)
        route = jnp.where(lane == 0, i1, 0.0)
        route = jnp.where(lane == 1, i2, route)
        route = jnp.where(lane == 2, r1, route)
        route = jnp.where(lane == 3, r2, route)
        route = jnp.where(lane == 4, w1, route)
        route = jnp.where(lane == 5, w2, route)
        route_ref[rows, :] = route
    carry_ref[...] = total
    cnt_ref[...] = jnp.broadcast_to(total, cnt_ref.shape)


def _outproj_router(h, ya, yb, wo, g, wr, br):
    T = h.shape[0]
    tm = min(ROW_TILE, T)
    const = lambda *shape: pl.BlockSpec(shape, lambda i: (0,) * len(shape))
    row = lambda width: pl.BlockSpec((tm, width), lambda i: (i, 0))
    return pl.pallas_call(
        functools.partial(_outproj_router_kernel, tm=tm),
        grid=(T // tm,),
        in_specs=[row(D_MODEL), row(ya.shape[1]), row(yb.shape[1]), const(*wo.shape),
                  const(1, D_MODEL), const(D_MODEL, 2 * LANES), const(1, LANES)],
        out_specs=[row(D_MODEL), row(D_MODEL), row(LANES), const(8, LANES)],
        out_shape=[jax.ShapeDtypeStruct((T, D_MODEL), F32),
                   jax.ShapeDtypeStruct((T, D_MODEL), F32),
                   jax.ShapeDtypeStruct((T, LANES), F32),
                   jax.ShapeDtypeStruct((8, LANES), F32)],
        scratch_shapes=[pltpu.VMEM((1, LANES), F32)],
        compiler_params=_cparams(1),
        name="outproj_router",
    )(h, ya, yb, wo, g, wr, br)


def _dispatch_kernel(pad_ref, pos_ref, m_ref, xs_hbm, zeros_ref, sem, pad_sem, *, tile):
    @pl.when(pl.program_id(0) == 0)
    def _():
        zeros_ref[...] = jnp.zeros_like(zeros_ref)

        def pad_copies(e, wait):
            first, n_single, n_block = pad_ref[0, e], pad_ref[1, e], pad_ref[2, e]
            for r in range(SUBLANES - 1):
                copy = pltpu.make_async_copy(zeros_ref.at[pl.ds(0, 1)], xs_hbm.at[pl.ds(first + r, 1)], pad_sem)
                pl.when(r < n_single)(copy.wait if wait else copy.start)
            done = first + n_single
            for bit in reversed(range(SUBLANE_BITS, PAD_BITS)):
                size = 1 << bit
                taken = (n_block & size) != 0
                copy = pltpu.make_async_copy(zeros_ref.at[pl.ds(0, size)],
                                             xs_hbm.at[pl.ds(pl.multiple_of(done, SUBLANES), size)], pad_sem)
                pl.when(taken)(copy.wait if wait else copy.start)
                done = done + jnp.where(taken, size, 0)

        def start(e, carry):
            pad_copies(e, False)
            return carry

        def finish(e, carry):
            pad_copies(e, True)
            return carry

        def tile_copy(j):
            return pltpu.make_async_copy(
                zeros_ref, xs_hbm.at[pl.ds(pl.multiple_of(j * EXPERT_TILE, EXPERT_TILE), EXPERT_TILE)], pad_sem)

        def start_tile(j, carry):
            tile_copy(j).start()
            return carry

        def finish_tile(j, carry):
            tile_copy(j).wait()
            return carry

        n_tiles = xs_hbm.shape[0] // EXPERT_TILE
        lax.fori_loop(0, N_EXPERTS, start, 0)
        lax.fori_loop(pad_ref[3, 0], n_tiles, start_tile, 0)
        lax.fori_loop(0, N_EXPERTS, finish, 0)
        lax.fori_loop(pad_ref[3, 0], n_tiles, finish_tile, 0)

    def issue(c, carry):
        base = pl.multiple_of(c * DMA_UNROLL, DMA_UNROLL)
        group = m_ref.at[pl.ds(base, DMA_UNROLL)]
        for u in range(DMA_UNROLL):
            for k in range(2):
                dst = xs_hbm.at[pl.ds(pos_ref[0, 0, k * tile + base + u], 1)]
                pltpu.make_async_copy(group.at[pl.ds(u, 1)], dst, sem).start(priority=k)
        return carry

    lax.fori_loop(0, tile // DMA_UNROLL, issue, 0)
    for _ in range(2):
        pltpu.make_async_copy(m_ref, xs_hbm.at[pl.ds(0, tile)], sem).wait()


def _dispatch(pads, pos3, m, n_rows):
    T, width = m.shape
    tile = pos3.shape[2] // 2
    grid_spec = pltpu.PrefetchScalarGridSpec(
        num_scalar_prefetch=1,
        grid=(T // tile,),
        in_specs=[pl.BlockSpec((1, 1, 2 * tile), lambda i, pads: (i, 0, 0), memory_space=pltpu.SMEM),
                  pl.BlockSpec((tile, width), lambda i, pads: (i, 0))],
        out_specs=pl.BlockSpec(memory_space=pl.ANY),
        scratch_shapes=[pltpu.VMEM((EXPERT_TILE, width), m.dtype),
                        pltpu.SemaphoreType.DMA(()), pltpu.SemaphoreType.DMA(())],
    )
    return pl.pallas_call(
        functools.partial(_dispatch_kernel, tile=tile),
        grid_spec=grid_spec,
        out_shape=jax.ShapeDtypeStruct((n_rows, width), m.dtype),
        compiler_params=pltpu.CompilerParams(dimension_semantics=("arbitrary",),
                                             has_side_effects=True, vmem_limit_bytes=VMEM_LIMIT),
        name="moe_dispatch",
    )(pads, pos3, m)


def _experts_kernel(te_ref, nused_ref, xs_ref, wg_ref, wu_ref, wd_ref, ys_ref, wgu_b, wd_b):
    j = pl.program_id(0)
    prev = te_ref[jnp.maximum(j - 1, 0)]

    @pl.when((j == 0) | (te_ref[j] != prev))
    def _():
        wgu_b[:, :D_EXPERT] = wg_ref[...].astype(BF16)
        wgu_b[:, D_EXPERT:] = wu_ref[...].astype(BF16)
        wd_b[...] = wd_ref[...].astype(BF16)

    @pl.when(j < nused_ref[0])
    def _():
        gu = jnp.dot(xs_ref[...].astype(BF16), wgu_b[...], preferred_element_type=F32)
        g = gu[:, :D_EXPERT]
        act = g * jax.nn.sigmoid(g) * gu[:, D_EXPERT:]
        ys_ref[...] = jnp.dot(act.astype(BF16), wd_b[...], preferred_element_type=F32)

    @pl.when(j >= nused_ref[0])
    def _():
        ys_ref[...] = jnp.zeros_like(ys_ref)


def _experts(layer, tile_expert, n_used, xs, wg, wu, wd):
    n_rows = xs.shape[0]
    nt = n_rows // EXPERT_TILE
    grid_spec = pltpu.PrefetchScalarGridSpec(
        num_scalar_prefetch=2,
        grid=(nt,),
        in_specs=[pl.BlockSpec((EXPERT_TILE, D_MODEL), lambda j, te, nu: (jnp.minimum(j, nu[0] - 1), 0)),
                  pl.BlockSpec((None, None, D_MODEL, D_EXPERT), lambda j, te, nu: (layer, te[j], 0, 0)),
                  pl.BlockSpec((None, None, D_MODEL, D_EXPERT), lambda j, te, nu: (layer, te[j], 0, 0)),
                  pl.BlockSpec((None, None, D_EXPERT, D_MODEL), lambda j, te, nu: (layer, te[j], 0, 0))],
        out_specs=pl.BlockSpec((EXPERT_TILE, D_MODEL), lambda j, te, nu: (j, 0)),
        scratch_shapes=[pltpu.VMEM((D_MODEL, 2 * D_EXPERT), BF16),
                        pltpu.VMEM((D_EXPERT, D_MODEL), BF16)],
    )
    return pl.pallas_call(
        _experts_kernel,
        grid_spec=grid_spec,
        out_shape=jax.ShapeDtypeStruct((n_rows, D_MODEL), F32),
        compiler_params=_cparams(1),
        name="moe_experts",
    )(tile_expert, n_used, xs, wg, wu, wd)


def _combine_ple_kernel(pos_ref, next_pos_ref, route_ref, h1_ref, ys_hbm, p_ref, wp_ref, g_ref, wgate_ref,
                        o_ref, ybuf, sems, *, tile):
    i = pl.program_id(0)
    n = pl.num_programs(0)
    slot = i % 2

    def gather(table, s):
        def issue(c, carry):
            base = pl.multiple_of(c * DMA_UNROLL, DMA_UNROLL)
            for k in range(2):
                group = ybuf.at[s, k, pl.ds(base, DMA_UNROLL)]
                for u in range(DMA_UNROLL):
                    pltpu.make_async_copy(ys_hbm.at[pl.ds(table[0, 0, k * tile + base + u], 1)],
                                          group.at[pl.ds(u, 1)], sems.at[s]).start(priority=k)
            return carry

        lax.fori_loop(0, tile // DMA_UNROLL, issue, 0)

    @pl.when(i == 0)
    def _():
        gather(pos_ref, 0)

    @pl.when(i + 1 < n)
    def _():
        gather(next_pos_ref, 1 - slot)

    halves = [slice(0, tile // 2), slice(tile // 2, tile)]
    p_b = p_ref[...].astype(BF16)
    ples = [jnp.dot(p_b[rows], wp_ref[...], preferred_element_type=F32) for rows in halves]
    for k in range(2):
        pltpu.make_async_copy(ys_hbm.at[pl.ds(0, tile)], ybuf.at[slot, k], sems.at[slot]).wait()

    route = route_ref[...]
    h2s = [h1_ref[rows, :] + route[rows, 4:5] * ybuf[slot, 0, rows, :] + route[rows, 5:6] * ybuf[slot, 1, rows, :]
           for rows in halves]
    normed = [_rms(h2, g_ref[...]).astype(BF16) for h2 in h2s]
    gates = [jnp.dot(x, wgate_ref[...], preferred_element_type=F32) for x in normed]
    for rows, h2, gate, ple in zip(halves, h2s, gates, ples):
        o_ref[rows, :] = h2 + jax.nn.sigmoid(gate) * ple


def _combine_ple(layer, pos3, route, h1, ys, p, wp, g, wgate):
    T = h1.shape[0]
    n, _, width = pos3.shape
    tile = width // 2
    const = lambda *shape: pl.BlockSpec(shape, lambda i: (0,) * len(shape))
    row = lambda width: pl.BlockSpec((tile, width), lambda i: (i, 0))
    return pl.pallas_call(
        functools.partial(_combine_ple_kernel, tile=tile),
        grid=(n,),
        in_specs=[pl.BlockSpec((1, 1, 2 * tile), lambda i: (0, 0, 0), memory_space=pltpu.SMEM),
                  pl.BlockSpec((1, 1, 2 * tile), lambda i: (jnp.minimum(i + 1, n - 1), 0, 0),
                               memory_space=pltpu.SMEM),
                  row(LANES), row(D_MODEL), pl.BlockSpec(memory_space=pl.ANY),
                  pl.BlockSpec((None, tile, D_PLE), lambda i: (layer, i, 0)),
                  const(D_PLE, D_MODEL), const(1, D_MODEL), const(D_MODEL, D_MODEL)],
        out_specs=row(D_MODEL),
        out_shape=jax.ShapeDtypeStruct((T, D_MODEL), F32),
        scratch_shapes=[pltpu.VMEM((2, 2, tile, D_MODEL), F32), pltpu.SemaphoreType.DMA((2,))],
        compiler_params=_cparams(1),
        name="combine_ple",
    )(pos3, pos3, route, h1, ys, p, wp, g, wgate)


def _rope(z, cos, sin_lo, sin_hi):
    half = ROT_DIM // 2
    outs = []
    for j in range(z.shape[1] // LANES):
        zj = z[:, j * LANES:(j + 1) * LANES]
        outs.append(zj * cos + pltpu.roll(zj, LANES - half, 1) * sin_lo + pltpu.roll(zj, half, 1) * sin_hi)
    return outs[0] if len(outs) == 1 else jnp.concatenate(outs, axis=1)


def _proj_odd_kernel(h_ref, g_ref, w_ref, qg_ref, kg_ref, cos_ref, slo_ref, shi_ref,
                     q_ref, k_ref, v_ref, glu_ref):
    a = _rms(h_ref[...], g_ref[...]).astype(BF16)
    cos, slo, shi = cos_ref[...], slo_ref[...], shi_ref[...]
    v0 = SWA_WIDTH + KV_WIDTH
    d0 = v0 + KV_WIDTH
    zq = jnp.dot(a, w_ref[:, 0:SWA_WIDTH], preferred_element_type=F32)
    zk = jnp.dot(a, w_ref[:, SWA_WIDTH:v0], preferred_element_type=F32)
    zv = jnp.dot(a, w_ref[:, v0:d0], preferred_element_type=F32)
    zd = jnp.dot(a, w_ref[:, d0:d0 + 2 * CONV_CH], preferred_element_type=F32)
    q_ref[...] = (_rope(_head_rms(zq, qg_ref[...]), cos, slo, shi) * (HEAD_DIM ** -0.5)).astype(BF16)
    k_ref[...] = _rope(_head_rms(zk, kg_ref[...]), cos, slo, shi).astype(BF16)
    v_ref[...] = zv.astype(BF16)
    glu_ref[...] = zd[:, :CONV_CH] * jax.nn.sigmoid(zd[:, CONV_CH:])


def _proj_odd(h, g, w, qg, kg, cos, slo, shi, *, seq):
    T = h.shape[0]
    tm = min(ROW_TILE, seq)
    tps = seq // tm
    const = lambda *shape: pl.BlockSpec(shape, lambda i: (0,) * len(shape))
    row = lambda width: pl.BlockSpec((tm, width), lambda i: (i, 0))
    tab = pl.BlockSpec((tm, LANES), lambda i: (i % tps, 0))
    return pl.pallas_call(
        _proj_odd_kernel,
        grid=(T // tm,),
        in_specs=[row(D_MODEL), const(1, D_MODEL), const(*w.shape), const(1, SWA_WIDTH),
                  const(1, KV_WIDTH), tab, tab, tab],
        out_specs=[row(SWA_WIDTH), row(KV_WIDTH), row(KV_WIDTH), row(CONV_CH)],
        out_shape=[jax.ShapeDtypeStruct((T, SWA_WIDTH), BF16),
                   jax.ShapeDtypeStruct((T, KV_WIDTH), BF16),
                   jax.ShapeDtypeStruct((T, KV_WIDTH), BF16),
                   jax.ShapeDtypeStruct((T, CONV_CH), F32)],
        compiler_params=_cparams(1),
        name="proj_odd",
    )(h, g, w, qg, kg, cos, slo, shi)


def _swa_kernel(sink_ref, q_ref, k_ref, v_ref, o_ref, *, seq):
    lo = _lane_lo((1, LANES))
    nt = (((1,), (1,)), ((), ()))
    W = WINDOW
    qi = lax.broadcasted_iota(jnp.int32, (W, 2 * W), 0)
    kj = lax.broadcasted_iota(jnp.int32, (W, 2 * W), 1)
    band = (kj > qi) & (kj <= qi + W)

    n_tiles = SWA_WIDTH // LANES
    heads = [head for j in range(n_tiles) for head in (j, n_tiles + j)]

    def window(n):
        start = max(n - 1, 0) * W
        return slice(start, start + 2 * W), (band if n > 0 else kj <= qi)

    def score_products(n):
        kwin, _ = window(n)
        ks = k_ref[kwin, :]
        scores = []
        for j in range(n_tiles):
            q = q_ref[n * W:(n + 1) * W, j * LANES:(j + 1) * LANES]
            zero = jnp.zeros_like(q)
            scores += [lax.dot_general(qh, ks, nt, preferred_element_type=F32)
                       for qh in (jnp.where(lo, q, zero), jnp.where(lo, zero, q))]
        return scores

    scores = score_products(0)
    for n in range(seq // W):
        next_scores = score_products(n + 1) if (n + 1) * W < seq else None
        kwin, mask = window(n)
        vs = v_ref[kwin, :]
        probs, sums = [], []
        for head, s in zip(heads, scores):
            s = jnp.where(mask, s, NEG)
            sink = sink_ref[head]
            m = jnp.maximum(jnp.max(s, axis=-1, keepdims=True), sink)
            p = jnp.exp(s - m)
            sums.append(jnp.sum(p, axis=-1, keepdims=True) + jnp.exp(sink - m))
            probs.append(p.astype(BF16))
        outs = [jnp.dot(p, vs, preferred_element_type=F32) / l for p, l in zip(probs, sums)]
        for j in range(n_tiles):
            o_ref[n * W:(n + 1) * W, j * LANES:(j + 1) * LANES] = (
                jnp.where(lo, outs[2 * j], outs[2 * j + 1]).astype(BF16))
        scores = next_scores


def _swa_attention(sinks, q, k, v, *, seq):
    T = q.shape[0]
    B = T // seq
    return pl.pallas_call(
        functools.partial(_swa_kernel, seq=seq),
        grid=(B,),
        in_specs=[pl.BlockSpec(memory_space=pltpu.SMEM),
                  pl.BlockSpec((seq, SWA_WIDTH), lambda b: (b, 0)),
                  pl.BlockSpec((seq, KV_WIDTH), lambda b: (b, 0)),
                  pl.BlockSpec((seq, KV_WIDTH), lambda b: (b, 0))],
        out_specs=pl.BlockSpec((seq, SWA_WIDTH), lambda b: (b, 0)),
        out_shape=jax.ShapeDtypeStruct((T, SWA_WIDTH), BF16),
        compiler_params=_cparams(1),
        name="swa_attention",
    )(sinks, q, k, v)


def _conv_kernel(prev_ref, cur_ref, w_ref, g_ref, b_ref, o_ref, shift_ref, *, tile, sub):
    r = pl.program_id(1)
    rows = CONV_HALO + tile
    tail = prev_ref[tile - CONV_HALO:, :]
    shift_ref[0, 0:CONV_HALO, :] = jnp.where(r > 0, tail, jnp.zeros_like(tail))
    shift_ref[0, CONV_HALO:rows, :] = cur_ref[...]
    shift_ref[0, rows:rows + SUBLANES, :] = jnp.zeros((SUBLANES, CONV_CH), F32)
    for o in range(1, SUBLANES):
        shift_ref[o, 0:rows, :] = shift_ref[0, o:o + rows, :]
    w = w_ref[...]
    first = CONV_HALO - (CONV_WIDTH - 1)
    for s in range(tile // sub):
        acc = jnp.zeros((sub, CONV_CH), F32)
        for j in range(CONV_WIDTH):
            start = s * sub + first + j
            o = start % SUBLANES
            acc = acc + shift_ref[o, start - o:start - o + sub, :] * w[j:j + 1, :]
        y = _layernorm(acc, g_ref[...], b_ref[...])
        o_ref[s * sub:(s + 1) * sub, :] = (y * jax.nn.sigmoid(y)).astype(BF16)


def _conv_module(glu, w, g, b, *, seq):
    T = glu.shape[0]
    B = T // seq
    tile = min(CONV_TILE, seq)
    nr = seq // tile
    const = lambda *shape: pl.BlockSpec(shape, lambda bb, r: (0,) * len(shape))
    return pl.pallas_call(
        functools.partial(_conv_kernel, tile=tile, sub=64),
        grid=(B, nr),
        in_specs=[pl.BlockSpec((tile, CONV_CH), lambda bb, r: (bb * nr + jnp.maximum(r - 1, 0), 0)),
                  pl.BlockSpec((tile, CONV_CH), lambda bb, r: (bb * nr + r, 0)),
                  const(CONV_WIDTH, CONV_CH), const(1, CONV_CH), const(1, CONV_CH)],
        out_specs=pl.BlockSpec((tile, CONV_CH), lambda bb, r: (bb * nr + r, 0)),
        out_shape=jax.ShapeDtypeStruct((T, CONV_CH), BF16),
        scratch_shapes=[pltpu.VMEM((SUBLANES, CONV_HALO + tile + SUBLANES, CONV_CH), F32)],
        compiler_params=_cparams(2),
        name="conv_module",
    )(glu, glu, w, g, b)


def _slot_tables_kernel(route_ref, off_ref, disp_ref, comb_ref, *, tm, tc):
    rt = route_ref[...].T
    slots = []
    for k in range(2):
        expert = rt[k:k + 1, :]
        first = jnp.zeros_like(expert)
        for e in range(N_EXPERTS):
            first = jnp.where(expert == float(e), off_ref[0:1, e:e + 1], first)
        slots.append((first + rt[2 + k:3 + k, :]).astype(jnp.int32))
    for k in range(2):
        disp_ref[0, :, k * tm:(k + 1) * tm] = slots[k]
        for b in range(tm // tc):
            comb_ref[b, :, k * tc:(k + 1) * tc] = slots[k][:, b * tc:(b + 1) * tc]


def _slot_tables(route, offset, *, tm, tc):
    T = route.shape[0]
    off = jnp.pad(offset.astype(F32), (0, LANES - N_EXPERTS)).reshape(1, LANES)
    return pl.pallas_call(
        functools.partial(_slot_tables_kernel, tm=tm, tc=tc),
        grid=(T // tm,),
        in_specs=[pl.BlockSpec((tm, LANES), lambda i: (i, 0)), pl.BlockSpec((1, LANES), lambda i: (0, 0))],
        out_specs=[pl.BlockSpec((1, 1, 2 * tm), lambda i: (i, 0, 0)),
                   pl.BlockSpec((tm // tc, 1, 2 * tc), lambda i: (i, 0, 0))],
        out_shape=[jax.ShapeDtypeStruct((T // tm, 1, 2 * tm), jnp.int32),
                   jax.ShapeDtypeStruct((T // tc, 1, 2 * tc), jnp.int32)],
        compiler_params=_cparams(1),
        name="moe_slot_tables",
    )(route, off)


def _routing_tables(counts, n_tiles):
    cnt = counts[0, :N_EXPERTS].astype(jnp.int32)
    tiles = (cnt + EXPERT_TILE - 1) // EXPERT_TILE
    tile_end = jnp.cumsum(tiles)
    offset = (tile_end - tiles) * EXPERT_TILE
    n_used = tile_end[-1]
    tile_id = jnp.minimum(jnp.arange(n_tiles, dtype=jnp.int32), n_used - 1)
    tile_expert = jnp.sum((tile_end[None, :] <= tile_id[:, None]).astype(jnp.int32), axis=1)
    first_pad = offset + cnt
    n_single = (-first_pad) % SUBLANES
    pads = jnp.stack([first_pad, n_single, tiles * EXPERT_TILE - cnt - n_single,
                      jnp.broadcast_to(tile_end[-1], cnt.shape)])
    return offset, pads, tile_expert, n_used.reshape(1).astype(jnp.int32)


def _moe_ple(h, ya, yb, wo, layer, norm_ffn, wr, br, wg, wu, wd, p, wp, ple_norm, wgate):
    T = h.shape[0]
    n_tiles = (2 * T) // EXPERT_TILE + N_EXPERTS
    h1, m, route, counts = _outproj_router(h, ya, yb, wo, norm_ffn, wr, br)
    offset, pads, tile_expert, n_used = _routing_tables(counts, n_tiles)
    disp_slots, comb_slots = _slot_tables(route, offset, tm=min(DISPATCH_TILE, T), tc=min(GATHER_TILE, T))
    xs = _dispatch(pads, disp_slots, m, n_tiles * EXPERT_TILE)
    ys = _experts(layer, tile_expert, n_used, xs, wg, wu, wd)
    return _combine_ple(layer, comb_slots, route, h1, ys, p, wp, ple_norm, wgate)


def _router_weights(w_coarse, b_coarse, w_fine, b_fine):
    wf = w_fine.transpose(1, 0, 2).reshape(D_MODEL, N_EXPERTS)
    wr = jnp.concatenate([wf, w_coarse, jnp.zeros((D_MODEL, LANES - N_EXPERTS - N_GROUPS), F32)], axis=1)
    br = jnp.concatenate([b_fine.reshape(-1), b_coarse, jnp.zeros((LANES - N_EXPERTS - N_GROUPS,), F32)])
    w_hi = wr.astype(BF16)
    w_lo = (wr - w_hi.astype(F32)).astype(BF16)
    return jnp.concatenate([w_hi, w_lo], axis=1), br.reshape(1, LANES)


def _rope_tables(seq):
    half = ROT_DIM // 2
    inv_freq = ROPE_THETA ** (-jnp.arange(half, dtype=F32) * 2.0 / ROT_DIM)
    ang = jnp.arange(seq, dtype=F32)[:, None] * inv_freq[None, :]
    cos, sin = jnp.cos(ang), jnp.sin(ang)
    zeros = jnp.zeros((seq, HEAD_DIM - ROT_DIM), F32)
    z8 = jnp.zeros((seq, half), F32)
    cos_h = jnp.concatenate([cos, cos, zeros + 1.0], axis=1)
    slo_h = jnp.concatenate([-sin, z8, zeros], axis=1)
    shi_h = jnp.concatenate([z8, sin, zeros], axis=1)
    two = lambda t: jnp.concatenate([t, t], axis=1)
    return two(cos_h), two(slo_h), two(shi_h)


def kernel(x, p, norm_mix, even_w_in, fox_b_f, gmlp_ln_g, gmlp_ln_b, gmlp_w_s, gmlp_b_s, fox_q_norm, fox_k_norm, even_w_out, odd_w_in, swa_q_norm, swa_k_norm, swa_sinks, conv_w, conv_ln_g, conv_ln_b, odd_w_out, norm_ffn, moe_w_coarse, moe_b_coarse, moe_w_fine, moe_b_fine, moe_w_gate, moe_w_up, moe_w_down, ple_w_proj, ple_norm, ple_w_gate):
    B, S, D = x.shape
    T = B * S
    h = x.reshape(T, D)
    p = p.reshape(p.shape[0], T, D_PLE)
    row = lambda v: v.reshape(1, -1)

    def moe_args(i):
        wr, br = _router_weights(moe_w_coarse[i], moe_b_coarse[i], moe_w_fine[i], moe_b_fine[i])
        return (i, row(norm_ffn[i]), wr, br, moe_w_gate, moe_w_up, moe_w_down, p,
                ple_w_proj[i].astype(BF16), row(ple_norm[i]), ple_w_gate[i].astype(BF16))

    n_main = 2 * GMLP_WIDTH + 3 * FOX_WIDTH
    w_f = jnp.pad(jnp.tile(even_w_in[0][:, n_main:], (1, BIAS_PIECES)),
                  ((0, 0), (0, LANES - BIAS_PIECES * FOX_HEADS)))
    w_in = jnp.concatenate([even_w_in[0][:, :n_main], w_f], axis=1).astype(BF16)
    b_f = jnp.pad(jnp.tile(fox_b_f[0], BIAS_PIECES), (0, LANES - BIAS_PIECES * FOX_HEADS)).reshape(1, LANES)
    bs_full = jnp.repeat(gmlp_b_s[0].T, HEAD_DIM, axis=1)
    ya, q, k, v, xq, xk = _proj_even(
        h, row(norm_mix[0]), w_in, b_f, row(gmlp_ln_g[0]), row(gmlp_ln_b[0]), gmlp_w_s[0], bs_full,
        row(jnp.tile(fox_q_norm[0], FOX_HEADS)), row(jnp.tile(fox_k_norm[0], FOX_HEADS)), seq=S)
    yb = _fox_attention(q, xq, k, xk, v, seq=S)
    h = _moe_ple(h, ya, yb, even_w_out[0].astype(BF16), *moe_args(0))

    order = jnp.array([0, 4, 1, 5, 2, 6, 3, 7])
    cols = (order[:, None] * HEAD_DIM + jnp.arange(HEAD_DIM)[None, :]).reshape(-1)
    w_odd = jnp.concatenate([odd_w_in[0][:, :SWA_WIDTH][:, cols], odd_w_in[0][:, SWA_WIDTH:]], axis=1).astype(BF16)
    w_out_odd = jnp.concatenate([odd_w_out[0][:SWA_WIDTH][cols], odd_w_out[0][SWA_WIDTH:]], axis=0).astype(BF16)
    cos, slo, shi = _rope_tables(S)
    q, k, v, glu = _proj_odd(h, row(norm_mix[1]), w_odd, row(jnp.tile(swa_q_norm[0], 8)),
                             row(jnp.tile(swa_k_norm[0], 2)), cos, slo, shi, seq=S)
    yc = _swa_attention(swa_sinks[0], q, k, v, seq=S)
    yd = _conv_module(glu, conv_w[0], row(conv_ln_g[0]), row(conv_ln_b[0]), seq=S)
    h = _moe_ple(h, yc, yd, w_out_odd, *moe_args(1))
    return h.reshape(B, S, D)
```

```python
import functools

import jax
import jax.numpy as jnp
import numpy as np
from jax import lax
from jax.experimental import pallas as pl
from jax.experimental.pallas import tpu as pltpu

F32 = jnp.float32
BF16 = jnp.bfloat16

D_MODEL = 1024
HEAD_DIM = 64
LANES = 128
GMLP_WIDTH = 512
CHUNK = 128
FOX_WIDTH = 512
FOX_HEADS = 8
SWA_WIDTH = 512
KV_WIDTH = 128
WINDOW = 128
CONV_CH = 512
CONV_WIDTH = 31
CONV_HALO = 32
ROPE_THETA = 500000.0
ROT_DIM = 16
N_GROUPS = 4
EXPERTS_PER_GROUP = 8
N_EXPERTS = 32
D_EXPERT = 256
D_PLE = 256
EPS = 1e-6
NEG = -1e30
LOG2E = 1.4426950408889634
BIAS_PIECES = 3

EXPERT_TILE = 512
PAD_BITS = 9
SUBLANES = 8
SUBLANE_BITS = 3
ROW_TILE = 1024
ROUTER_BLOCKS = 8
GATHER_TILE = 512
COMBINE_BLOCKS = 4
DISPATCH_TILE = 1024
DMA_UNROLL = 16
ATTN_TILE = 256
FOX_Q_ROWS = 512
FOX_PAIRS_PER_STEP = 2
CONV_TILE = 512
VMEM_LIMIT = 56 * 1024 * 1024


def _cparams(n_axes=1):
    return pltpu.CompilerParams(dimension_semantics=("arbitrary",) * n_axes,
                                vmem_limit_bytes=VMEM_LIMIT)


def _rms(x, gain):
    return x * lax.rsqrt(jnp.mean(x * x, axis=-1, keepdims=True) + EPS) * gain


def _layernorm(x, g, b):
    mu = jnp.mean(x, axis=-1, keepdims=True)
    xc = x - mu
    var = jnp.mean(xc * xc, axis=-1, keepdims=True)
    return xc * lax.rsqrt(var + EPS) * g + b


def _head_rms(z, gain):
    lo = _lane_lo((1, LANES))
    outs = []
    for j in range(z.shape[1] // LANES):
        zj = z[:, j * LANES:(j + 1) * LANES]
        sq = zj * zj
        sum_lo = jnp.sum(jnp.where(lo, sq, 0.0), axis=-1, keepdims=True)
        sum_hi = jnp.sum(jnp.where(lo, 0.0, sq), axis=-1, keepdims=True)
        ms = jnp.where(lo, sum_lo, sum_hi) * (1.0 / HEAD_DIM)
        outs.append(zj * lax.rsqrt(ms + EPS))
    zn = outs[0] if len(outs) == 1 else jnp.concatenate(outs, axis=1)
    return zn * gain


def _lane_lo(shape):
    return (lax.broadcasted_iota(jnp.int32, shape, len(shape) - 1) % LANES) < HEAD_DIM


def _proj_even_kernel(h_ref, g_ref, w_ref, bf_ref, lng_ref, lnb_ref, ws_ref, bs_ref, qg_ref, kg_ref,
                      pq_ref, pk_ref, oneq_ref, onek_ref,
                      ya_ref, q_ref, k_ref, v_ref, xq_ref, xk_ref, carry_ref, *, tm, tiles_per_seq):
    i = pl.program_id(0)
    a = _rms(h_ref[...], g_ref[...]).astype(BF16)

    q0 = 2 * GMLP_WIDTH
    za = jnp.dot(a, w_ref[:, 0:q0], preferred_element_type=F32)
    zq = jnp.dot(a, w_ref[:, q0:q0 + FOX_WIDTH], preferred_element_type=F32)
    zk = jnp.dot(a, w_ref[:, q0 + FOX_WIDTH:q0 + 2 * FOX_WIDTH], preferred_element_type=F32)
    zv = jnp.dot(a, w_ref[:, q0 + 2 * FOX_WIDTH:q0 + 3 * FOX_WIDTH], preferred_element_type=F32)
    zf = jnp.dot(a, w_ref[:, q0 + 3 * FOX_WIDTH:], preferred_element_type=F32) + bf_ref[...]

    q_ref[...] = (_head_rms(zq, qg_ref[...]) * (LOG2E * HEAD_DIM ** -0.5)).astype(BF16)
    k_ref[...] = _head_rms(zk, kg_ref[...]).astype(BF16)
    v_ref[...] = zv.astype(BF16)

    rr = lax.broadcasted_iota(jnp.int32, (CHUNK, CHUNK), 0)
    cc = lax.broadcasted_iota(jnp.int32, (CHUNK, CHUNK), 1)

    ls = jnp.minimum(zf, 0.0) - jnp.log(1.0 + jnp.exp(-jnp.abs(zf)))

    @pl.when(i % tiles_per_seq == 0)
    def _():
        carry_ref[...] = jnp.zeros_like(carry_ref)

    def split3(x):
        hi = x.astype(BF16)
        r1 = x - hi.astype(F32)
        mid = r1.astype(BF16)
        return hi, mid, (r1 - mid.astype(F32)).astype(BF16)

    tri = jnp.where(rr >= cc, 1.0, 0.0).astype(BF16)
    pieces = jnp.concatenate(split3(ls), axis=1)
    running = carry_ref[...]
    blocks = []
    for b in range(tm // CHUNK):
        d = jnp.dot(tri, pieces[b * CHUNK:(b + 1) * CHUNK, :], preferred_element_type=F32)
        blk = (d[:, :LANES] + d[:, LANES:2 * LANES]) + d[:, 2 * LANES:] + running
        running = blk[CHUNK - 1:CHUNK, :]
        blocks.append(blk)
    carry_ref[...] = running
    c = jnp.concatenate(blocks, axis=0)

    za = jax.nn.gelu(za)
    u = za[:, :GMLP_WIDTH]
    vln = _layernorm(za[:, GMLP_WIDTH:], lng_ref[...], lnb_ref[...]).astype(BF16)
    lo = _lane_lo((CHUNK, LANES))
    for j in range(GMLP_WIDTH // LANES):
        w_a = jnp.where(rr >= cc, ws_ref[2 * j], 0.0).astype(BF16)
        w_b = jnp.where(rr >= cc, ws_ref[2 * j + 1], 0.0).astype(BF16)
        cols = slice(j * LANES, (j + 1) * LANES)
        for blk in range(tm // CHUNK):
            rows = slice(blk * CHUNK, (blk + 1) * CHUNK)
            vp = vln[rows, cols]
            mixed = jnp.where(lo, jnp.dot(w_a, vp, preferred_element_type=F32),
                              jnp.dot(w_b, vp, preferred_element_type=F32)) + bs_ref[:, cols]
            ya_ref[rows, cols] = (u[rows, cols] * mixed).astype(BF16)

    hi, mid, low = split3(c * LOG2E)
    group = lax.broadcasted_iota(jnp.int32, (tm, LANES), 1) // FOX_HEADS
    sel = jnp.where(group == 0, hi, jnp.where(group == 1, mid, low))
    xq_ref[...] = (jnp.dot(sel, pq_ref[...], preferred_element_type=F32) + oneq_ref[...]).astype(BF16)
    xk_ref[...] = (jnp.dot(sel, pk_ref[...], preferred_element_type=F32) + onek_ref[...]).astype(BF16)


def _proj_even(h, g, w, bf, lng, lnb, ws, bs_full, qg, kg, *, seq):
    T = h.shape[0]
    tm = min(ROW_TILE, seq)
    n_in = w.shape[1]
    const = lambda *shape: pl.BlockSpec(shape, lambda i: (0,) * len(shape))
    row = lambda width: pl.BlockSpec((tm, width), lambda i: (i, 0))
    tps = seq // tm
    return pl.pallas_call(
        functools.partial(_proj_even_kernel, tm=tm, tiles_per_seq=tps),
        grid=(T // tm,),
        in_specs=[row(D_MODEL), const(1, D_MODEL), const(D_MODEL, n_in), const(1, LANES),
                  const(1, GMLP_WIDTH), const(1, GMLP_WIDTH), const(8, CHUNK, CHUNK),
                  const(CHUNK, GMLP_WIDTH), const(1, FOX_WIDTH), const(1, FOX_WIDTH),
                  const(LANES, FOX_WIDTH), const(LANES, FOX_WIDTH),
                  const(1, FOX_WIDTH), const(1, FOX_WIDTH)],
        out_specs=[row(GMLP_WIDTH)] + [row(FOX_WIDTH)] * 5,
        out_shape=[jax.ShapeDtypeStruct((T, GMLP_WIDTH), BF16)]
                  + [jax.ShapeDtypeStruct((T, FOX_WIDTH), BF16)] * 5,
        scratch_shapes=[pltpu.VMEM((1, LANES), F32)],
        compiler_params=_cparams(1),
        name="proj_even",
    )(h, g, w, bf, lng, lnb, ws, bs_full, qg, kg, *_bias_placement())


def _bias_placement():
    pq = np.zeros((LANES, FOX_WIDTH), np.float32)
    pk = np.zeros((LANES, FOX_WIDTH), np.float32)
    oneq = np.zeros((1, FOX_WIDTH), np.float32)
    onek = np.zeros((1, FOX_WIDTH), np.float32)
    for head in range(FOX_HEADS):
        base = (head // 2) * LANES + (HEAD_DIM if head % 2 == 0 else 0)
        for piece in range(BIAS_PIECES):
            pq[piece * FOX_HEADS + head, base + piece] = 1.0
            onek[0, base + piece] = 1.0
            pk[piece * FOX_HEADS + head, base + BIAS_PIECES + piece] = -1.0
            oneq[0, base + BIAS_PIECES + piece] = 1.0
    return (jnp.asarray(pq, BF16), jnp.asarray(pk, BF16), jnp.asarray(oneq), jnp.asarray(onek))


def _fox_tile(i, q_ref, xq_ref, k_ref, xk_ref, v_ref, o_ref, *, tq, tk):
    part_rows = tk // 2
    lo = _lane_lo((1, LANES))
    rr = lax.broadcasted_iota(jnp.int32, (part_rows, tk), 0)
    cc = lax.broadcasted_iota(jnp.int32, (part_rows, tk), 1)
    nt = (((1,), (1,)), ((), ()))
    pairs = [slice(pp * LANES, (pp + 1) * LANES) for pp in range(FOX_PAIRS_PER_STEP)]
    first_row = [i * tq + part * part_rows for part in range(tq // part_rows)]
    n_tiles = [row // tk + 1 for row in first_row]
    diag_offset = [row % tk for row in first_row]
    chains = [(pp, part, head) for pp in range(len(pairs)) for part in range(len(first_row)) for head in range(2)]
    q_aug = {}
    for pp, cols in enumerate(pairs):
        for part in range(len(first_row)):
            rows = slice(part * part_rows, (part + 1) * part_rows)
            q, xq = q_ref[rows, cols], xq_ref[rows, cols]
            q_aug[pp, part, 0], q_aug[pp, part, 1] = jnp.where(lo, q, xq), jnp.where(lo, xq, q)

    def score_products(j):
        rows = slice(j * tk, (j + 1) * tk)
        k_aug = {}
        for pp, cols in enumerate(pairs):
            ks, xk = k_ref[rows, cols], xk_ref[rows, cols]
            k_aug[pp, 0], k_aug[pp, 1] = jnp.where(lo, ks, xk), jnp.where(lo, xk, ks)
        return {(pp, part, head): lax.dot_general(q_aug[pp, part, head], k_aug[pp, head], nt,
                                                  preferred_element_type=F32)
                for pp, part, head in chains if j < n_tiles[part]}

    maxes = {c: jnp.full((part_rows, 1), NEG, F32) for c in chains}
    accs = {c: jnp.zeros((part_rows, LANES), F32) for c in chains}
    scores = score_products(0)
    for j in range(max(n_tiles)):
        next_scores = score_products(j + 1) if j + 1 < max(n_tiles) else {}
        v_aug = {}
        for pp, cols in enumerate(pairs):
            vs = v_ref[j * tk:(j + 1) * tk, cols]
            one = jnp.ones_like(vs)
            v_aug[pp, 0], v_aug[pp, 1] = jnp.where(lo, vs, one), jnp.where(lo, one, vs)
        for c, s in scores.items():
            pp, part, head = c
            if j == n_tiles[part] - 1:
                s = jnp.where(cc <= rr + diag_offset[part], s, NEG)
            n = jnp.maximum(maxes[c], jnp.max(s, axis=-1, keepdims=True))
            p = jnp.exp2(s - n).astype(BF16)
            accs[c] = accs[c] * jnp.exp2(maxes[c] - n) + jnp.dot(p, v_aug[pp, head], preferred_element_type=F32)
            maxes[c] = n
        scores = next_scores
    for pp, cols in enumerate(pairs):
        for part in range(len(first_row)):
            norm = [accs[pp, part, head] / pltpu.roll(accs[pp, part, head], HEAD_DIM, 1) for head in range(2)]
            o_ref[part * part_rows:(part + 1) * part_rows, cols] = jnp.where(lo, norm[0], norm[1]).astype(BF16)


def _fox_kernel(q_ref, xq_ref, k_ref, xk_ref, v_ref, o_ref, *, tq, tk, nq):
    i = pl.program_id(2)
    for c in range(nq):
        pl.when(i == c)(functools.partial(_fox_tile, c, q_ref, xq_ref, k_ref, xk_ref, v_ref, o_ref, tq=tq, tk=tk))


def _fox_attention(q, xq, k, xk, v, *, seq):
    T = q.shape[0]
    B = T // seq
    tk = min(ATTN_TILE, seq)
    tq = min(FOX_Q_ROWS, seq)
    nq = seq // tq
    width = FOX_PAIRS_PER_STEP * LANES
    tile = pl.BlockSpec((tq, width), lambda b, hp, i: (b * nq + i, hp))
    whole = pl.BlockSpec((seq, width), lambda b, hp, i: (b, hp))
    return pl.pallas_call(
        functools.partial(_fox_kernel, tq=tq, tk=tk, nq=nq),
        grid=(B, FOX_WIDTH // width, nq),
        in_specs=[tile, tile, whole, whole, whole],
        out_specs=tile,
        out_shape=jax.ShapeDtypeStruct((T, FOX_WIDTH), BF16),
        compiler_params=_cparams(3),
        name="fox_attention",
    )(q, xq, k, xk, v)


def _outproj_router_kernel(h_ref, ya_ref, yb_ref, wo_ref, g_ref, wr_ref, br_ref,
                           h1_ref, m_ref, route_ref, cnt_ref, carry_ref, *, tm):
    i = pl.program_id(0)
    half = wo_ref.shape[0] // 2
    nb = ROUTER_BLOCKS
    rb = tm // nb
    blocks = [slice(b * rb, (b + 1) * rb) for b in range(nb)]

    @pl.when(i == 0)
    def _():
        carry_ref[...] = jnp.zeros_like(carry_ref)

    mixes = [jnp.dot(ya_ref[rows, :], wo_ref[0:half, :], preferred_element_type=F32)
             + jnp.dot(yb_ref[rows, :], wo_ref[half:, :], preferred_element_type=F32) for rows in blocks]
    ms = []
    for rows, mix in zip(blocks, mixes):
        h1 = h_ref[rows, :] + mix
        h1_ref[rows, :] = h1
        m = _rms(h1, g_ref[...])
        m_ref[rows, :] = m
        ms.append(m)

    logits = []
    for m in ms:
        m_hi = m.astype(BF16)
        m_lo = (m - m_hi.astype(F32)).astype(BF16)
        hh = jnp.dot(m_hi, wr_ref[...], preferred_element_type=F32)
        lh = jnp.dot(m_lo, wr_ref[:, :LANES], preferred_element_type=F32)
        logits.append(hh[:, :LANES] + (hh[:, LANES:] + lh) + br_ref[...])

    lane_i = lax.broadcasted_iota(jnp.int32, (rb, LANES), 1)
    lane = lane_i.astype(F32)
    group_of_lane = (lane_i // EXPERTS_PER_GROUP).astype(F32)
    is_coarse = (lane_i >= N_EXPERTS) & (lane_i < N_EXPERTS + N_GROUPS)
    picks = []
    for lg in logits:
        coarse = jnp.where(is_coarse, lg, NEG)
        cmax = jnp.max(coarse, axis=-1, keepdims=True)
        gidx = jnp.min(jnp.where(coarse == cmax, lane - N_EXPERTS, float(LANES)), axis=-1, keepdims=True)
        p_g = 1.0 / jnp.sum(jnp.where(is_coarse, jnp.exp(coarse - cmax), 0.0), axis=-1, keepdims=True)
        in_group = (lane_i < N_EXPERTS) & (group_of_lane == gidx)
        fine = jnp.where(in_group, lg, NEG)
        v1 = jnp.max(fine, axis=-1, keepdims=True)
        i1 = jnp.min(jnp.where(fine == v1, lane, float(LANES)), axis=-1, keepdims=True)
        fine2 = jnp.where(lane == i1, NEG, fine)
        v2 = jnp.max(fine2, axis=-1, keepdims=True)
        i2 = jnp.min(jnp.where(fine2 == v2, lane, float(LANES)), axis=-1, keepdims=True)
        e2 = jnp.exp(v2 - v1)
        picks.append((i1, i2, p_g / (1.0 + e2), p_g * e2 / (1.0 + e2)))

    tr = lax.broadcasted_iota(jnp.int32, (rb, rb), 0)
    tc = lax.broadcasted_iota(jnp.int32, (rb, rb), 1)
    strict = jnp.where(tr > tc, 1.0, 0.0).astype(BF16)
    onehots = [jnp.where((lane == i1) | (lane == i2), 1.0, 0.0).astype(F32) for i1, i2, _, _ in picks]
    befores = [jnp.dot(strict, oh.astype(BF16), preferred_element_type=F32) for oh in onehots]
    total = carry_ref[...]
    for rows, (i1, i2, w1, w2), oh, before in zip(blocks, picks, onehots, befores):
        before = before + total
        r1 = jnp.sum(jnp.where(lane == i1, before, 0.0), axis=-1, keepdims=True)
        r2 = jnp.sum(jnp.where(lane == i2, before, 0.0), axis=-1, keepdims=True)
        total = total + jnp.sum(oh, axis=0, keepdims=True)
        route = jnp.where(lane == 0, i1, 0.0)
        route = jnp.where(lane == 1, i2, route)
        route = jnp.where(lane == 2, r1, route)
        route = jnp.where(lane == 3, r2, route)
        route = jnp.where(lane == 4, w1, route)
        route = jnp.where(lane == 5, w2, route)
        route_ref[rows, :] = route
    carry_ref[...] = total
    cnt_ref[...] = jnp.broadcast_to(total, cnt_ref.shape)


def _outproj_router(h, ya, yb, wo, g, wr, br):
    T = h.shape[0]
    tm = min(ROW_TILE, T)
    const = lambda *shape: pl.BlockSpec(shape, lambda i: (0,) * len(shape))
    row = lambda width: pl.BlockSpec((tm, width), lambda i: (i, 0))
    return pl.pallas_call(
        functools.partial(_outproj_router_kernel, tm=tm),
        grid=(T // tm,),
        in_specs=[row(D_MODEL), row(ya.shape[1]), row(yb.shape[1]), const(*wo.shape),
                  const(1, D_MODEL), const(D_MODEL, 2 * LANES), const(1, LANES)],
        out_specs=[row(D_MODEL), row(D_MODEL), row(LANES), const(8, LANES)],
        out_shape=[jax.ShapeDtypeStruct((T, D_MODEL), F32),
                   jax.ShapeDtypeStruct((T, D_MODEL), F32),
                   jax.ShapeDtypeStruct((T, LANES), F32),
                   jax.ShapeDtypeStruct((8, LANES), F32)],
        scratch_shapes=[pltpu.VMEM((1, LANES), F32)],
        compiler_params=_cparams(1),
        name="outproj_router",
    )(h, ya, yb, wo, g, wr, br)


def _dispatch_kernel(pad_ref, pos_ref, m_ref, xs_hbm, zeros_ref, sem, pad_sem, *, tile):
    @pl.when(pl.program_id(0) == 0)
    def _():
        zeros_ref[...] = jnp.zeros_like(zeros_ref)

        def pad_copies(e, wait):
            first, n_single, n_block = pad_ref[0, e], pad_ref[1, e], pad_ref[2, e]
            for r in range(SUBLANES - 1):
                copy = pltpu.make_async_copy(zeros_ref.at[pl.ds(0, 1)], xs_hbm.at[pl.ds(first + r, 1)], pad_sem)
                pl.when(r < n_single)(copy.wait if wait else copy.start)
            done = first + n_single
            for bit in reversed(range(SUBLANE_BITS, PAD_BITS)):
                size = 1 << bit
                taken = (n_block & size) != 0
                copy = pltpu.make_async_copy(zeros_ref.at[pl.ds(0, size)],
                                             xs_hbm.at[pl.ds(pl.multiple_of(done, SUBLANES), size)], pad_sem)
                pl.when(taken)(copy.wait if wait else copy.start)
                done = done + jnp.where(taken, size, 0)

        def start(e, carry):
            pad_copies(e, False)
            return carry

        def finish(e, carry):
            pad_copies(e, True)
            return carry

        def tile_copy(j):
            return pltpu.make_async_copy(
                zeros_ref, xs_hbm.at[pl.ds(pl.multiple_of(j * EXPERT_TILE, EXPERT_TILE), EXPERT_TILE)], pad_sem)

        def start_tile(j, carry):
            tile_copy(j).start()
            return carry

        def finish_tile(j, carry):
            tile_copy(j).wait()
            return carry

        n_tiles = xs_hbm.shape[0] // EXPERT_TILE
        lax.fori_loop(0, N_EXPERTS, start, 0)
        lax.fori_loop(pad_ref[3, 0], n_tiles, start_tile, 0)
        lax.fori_loop(0, N_EXPERTS, finish, 0)
        lax.fori_loop(pad_ref[3, 0], n_tiles, finish_tile, 0)

    def issue(c, carry):
        base = pl.multiple_of(c * DMA_UNROLL, DMA_UNROLL)
        group = m_ref.at[pl.ds(base, DMA_UNROLL)]
        for u in range(DMA_UNROLL):
            for k in range(2):
                dst = xs_hbm.at[pl.ds(pos_ref[0, 0, k * tile + base + u], 1)]
                pltpu.make_async_copy(group.at[pl.ds(u, 1)], dst, sem).start(priority=k)
        return carry

    lax.fori_loop(0, tile // DMA_UNROLL, issue, 0)
    for _ in range(2):
        pltpu.make_async_copy(m_ref, xs_hbm.at[pl.ds(0, tile)], sem).wait()


def _dispatch(pads, pos3, m, n_rows):
    T, width = m.shape
    tile = pos3.shape[2] // 2
    grid_spec = pltpu.PrefetchScalarGridSpec(
        num_scalar_prefetch=1,
        grid=(T // tile,),
        in_specs=[pl.BlockSpec((1, 1, 2 * tile), lambda i, pads: (i, 0, 0), memory_space=pltpu.SMEM),
                  pl.BlockSpec((tile, width), lambda i, pads: (i, 0))],
        out_specs=pl.BlockSpec(memory_space=pl.ANY),
        scratch_shapes=[pltpu.VMEM((EXPERT_TILE, width), m.dtype),
                        pltpu.SemaphoreType.DMA(()), pltpu.SemaphoreType.DMA(())],
    )
    return pl.pallas_call(
        functools.partial(_dispatch_kernel, tile=tile),
        grid_spec=grid_spec,
        out_shape=jax.ShapeDtypeStruct((n_rows, width), m.dtype),
        compiler_params=pltpu.CompilerParams(dimension_semantics=("arbitrary",),
                                             has_side_effects=True, vmem_limit_bytes=VMEM_LIMIT),
        name="moe_dispatch",
    )(pads, pos3, m)


def _experts_kernel(te_ref, nused_ref, xs_ref, wg_ref, wu_ref, wd_ref, ys_ref, wgu_b, wd_b):
    j = pl.program_id(0)
    prev = te_ref[jnp.maximum(j - 1, 0)]

    @pl.when((j == 0) | (te_ref[j] != prev))
    def _():
        wgu_b[:, :D_EXPERT] = wg_ref[...].astype(BF16)
        wgu_b[:, D_EXPERT:] = wu_ref[...].astype(BF16)
        wd_b[...] = wd_ref[...].astype(BF16)

    @pl.when(j < nused_ref[0])
    def _():
        gu = jnp.dot(xs_ref[...].astype(BF16), wgu_b[...], preferred_element_type=F32)
        g = gu[:, :D_EXPERT]
        act = g * jax.nn.sigmoid(g) * gu[:, D_EXPERT:]
        ys_ref[...] = jnp.dot(act.astype(BF16), wd_b[...], preferred_element_type=F32)

    @pl.when(j >= nused_ref[0])
    def _():
        ys_ref[...] = jnp.zeros_like(ys_ref)


def _experts(layer, tile_expert, n_used, xs, wg, wu, wd):
    n_rows = xs.shape[0]
    nt = n_rows // EXPERT_TILE
    grid_spec = pltpu.PrefetchScalarGridSpec(
        num_scalar_prefetch=2,
        grid=(nt,),
        in_specs=[pl.BlockSpec((EXPERT_TILE, D_MODEL), lambda j, te, nu: (jnp.minimum(j, nu[0] - 1), 0)),
                  pl.BlockSpec((None, None, D_MODEL, D_EXPERT), lambda j, te, nu: (layer, te[j], 0, 0)),
                  pl.BlockSpec((None, None, D_MODEL, D_EXPERT), lambda j, te, nu: (layer, te[j], 0, 0)),
                  pl.BlockSpec((None, None, D_EXPERT, D_MODEL), lambda j, te, nu: (layer, te[j], 0, 0))],
        out_specs=pl.BlockSpec((EXPERT_TILE, D_MODEL), lambda j, te, nu: (j, 0)),
        scratch_shapes=[pltpu.VMEM((D_MODEL, 2 * D_EXPERT), BF16),
                        pltpu.VMEM((D_EXPERT, D_MODEL), BF16)],
    )
    return pl.pallas_call(
        _experts_kernel,
        grid_spec=grid_spec,
        out_shape=jax.ShapeDtypeStruct((n_rows, D_MODEL), F32),
        compiler_params=_cparams(1),
        name="moe_experts",
    )(tile_expert, n_used, xs, wg, wu, wd)


def _combine_ple_kernel(pos_ref, next_pos_ref, route_ref, h1_ref, ys_hbm, p_ref, wp_ref, g_ref, wgate_ref,
                        o_ref, ybuf, sems, *, tile):
    i = pl.program_id(0)
    n = pl.num_programs(0)
    slot = i % 2

    def gather(table, s):
        def issue(c, carry):
            base = pl.multiple_of(c * DMA_UNROLL, DMA_UNROLL)
            for k in range(2):
                group = ybuf.at[s, k, pl.ds(base, DMA_UNROLL)]
                for u in range(DMA_UNROLL):
                    pltpu.make_async_copy(ys_hbm.at[pl.ds(table[0, 0, k * tile + base + u], 1)],
                                          group.at[pl.ds(u, 1)], sems.at[s]).start(priority=k)
            return carry

        lax.fori_loop(0, tile // DMA_UNROLL, issue, 0)

    @pl.when(i == 0)
    def _():
        gather(pos_ref, 0)

    @pl.when(i + 1 < n)
    def _():
        gather(next_pos_ref, 1 - slot)

    rb = tile // COMBINE_BLOCKS
    blocks = [slice(b * rb, (b + 1) * rb) for b in range(COMBINE_BLOCKS)]
    p_b = p_ref[...].astype(BF16)
    ples = [jnp.dot(p_b[rows], wp_ref[...], preferred_element_type=F32) for rows in blocks]
    for k in range(2):
        pltpu.make_async_copy(ys_hbm.at[pl.ds(0, tile)], ybuf.at[slot, k], sems.at[slot]).wait()

    route = route_ref[...]
    h2s = [h1_ref[rows, :] + route[rows, 4:5] * ybuf[slot, 0, rows, :] + route[rows, 5:6] * ybuf[slot, 1, rows, :]
           for rows in blocks]
    normed = [_rms(h2, g_ref[...]).astype(BF16) for h2 in h2s]
    gates = [jnp.dot(x, wgate_ref[...], preferred_element_type=F32) for x in normed]
    for rows, h2, gate, ple in zip(blocks, h2s, gates, ples):
        o_ref[rows, :] = h2 + jax.nn.sigmoid(gate) * ple


def _combine_ple(layer, pos3, route, h1, ys, p, wp, g, wgate):
    T = h1.shape[0]
    n, _, width = pos3.shape
    tile = width // 2
    const = lambda *shape: pl.BlockSpec(shape, lambda i: (0,) * len(shape))
    row = lambda width: pl.BlockSpec((tile, width), lambda i: (i, 0))
    return pl.pallas_call(
        functools.partial(_combine_ple_kernel, tile=tile),
        grid=(n,),
        in_specs=[pl.BlockSpec((1, 1, 2 * tile), lambda i: (0, 0, 0), memory_space=pltpu.SMEM),
                  pl.BlockSpec((1, 1, 2 * tile), lambda i: (jnp.minimum(i + 1, n - 1), 0, 0),
                               memory_space=pltpu.SMEM),
                  row(LANES), row(D_MODEL), pl.BlockSpec(memory_space=pl.ANY),
                  pl.BlockSpec((None, tile, D_PLE), lambda i: (layer, i, 0)),
                  const(D_PLE, D_MODEL), const(1, D_MODEL), const(D_MODEL, D_MODEL)],
        out_specs=row(D_MODEL),
        out_shape=jax.ShapeDtypeStruct((T, D_MODEL), F32),
        scratch_shapes=[pltpu.VMEM((2, 2, tile, D_MODEL), F32), pltpu.SemaphoreType.DMA((2,))],
        compiler_params=_cparams(1),
        name="combine_ple",
    )(pos3, pos3, route, h1, ys, p, wp, g, wgate)


def _rope(z, cos, sin_lo, sin_hi):
    half = ROT_DIM // 2
    outs = []
    for j in range(z.shape[1] // LANES):
        zj = z[:, j * LANES:(j + 1) * LANES]
        outs.append(zj * cos + pltpu.roll(zj, LANES - half, 1) * sin_lo + pltpu.roll(zj, half, 1) * sin_hi)
    return outs[0] if len(outs) == 1 else jnp.concatenate(outs, axis=1)


def _proj_odd_kernel(h_ref, g_ref, w_ref, qg_ref, kg_ref, cos_ref, slo_ref, shi_ref,
                     q_ref, k_ref, v_ref, glu_ref):
    a = _rms(h_ref[...], g_ref[...]).astype(BF16)
    cos, slo, shi = cos_ref[...], slo_ref[...], shi_ref[...]
    v0 = SWA_WIDTH + KV_WIDTH
    d0 = v0 + KV_WIDTH
    zq = jnp.dot(a, w_ref[:, 0:SWA_WIDTH], preferred_element_type=F32)
    zk = jnp.dot(a, w_ref[:, SWA_WIDTH:v0], preferred_element_type=F32)
    zv = jnp.dot(a, w_ref[:, v0:d0], preferred_element_type=F32)
    zd = jnp.dot(a, w_ref[:, d0:d0 + 2 * CONV_CH], preferred_element_type=F32)
    q_ref[...] = (_rope(_head_rms(zq, qg_ref[...]), cos, slo, shi) * (HEAD_DIM ** -0.5)).astype(BF16)
    k_ref[...] = _rope(_head_rms(zk, kg_ref[...]), cos, slo, shi).astype(BF16)
    v_ref[...] = zv.astype(BF16)
    glu_ref[...] = zd[:, :CONV_CH] * jax.nn.sigmoid(zd[:, CONV_CH:])


def _proj_odd(h, g, w, qg, kg, cos, slo, shi, *, seq):
    T = h.shape[0]
    tm = min(ROW_TILE, seq)
    tps = seq // tm
    const = lambda *shape: pl.BlockSpec(shape, lambda i: (0,) * len(shape))
    row = lambda width: pl.BlockSpec((tm, width), lambda i: (i, 0))
    tab = pl.BlockSpec((tm, LANES), lambda i: (i % tps, 0))
    return pl.pallas_call(
        _proj_odd_kernel,
        grid=(T // tm,),
        in_specs=[row(D_MODEL), const(1, D_MODEL), const(*w.shape), const(1, SWA_WIDTH),
                  const(1, KV_WIDTH), tab, tab, tab],
        out_specs=[row(SWA_WIDTH), row(KV_WIDTH), row(KV_WIDTH), row(CONV_CH)],
        out_shape=[jax.ShapeDtypeStruct((T, SWA_WIDTH), BF16),
                   jax.ShapeDtypeStruct((T, KV_WIDTH), BF16),
                   jax.ShapeDtypeStruct((T, KV_WIDTH), BF16),
                   jax.ShapeDtypeStruct((T, CONV_CH), F32)],
        compiler_params=_cparams(1),
        name="proj_odd",
    )(h, g, w, qg, kg, cos, slo, shi)


def _swa_kernel(sink_ref, q_ref, k_ref, v_ref, o_ref, *, seq):
    lo = _lane_lo((1, LANES))
    nt = (((1,), (1,)), ((), ()))
    W = WINDOW
    qi = lax.broadcasted_iota(jnp.int32, (W, 2 * W), 0)
    kj = lax.broadcasted_iota(jnp.int32, (W, 2 * W), 1)
    band = (kj > qi) & (kj <= qi + W)

    n_tiles = SWA_WIDTH // LANES
    heads = [head for j in range(n_tiles) for head in (j, n_tiles + j)]

    def window(n):
        start = max(n - 1, 0) * W
        return slice(start, start + 2 * W), (band if n > 0 else kj <= qi)

    def score_products(n):
        kwin, _ = window(n)
        ks = k_ref[kwin, :]
        scores = []
        for j in range(n_tiles):
            q = q_ref[n * W:(n + 1) * W, j * LANES:(j + 1) * LANES]
            zero = jnp.zeros_like(q)
            scores += [lax.dot_general(qh, ks, nt, preferred_element_type=F32)
                       for qh in (jnp.where(lo, q, zero), jnp.where(lo, zero, q))]
        return scores

    scores = score_products(0)
    for n in range(seq // W):
        next_scores = score_products(n + 1) if (n + 1) * W < seq else None
        kwin, mask = window(n)
        vs = v_ref[kwin, :]
        probs, sums = [], []
        for head, s in zip(heads, scores):
            s = jnp.where(mask, s, NEG)
            sink = sink_ref[head]
            m = jnp.maximum(jnp.max(s, axis=-1, keepdims=True), sink)
            p = jnp.exp(s - m)
            sums.append(jnp.sum(p, axis=-1, keepdims=True) + jnp.exp(sink - m))
            probs.append(p.astype(BF16))
        outs = [jnp.dot(p, vs, preferred_element_type=F32) / l for p, l in zip(probs, sums)]
        for j in range(n_tiles):
            o_ref[n * W:(n + 1) * W, j * LANES:(j + 1) * LANES] = (
                jnp.where(lo, outs[2 * j], outs[2 * j + 1]).astype(BF16))
        scores = next_scores


def _swa_attention(sinks, q, k, v, *, seq):
    T = q.shape[0]
    B = T // seq
    return pl.pallas_call(
        functools.partial(_swa_kernel, seq=seq),
        grid=(B,),
        in_specs=[pl.BlockSpec(memory_space=pltpu.SMEM),
                  pl.BlockSpec((seq, SWA_WIDTH), lambda b: (b, 0)),
                  pl.BlockSpec((seq, KV_WIDTH), lambda b: (b, 0)),
                  pl.BlockSpec((seq, KV_WIDTH), lambda b: (b, 0))],
        out_specs=pl.BlockSpec((seq, SWA_WIDTH), lambda b: (b, 0)),
        out_shape=jax.ShapeDtypeStruct((T, SWA_WIDTH), BF16),
        compiler_params=_cparams(1),
        name="swa_attention",
    )(sinks, q, k, v)


def _conv_kernel(prev_ref, cur_ref, w_ref, g_ref, b_ref, o_ref, shift_ref, *, tile, sub):
    r = pl.program_id(1)
    rows = CONV_HALO + tile
    tail = prev_ref[tile - CONV_HALO:, :]
    shift_ref[0, 0:CONV_HALO, :] = jnp.where(r > 0, tail, jnp.zeros_like(tail))
    shift_ref[0, CONV_HALO:rows, :] = cur_ref[...]
    shift_ref[0, rows:rows + SUBLANES, :] = jnp.zeros((SUBLANES, CONV_CH), F32)
    for o in range(1, SUBLANES):
        shift_ref[o, 0:rows, :] = shift_ref[0, o:o + rows, :]
    w = w_ref[...]
    first = CONV_HALO - (CONV_WIDTH - 1)
    for s in range(tile // sub):
        acc = jnp.zeros((sub, CONV_CH), F32)
        for j in range(CONV_WIDTH):
            start = s * sub + first + j
            o = start % SUBLANES
            acc = acc + shift_ref[o, start - o:start - o + sub, :] * w[j:j + 1, :]
        y = _layernorm(acc, g_ref[...], b_ref[...])
        o_ref[s * sub:(s + 1) * sub, :] = (y * jax.nn.sigmoid(y)).astype(BF16)


def _conv_module(glu, w, g, b, *, seq):
    T = glu.shape[0]
    B = T // seq
    tile = min(CONV_TILE, seq)
    nr = seq // tile
    const = lambda *shape: pl.BlockSpec(shape, lambda bb, r: (0,) * len(shape))
    return pl.pallas_call(
        functools.partial(_conv_kernel, tile=tile, sub=64),
        grid=(B, nr),
        in_specs=[pl.BlockSpec((tile, CONV_CH), lambda bb, r: (bb * nr + jnp.maximum(r - 1, 0), 0)),
                  pl.BlockSpec((tile, CONV_CH), lambda bb, r: (bb * nr + r, 0)),
                  const(CONV_WIDTH, CONV_CH), const(1, CONV_CH), const(1, CONV_CH)],
        out_specs=pl.BlockSpec((tile, CONV_CH), lambda bb, r: (bb * nr + r, 0)),
        out_shape=jax.ShapeDtypeStruct((T, CONV_CH), BF16),
        scratch_shapes=[pltpu.VMEM((SUBLANES, CONV_HALO + tile + SUBLANES, CONV_CH), F32)],
        compiler_params=_cparams(2),
        name="conv_module",
    )(glu, glu, w, g, b)


def _slot_tables_kernel(route_ref, off_ref, disp_ref, comb_ref, *, tm, tc):
    rt = route_ref[...].T
    slots = []
    for k in range(2):
        expert = rt[k:k + 1, :]
        first = jnp.zeros_like(expert)
        for e in range(N_EXPERTS):
            first = jnp.where(expert == float(e), off_ref[0:1, e:e + 1], first)
        slots.append((first + rt[2 + k:3 + k, :]).astype(jnp.int32))
    for k in range(2):
        disp_ref[0, :, k * tm:(k + 1) * tm] = slots[k]
        for b in range(tm // tc):
            comb_ref[b, :, k * tc:(k + 1) * tc] = slots[k][:, b * tc:(b + 1) * tc]


def _slot_tables(route, offset, *, tm, tc):
    T = route.shape[0]
    off = jnp.pad(offset.astype(F32), (0, LANES - N_EXPERTS)).reshape(1, LANES)
    return pl.pallas_call(
        functools.partial(_slot_tables_kernel, tm=tm, tc=tc),
        grid=(T // tm,),
        in_specs=[pl.BlockSpec((tm, LANES), lambda i: (i, 0)), pl.BlockSpec((1, LANES), lambda i: (0, 0))],
        out_specs=[pl.BlockSpec((1, 1, 2 * tm), lambda i: (i, 0, 0)),
                   pl.BlockSpec((tm // tc, 1, 2 * tc), lambda i: (i, 0, 0))],
        out_shape=[jax.ShapeDtypeStruct((T // tm, 1, 2 * tm), jnp.int32),
                   jax.ShapeDtypeStruct((T // tc, 1, 2 * tc), jnp.int32)],
        compiler_params=_cparams(1),
        name="moe_slot_tables",
    )(route, off)


def _routing_tables(counts, n_tiles):
    cnt = counts[0, :N_EXPERTS].astype(jnp.int32)
    tiles = (cnt + EXPERT_TILE - 1) // EXPERT_TILE
    tile_end = jnp.cumsum(tiles)
    offset = (tile_end - tiles) * EXPERT_TILE
    n_used = tile_end[-1]
    tile_id = jnp.minimum(jnp.arange(n_tiles, dtype=jnp.int32), n_used - 1)
    tile_expert = jnp.sum((tile_end[None, :] <= tile_id[:, None]).astype(jnp.int32), axis=1)
    first_pad = offset + cnt
    n_single = (-first_pad) % SUBLANES
    pads = jnp.stack([first_pad, n_single, tiles * EXPERT_TILE - cnt - n_single,
                      jnp.broadcast_to(tile_end[-1], cnt.shape)])
    return offset, pads, tile_expert, n_used.reshape(1).astype(jnp.int32)


def _moe_ple(h, ya, yb, wo, layer, norm_ffn, wr, br, wg, wu, wd, p, wp, ple_norm, wgate):
    T = h.shape[0]
    n_tiles = (2 * T) // EXPERT_TILE + N_EXPERTS
    h1, m, route, counts = _outproj_router(h, ya, yb, wo, norm_ffn, wr, br)
    offset, pads, tile_expert, n_used = _routing_tables(counts, n_tiles)
    disp_slots, comb_slots = _slot_tables(route, offset, tm=min(DISPATCH_TILE, T), tc=min(GATHER_TILE, T))
    xs = _dispatch(pads, disp_slots, m, n_tiles * EXPERT_TILE)
    ys = _experts(layer, tile_expert, n_used, xs, wg, wu, wd)
    return _combine_ple(layer, comb_slots, route, h1, ys, p, wp, ple_norm, wgate)


def _router_weights(w_coarse, b_coarse, w_fine, b_fine):
    wf = w_fine.transpose(1, 0, 2).reshape(D_MODEL, N_EXPERTS)
    wr = jnp.concatenate([wf, w_coarse, jnp.zeros((D_MODEL, LANES - N_EXPERTS - N_GROUPS), F32)], axis=1)
    br = jnp.concatenate([b_fine.reshape(-1), b_coarse, jnp.zeros((LANES - N_EXPERTS - N_GROUPS,), F32)])
    w_hi = wr.astype(BF16)
    w_lo = (wr - w_hi.astype(F32)).astype(BF16)
    return jnp.concatenate([w_hi, w_lo], axis=1), br.reshape(1, LANES)


def _rope_tables(seq):
    half = ROT_DIM // 2
    inv_freq = ROPE_THETA ** (-jnp.arange(half, dtype=F32) * 2.0 / ROT_DIM)
    ang = jnp.arange(seq, dtype=F32)[:, None] * inv_freq[None, :]
    cos, sin = jnp.cos(ang), jnp.sin(ang)
    zeros = jnp.zeros((seq, HEAD_DIM - ROT_DIM), F32)
    z8 = jnp.zeros((seq, half), F32)
    cos_h = jnp.concatenate([cos, cos, zeros + 1.0], axis=1)
    slo_h = jnp.concatenate([-sin, z8, zeros], axis=1)
    shi_h = jnp.concatenate([z8, sin, zeros], axis=1)
    two = lambda t: jnp.concatenate([t, t], axis=1)
    return two(cos_h), two(slo_h), two(shi_h)


def kernel(x, p, norm_mix, even_w_in, fox_b_f, gmlp_ln_g, gmlp_ln_b, gmlp_w_s, gmlp_b_s, fox_q_norm, fox_k_norm, even_w_out, odd_w_in, swa_q_norm, swa_k_norm, swa_sinks, conv_w, conv_ln_g, conv_ln_b, odd_w_out, norm_ffn, moe_w_coarse, moe_b_coarse, moe_w_fine, moe_b_fine, moe_w_gate, moe_w_up, moe_w_down, ple_w_proj, ple_norm, ple_w_gate):
    B, S, D = x.shape
    T = B * S
    h = x.reshape(T, D)
    p = p.reshape(p.shape[0], T, D_PLE)
    row = lambda v: v.reshape(1, -1)

    def moe_args(i):
        wr, br = _router_weights(moe_w_coarse[i], moe_b_coarse[i], moe_w_fine[i], moe_b_fine[i])
        return (i, row(norm_ffn[i]), wr, br, moe_w_gate, moe_w_up, moe_w_down, p,
                ple_w_proj[i].astype(BF16), row(ple_norm[i]), ple_w_gate[i].astype(BF16))

    n_main = 2 * GMLP_WIDTH + 3 * FOX_WIDTH
    w_f = jnp.pad(jnp.tile(even_w_in[0][:, n_main:], (1, BIAS_PIECES)),
                  ((0, 0), (0, LANES - BIAS_PIECES * FOX_HEADS)))
    w_in = jnp.concatenate([even_w_in[0][:, :n_main], w_f], axis=1).astype(BF16)
    b_f = jnp.pad(jnp.tile(fox_b_f[0], BIAS_PIECES), (0, LANES - BIAS_PIECES * FOX_HEADS)).reshape(1, LANES)
    bs_full = jnp.repeat(gmlp_b_s[0].T, HEAD_DIM, axis=1)
    ya, q, k, v, xq, xk = _proj_even(
        h, row(norm_mix[0]), w_in, b_f, row(gmlp_ln_g[0]), row(gmlp_ln_b[0]), gmlp_w_s[0], bs_full,
        row(jnp.tile(fox_q_norm[0], FOX_HEADS)), row(jnp.tile(fox_k_norm[0], FOX_HEADS)), seq=S)
    yb = _fox_attention(q, xq, k, xk, v, seq=S)
    h = _moe_ple(h, ya, yb, even_w_out[0].astype(BF16), *moe_args(0))

    order = jnp.array([0, 4, 1, 5, 2, 6, 3, 7])
    cols = (order[:, None] * HEAD_DIM + jnp.arange(HEAD_DIM)[None, :]).reshape(-1)
    w_odd = jnp.concatenate([odd_w_in[0][:, :SWA_WIDTH][:, cols], odd_w_in[0][:, SWA_WIDTH:]], axis=1).astype(BF16)
    w_out_odd = jnp.concatenate([odd_w_out[0][:SWA_WIDTH][cols], odd_w_out[0][SWA_WIDTH:]], axis=0).astype(BF16)
    cos, slo, shi = _rope_tables(S)
    q, k, v, glu = _proj_odd(h, row(norm_mix[1]), w_odd, row(jnp.tile(swa_q_norm[0], 8)),
                             row(jnp.tile(swa_k_norm[0], 2)), cos, slo, shi, seq=S)
    yc = _swa_attention(swa_sinks[0], q, k, v, seq=S)
    yd = _conv_module(glu, conv_w[0], row(conv_ln_g[0]), row(conv_ln_b[0]), seq=S)
    h = _moe_ple(h, yc, yd, w_out_odd, *moe_args(1))
    return h.reshape(B, S, D)
```

```python
import functools

import jax
import jax.numpy as jnp
import numpy as np
from jax import lax
from jax.experimental import pallas as pl
from jax.experimental.pallas import tpu as pltpu

F32 = jnp.float32
BF16 = jnp.bfloat16

D_MODEL = 1024
HEAD_DIM = 64
LANES = 128
GMLP_WIDTH = 512
CHUNK = 128
FOX_WIDTH = 512
FOX_HEADS = 8
SWA_WIDTH = 512
KV_WIDTH = 128
WINDOW = 128
CONV_CH = 512
CONV_WIDTH = 31
CONV_HALO = 32
ROPE_THETA = 500000.0
ROT_DIM = 16
N_GROUPS = 4
EXPERTS_PER_GROUP = 8
N_EXPERTS = 32
D_EXPERT = 256
D_PLE = 256
EPS = 1e-6
NEG = -1e30
LOG2E = 1.4426950408889634
BIAS_PIECES = 3

EXPERT_TILE = 512
PAD_BITS = 9
SUBLANES = 8
SUBLANE_BITS = 3
ROW_TILE = 1024
ROUTER_BLOCKS = 8
GATHER_TILE = 512
COMBINE_BLOCKS = 4
DISPATCH_TILE = 1024
DMA_UNROLL = 16
ATTN_TILE = 256
FOX_Q_ROWS = 512
FOX_PAIRS_PER_STEP = 2
CONV_TILE = 512
VMEM_LIMIT = 56 * 1024 * 1024


def _cparams(n_axes=1):
    return pltpu.CompilerParams(dimension_semantics=("arbitrary",) * n_axes,
                                vmem_limit_bytes=VMEM_LIMIT)


def _rms(x, gain):
    return x * lax.rsqrt(jnp.mean(x * x, axis=-1, keepdims=True) + EPS) * gain


def _layernorm(x, g, b):
    mu = jnp.mean(x, axis=-1, keepdims=True)
    xc = x - mu
    var = jnp.mean(xc * xc, axis=-1, keepdims=True)
    return xc * lax.rsqrt(var + EPS) * g + b


def _head_rms(z, gain):
    lo = _lane_lo((1, LANES))
    outs = []
    for j in range(z.shape[1] // LANES):
        zj = z[:, j * LANES:(j + 1) * LANES]
        sq = zj * zj
        sum_lo = jnp.sum(jnp.where(lo, sq, 0.0), axis=-1, keepdims=True)
        sum_hi = jnp.sum(jnp.where(lo, 0.0, sq), axis=-1, keepdims=True)
        ms = jnp.where(lo, sum_lo, sum_hi) * (1.0 / HEAD_DIM)
        outs.append(zj * lax.rsqrt(ms + EPS))
    zn = outs[0] if len(outs) == 1 else jnp.concatenate(outs, axis=1)
    return zn * gain


def _lane_lo(shape):
    return (lax.broadcasted_iota(jnp.int32, shape, len(shape) - 1) % LANES) < HEAD_DIM


def _proj_even_kernel(h_ref, g_ref, w_ref, bf_ref, lng_ref, lnb_ref, ws_ref, bs_ref, qg_ref, kg_ref,
                      pq_ref, pk_ref, oneq_ref, onek_ref,
                      ya_ref, q_ref, k_ref, v_ref, xq_ref, xk_ref, carry_ref, *, tm, tiles_per_seq):
    i = pl.program_id(0)
    a = _rms(h_ref[...], g_ref[...]).astype(BF16)

    q0 = 2 * GMLP_WIDTH
    za = jnp.dot(a, w_ref[:, 0:q0], preferred_element_type=F32)
    zq = jnp.dot(a, w_ref[:, q0:q0 + FOX_WIDTH], preferred_element_type=F32)
    zk = jnp.dot(a, w_ref[:, q0 + FOX_WIDTH:q0 + 2 * FOX_WIDTH], preferred_element_type=F32)
    zv = jnp.dot(a, w_ref[:, q0 + 2 * FOX_WIDTH:q0 + 3 * FOX_WIDTH], preferred_element_type=F32)
    zf = jnp.dot(a, w_ref[:, q0 + 3 * FOX_WIDTH:], preferred_element_type=F32) + bf_ref[...]

    q_ref[...] = (_head_rms(zq, qg_ref[...]) * (LOG2E * HEAD_DIM ** -0.5)).astype(BF16)
    k_ref[...] = _head_rms(zk, kg_ref[...]).astype(BF16)
    v_ref[...] = zv.astype(BF16)

    rr = lax.broadcasted_iota(jnp.int32, (CHUNK, CHUNK), 0)
    cc = lax.broadcasted_iota(jnp.int32, (CHUNK, CHUNK), 1)

    ls = jnp.minimum(zf, 0.0) - jnp.log(1.0 + jnp.exp(-jnp.abs(zf)))

    @pl.when(i % tiles_per_seq == 0)
    def _():
        carry_ref[...] = jnp.zeros_like(carry_ref)

    def split3(x):
        hi = x.astype(BF16)
        r1 = x - hi.astype(F32)
        mid = r1.astype(BF16)
        return hi, mid, (r1 - mid.astype(F32)).astype(BF16)

    tri = jnp.where(rr >= cc, 1.0, 0.0).astype(BF16)
    pieces = jnp.concatenate(split3(ls), axis=1)
    running = carry_ref[...]
    blocks = []
    for b in range(tm // CHUNK):
        d = jnp.dot(tri, pieces[b * CHUNK:(b + 1) * CHUNK, :], preferred_element_type=F32)
        blk = (d[:, :LANES] + d[:, LANES:2 * LANES]) + d[:, 2 * LANES:] + running
        running = blk[CHUNK - 1:CHUNK, :]
        blocks.append(blk)
    carry_ref[...] = running
    c = jnp.concatenate(blocks, axis=0)

    za = jax.nn.gelu(za)
    u = za[:, :GMLP_WIDTH]
    vln = _layernorm(za[:, GMLP_WIDTH:], lng_ref[...], lnb_ref[...]).astype(BF16)
    lo = _lane_lo((CHUNK, LANES))
    for j in range(GMLP_WIDTH // LANES):
        w_a = jnp.where(rr >= cc, ws_ref[2 * j], 0.0).astype(BF16)
        w_b = jnp.where(rr >= cc, ws_ref[2 * j + 1], 0.0).astype(BF16)
        cols = slice(j * LANES, (j + 1) * LANES)
        for blk in range(tm // CHUNK):
            rows = slice(blk * CHUNK, (blk + 1) * CHUNK)
            vp = vln[rows, cols]
            mixed = jnp.where(lo, jnp.dot(w_a, vp, preferred_element_type=F32),
                              jnp.dot(w_b, vp, preferred_element_type=F32)) + bs_ref[:, cols]
            ya_ref[rows, cols] = (u[rows, cols] * mixed).astype(BF16)

    hi, mid, low = split3(c * LOG2E)
    group = lax.broadcasted_iota(jnp.int32, (tm, LANES), 1) // FOX_HEADS
    sel = jnp.where(group == 0, hi, jnp.where(group == 1, mid, low))
    xq_ref[...] = (jnp.dot(sel, pq_ref[...], preferred_element_type=F32) + oneq_ref[...]).astype(BF16)
    xk_ref[...] = (jnp.dot(sel, pk_ref[...], preferred_element_type=F32) + onek_ref[...]).astype(BF16)


def _proj_even(h, g, w, bf, lng, lnb, ws, bs_full, qg, kg, *, seq):
    T = h.shape[0]
    tm = min(ROW_TILE, seq)
    n_in = w.shape[1]
    const = lambda *shape: pl.BlockSpec(shape, lambda i: (0,) * len(shape))
    row = lambda width: pl.BlockSpec((tm, width), lambda i: (i, 0))
    tps = seq // tm
    return pl.pallas_call(
        functools.partial(_proj_even_kernel, tm=tm, tiles_per_seq=tps),
        grid=(T // tm,),
        in_specs=[row(D_MODEL), const(1, D_MODEL), const(D_MODEL, n_in), const(1, LANES),
                  const(1, GMLP_WIDTH), const(1, GMLP_WIDTH), const(8, CHUNK, CHUNK),
                  const(CHUNK, GMLP_WIDTH), const(1, FOX_WIDTH), const(1, FOX_WIDTH),
                  const(LANES, FOX_WIDTH), const(LANES, FOX_WIDTH),
                  const(1, FOX_WIDTH), const(1, FOX_WIDTH)],
        out_specs=[row(GMLP_WIDTH)] + [row(FOX_WIDTH)] * 5,
        out_shape=[jax.ShapeDtypeStruct((T, GMLP_WIDTH), BF16)]
                  + [jax.ShapeDtypeStruct((T, FOX_WIDTH), BF16)] * 5,
        scratch_shapes=[pltpu.VMEM((1, LANES), F32)],
        compiler_params=_cparams(1),
        name="proj_even",
    )(h, g, w, bf, lng, lnb, ws, bs_full, qg, kg, *_bias_placement())


def _bias_placement():
    pq = np.zeros((LANES, FOX_WIDTH), np.float32)
    pk = np.zeros((LANES, FOX_WIDTH), np.float32)
    oneq = np.zeros((1, FOX_WIDTH), np.float32)
    onek = np.zeros((1, FOX_WIDTH), np.float32)
    for head in range(FOX_HEADS):
        base = (head // 2) * LANES + (HEAD_DIM if head % 2 == 0 else 0)
        for piece in range(BIAS_PIECES):
            pq[piece * FOX_HEADS + head, base + piece] = 1.0
            onek[0, base + piece] = 1.0
            pk[piece * FOX_HEADS + head, base + BIAS_PIECES + piece] = -1.0
            oneq[0, base + BIAS_PIECES + piece] = 1.0
    return (jnp.asarray(pq, BF16), jnp.asarray(pk, BF16), jnp.asarray(oneq), jnp.asarray(onek))


def _fox_tile(i, q_ref, xq_ref, k_ref, xk_ref, v_ref, o_ref, *, tq, tk):
    part_rows = tk // 2
    lo = _lane_lo((1, LANES))
    rr = lax.broadcasted_iota(jnp.int32, (part_rows, tk), 0)
    cc = lax.broadcasted_iota(jnp.int32, (part_rows, tk), 1)
    nt = (((1,), (1,)), ((), ()))
    pairs = [slice(pp * LANES, (pp + 1) * LANES) for pp in range(FOX_PAIRS_PER_STEP)]
    first_row = [i * tq + part * part_rows for part in range(tq // part_rows)]
    n_tiles = [row // tk + 1 for row in first_row]
    diag_offset = [row % tk for row in first_row]
    chains = [(pp, part, head) for pp in range(len(pairs)) for part in range(len(first_row)) for head in range(2)]
    q_aug = {}
    for pp, cols in enumerate(pairs):
        for part in range(len(first_row)):
            rows = slice(part * part_rows, (part + 1) * part_rows)
            q, xq = q_ref[rows, cols], xq_ref[rows, cols]
            q_aug[pp, part, 0], q_aug[pp, part, 1] = jnp.where(lo, q, xq), jnp.where(lo, xq, q)

    def score_products(j):
        rows = slice(j * tk, (j + 1) * tk)
        k_aug = {}
        for pp, cols in enumerate(pairs):
            ks, xk = k_ref[rows, cols], xk_ref[rows, cols]
            k_aug[pp, 0], k_aug[pp, 1] = jnp.where(lo, ks, xk), jnp.where(lo, xk, ks)
        return {(pp, part, head): lax.dot_general(q_aug[pp, part, head], k_aug[pp, head], nt,
                                                  preferred_element_type=F32)
                for pp, part, head in chains if j < n_tiles[part]}

    maxes = {c: jnp.full((part_rows, 1), NEG, F32) for c in chains}
    accs = {c: jnp.zeros((part_rows, LANES), F32) for c in chains}
    scores = score_products(0)
    for j in range(max(n_tiles)):
        next_scores = score_products(j + 1) if j + 1 < max(n_tiles) else {}
        v_aug = {}
        for pp, cols in enumerate(pairs):
            vs = v_ref[j * tk:(j + 1) * tk, cols]
            one = jnp.ones_like(vs)
            v_aug[pp, 0], v_aug[pp, 1] = jnp.where(lo, vs, one), jnp.where(lo, one, vs)
        for c, s in scores.items():
            pp, part, head = c
            if j == n_tiles[part] - 1:
                s = jnp.where(cc <= rr + diag_offset[part], s, NEG)
            n = jnp.maximum(maxes[c], jnp.max(s, axis=-1, keepdims=True))
            p = jnp.exp2(s - n).astype(BF16)
            accs[c] = accs[c] * jnp.exp2(maxes[c] - n) + jnp.dot(p, v_aug[pp, head], preferred_element_type=F32)
            maxes[c] = n
        scores = next_scores
    for pp, cols in enumerate(pairs):
        for part in range(len(first_row)):
            norm = [accs[pp, part, head] / pltpu.roll(accs[pp, part, head], HEAD_DIM, 1) for head in range(2)]
            o_ref[part * part_rows:(part + 1) * part_rows, cols] = jnp.where(lo, norm[0], norm[1]).astype(BF16)


def _fox_kernel(q_ref, xq_ref, k_ref, xk_ref, v_ref, o_ref, *, tq, tk, nq):
    i = pl.program_id(2)
    for c in range(nq):
        pl.when(i == c)(functools.partial(_fox_tile, c, q_ref, xq_ref, k_ref, xk_ref, v_ref, o_ref, tq=tq, tk=tk))


def _fox_attention(q, xq, k, xk, v, *, seq):
    T = q.shape[0]
    B = T // seq
    tk = min(ATTN_TILE, seq)
    tq = min(FOX_Q_ROWS, seq)
    nq = seq // tq
    width = FOX_PAIRS_PER_STEP * LANES
    tile = pl.BlockSpec((tq, width), lambda b, hp, i: (b * nq + i, hp))
    whole = pl.BlockSpec((seq, width), lambda b, hp, i: (b, hp))
    return pl.pallas_call(
        functools.partial(_fox_kernel, tq=tq, tk=tk, nq=nq),
        grid=(B, FOX_WIDTH // width, nq),
        in_specs=[tile, tile, whole, whole, whole],
        out_specs=tile,
        out_shape=jax.ShapeDtypeStruct((T, FOX_WIDTH), BF16),
        compiler_params=_cparams(3),
        name="fox_attention",
    )(q, xq, k, xk, v)


def _outproj_router_kernel(h_ref, ya_ref, yb_ref, wo_ref, g_ref, wr_ref, br_ref,
                           h1_ref, m_ref, route_ref, cnt_ref, carry_ref, *, tm):
    i = pl.program_id(0)
    half = wo_ref.shape[0] // 2
    nb = ROUTER_BLOCKS
    rb = tm // nb
    blocks = [slice(b * rb, (b + 1) * rb) for b in range(nb)]

    @pl.when(i == 0)
    def _():
        carry_ref[...] = jnp.zeros_like(carry_ref)

    mixes = [jnp.dot(ya_ref[rows, :], wo_ref[0:half, :], preferred_element_type=F32)
             + jnp.dot(yb_ref[rows, :], wo_ref[half:, :], preferred_element_type=F32) for rows in blocks]
    ms = []
    for rows, mix in zip(blocks, mixes):
        h1 = h_ref[rows, :] + mix
        h1_ref[rows, :] = h1
        m = _rms(h1, g_ref[...])
        m_ref[rows, :] = m
        ms.append(m)

    logits = []
    for m in ms:
        m_hi = m.astype(BF16)
        m_lo = (m - m_hi.astype(F32)).astype(BF16)
        hh = jnp.dot(m_hi, wr_ref[...], preferred_element_type=F32)
        lh = jnp.dot(m_lo, wr_ref[:, :LANES], preferred_element_type=F32)
        logits.append(hh[:, :LANES] + (hh[:, LANES:] + lh) + br_ref[...])

    lane_i = lax.broadcasted_iota(jnp.int32, (rb, LANES), 1)
    lane = lane_i.astype(F32)
    group_of_lane = (lane_i // EXPERTS_PER_GROUP).astype(F32)
    is_coarse = (lane_i >= N_EXPERTS) & (lane_i < N_EXPERTS + N_GROUPS)
    picks = []
    for lg in logits:
        coarse = jnp.where(is_coarse, lg, NEG)
        cmax = jnp.max(coarse, axis=-1, keepdims=True)
        gidx = jnp.min(jnp.where(coarse == cmax, lane - N_EXPERTS, float(LANES)), axis=-1, keepdims=True)
        p_g = 1.0 / jnp.sum(jnp.where(is_coarse, jnp.exp(coarse - cmax), 0.0), axis=-1, keepdims=True)
        in_group = (lane_i < N_EXPERTS) & (group_of_lane == gidx)
        fine = jnp.where(in_group, lg, NEG)
        v1 = jnp.max(fine, axis=-1, keepdims=True)
        i1 = jnp.min(jnp.where(fine == v1, lane, float(LANES)), axis=-1, keepdims=True)
        fine2 = jnp.where(lane == i1, NEG, fine)
        v2 = jnp.max(fine2, axis=-1, keepdims=True)
        i2 = jnp.min(jnp.where(fine2 == v2, lane, float(LANES)), axis=-1, keepdims=True)
        e2 = jnp.exp(v2 - v1)
        picks.append((i1, i2, p_g / (1.0 + e2), p_g * e2 / (1.0 + e2)))

    tr = lax.broadcasted_iota(jnp.int32, (rb, rb), 0)
    tc = lax.broadcasted_iota(jnp.int32, (rb, rb), 1)
    strict = jnp.where(tr > tc, 1.0, 0.0).astype(BF16)
    onehots = [jnp.where((lane == i1) | (lane == i2), 1.0, 0.0).astype(F32) for i1, i2, _, _ in picks]
    befores = [jnp.dot(strict, oh.astype(BF16), preferred_element_type=F32) for oh in onehots]
    total = carry_ref[...]
    for rows, (i1, i2, w1, w2), oh, before in zip(blocks, picks, onehots, befores):
        before = before + total
        r1 = jnp.sum(jnp.where(lane == i1, before, 0.0), axis=-1, keepdims=True)
        r2 = jnp.sum(jnp.where(lane == i2, before, 0.0), axis=-1, keepdims=True)
        total = total + jnp.sum(oh, axis=0, keepdims=True)
        route = jnp.where(lane == 0, i1, 0.0)
        route = jnp.where(lane == 1, i2, route)
        route = jnp.where(lane == 2, r1, route)
        route = jnp.where(lane == 3, r2, route)
        route = jnp.where(lane == 4, w1, route)
        route = jnp.where(lane == 5, w2, route)
        route_ref[rows, :] = route
    carry_ref[...] = total
    cnt_ref[...] = jnp.broadcast_to(total, cnt_ref.shape)


def _outproj_router(h, ya, yb, wo, g, wr, br):
    T = h.shape[0]
    tm = min(ROW_TILE, T)
    const = lambda *shape: pl.BlockSpec(shape, lambda i: (0,) * len(shape))
    row = lambda width: pl.BlockSpec((tm, width), lambda i: (i, 0))
    return pl.pallas_call(
        functools.partial(_outproj_router_kernel, tm=tm),
        grid=(T // tm,),
        in_specs=[row(D_MODEL), row(ya.shape[1]), row(yb.shape[1]), const(*wo.shape),
                  const(1, D_MODEL), const(D_MODEL, 2 * LANES), const(1, LANES)],
        out_specs=[row(D_MODEL), row(D_MODEL), row(LANES), const(8, LANES)],
        out_shape=[jax.ShapeDtypeStruct((T, D_MODEL), F32),
                   jax.ShapeDtypeStruct((T, D_MODEL), F32),
                   jax.ShapeDtypeStruct((T, LANES), F32),
                   jax.ShapeDtypeStruct((8, LANES), F32)],
        scratch_shapes=[pltpu.VMEM((1, LANES), F32)],
        compiler_params=_cparams(1),
        name="outproj_router",
    )(h, ya, yb, wo, g, wr, br)


def _dispatch_kernel(pad_ref, pos_ref, m_ref, xs_hbm, zeros_ref, sem, pad_sem, *, tile):
    i = pl.program_id(0)
    n_tiles = xs_hbm.shape[0] // EXPERT_TILE

    def pad_copies(e, wait):
        first, n_single, n_block = pad_ref[0, e], pad_ref[1, e], pad_ref[2, e]
        for r in range(SUBLANES - 1):
            copy = pltpu.make_async_copy(zeros_ref.at[pl.ds(0, 1)], xs_hbm.at[pl.ds(first + r, 1)], pad_sem)
            pl.when(r < n_single)(copy.wait if wait else copy.start)
        done = first + n_single
        for bit in reversed(range(SUBLANE_BITS, PAD_BITS)):
            size = 1 << bit
            taken = (n_block & size) != 0
            copy = pltpu.make_async_copy(zeros_ref.at[pl.ds(0, size)],
                                         xs_hbm.at[pl.ds(pl.multiple_of(done, SUBLANES), size)], pad_sem)
            pl.when(taken)(copy.wait if wait else copy.start)
            done = done + jnp.where(taken, size, 0)

    def tile_copy(j):
        return pltpu.make_async_copy(
            zeros_ref, xs_hbm.at[pl.ds(pl.multiple_of(j * EXPERT_TILE, EXPERT_TILE), EXPERT_TILE)], pad_sem)

    def zero_copies(wait):
        def per_expert(e, carry):
            pad_copies(e, wait)
            return carry

        def per_tile(j, carry):
            tile_copy(j).wait() if wait else tile_copy(j).start()
            return carry

        lax.fori_loop(0, N_EXPERTS, per_expert, 0)
        lax.fori_loop(pad_ref[3, 0], n_tiles, per_tile, 0)

    @pl.when(i == 0)
    def _():
        zeros_ref[...] = jnp.zeros_like(zeros_ref)
        zero_copies(False)

    def issue(c, carry):
        base = pl.multiple_of(c * DMA_UNROLL, DMA_UNROLL)
        group = m_ref.at[pl.ds(base, DMA_UNROLL)]
        for u in range(DMA_UNROLL):
            for k in range(2):
                dst = xs_hbm.at[pl.ds(pos_ref[0, 0, k * tile + base + u], 1)]
                pltpu.make_async_copy(group.at[pl.ds(u, 1)], dst, sem).start(priority=k)
        return carry

    lax.fori_loop(0, tile // DMA_UNROLL, issue, 0)
    for _ in range(2):
        pltpu.make_async_copy(m_ref, xs_hbm.at[pl.ds(0, tile)], sem).wait()

    @pl.when(i == pl.num_programs(0) - 1)
    def _():
        zero_copies(True)


def _dispatch(pads, pos3, m, n_rows):
    T, width = m.shape
    tile = pos3.shape[2] // 2
    grid_spec = pltpu.PrefetchScalarGridSpec(
        num_scalar_prefetch=1,
        grid=(T // tile,),
        in_specs=[pl.BlockSpec((1, 1, 2 * tile), lambda i, pads: (i, 0, 0), memory_space=pltpu.SMEM),
                  pl.BlockSpec((tile, width), lambda i, pads: (i, 0))],
        out_specs=pl.BlockSpec(memory_space=pl.ANY),
        scratch_shapes=[pltpu.VMEM((EXPERT_TILE, width), m.dtype),
                        pltpu.SemaphoreType.DMA(()), pltpu.SemaphoreType.DMA(())],
    )
    return pl.pallas_call(
        functools.partial(_dispatch_kernel, tile=tile),
        grid_spec=grid_spec,
        out_shape=jax.ShapeDtypeStruct((n_rows, width), m.dtype),
        compiler_params=pltpu.CompilerParams(dimension_semantics=("arbitrary",),
                                             has_side_effects=True, vmem_limit_bytes=VMEM_LIMIT),
        name="moe_dispatch",
    )(pads, pos3, m)


def _experts_kernel(te_ref, nused_ref, xs_ref, wg_ref, wu_ref, wd_ref, ys_ref, wgu_b, wd_b):
    j = pl.program_id(0)
    prev = te_ref[jnp.maximum(j - 1, 0)]

    @pl.when((j == 0) | (te_ref[j] != prev))
    def _():
        wgu_b[:, :D_EXPERT] = wg_ref[...].astype(BF16)
        wgu_b[:, D_EXPERT:] = wu_ref[...].astype(BF16)
        wd_b[...] = wd_ref[...].astype(BF16)

    @pl.when(j < nused_ref[0])
    def _():
        gu = jnp.dot(xs_ref[...].astype(BF16), wgu_b[...], preferred_element_type=F32)
        g = gu[:, :D_EXPERT]
        act = g * jax.nn.sigmoid(g) * gu[:, D_EXPERT:]
        ys_ref[...] = jnp.dot(act.astype(BF16), wd_b[...], preferred_element_type=F32)

    @pl.when(j >= nused_ref[0])
    def _():
        ys_ref[...] = jnp.zeros_like(ys_ref)


def _experts(layer, tile_expert, n_used, xs, wg, wu, wd):
    n_rows = xs.shape[0]
    nt = n_rows // EXPERT_TILE
    grid_spec = pltpu.PrefetchScalarGridSpec(
        num_scalar_prefetch=2,
        grid=(nt,),
        in_specs=[pl.BlockSpec((EXPERT_TILE, D_MODEL), lambda j, te, nu: (jnp.minimum(j, nu[0] - 1), 0)),
                  pl.BlockSpec((None, None, D_MODEL, D_EXPERT), lambda j, te, nu: (layer, te[j], 0, 0)),
                  pl.BlockSpec((None, None, D_MODEL, D_EXPERT), lambda j, te, nu: (layer, te[j], 0, 0)),
                  pl.BlockSpec((None, None, D_EXPERT, D_MODEL), lambda j, te, nu: (layer, te[j], 0, 0))],
        out_specs=pl.BlockSpec((EXPERT_TILE, D_MODEL), lambda j, te, nu: (j, 0)),
        scratch_shapes=[pltpu.VMEM((D_MODEL, 2 * D_EXPERT), BF16),
                        pltpu.VMEM((D_EXPERT, D_MODEL), BF16)],
    )
    return pl.pallas_call(
        _experts_kernel,
        grid_spec=grid_spec,
        out_shape=jax.ShapeDtypeStruct((n_rows, D_MODEL), F32),
        compiler_params=_cparams(1),
        name="moe_experts",
    )(tile_expert, n_used, xs, wg, wu, wd)


def _combine_ple_kernel(pos_ref, next_pos_ref, route_ref, h1_ref, ys_hbm, p_ref, wp_ref, g_ref, wgate_ref,
                        o_ref, ybuf, sems, *, tile):
    i = pl.program_id(0)
    n = pl.num_programs(0)
    slot = i % 2

    def gather(table, s):
        def issue(c, carry):
            base = pl.multiple_of(c * DMA_UNROLL, DMA_UNROLL)
            for k in range(2):
                group = ybuf.at[s, k, pl.ds(base, DMA_UNROLL)]
                for u in range(DMA_UNROLL):
                    pltpu.make_async_copy(ys_hbm.at[pl.ds(table[0, 0, k * tile + base + u], 1)],
                                          group.at[pl.ds(u, 1)], sems.at[s]).start(priority=k)
            return carry

        lax.fori_loop(0, tile // DMA_UNROLL, issue, 0)

    @pl.when(i == 0)
    def _():
        gather(pos_ref, 0)

    @pl.when(i + 1 < n)
    def _():
        gather(next_pos_ref, 1 - slot)

    rb = tile // COMBINE_BLOCKS
    blocks = [slice(b * rb, (b + 1) * rb) for b in range(COMBINE_BLOCKS)]
    p_b = p_ref[...].astype(BF16)
    ples = [jnp.dot(p_b[rows], wp_ref[...], preferred_element_type=F32) for rows in blocks]
    for k in range(2):
        pltpu.make_async_copy(ys_hbm.at[pl.ds(0, tile)], ybuf.at[slot, k], sems.at[slot]).wait()

    route = route_ref[...]
    h2s = [h1_ref[rows, :] + route[rows, 4:5] * ybuf[slot, 0, rows, :] + route[rows, 5:6] * ybuf[slot, 1, rows, :]
           for rows in blocks]
    normed = [_rms(h2, g_ref[...]).astype(BF16) for h2 in h2s]
    gates = [jnp.dot(x, wgate_ref[...], preferred_element_type=F32) for x in normed]
    for rows, h2, gate, ple in zip(blocks, h2s, gates, ples):
        o_ref[rows, :] = h2 + jax.nn.sigmoid(gate) * ple


def _combine_ple(layer, pos3, route, h1, ys, p, wp, g, wgate):
    T = h1.shape[0]
    n, _, width = pos3.shape
    tile = width // 2
    const = lambda *shape: pl.BlockSpec(shape, lambda i: (0,) * len(shape))
    row = lambda width: pl.BlockSpec((tile, width), lambda i: (i, 0))
    return pl.pallas_call(
        functools.partial(_combine_ple_kernel, tile=tile),
        grid=(n,),
        in_specs=[pl.BlockSpec((1, 1, 2 * tile), lambda i: (0, 0, 0), memory_space=pltpu.SMEM),
                  pl.BlockSpec((1, 1, 2 * tile), lambda i: (jnp.minimum(i + 1, n - 1), 0, 0),
                               memory_space=pltpu.SMEM),
                  row(LANES), row(D_MODEL), pl.BlockSpec(memory_space=pl.ANY),
                  pl.BlockSpec((None, tile, D_PLE), lambda i: (layer, i, 0)),
                  const(D_PLE, D_MODEL), const(1, D_MODEL), const(D_MODEL, D_MODEL)],
        out_specs=row(D_MODEL),
        out_shape=jax.ShapeDtypeStruct((T, D_MODEL), F32),
        scratch_shapes=[pltpu.VMEM((2, 2, tile, D_MODEL), F32), pltpu.SemaphoreType.DMA((2,))],
        compiler_params=_cparams(1),
        name="combine_ple",
    )(pos3, pos3, route, h1, ys, p, wp, g, wgate)


def _rope(z, cos, sin_lo, sin_hi):
    half = ROT_DIM // 2
    outs = []
    for j in range(z.shape[1] // LANES):
        zj = z[:, j * LANES:(j + 1) * LANES]
        outs.append(zj * cos + pltpu.roll(zj, LANES - half, 1) * sin_lo + pltpu.roll(zj, half, 1) * sin_hi)
    return outs[0] if len(outs) == 1 else jnp.concatenate(outs, axis=1)


def _proj_odd_kernel(h_ref, g_ref, w_ref, qg_ref, kg_ref, cos_ref, slo_ref, shi_ref,
                     q_ref, k_ref, v_ref, glu_ref):
    a = _rms(h_ref[...], g_ref[...]).astype(BF16)
    cos, slo, shi = cos_ref[...], slo_ref[...], shi_ref[...]
    v0 = SWA_WIDTH + KV_WIDTH
    d0 = v0 + KV_WIDTH
    zq = jnp.dot(a, w_ref[:, 0:SWA_WIDTH], preferred_element_type=F32)
    zk = jnp.dot(a, w_ref[:, SWA_WIDTH:v0], preferred_element_type=F32)
    zv = jnp.dot(a, w_ref[:, v0:d0], preferred_element_type=F32)
    zd = jnp.dot(a, w_ref[:, d0:d0 + 2 * CONV_CH], preferred_element_type=F32)
    q_ref[...] = (_rope(_head_rms(zq, qg_ref[...]), cos, slo, shi) * (HEAD_DIM ** -0.5)).astype(BF16)
    k_ref[...] = _rope(_head_rms(zk, kg_ref[...]), cos, slo, shi).astype(BF16)
    v_ref[...] = zv.astype(BF16)
    glu_ref[...] = zd[:, :CONV_CH] * jax.nn.sigmoid(zd[:, CONV_CH:])


def _proj_odd(h, g, w, qg, kg, cos, slo, shi, *, seq):
    T = h.shape[0]
    tm = min(ROW_TILE, seq)
    tps = seq // tm
    const = lambda *shape: pl.BlockSpec(shape, lambda i: (0,) * len(shape))
    row = lambda width: pl.BlockSpec((tm, width), lambda i: (i, 0))
    tab = pl.BlockSpec((tm, LANES), lambda i: (i % tps, 0))
    return pl.pallas_call(
        _proj_odd_kernel,
        grid=(T // tm,),
        in_specs=[row(D_MODEL), const(1, D_MODEL), const(*w.shape), const(1, SWA_WIDTH),
                  const(1, KV_WIDTH), tab, tab, tab],
        out_specs=[row(SWA_WIDTH), row(KV_WIDTH), row(KV_WIDTH), row(CONV_CH)],
        out_shape=[jax.ShapeDtypeStruct((T, SWA_WIDTH), BF16),
                   jax.ShapeDtypeStruct((T, KV_WIDTH), BF16),
                   jax.ShapeDtypeStruct((T, KV_WIDTH), BF16),
                   jax.ShapeDtypeStruct((T, CONV_CH), F32)],
        compiler_params=_cparams(1),
        name="proj_odd",
    )(h, g, w, qg, kg, cos, slo, shi)


def _swa_kernel(sink_ref, q_ref, k_ref, v_ref, o_ref, *, seq):
    lo = _lane_lo((1, LANES))
    nt = (((1,), (1,)), ((), ()))
    W = WINDOW
    qi = lax.broadcasted_iota(jnp.int32, (W, 2 * W), 0)
    kj = lax.broadcasted_iota(jnp.int32, (W, 2 * W), 1)
    band = (kj > qi) & (kj <= qi + W)

    n_tiles = SWA_WIDTH // LANES
    heads = [head for j in range(n_tiles) for head in (j, n_tiles + j)]

    def window(n):
        start = max(n - 1, 0) * W
        return slice(start, start + 2 * W), (band if n > 0 else kj <= qi)

    def score_products(n):
        kwin, _ = window(n)
        ks = k_ref[kwin, :]
        scores = []
        for j in range(n_tiles):
            q = q_ref[n * W:(n + 1) * W, j * LANES:(j + 1) * LANES]
            zero = jnp.zeros_like(q)
            scores += [lax.dot_general(qh, ks, nt, preferred_element_type=F32)
                       for qh in (jnp.where(lo, q, zero), jnp.where(lo, zero, q))]
        return scores

    scores = score_products(0)
    for n in range(seq // W):
        next_scores = score_products(n + 1) if (n + 1) * W < seq else None
        kwin, mask = window(n)
        vs = v_ref[kwin, :]
        probs, sums = [], []
        for head, s in zip(heads, scores):
            s = jnp.where(mask, s, NEG)
            sink = sink_ref[head]
            m = jnp.maximum(jnp.max(s, axis=-1, keepdims=True), sink)
            p = jnp.exp(s - m)
            sums.append(jnp.sum(p, axis=-1, keepdims=True) + jnp.exp(sink - m))
            probs.append(p.astype(BF16))
        outs = [jnp.dot(p, vs, preferred_element_type=F32) / l for p, l in zip(probs, sums)]
        for j in range(n_tiles):
            o_ref[n * W:(n + 1) * W, j * LANES:(j + 1) * LANES] = (
                jnp.where(lo, outs[2 * j], outs[2 * j + 1]).astype(BF16))
        scores = next_scores


def _swa_attention(sinks, q, k, v, *, seq):
    T = q.shape[0]
    B = T // seq
    return pl.pallas_call(
        functools.partial(_swa_kernel, seq=seq),
        grid=(B,),
        in_specs=[pl.BlockSpec(memory_space=pltpu.SMEM),
                  pl.BlockSpec((seq, SWA_WIDTH), lambda b: (b, 0)),
                  pl.BlockSpec((seq, KV_WIDTH), lambda b: (b, 0)),
                  pl.BlockSpec((seq, KV_WIDTH), lambda b: (b, 0))],
        out_specs=pl.BlockSpec((seq, SWA_WIDTH), lambda b: (b, 0)),
        out_shape=jax.ShapeDtypeStruct((T, SWA_WIDTH), BF16),
        compiler_params=_cparams(1),
        name="swa_attention",
    )(sinks, q, k, v)


def _conv_kernel(prev_ref, cur_ref, w_ref, g_ref, b_ref, o_ref, shift_ref, *, tile, sub):
    r = pl.program_id(1)
    rows = CONV_HALO + tile
    tail = prev_ref[tile - CONV_HALO:, :]
    shift_ref[0, 0:CONV_HALO, :] = jnp.where(r > 0, tail, jnp.zeros_like(tail))
    shift_ref[0, CONV_HALO:rows, :] = cur_ref[...]
    shift_ref[0, rows:rows + SUBLANES, :] = jnp.zeros((SUBLANES, CONV_CH), F32)
    for o in range(1, SUBLANES):
        shift_ref[o, 0:rows, :] = shift_ref[0, o:o + rows, :]
    w = w_ref[...]
    first = CONV_HALO - (CONV_WIDTH - 1)
    for s in range(tile // sub):
        acc = jnp.zeros((sub, CONV_CH), F32)
        for j in range(CONV_WIDTH):
            start = s * sub + first + j
            o = start % SUBLANES
            acc = acc + shift_ref[o, start - o:start - o + sub, :] * w[j:j + 1, :]
        y = _layernorm(acc, g_ref[...], b_ref[...])
        o_ref[s * sub:(s + 1) * sub, :] = (y * jax.nn.sigmoid(y)).astype(BF16)


def _conv_module(glu, w, g, b, *, seq):
    T = glu.shape[0]
    B = T // seq
    tile = min(CONV_TILE, seq)
    nr = seq // tile
    const = lambda *shape: pl.BlockSpec(shape, lambda bb, r: (0,) * len(shape))
    return pl.pallas_call(
        functools.partial(_conv_kernel, tile=tile, sub=64),
        grid=(B, nr),
        in_specs=[pl.BlockSpec((tile, CONV_CH), lambda bb, r: (bb * nr + jnp.maximum(r - 1, 0), 0)),
                  pl.BlockSpec((tile, CONV_CH), lambda bb, r: (bb * nr + r, 0)),
                  const(CONV_WIDTH, CONV_CH), const(1, CONV_CH), const(1, CONV_CH)],
        out_specs=pl.BlockSpec((tile, CONV_CH), lambda bb, r: (bb * nr + r, 0)),
        out_shape=jax.ShapeDtypeStruct((T, CONV_CH), BF16),
        scratch_shapes=[pltpu.VMEM((SUBLANES, CONV_HALO + tile + SUBLANES, CONV_CH), F32)],
        compiler_params=_cparams(2),
        name="conv_module",
    )(glu, glu, w, g, b)


def _slot_tables_kernel(route_ref, off_ref, disp_ref, comb_ref, *, tm, tc):
    rt = route_ref[...].T
    slots = []
    for k in range(2):
        expert = rt[k:k + 1, :]
        first = jnp.zeros_like(expert)
        for e in range(N_EXPERTS):
            first = jnp.where(expert == float(e), off_ref[0:1, e:e + 1], first)
        slots.append((first + rt[2 + k:3 + k, :]).astype(jnp.int32))
    for k in range(2):
        disp_ref[0, :, k * tm:(k + 1) * tm] = slots[k]
        for b in range(tm // tc):
            comb_ref[b, :, k * tc:(k + 1) * tc] = slots[k][:, b * tc:(b + 1) * tc]


def _slot_tables(route, offset, *, tm, tc):
    T = route.shape[0]
    off = jnp.pad(offset.astype(F32), (0, LANES - N_EXPERTS)).reshape(1, LANES)
    return pl.pallas_call(
        functools.partial(_slot_tables_kernel, tm=tm, tc=tc),
        grid=(T // tm,),
        in_specs=[pl.BlockSpec((tm, LANES), lambda i: (i, 0)), pl.BlockSpec((1, LANES), lambda i: (0, 0))],
        out_specs=[pl.BlockSpec((1, 1, 2 * tm), lambda i: (i, 0, 0)),
                   pl.BlockSpec((tm // tc, 1, 2 * tc), lambda i: (i, 0, 0))],
        out_shape=[jax.ShapeDtypeStruct((T // tm, 1, 2 * tm), jnp.int32),
                   jax.ShapeDtypeStruct((T // tc, 1, 2 * tc), jnp.int32)],
        compiler_params=_cparams(1),
        name="moe_slot_tables",
    )(route, off)


def _routing_tables(counts, n_tiles):
    cnt = counts[0, :N_EXPERTS].astype(jnp.int32)
    tiles = (cnt + EXPERT_TILE - 1) // EXPERT_TILE
    tile_end = jnp.cumsum(tiles)
    offset = (tile_end - tiles) * EXPERT_TILE
    n_used = tile_end[-1]
    tile_id = jnp.minimum(jnp.arange(n_tiles, dtype=jnp.int32), n_used - 1)
    tile_expert = jnp.sum((tile_end[None, :] <= tile_id[:, None]).astype(jnp.int32), axis=1)
    first_pad = offset + cnt
    n_single = (-first_pad) % SUBLANES
    pads = jnp.stack([first_pad, n_single, tiles * EXPERT_TILE - cnt - n_single,
                      jnp.broadcast_to(tile_end[-1], cnt.shape)])
    return offset, pads, tile_expert, n_used.reshape(1).astype(jnp.int32)


def _moe_ple(h, ya, yb, wo, layer, norm_ffn, wr, br, wg, wu, wd, p, wp, ple_norm, wgate):
    T = h.shape[0]
    n_tiles = (2 * T) // EXPERT_TILE + N_EXPERTS
    h1, m, route, counts = _outproj_router(h, ya, yb, wo, norm_ffn, wr, br)
    offset, pads, tile_expert, n_used = _routing_tables(counts, n_tiles)
    disp_slots, comb_slots = _slot_tables(route, offset, tm=min(DISPATCH_TILE, T), tc=min(GATHER_TILE, T))
    xs = _dispatch(pads, disp_slots, m, n_tiles * EXPERT_TILE)
    ys = _experts(layer, tile_expert, n_used, xs, wg, wu, wd)
    return _combine_ple(layer, comb_slots, route, h1, ys, p, wp, ple_norm, wgate)


def _router_weights(w_coarse, b_coarse, w_fine, b_fine):
    wf = w_fine.transpose(1, 0, 2).reshape(D_MODEL, N_EXPERTS)
    wr = jnp.concatenate([wf, w_coarse, jnp.zeros((D_MODEL, LANES - N_EXPERTS - N_GROUPS), F32)], axis=1)
    br = jnp.concatenate([b_fine.reshape(-1), b_coarse, jnp.zeros((LANES - N_EXPERTS - N_GROUPS,), F32)])
    w_hi = wr.astype(BF16)
    w_lo = (wr - w_hi.astype(F32)).astype(BF16)
    return jnp.concatenate([w_hi, w_lo], axis=1), br.reshape(1, LANES)


def _rope_tables(seq):
    half = ROT_DIM // 2
    inv_freq = ROPE_THETA ** (-jnp.arange(half, dtype=F32) * 2.0 / ROT_DIM)
    ang = jnp.arange(seq, dtype=F32)[:, None] * inv_freq[None, :]
    cos, sin = jnp.cos(ang), jnp.sin(ang)
    zeros = jnp.zeros((seq, HEAD_DIM - ROT_DIM), F32)
    z8 = jnp.zeros((seq, half), F32)
    cos_h = jnp.concatenate([cos, cos, zeros + 1.0], axis=1)
    slo_h = jnp.concatenate([-sin, z8, zeros], axis=1)
    shi_h = jnp.concatenate([z8, sin, zeros], axis=1)
    two = lambda t: jnp.concatenate([t, t], axis=1)
    return two(cos_h), two(slo_h), two(shi_h)


def kernel(x, p, norm_mix, even_w_in, fox_b_f, gmlp_ln_g, gmlp_ln_b, gmlp_w_s, gmlp_b_s, fox_q_norm, fox_k_norm, even_w_out, odd_w_in, swa_q_norm, swa_k_norm, swa_sinks, conv_w, conv_ln_g, conv_ln_b, odd_w_out, norm_ffn, moe_w_coarse, moe_b_coarse, moe_w_fine, moe_b_fine, moe_w_gate, moe_w_up, moe_w_down, ple_w_proj, ple_norm, ple_w_gate):
    B, S, D = x.shape
    T = B * S
    h = x.reshape(T, D)
    p = p.reshape(p.shape[0], T, D_PLE)
    row = lambda v: v.reshape(1, -1)

    def moe_args(i):
        wr, br = _router_weights(moe_w_coarse[i], moe_b_coarse[i], moe_w_fine[i], moe_b_fine[i])
        return (i, row(norm_ffn[i]), wr, br, moe_w_gate, moe_w_up, moe_w_down, p,
                ple_w_proj[i].astype(BF16), row(ple_norm[i]), ple_w_gate[i].astype(BF16))

    n_main = 2 * GMLP_WIDTH + 3 * FOX_WIDTH
    w_f = jnp.pad(jnp.tile(even_w_in[0][:, n_main:], (1, BIAS_PIECES)),
                  ((0, 0), (0, LANES - BIAS_PIECES * FOX_HEADS)))
    w_in = jnp.concatenate([even_w_in[0][:, :n_main], w_f], axis=1).astype(BF16)
    b_f = jnp.pad(jnp.tile(fox_b_f[0], BIAS_PIECES), (0, LANES - BIAS_PIECES * FOX_HEADS)).reshape(1, LANES)
    bs_full = jnp.repeat(gmlp_b_s[0].T, HEAD_DIM, axis=1)
    ya, q, k, v, xq, xk = _proj_even(
        h, row(norm_mix[0]), w_in, b_f, row(gmlp_ln_g[0]), row(gmlp_ln_b[0]), gmlp_w_s[0], bs_full,
        row(jnp.tile(fox_q_norm[0], FOX_HEADS)), row(jnp.tile(fox_k_norm[0], FOX_HEADS)), seq=S)
    yb = _fox_attention(q, xq, k, xk, v, seq=S)
    h = _moe_ple(h, ya, yb, even_w_out[0].astype(BF16), *moe_args(0))

    order = jnp.array([0, 4, 1, 5, 2, 6, 3, 7])
    cols = (order[:, None] * HEAD_DIM + jnp.arange(HEAD_DIM)[None, :]).reshape(-1)
    w_odd = jnp.concatenate([odd_w_in[0][:, :SWA_WIDTH][:, cols], odd_w_in[0][:, SWA_WIDTH:]], axis=1).astype(BF16)
    w_out_odd = jnp.concatenate([odd_w_out[0][:SWA_WIDTH][cols], odd_w_out[0][SWA_WIDTH:]], axis=0).astype(BF16)
    cos, slo, shi = _rope_tables(S)
    q, k, v, glu = _proj_odd(h, row(norm_mix[1]), w_odd, row(jnp.tile(swa_q_norm[0], 8)),
                             row(jnp.tile(swa_k_norm[0], 2)), cos, slo, shi, seq=S)
    yc = _swa_attention(swa_sinks[0], q, k, v, seq=S)
    yd = _conv_module(glu, conv_w[0], row(conv_ln_g[0]), row(conv_ln_b[0]), seq=S)
    h = _moe_ple(h, yc, yd, w_out_odd, *moe_args(1))
    return h.reshape(B, S, D)
```

```python
import functools

import jax
import jax.numpy as jnp
import numpy as np
from jax import lax
from jax.experimental import pallas as pl
from jax.experimental.pallas import tpu as pltpu

F32 = jnp.float32
BF16 = jnp.bfloat16

D_MODEL = 1024
HEAD_DIM = 64
LANES = 128
GMLP_WIDTH = 512
CHUNK = 128
FOX_WIDTH = 512
FOX_HEADS = 8
SWA_WIDTH = 512
KV_WIDTH = 128
WINDOW = 128
CONV_CH = 512
CONV_WIDTH = 31
CONV_HALO = 32
ROPE_THETA = 500000.0
ROT_DIM = 16
N_GROUPS = 4
EXPERTS_PER_GROUP = 8
N_EXPERTS = 32
D_EXPERT = 256
D_PLE = 256
EPS = 1e-6
NEG = -1e30
LOG2E = 1.4426950408889634
BIAS_PIECES = 3

EXPERT_TILE = 512
EXPERT_BUFFERS = 3
PAD_BITS = 9
SUBLANES = 8
SUBLANE_BITS = 3
ROW_TILE = 1024
ROUTER_BLOCKS = 8
GATHER_TILE = 512
COMBINE_BLOCKS = 4
DISPATCH_TILE = 1024
DMA_UNROLL = 16
ATTN_TILE = 256
FOX_Q_ROWS = 512
FOX_PAIRS_PER_STEP = 2
CONV_TILE = 512
VMEM_LIMIT = 56 * 1024 * 1024


def _cparams(n_axes=1):
    return pltpu.CompilerParams(dimension_semantics=("arbitrary",) * n_axes,
                                vmem_limit_bytes=VMEM_LIMIT)


def _rms(x, gain):
    return x * lax.rsqrt(jnp.mean(x * x, axis=-1, keepdims=True) + EPS) * gain


def _layernorm(x, g, b):
    mu = jnp.mean(x, axis=-1, keepdims=True)
    xc = x - mu
    var = jnp.mean(xc * xc, axis=-1, keepdims=True)
    return xc * lax.rsqrt(var + EPS) * g + b


def _head_rms(z, gain):
    lo = _lane_lo((1, LANES))
    outs = []
    for j in range(z.shape[1] // LANES):
        zj = z[:, j * LANES:(j + 1) * LANES]
        sq = zj * zj
        sum_lo = jnp.sum(jnp.where(lo, sq, 0.0), axis=-1, keepdims=True)
        sum_hi = jnp.sum(jnp.where(lo, 0.0, sq), axis=-1, keepdims=True)
        ms = jnp.where(lo, sum_lo, sum_hi) * (1.0 / HEAD_DIM)
        outs.append(zj * lax.rsqrt(ms + EPS))
    zn = outs[0] if len(outs) == 1 else jnp.concatenate(outs, axis=1)
    return zn * gain


def _lane_lo(shape):
    return (lax.broadcasted_iota(jnp.int32, shape, len(shape) - 1) % LANES) < HEAD_DIM


def _proj_even_kernel(h_ref, g_ref, w_ref, bf_ref, lng_ref, lnb_ref, ws_ref, bs_ref, qg_ref, kg_ref,
                      pq_ref, pk_ref, oneq_ref, onek_ref,
                      ya_ref, q_ref, k_ref, v_ref, xq_ref, xk_ref, carry_ref, *, tm, tiles_per_seq):
    i = pl.program_id(0)
    a = _rms(h_ref[...], g_ref[...]).astype(BF16)

    q0 = 2 * GMLP_WIDTH
    za = jnp.dot(a, w_ref[:, 0:q0], preferred_element_type=F32)
    zq = jnp.dot(a, w_ref[:, q0:q0 + FOX_WIDTH], preferred_element_type=F32)
    zk = jnp.dot(a, w_ref[:, q0 + FOX_WIDTH:q0 + 2 * FOX_WIDTH], preferred_element_type=F32)
    zv = jnp.dot(a, w_ref[:, q0 + 2 * FOX_WIDTH:q0 + 3 * FOX_WIDTH], preferred_element_type=F32)
    zf = jnp.dot(a, w_ref[:, q0 + 3 * FOX_WIDTH:], preferred_element_type=F32) + bf_ref[...]

    q_ref[...] = (_head_rms(zq, qg_ref[...]) * (LOG2E * HEAD_DIM ** -0.5)).astype(BF16)
    k_ref[...] = _head_rms(zk, kg_ref[...]).astype(BF16)
    v_ref[...] = zv.astype(BF16)

    rr = lax.broadcasted_iota(jnp.int32, (CHUNK, CHUNK), 0)
    cc = lax.broadcasted_iota(jnp.int32, (CHUNK, CHUNK), 1)

    ls = jnp.minimum(zf, 0.0) - jnp.log(1.0 + jnp.exp(-jnp.abs(zf)))

    @pl.when(i % tiles_per_seq == 0)
    def _():
        carry_ref[...] = jnp.zeros_like(carry_ref)

    def split3(x):
        hi = x.astype(BF16)
        r1 = x - hi.astype(F32)
        mid = r1.astype(BF16)
        return hi, mid, (r1 - mid.astype(F32)).astype(BF16)

    tri = jnp.where(rr >= cc, 1.0, 0.0).astype(BF16)
    pieces = jnp.concatenate(split3(ls), axis=1)
    running = carry_ref[...]
    blocks = []
    for b in range(tm // CHUNK):
        d = jnp.dot(tri, pieces[b * CHUNK:(b + 1) * CHUNK, :], preferred_element_type=F32)
        blk = (d[:, :LANES] + d[:, LANES:2 * LANES]) + d[:, 2 * LANES:] + running
        running = blk[CHUNK - 1:CHUNK, :]
        blocks.append(blk)
    carry_ref[...] = running
    c = jnp.concatenate(blocks, axis=0)

    za = jax.nn.gelu(za)
    u = za[:, :GMLP_WIDTH]
    vln = _layernorm(za[:, GMLP_WIDTH:], lng_ref[...], lnb_ref[...]).astype(BF16)
    lo = _lane_lo((CHUNK, LANES))
    for j in range(GMLP_WIDTH // LANES):
        w_a = jnp.where(rr >= cc, ws_ref[2 * j], 0.0).astype(BF16)
        w_b = jnp.where(rr >= cc, ws_ref[2 * j + 1], 0.0).astype(BF16)
        cols = slice(j * LANES, (j + 1) * LANES)
        for blk in range(tm // CHUNK):
            rows = slice(blk * CHUNK, (blk + 1) * CHUNK)
            vp = vln[rows, cols]
            mixed = jnp.where(lo, jnp.dot(w_a, vp, preferred_element_type=F32),
                              jnp.dot(w_b, vp, preferred_element_type=F32)) + bs_ref[:, cols]
            ya_ref[rows, cols] = (u[rows, cols] * mixed).astype(BF16)

    hi, mid, low = split3(c * LOG2E)
    group = lax.broadcasted_iota(jnp.int32, (tm, LANES), 1) // FOX_HEADS
    sel = jnp.where(group == 0, hi, jnp.where(group == 1, mid, low))
    xq_ref[...] = (jnp.dot(sel, pq_ref[...], preferred_element_type=F32) + oneq_ref[...]).astype(BF16)
    xk_ref[...] = (jnp.dot(sel, pk_ref[...], preferred_element_type=F32) + onek_ref[...]).astype(BF16)


def _proj_even(h, g, w, bf, lng, lnb, ws, bs_full, qg, kg, *, seq):
    T = h.shape[0]
    tm = min(ROW_TILE, seq)
    n_in = w.shape[1]
    const = lambda *shape: pl.BlockSpec(shape, lambda i: (0,) * len(shape))
    row = lambda width: pl.BlockSpec((tm, width), lambda i: (i, 0))
    tps = seq // tm
    return pl.pallas_call(
        functools.partial(_proj_even_kernel, tm=tm, tiles_per_seq=tps),
        grid=(T // tm,),
        in_specs=[row(D_MODEL), const(1, D_MODEL), const(D_MODEL, n_in), const(1, LANES),
                  const(1, GMLP_WIDTH), const(1, GMLP_WIDTH), const(8, CHUNK, CHUNK),
                  const(CHUNK, GMLP_WIDTH), const(1, FOX_WIDTH), const(1, FOX_WIDTH),
                  const(LANES, FOX_WIDTH), const(LANES, FOX_WIDTH),
                  const(1, FOX_WIDTH), const(1, FOX_WIDTH)],
        out_specs=[row(GMLP_WIDTH)] + [row(FOX_WIDTH)] * 5,
        out_shape=[jax.ShapeDtypeStruct((T, GMLP_WIDTH), BF16)]
                  + [jax.ShapeDtypeStruct((T, FOX_WIDTH), BF16)] * 5,
        scratch_shapes=[pltpu.VMEM((1, LANES), F32)],
        compiler_params=_cparams(1),
        name="proj_even",
    )(h, g, w, bf, lng, lnb, ws, bs_full, qg, kg, *_bias_placement())


def _bias_placement():
    pq = np.zeros((LANES, FOX_WIDTH), np.float32)
    pk = np.zeros((LANES, FOX_WIDTH), np.float32)
    oneq = np.zeros((1, FOX_WIDTH), np.float32)
    onek = np.zeros((1, FOX_WIDTH), np.float32)
    for head in range(FOX_HEADS):
        base = (head // 2) * LANES + (HEAD_DIM if head % 2 == 0 else 0)
        for piece in range(BIAS_PIECES):
            pq[piece * FOX_HEADS + head, base + piece] = 1.0
            onek[0, base + piece] = 1.0
            pk[piece * FOX_HEADS + head, base + BIAS_PIECES + piece] = -1.0
            oneq[0, base + BIAS_PIECES + piece] = 1.0
    return (jnp.asarray(pq, BF16), jnp.asarray(pk, BF16), jnp.asarray(oneq), jnp.asarray(onek))


def _fox_tile(i, q_ref, xq_ref, k_ref, xk_ref, v_ref, o_ref, *, tq, tk):
    part_rows = tk // 2
    lo = _lane_lo((1, LANES))
    rr = lax.broadcasted_iota(jnp.int32, (part_rows, tk), 0)
    cc = lax.broadcasted_iota(jnp.int32, (part_rows, tk), 1)
    nt = (((1,), (1,)), ((), ()))
    pairs = [slice(pp * LANES, (pp + 1) * LANES) for pp in range(FOX_PAIRS_PER_STEP)]
    first_row = [i * tq + part * part_rows for part in range(tq // part_rows)]
    n_tiles = [row // tk + 1 for row in first_row]
    diag_offset = [row % tk for row in first_row]
    chains = [(pp, part, head) for pp in range(len(pairs)) for part in range(len(first_row)) for head in range(2)]
    q_aug = {}
    for pp, cols in enumerate(pairs):
        for part in range(len(first_row)):
            rows = slice(part * part_rows, (part + 1) * part_rows)
            q, xq = q_ref[rows, cols], xq_ref[rows, cols]
            q_aug[pp, part, 0], q_aug[pp, part, 1] = jnp.where(lo, q, xq), jnp.where(lo, xq, q)

    def score_products(j):
        rows = slice(j * tk, (j + 1) * tk)
        k_aug = {}
        for pp, cols in enumerate(pairs):
            ks, xk = k_ref[rows, cols], xk_ref[rows, cols]
            k_aug[pp, 0], k_aug[pp, 1] = jnp.where(lo, ks, xk), jnp.where(lo, xk, ks)
        return {(pp, part, head): lax.dot_general(q_aug[pp, part, head], k_aug[pp, head], nt,
                                                  preferred_element_type=F32)
                for pp, part, head in chains if j < n_tiles[part]}

    maxes = {c: jnp.full((part_rows, 1), NEG, F32) for c in chains}
    accs = {c: jnp.zeros((part_rows, LANES), F32) for c in chains}
    scores = score_products(0)
    for j in range(max(n_tiles)):
        next_scores = score_products(j + 1) if j + 1 < max(n_tiles) else {}
        v_aug = {}
        for pp, cols in enumerate(pairs):
            vs = v_ref[j * tk:(j + 1) * tk, cols]
            one = jnp.ones_like(vs)
            v_aug[pp, 0], v_aug[pp, 1] = jnp.where(lo, vs, one), jnp.where(lo, one, vs)
        for c, s in scores.items():
            pp, part, head = c
            if j == n_tiles[part] - 1:
                s = jnp.where(cc <= rr + diag_offset[part], s, NEG)
            n = jnp.maximum(maxes[c], jnp.max(s, axis=-1, keepdims=True))
            p = jnp.exp2(s - n).astype(BF16)
            accs[c] = accs[c] * jnp.exp2(maxes[c] - n) + jnp.dot(p, v_aug[pp, head], preferred_element_type=F32)
            maxes[c] = n
        scores = next_scores
    for pp, cols in enumerate(pairs):
        for part in range(len(first_row)):
            norm = [accs[pp, part, head] / pltpu.roll(accs[pp, part, head], HEAD_DIM, 1) for head in range(2)]
            o_ref[part * part_rows:(part + 1) * part_rows, cols] = jnp.where(lo, norm[0], norm[1]).astype(BF16)


def _fox_kernel(q_ref, xq_ref, k_ref, xk_ref, v_ref, o_ref, *, tq, tk, nq):
    i = pl.program_id(2)
    for c in range(nq):
        pl.when(i == c)(functools.partial(_fox_tile, c, q_ref, xq_ref, k_ref, xk_ref, v_ref, o_ref, tq=tq, tk=tk))


def _fox_attention(q, xq, k, xk, v, *, seq):
    T = q.shape[0]
    B = T // seq
    tk = min(ATTN_TILE, seq)
    tq = min(FOX_Q_ROWS, seq)
    nq = seq // tq
    width = FOX_PAIRS_PER_STEP * LANES
    tile = pl.BlockSpec((tq, width), lambda b, hp, i: (b * nq + i, hp))
    whole = pl.BlockSpec((seq, width), lambda b, hp, i: (b, hp))
    return pl.pallas_call(
        functools.partial(_fox_kernel, tq=tq, tk=tk, nq=nq),
        grid=(B, FOX_WIDTH // width, nq),
        in_specs=[tile, tile, whole, whole, whole],
        out_specs=tile,
        out_shape=jax.ShapeDtypeStruct((T, FOX_WIDTH), BF16),
        compiler_params=_cparams(3),
        name="fox_attention",
    )(q, xq, k, xk, v)


def _outproj_router_kernel(h_ref, ya_ref, yb_ref, wo_ref, g_ref, wr_ref, br_ref,
                           h1_ref, m_ref, route_ref, cnt_ref, carry_ref, *, tm):
    i = pl.program_id(0)
    half = wo_ref.shape[0] // 2
    nb = ROUTER_BLOCKS
    rb = tm // nb
    blocks = [slice(b * rb, (b + 1) * rb) for b in range(nb)]

    @pl.when(i == 0)
    def _():
        carry_ref[...] = jnp.zeros_like(carry_ref)

    mixes = [jnp.dot(ya_ref[rows, :], wo_ref[0:half, :], preferred_element_type=F32)
             + jnp.dot(yb_ref[rows, :], wo_ref[half:, :], preferred_element_type=F32) for rows in blocks]
    ms = []
    for rows, mix in zip(blocks, mixes):
        h1 = h_ref[rows, :] + mix
        h1_ref[rows, :] = h1
        m = _rms(h1, g_ref[...])
        m_ref[rows, :] = m
        ms.append(m)

    logits = []
    for m in ms:
        m_hi = m.astype(BF16)
        m_lo = (m - m_hi.astype(F32)).astype(BF16)
        hh = jnp.dot(m_hi, wr_ref[...], preferred_element_type=F32)
        lh = jnp.dot(m_lo, wr_ref[:, :LANES], preferred_element_type=F32)
        logits.append(hh[:, :LANES] + (hh[:, LANES:] + lh) + br_ref[...])

    lane_i = lax.broadcasted_iota(jnp.int32, (rb, LANES), 1)
    lane = lane_i.astype(F32)
    group_of_lane = (lane_i // EXPERTS_PER_GROUP).astype(F32)
    is_coarse = (lane_i >= N_EXPERTS) & (lane_i < N_EXPERTS + N_GROUPS)
    picks = []
    for lg in logits:
        coarse = jnp.where(is_coarse, lg, NEG)
        cmax = jnp.max(coarse, axis=-1, keepdims=True)
        gidx = jnp.min(jnp.where(coarse == cmax, lane - N_EXPERTS, float(LANES)), axis=-1, keepdims=True)
        p_g = 1.0 / jnp.sum(jnp.where(is_coarse, jnp.exp(coarse - cmax), 0.0), axis=-1, keepdims=True)
        in_group = (lane_i < N_EXPERTS) & (group_of_lane == gidx)
        fine = jnp.where(in_group, lg, NEG)
        v1 = jnp.max(fine, axis=-1, keepdims=True)
        i1 = jnp.min(jnp.where(fine == v1, lane, float(LANES)), axis=-1, keepdims=True)
        fine2 = jnp.where(lane == i1, NEG, fine)
        v2 = jnp.max(fine2, axis=-1, keepdims=True)
        i2 = jnp.min(jnp.where(fine2 == v2, lane, float(LANES)), axis=-1, keepdims=True)
        e2 = jnp.exp(v2 - v1)
        picks.append((i1, i2, p_g / (1.0 + e2), p_g * e2 / (1.0 + e2)))

    tr = lax.broadcasted_iota(jnp.int32, (rb, rb), 0)
    tc = lax.broadcasted_iota(jnp.int32, (rb, rb), 1)
    strict = jnp.where(tr > tc, 1.0, 0.0).astype(BF16)
    onehots = [jnp.where((lane == i1) | (lane == i2), 1.0, 0.0).astype(F32) for i1, i2, _, _ in picks]
    befores = [jnp.dot(strict, oh.astype(BF16), preferred_element_type=F32) for oh in onehots]
    total = carry_ref[...]
    for rows, (i1, i2, w1, w2), oh, before in zip(blocks, picks, onehots, befores):
        before = before + total
        r1 = jnp.sum(jnp.where(lane == i1, before, 0.0), axis=-1, keepdims=True)
        r2 = jnp.sum(jnp.where(lane == i2, before, 0.0), axis=-1, keepdims=True)
        total = total + jnp.sum(oh, axis=0, keepdims=True)
        route = jnp.where(lane == 0, i1, 0.0)
        route = jnp.where(lane == 1, i2, route)
        route = jnp.where(lane == 2, r1, route)
        route = jnp.where(lane == 3, r2, route)
        route = jnp.where(lane == 4, w1, route)
        route = jnp.where(lane == 5, w2, route)
        route_ref[rows, :] = route
    carry_ref[...] = total
    cnt_ref[...] = jnp.broadcast_to(total, cnt_ref.shape)


def _outproj_router(h, ya, yb, wo, g, wr, br):
    T = h.shape[0]
    tm = min(ROW_TILE, T)
    const = lambda *shape: pl.BlockSpec(shape, lambda i: (0,) * len(shape))
    row = lambda width: pl.BlockSpec((tm, width), lambda i: (i, 0))
    return pl.pallas_call(
        functools.partial(_outproj_router_kernel, tm=tm),
        grid=(T // tm,),
        in_specs=[row(D_MODEL), row(ya.shape[1]), row(yb.shape[1]), const(*wo.shape),
                  const(1, D_MODEL), const(D_MODEL, 2 * LANES), const(1, LANES)],
        out_specs=[row(D_MODEL), row(D_MODEL), row(LANES), const(8, LANES)],
        out_shape=[jax.ShapeDtypeStruct((T, D_MODEL), F32),
                   jax.ShapeDtypeStruct((T, D_MODEL), F32),
                   jax.ShapeDtypeStruct((T, LANES), F32),
                   jax.ShapeDtypeStruct((8, LANES), F32)],
        scratch_shapes=[pltpu.VMEM((1, LANES), F32)],
        compiler_params=_cparams(1),
        name="outproj_router",
    )(h, ya, yb, wo, g, wr, br)


def _dispatch_kernel(pad_ref, pos_ref, m_ref, xs_hbm, zeros_ref, sem, pad_sem, *, tile):
    @pl.when(pl.program_id(0) == 0)
    def _():
        zeros_ref[...] = jnp.zeros_like(zeros_ref)

        def pad_copies(e, wait):
            first, n_single, n_block = pad_ref[0, e], pad_ref[1, e], pad_ref[2, e]
            for r in range(SUBLANES - 1):
                copy = pltpu.make_async_copy(zeros_ref.at[pl.ds(0, 1)], xs_hbm.at[pl.ds(first + r, 1)], pad_sem)
                pl.when(r < n_single)(copy.wait if wait else copy.start)
            done = first + n_single
            for bit in reversed(range(SUBLANE_BITS, PAD_BITS)):
                size = 1 << bit
                taken = (n_block & size) != 0
                copy = pltpu.make_async_copy(zeros_ref.at[pl.ds(0, size)],
                                             xs_hbm.at[pl.ds(pl.multiple_of(done, SUBLANES), size)], pad_sem)
                pl.when(taken)(copy.wait if wait else copy.start)
                done = done + jnp.where(taken, size, 0)

        def start(e, carry):
            pad_copies(e, False)
            return carry

        def finish(e, carry):
            pad_copies(e, True)
            return carry

        def tile_copy(j):
            return pltpu.make_async_copy(
                zeros_ref, xs_hbm.at[pl.ds(pl.multiple_of(j * EXPERT_TILE, EXPERT_TILE), EXPERT_TILE)], pad_sem)

        def start_tile(j, carry):
            tile_copy(j).start()
            return carry

        def finish_tile(j, carry):
            tile_copy(j).wait()
            return carry

        n_tiles = xs_hbm.shape[0] // EXPERT_TILE
        lax.fori_loop(0, N_EXPERTS, start, 0)
        lax.fori_loop(pad_ref[3, 0], n_tiles, start_tile, 0)
        lax.fori_loop(0, N_EXPERTS, finish, 0)
        lax.fori_loop(pad_ref[3, 0], n_tiles, finish_tile, 0)

    def issue(c, carry):
        base = pl.multiple_of(c * DMA_UNROLL, DMA_UNROLL)
        group = m_ref.at[pl.ds(base, DMA_UNROLL)]
        for u in range(DMA_UNROLL):
            for k in range(2):
                dst = xs_hbm.at[pl.ds(pos_ref[0, 0, k * tile + base + u], 1)]
                pltpu.make_async_copy(group.at[pl.ds(u, 1)], dst, sem).start(priority=k)
        return carry

    lax.fori_loop(0, tile // DMA_UNROLL, issue, 0)
    for _ in range(2):
        pltpu.make_async_copy(m_ref, xs_hbm.at[pl.ds(0, tile)], sem).wait()


def _dispatch(pads, pos3, m, n_rows):
    T, width = m.shape
    tile = pos3.shape[2] // 2
    grid_spec = pltpu.PrefetchScalarGridSpec(
        num_scalar_prefetch=1,
        grid=(T // tile,),
        in_specs=[pl.BlockSpec((1, 1, 2 * tile), lambda i, pads: (i, 0, 0), memory_space=pltpu.SMEM),
                  pl.BlockSpec((tile, width), lambda i, pads: (i, 0))],
        out_specs=pl.BlockSpec(memory_space=pl.ANY),
        scratch_shapes=[pltpu.VMEM((EXPERT_TILE, width), m.dtype),
                        pltpu.SemaphoreType.DMA(()), pltpu.SemaphoreType.DMA(())],
    )
    return pl.pallas_call(
        functools.partial(_dispatch_kernel, tile=tile),
        grid_spec=grid_spec,
        out_shape=jax.ShapeDtypeStruct((n_rows, width), m.dtype),
        compiler_params=pltpu.CompilerParams(dimension_semantics=("arbitrary",),
                                             has_side_effects=True, vmem_limit_bytes=VMEM_LIMIT),
        name="moe_dispatch",
    )(pads, pos3, m)


def _experts_kernel(te_ref, nused_ref, xs_hbm, wg_ref, wu_ref, wd_ref, ys_ref, xbuf, sems, wgu_b, wd_b):
    j = pl.program_id(0)
    n_used = nused_ref[0]
    prev = te_ref[jnp.maximum(j - 1, 0)]

    def tile_copy(t):
        start = t * EXPERT_TILE
        rows = pl.ds(start if isinstance(t, int) else pl.multiple_of(start, EXPERT_TILE), EXPERT_TILE)
        return pltpu.make_async_copy(xs_hbm.at[rows], xbuf.at[t % EXPERT_BUFFERS], sems.at[t % EXPERT_BUFFERS])

    @pl.when(j == 0)
    def _():
        for t in range(EXPERT_BUFFERS - 1):
            pl.when(t < n_used)(lambda t=t: tile_copy(t).start())

    @pl.when(j + EXPERT_BUFFERS - 1 < n_used)
    def _():
        tile_copy(j + EXPERT_BUFFERS - 1).start()

    @pl.when((j == 0) | (te_ref[j] != prev))
    def _():
        wgu_b[:, :D_EXPERT] = wg_ref[...].astype(BF16)
        wgu_b[:, D_EXPERT:] = wu_ref[...].astype(BF16)
        wd_b[...] = wd_ref[...].astype(BF16)

    @pl.when(j < n_used)
    def _():
        tile_copy(j).wait()
        gu = jnp.dot(xbuf[j % EXPERT_BUFFERS].astype(BF16), wgu_b[...], preferred_element_type=F32)
        g = gu[:, :D_EXPERT]
        act = g * jax.nn.sigmoid(g) * gu[:, D_EXPERT:]
        ys_ref[...] = jnp.dot(act.astype(BF16), wd_b[...], preferred_element_type=F32)

    @pl.when(j >= n_used)
    def _():
        ys_ref[...] = jnp.zeros_like(ys_ref)


def _experts(layer, tile_expert, n_used, xs, wg, wu, wd):
    n_rows = xs.shape[0]
    nt = n_rows // EXPERT_TILE
    grid_spec = pltpu.PrefetchScalarGridSpec(
        num_scalar_prefetch=2,
        grid=(nt,),
        in_specs=[pl.BlockSpec(memory_space=pl.ANY),
                  pl.BlockSpec((None, None, D_MODEL, D_EXPERT), lambda j, te, nu: (layer, te[j], 0, 0)),
                  pl.BlockSpec((None, None, D_MODEL, D_EXPERT), lambda j, te, nu: (layer, te[j], 0, 0)),
                  pl.BlockSpec((None, None, D_EXPERT, D_MODEL), lambda j, te, nu: (layer, te[j], 0, 0))],
        out_specs=pl.BlockSpec((EXPERT_TILE, D_MODEL), lambda j, te, nu: (j, 0)),
        scratch_shapes=[pltpu.VMEM((EXPERT_BUFFERS, EXPERT_TILE, D_MODEL), F32),
                        pltpu.SemaphoreType.DMA((EXPERT_BUFFERS,)),
                        pltpu.VMEM((D_MODEL, 2 * D_EXPERT), BF16),
                        pltpu.VMEM((D_EXPERT, D_MODEL), BF16)],
    )
    return pl.pallas_call(
        _experts_kernel,
        grid_spec=grid_spec,
        out_shape=jax.ShapeDtypeStruct((n_rows, D_MODEL), F32),
        compiler_params=_cparams(1),
        name="moe_experts",
    )(tile_expert, n_used, xs, wg, wu, wd)


def _combine_ple_kernel(pos_ref, next_pos_ref, route_ref, h1_ref, ys_hbm, p_ref, wp_ref, g_ref, wgate_ref,
                        o_ref, ybuf, sems, *, tile):
    i = pl.program_id(0)
    n = pl.num_programs(0)
    slot = i % 2

    def gather(table, s):
        def issue(c, carry):
            base = pl.multiple_of(c * DMA_UNROLL, DMA_UNROLL)
            for k in range(2):
                group = ybuf.at[s, k, pl.ds(base, DMA_UNROLL)]
                for u in range(DMA_UNROLL):
                    pltpu.make_async_copy(ys_hbm.at[pl.ds(table[0, 0, k * tile + base + u], 1)],
                                          group.at[pl.ds(u, 1)], sems.at[s]).start(priority=k)
            return carry

        lax.fori_loop(0, tile // DMA_UNROLL, issue, 0)

    @pl.when(i == 0)
    def _():
        gather(pos_ref, 0)

    @pl.when(i + 1 < n)
    def _():
        gather(next_pos_ref, 1 - slot)

    rb = tile // COMBINE_BLOCKS
    blocks = [slice(b * rb, (b + 1) * rb) for b in range(COMBINE_BLOCKS)]
    p_b = p_ref[...].astype(BF16)
    ples = [jnp.dot(p_b[rows], wp_ref[...], preferred_element_type=F32) for rows in blocks]
    for k in range(2):
        pltpu.make_async_copy(ys_hbm.at[pl.ds(0, tile)], ybuf.at[slot, k], sems.at[slot]).wait()

    route = route_ref[...]
    h2s = [h1_ref[rows, :] + route[rows, 4:5] * ybuf[slot, 0, rows, :] + route[rows, 5:6] * ybuf[slot, 1, rows, :]
           for rows in blocks]
    normed = [_rms(h2, g_ref[...]).astype(BF16) for h2 in h2s]
    gates = [jnp.dot(x, wgate_ref[...], preferred_element_type=F32) for x in normed]
    for rows, h2, gate, ple in zip(blocks, h2s, gates, ples):
        o_ref[rows, :] = h2 + jax.nn.sigmoid(gate) * ple


def _combine_ple(layer, pos3, route, h1, ys, p, wp, g, wgate):
    T = h1.shape[0]
    n, _, width = pos3.shape
    tile = width // 2
    const = lambda *shape: pl.BlockSpec(shape, lambda i: (0,) * len(shape))
    row = lambda width: pl.BlockSpec((tile, width), lambda i: (i, 0))
    return pl.pallas_call(
        functools.partial(_combine_ple_kernel, tile=tile),
        grid=(n,),
        in_specs=[pl.BlockSpec((1, 1, 2 * tile), lambda i: (0, 0, 0), memory_space=pltpu.SMEM),
                  pl.BlockSpec((1, 1, 2 * tile), lambda i: (jnp.minimum(i + 1, n - 1), 0, 0),
                               memory_space=pltpu.SMEM),
                  row(LANES), row(D_MODEL), pl.BlockSpec(memory_space=pl.ANY),
                  pl.BlockSpec((None, tile, D_PLE), lambda i: (layer, i, 0)),
                  const(D_PLE, D_MODEL), const(1, D_MODEL), const(D_MODEL, D_MODEL)],
        out_specs=row(D_MODEL),
        out_shape=jax.ShapeDtypeStruct((T, D_MODEL), F32),
        scratch_shapes=[pltpu.VMEM((2, 2, tile, D_MODEL), F32), pltpu.SemaphoreType.DMA((2,))],
        compiler_params=_cparams(1),
        name="combine_ple",
    )(pos3, pos3, route, h1, ys, p, wp, g, wgate)


def _rope(z, cos, sin_lo, sin_hi):
    half = ROT_DIM // 2
    outs = []
    for j in range(z.shape[1] // LANES):
        zj = z[:, j * LANES:(j + 1) * LANES]
        outs.append(zj * cos + pltpu.roll(zj, LANES - half, 1) * sin_lo + pltpu.roll(zj, half, 1) * sin_hi)
    return outs[0] if len(outs) == 1 else jnp.concatenate(outs, axis=1)


def _proj_odd_kernel(h_ref, g_ref, w_ref, qg_ref, kg_ref, cos_ref, slo_ref, shi_ref,
                     q_ref, k_ref, v_ref, glu_ref):
    a = _rms(h_ref[...], g_ref[...]).astype(BF16)
    cos, slo, shi = cos_ref[...], slo_ref[...], shi_ref[...]
    v0 = SWA_WIDTH + KV_WIDTH
    d0 = v0 + KV_WIDTH
    zq = jnp.dot(a, w_ref[:, 0:SWA_WIDTH], preferred_element_type=F32)
    zk = jnp.dot(a, w_ref[:, SWA_WIDTH:v0], preferred_element_type=F32)
    zv = jnp.dot(a, w_ref[:, v0:d0], preferred_element_type=F32)
    zd = jnp.dot(a, w_ref[:, d0:d0 + 2 * CONV_CH], preferred_element_type=F32)
    q_ref[...] = (_rope(_head_rms(zq, qg_ref[...]), cos, slo, shi) * (HEAD_DIM ** -0.5)).astype(BF16)
    k_ref[...] = _rope(_head_rms(zk, kg_ref[...]), cos, slo, shi).astype(BF16)
    v_ref[...] = zv.astype(BF16)
    glu_ref[...] = zd[:, :CONV_CH] * jax.nn.sigmoid(zd[:, CONV_CH:])


def _proj_odd(h, g, w, qg, kg, cos, slo, shi, *, seq):
    T = h.shape[0]
    tm = min(ROW_TILE, seq)
    tps = seq // tm
    const = lambda *shape: pl.BlockSpec(shape, lambda i: (0,) * len(shape))
    row = lambda width: pl.BlockSpec((tm, width), lambda i: (i, 0))
    tab = pl.BlockSpec((tm, LANES), lambda i: (i % tps, 0))
    return pl.pallas_call(
        _proj_odd_kernel,
        grid=(T // tm,),
        in_specs=[row(D_MODEL), const(1, D_MODEL), const(*w.shape), const(1, SWA_WIDTH),
                  const(1, KV_WIDTH), tab, tab, tab],
        out_specs=[row(SWA_WIDTH), row(KV_WIDTH), row(KV_WIDTH), row(CONV_CH)],
        out_shape=[jax.ShapeDtypeStruct((T, SWA_WIDTH), BF16),
                   jax.ShapeDtypeStruct((T, KV_WIDTH), BF16),
                   jax.ShapeDtypeStruct((T, KV_WIDTH), BF16),
                   jax.ShapeDtypeStruct((T, CONV_CH), F32)],
        compiler_params=_cparams(1),
        name="proj_odd",
    )(h, g, w, qg, kg, cos, slo, shi)


def _swa_kernel(sink_ref, q_ref, k_ref, v_ref, o_ref, *, seq):
    lo = _lane_lo((1, LANES))
    nt = (((1,), (1,)), ((), ()))
    W = WINDOW
    qi = lax.broadcasted_iota(jnp.int32, (W, 2 * W), 0)
    kj = lax.broadcasted_iota(jnp.int32, (W, 2 * W), 1)
    band = (kj > qi) & (kj <= qi + W)

    n_tiles = SWA_WIDTH // LANES
    heads = [head for j in range(n_tiles) for head in (j, n_tiles + j)]

    def window(n):
        start = max(n - 1, 0) * W
        return slice(start, start + 2 * W), (band if n > 0 else kj <= qi)

    def score_products(n):
        kwin, _ = window(n)
        ks = k_ref[kwin, :]
        scores = []
        for j in range(n_tiles):
            q = q_ref[n * W:(n + 1) * W, j * LANES:(j + 1) * LANES]
            zero = jnp.zeros_like(q)
            scores += [lax.dot_general(qh, ks, nt, preferred_element_type=F32)
                       for qh in (jnp.where(lo, q, zero), jnp.where(lo, zero, q))]
        return scores

    scores = score_products(0)
    for n in range(seq // W):
        next_scores = score_products(n + 1) if (n + 1) * W < seq else None
        kwin, mask = window(n)
        vs = v_ref[kwin, :]
        probs, sums = [], []
        for head, s in zip(heads, scores):
            s = jnp.where(mask, s, NEG)
            sink = sink_ref[head]
            m = jnp.maximum(jnp.max(s, axis=-1, keepdims=True), sink)
            p = jnp.exp(s - m)
            sums.append(jnp.sum(p, axis=-1, keepdims=True) + jnp.exp(sink - m))
            probs.append(p.astype(BF16))
        outs = [jnp.dot(p, vs, preferred_element_type=F32) / l for p, l in zip(probs, sums)]
        for j in range(n_tiles):
            o_ref[n * W:(n + 1) * W, j * LANES:(j + 1) * LANES] = (
                jnp.where(lo, outs[2 * j], outs[2 * j + 1]).astype(BF16))
        scores = next_scores


def _swa_attention(sinks, q, k, v, *, seq):
    T = q.shape[0]
    B = T // seq
    return pl.pallas_call(
        functools.partial(_swa_kernel, seq=seq),
        grid=(B,),
        in_specs=[pl.BlockSpec(memory_space=pltpu.SMEM),
                  pl.BlockSpec((seq, SWA_WIDTH), lambda b: (b, 0)),
                  pl.BlockSpec((seq, KV_WIDTH), lambda b: (b, 0)),
                  pl.BlockSpec((seq, KV_WIDTH), lambda b: (b, 0))],
        out_specs=pl.BlockSpec((seq, SWA_WIDTH), lambda b: (b, 0)),
        out_shape=jax.ShapeDtypeStruct((T, SWA_WIDTH), BF16),
        compiler_params=_cparams(1),
        name="swa_attention",
    )(sinks, q, k, v)


def _conv_kernel(prev_ref, cur_ref, w_ref, g_ref, b_ref, o_ref, shift_ref, *, tile, sub):
    r = pl.program_id(1)
    rows = CONV_HALO + tile
    tail = prev_ref[tile - CONV_HALO:, :]
    shift_ref[0, 0:CONV_HALO, :] = jnp.where(r > 0, tail, jnp.zeros_like(tail))
    shift_ref[0, CONV_HALO:rows, :] = cur_ref[...]
    shift_ref[0, rows:rows + SUBLANES, :] = jnp.zeros((SUBLANES, CONV_CH), F32)
    for o in range(1, SUBLANES):
        shift_ref[o, 0:rows, :] = shift_ref[0, o:o + rows, :]
    w = w_ref[...]
    first = CONV_HALO - (CONV_WIDTH - 1)
    for s in range(tile // sub):
        acc = jnp.zeros((sub, CONV_CH), F32)
        for j in range(CONV_WIDTH):
            start = s * sub + first + j
            o = start % SUBLANES
            acc = acc + shift_ref[o, start - o:start - o + sub, :] * w[j:j + 1, :]
        y = _layernorm(acc, g_ref[...], b_ref[...])
        o_ref[s * sub:(s + 1) * sub, :] = (y * jax.nn.sigmoid(y)).astype(BF16)


def _conv_module(glu, w, g, b, *, seq):
    T = glu.shape[0]
    B = T // seq
    tile = min(CONV_TILE, seq)
    nr = seq // tile
    const = lambda *shape: pl.BlockSpec(shape, lambda bb, r: (0,) * len(shape))
    return pl.pallas_call(
        functools.partial(_conv_kernel, tile=tile, sub=64),
        grid=(B, nr),
        in_specs=[pl.BlockSpec((tile, CONV_CH), lambda bb, r: (bb * nr + jnp.maximum(r - 1, 0), 0)),
                  pl.BlockSpec((tile, CONV_CH), lambda bb, r: (bb * nr + r, 0)),
                  const(CONV_WIDTH, CONV_CH), const(1, CONV_CH), const(1, CONV_CH)],
        out_specs=pl.BlockSpec((tile, CONV_CH), lambda bb, r: (bb * nr + r, 0)),
        out_shape=jax.ShapeDtypeStruct((T, CONV_CH), BF16),
        scratch_shapes=[pltpu.VMEM((SUBLANES, CONV_HALO + tile + SUBLANES, CONV_CH), F32)],
        compiler_params=_cparams(2),
        name="conv_module",
    )(glu, glu, w, g, b)


def _slot_tables_kernel(route_ref, off_ref, disp_ref, comb_ref, *, tm, tc):
    rt = route_ref[...].T
    slots = []
    for k in range(2):
        expert = rt[k:k + 1, :]
        first = jnp.zeros_like(expert)
        for e in range(N_EXPERTS):
            first = jnp.where(expert == float(e), off_ref[0:1, e:e + 1], first)
        slots.append((first + rt[2 + k:3 + k, :]).astype(jnp.int32))
    for k in range(2):
        disp_ref[0, :, k * tm:(k + 1) * tm] = slots[k]
        for b in range(tm // tc):
            comb_ref[b, :, k * tc:(k + 1) * tc] = slots[k][:, b * tc:(b + 1) * tc]


def _slot_tables(route, offset, *, tm, tc):
    T = route.shape[0]
    off = jnp.pad(offset.astype(F32), (0, LANES - N_EXPERTS)).reshape(1, LANES)
    return pl.pallas_call(
        functools.partial(_slot_tables_kernel, tm=tm, tc=tc),
        grid=(T // tm,),
        in_specs=[pl.BlockSpec((tm, LANES), lambda i: (i, 0)), pl.BlockSpec((1, LANES), lambda i: (0, 0))],
        out_specs=[pl.BlockSpec((1, 1, 2 * tm), lambda i: (i, 0, 0)),
                   pl.BlockSpec((tm // tc, 1, 2 * tc), lambda i: (i, 0, 0))],
        out_shape=[jax.ShapeDtypeStruct((T // tm, 1, 2 * tm), jnp.int32),
                   jax.ShapeDtypeStruct((T // tc, 1, 2 * tc), jnp.int32)],
        compiler_params=_cparams(1),
        name="moe_slot_tables",
    )(route, off)


def _routing_tables(counts, n_tiles):
    cnt = counts[0, :N_EXPERTS].astype(jnp.int32)
    tiles = (cnt + EXPERT_TILE - 1) // EXPERT_TILE
    tile_end = jnp.cumsum(tiles)
    offset = (tile_end - tiles) * EXPERT_TILE
    n_used = tile_end[-1]
    tile_id = jnp.minimum(jnp.arange(n_tiles, dtype=jnp.int32), n_used - 1)
    tile_expert = jnp.sum((tile_end[None, :] <= tile_id[:, None]).astype(jnp.int32), axis=1)
    first_pad = offset + cnt
    n_single = (-first_pad) % SUBLANES
    pads = jnp.stack([first_pad, n_single, tiles * EXPERT_TILE - cnt - n_single,
                      jnp.broadcast_to(tile_end[-1], cnt.shape)])
    return offset, pads, tile_expert, n_used.reshape(1).astype(jnp.int32)


def _moe_ple(h, ya, yb, wo, layer, norm_ffn, wr, br, wg, wu, wd, p, wp, ple_norm, wgate):
    T = h.shape[0]
    n_tiles = (2 * T) // EXPERT_TILE + N_EXPERTS
    h1, m, route, counts = _outproj_router(h, ya, yb, wo, norm_ffn, wr, br)
    offset, pads, tile_expert, n_used = _routing_tables(counts, n_tiles)
    disp_slots, comb_slots = _slot_tables(route, offset, tm=min(DISPATCH_TILE, T), tc=min(GATHER_TILE, T))
    xs = _dispatch(pads, disp_slots, m, n_tiles * EXPERT_TILE)
    ys = _experts(layer, tile_expert, n_used, xs, wg, wu, wd)
    return _combine_ple(layer, comb_slots, route, h1, ys, p, wp, ple_norm, wgate)


def _router_weights(w_coarse, b_coarse, w_fine, b_fine):
    wf = w_fine.transpose(1, 0, 2).reshape(D_MODEL, N_EXPERTS)
    wr = jnp.concatenate([wf, w_coarse, jnp.zeros((D_MODEL, LANES - N_EXPERTS - N_GROUPS), F32)], axis=1)
    br = jnp.concatenate([b_fine.reshape(-1), b_coarse, jnp.zeros((LANES - N_EXPERTS - N_GROUPS,), F32)])
    w_hi = wr.astype(BF16)
    w_lo = (wr - w_hi.astype(F32)).astype(BF16)
    return jnp.concatenate([w_hi, w_lo], axis=1), br.reshape(1, LANES)


def _rope_tables(seq):
    half = ROT_DIM // 2
    inv_freq = ROPE_THETA ** (-jnp.arange(half, dtype=F32) * 2.0 / ROT_DIM)
    ang = jnp.arange(seq, dtype=F32)[:, None] * inv_freq[None, :]
    cos, sin = jnp.cos(ang), jnp.sin(ang)
    zeros = jnp.zeros((seq, HEAD_DIM - ROT_DIM), F32)
    z8 = jnp.zeros((seq, half), F32)
    cos_h = jnp.concatenate([cos, cos, zeros + 1.0], axis=1)
    slo_h = jnp.concatenate([-sin, z8, zeros], axis=1)
    shi_h = jnp.concatenate([z8, sin, zeros], axis=1)
    two = lambda t: jnp.concatenate([t, t], axis=1)
    return two(cos_h), two(slo_h), two(shi_h)


def kernel(x, p, norm_mix, even_w_in, fox_b_f, gmlp_ln_g, gmlp_ln_b, gmlp_w_s, gmlp_b_s, fox_q_norm, fox_k_norm, even_w_out, odd_w_in, swa_q_norm, swa_k_norm, swa_sinks, conv_w, conv_ln_g, conv_ln_b, odd_w_out, norm_ffn, moe_w_coarse, moe_b_coarse, moe_w_fine, moe_b_fine, moe_w_gate, moe_w_up, moe_w_down, ple_w_proj, ple_norm, ple_w_gate):
    B, S, D = x.shape
    T = B * S
    h = x.reshape(T, D)
    p = p.reshape(p.shape[0], T, D_PLE)
    row = lambda v: v.reshape(1, -1)

    def moe_args(i):
        wr, br = _router_weights(moe_w_coarse[i], moe_b_coarse[i], moe_w_fine[i], moe_b_fine[i])
        return (i, row(norm_ffn[i]), wr, br, moe_w_gate, moe_w_up, moe_w_down, p,
                ple_w_proj[i].astype(BF16), row(ple_norm[i]), ple_w_gate[i].astype(BF16))

    n_main = 2 * GMLP_WIDTH + 3 * FOX_WIDTH
    w_f = jnp.pad(jnp.tile(even_w_in[0][:, n_main:], (1, BIAS_PIECES)),
                  ((0, 0), (0, LANES - BIAS_PIECES * FOX_HEADS)))
    w_in = jnp.concatenate([even_w_in[0][:, :n_main], w_f], axis=1).astype(BF16)
    b_f = jnp.pad(jnp.tile(fox_b_f[0], BIAS_PIECES), (0, LANES - BIAS_PIECES * FOX_HEADS)).reshape(1, LANES)
    bs_full = jnp.repeat(gmlp_b_s[0].T, HEAD_DIM, axis=1)
    ya, q, k, v, xq, xk = _proj_even(
        h, row(norm_mix[0]), w_in, b_f, row(gmlp_ln_g[0]), row(gmlp_ln_b[0]), gmlp_w_s[0], bs_full,
        row(jnp.tile(fox_q_norm[0], FOX_HEADS)), row(jnp.tile(fox_k_norm[0], FOX_HEADS)), seq=S)
    yb = _fox_attention(q, xq, k, xk, v, seq=S)
    h = _moe_ple(h, ya, yb, even_w_out[0].astype(BF16), *moe_args(0))

    order = jnp.array([0, 4, 1, 5, 2, 6, 3, 7])
    cols = (order[:, None] * HEAD_DIM + jnp.arange(HEAD_DIM)[None, :]).reshape(-1)
    w_odd = jnp.concatenate([odd_w_in[0][:, :SWA_WIDTH][:, cols], odd_w_in[0][:, SWA_WIDTH:]], axis=1).astype(BF16)
    w_out_odd = jnp.concatenate([odd_w_out[0][:SWA_WIDTH][cols], odd_w_out[0][SWA_WIDTH:]], axis=0).astype(BF16)
    cos, slo, shi = _rope_tables(S)
    q, k, v, glu = _proj_odd(h, row(norm_mix[1]), w_odd, row(jnp.tile(swa_q_norm[0], 8)),
                             row(jnp.tile(swa_k_norm[0], 2)), cos, slo, shi, seq=S)
    yc = _swa_attention(swa_sinks[0], q, k, v, seq=S)
    yd = _conv_module(glu, conv_w[0], row(conv_ln_g[0]), row(conv_ln_b[0]), seq=S)
    h = _moe_ple(h, yc, yd, w_out_odd, *moe_args(1))
    return h.reshape(B, S, D)
```

```python
import functools

import jax
import jax.numpy as jnp
import numpy as np
from jax import lax
from jax.experimental import pallas as pl
from jax.experimental.pallas import tpu as pltpu

F32 = jnp.float32
BF16 = jnp.bfloat16

D_MODEL = 1024
HEAD_DIM = 64
LANES = 128
GMLP_WIDTH = 512
CHUNK = 128
FOX_WIDTH = 512
FOX_HEADS = 8
SWA_WIDTH = 512
KV_WIDTH = 128
WINDOW = 128
CONV_CH = 512
CONV_WIDTH = 31
CONV_HALO = 32
ROPE_THETA = 500000.0
ROT_DIM = 16
N_GROUPS = 4
EXPERTS_PER_GROUP = 8
N_EXPERTS = 32
D_EXPERT = 256
D_PLE = 256
EPS = 1e-6
NEG = -1e30
LOG2E = 1.4426950408889634
BIAS_PIECES = 3

EXPERT_TILE = 512
EXPERT_BUFFERS = 4
PAD_BITS = 9
SUBLANES = 8
SUBLANE_BITS = 3
ROW_TILE = 1024
ROUTER_BLOCKS = 8
GATHER_TILE = 512
COMBINE_BLOCKS = 4
DISPATCH_TILE = 1024
DMA_UNROLL = 16
ATTN_TILE = 256
FOX_Q_ROWS = 512
FOX_PAIRS_PER_STEP = 2
CONV_TILE = 512
VMEM_LIMIT = 56 * 1024 * 1024


def _cparams(n_axes=1):
    return pltpu.CompilerParams(dimension_semantics=("arbitrary",) * n_axes,
                                vmem_limit_bytes=VMEM_LIMIT)


def _rms(x, gain):
    return x * lax.rsqrt(jnp.mean(x * x, axis=-1, keepdims=True) + EPS) * gain


def _layernorm(x, g, b):
    mu = jnp.mean(x, axis=-1, keepdims=True)
    xc = x - mu
    var = jnp.mean(xc * xc, axis=-1, keepdims=True)
    return xc * lax.rsqrt(var + EPS) * g + b


def _head_rms(z, gain):
    lo = _lane_lo((1, LANES))
    outs = []
    for j in range(z.shape[1] // LANES):
        zj = z[:, j * LANES:(j + 1) * LANES]
        sq = zj * zj
        sum_lo = jnp.sum(jnp.where(lo, sq, 0.0), axis=-1, keepdims=True)
        sum_hi = jnp.sum(jnp.where(lo, 0.0, sq), axis=-1, keepdims=True)
        ms = jnp.where(lo, sum_lo, sum_hi) * (1.0 / HEAD_DIM)
        outs.append(zj * lax.rsqrt(ms + EPS))
    zn = outs[0] if len(outs) == 1 else jnp.concatenate(outs, axis=1)
    return zn * gain


def _lane_lo(shape):
    return (lax.broadcasted_iota(jnp.int32, shape, len(shape) - 1) % LANES) < HEAD_DIM


def _proj_even_kernel(h_ref, g_ref, w_ref, bf_ref, lng_ref, lnb_ref, ws_ref, bs_ref, qg_ref, kg_ref,
                      pq_ref, pk_ref, oneq_ref, onek_ref,
                      ya_ref, q_ref, k_ref, v_ref, xq_ref, xk_ref, carry_ref, *, tm, tiles_per_seq):
    i = pl.program_id(0)
    a = _rms(h_ref[...], g_ref[...]).astype(BF16)

    q0 = 2 * GMLP_WIDTH
    za = jnp.dot(a, w_ref[:, 0:q0], preferred_element_type=F32)
    zq = jnp.dot(a, w_ref[:, q0:q0 + FOX_WIDTH], preferred_element_type=F32)
    zk = jnp.dot(a, w_ref[:, q0 + FOX_WIDTH:q0 + 2 * FOX_WIDTH], preferred_element_type=F32)
    zv = jnp.dot(a, w_ref[:, q0 + 2 * FOX_WIDTH:q0 + 3 * FOX_WIDTH], preferred_element_type=F32)
    zf = jnp.dot(a, w_ref[:, q0 + 3 * FOX_WIDTH:], preferred_element_type=F32) + bf_ref[...]

    q_ref[...] = (_head_rms(zq, qg_ref[...]) * (LOG2E * HEAD_DIM ** -0.5)).astype(BF16)
    k_ref[...] = _head_rms(zk, kg_ref[...]).astype(BF16)
    v_ref[...] = zv.astype(BF16)

    rr = lax.broadcasted_iota(jnp.int32, (CHUNK, CHUNK), 0)
    cc = lax.broadcasted_iota(jnp.int32, (CHUNK, CHUNK), 1)

    ls = jnp.minimum(zf, 0.0) - jnp.log(1.0 + jnp.exp(-jnp.abs(zf)))

    @pl.when(i % tiles_per_seq == 0)
    def _():
        carry_ref[...] = jnp.zeros_like(carry_ref)

    def split3(x):
        hi = x.astype(BF16)
        r1 = x - hi.astype(F32)
        mid = r1.astype(BF16)
        return hi, mid, (r1 - mid.astype(F32)).astype(BF16)

    tri = jnp.where(rr >= cc, 1.0, 0.0).astype(BF16)
    pieces = jnp.concatenate(split3(ls), axis=1)
    running = carry_ref[...]
    blocks = []
    for b in range(tm // CHUNK):
        d = jnp.dot(tri, pieces[b * CHUNK:(b + 1) * CHUNK, :], preferred_element_type=F32)
        blk = (d[:, :LANES] + d[:, LANES:2 * LANES]) + d[:, 2 * LANES:] + running
        running = blk[CHUNK - 1:CHUNK, :]
        blocks.append(blk)
    carry_ref[...] = running
    c = jnp.concatenate(blocks, axis=0)

    za = jax.nn.gelu(za)
    u = za[:, :GMLP_WIDTH]
    vln = _layernorm(za[:, GMLP_WIDTH:], lng_ref[...], lnb_ref[...]).astype(BF16)
    lo = _lane_lo((CHUNK, LANES))
    for j in range(GMLP_WIDTH // LANES):
        w_a = jnp.where(rr >= cc, ws_ref[2 * j], 0.0).astype(BF16)
        w_b = jnp.where(rr >= cc, ws_ref[2 * j + 1], 0.0).astype(BF16)
        cols = slice(j * LANES, (j + 1) * LANES)
        for blk in range(tm // CHUNK):
            rows = slice(blk * CHUNK, (blk + 1) * CHUNK)
            vp = vln[rows, cols]
            mixed = jnp.where(lo, jnp.dot(w_a, vp, preferred_element_type=F32),
                              jnp.dot(w_b, vp, preferred_element_type=F32)) + bs_ref[:, cols]
            ya_ref[rows, cols] = (u[rows, cols] * mixed).astype(BF16)

    hi, mid, low = split3(c * LOG2E)
    group = lax.broadcasted_iota(jnp.int32, (tm, LANES), 1) // FOX_HEADS
    sel = jnp.where(group == 0, hi, jnp.where(group == 1, mid, low))
    xq_ref[...] = (jnp.dot(sel, pq_ref[...], preferred_element_type=F32) + oneq_ref[...]).astype(BF16)
    xk_ref[...] = (jnp.dot(sel, pk_ref[...], preferred_element_type=F32) + onek_ref[...]).astype(BF16)


def _proj_even(h, g, w, bf, lng, lnb, ws, bs_full, qg, kg, *, seq):
    T = h.shape[0]
    tm = min(ROW_TILE, seq)
    n_in = w.shape[1]
    const = lambda *shape: pl.BlockSpec(shape, lambda i: (0,) * len(shape))
    row = lambda width: pl.BlockSpec((tm, width), lambda i: (i, 0))
    tps = seq // tm
    return pl.pallas_call(
        functools.partial(_proj_even_kernel, tm=tm, tiles_per_seq=tps),
        grid=(T // tm,),
        in_specs=[row(D_MODEL), const(1, D_MODEL), const(D_MODEL, n_in), const(1, LANES),
                  const(1, GMLP_WIDTH), const(1, GMLP_WIDTH), const(8, CHUNK, CHUNK),
                  const(CHUNK, GMLP_WIDTH), const(1, FOX_WIDTH), const(1, FOX_WIDTH),
                  const(LANES, FOX_WIDTH), const(LANES, FOX_WIDTH),
                  const(1, FOX_WIDTH), const(1, FOX_WIDTH)],
        out_specs=[row(GMLP_WIDTH)] + [row(FOX_WIDTH)] * 5,
        out_shape=[jax.ShapeDtypeStruct((T, GMLP_WIDTH), BF16)]
                  + [jax.ShapeDtypeStruct((T, FOX_WIDTH), BF16)] * 5,
        scratch_shapes=[pltpu.VMEM((1, LANES), F32)],
        compiler_params=_cparams(1),
        name="proj_even",
    )(h, g, w, bf, lng, lnb, ws, bs_full, qg, kg, *_bias_placement())


def _bias_placement():
    pq = np.zeros((LANES, FOX_WIDTH), np.float32)
    pk = np.zeros((LANES, FOX_WIDTH), np.float32)
    oneq = np.zeros((1, FOX_WIDTH), np.float32)
    onek = np.zeros((1, FOX_WIDTH), np.float32)
    for head in range(FOX_HEADS):
        base = (head // 2) * LANES + (HEAD_DIM if head % 2 == 0 else 0)
        for piece in range(BIAS_PIECES):
            pq[piece * FOX_HEADS + head, base + piece] = 1.0
            onek[0, base + piece] = 1.0
            pk[piece * FOX_HEADS + head, base + BIAS_PIECES + piece] = -1.0
            oneq[0, base + BIAS_PIECES + piece] = 1.0
    return (jnp.asarray(pq, BF16), jnp.asarray(pk, BF16), jnp.asarray(oneq), jnp.asarray(onek))


def _fox_tile(i, q_ref, xq_ref, k_ref, xk_ref, v_ref, o_ref, *, tq, tk):
    part_rows = tk // 2
    lo = _lane_lo((1, LANES))
    rr = lax.broadcasted_iota(jnp.int32, (part_rows, tk), 0)
    cc = lax.broadcasted_iota(jnp.int32, (part_rows, tk), 1)
    nt = (((1,), (1,)), ((), ()))
    pairs = [slice(pp * LANES, (pp + 1) * LANES) for pp in range(FOX_PAIRS_PER_STEP)]
    first_row = [i * tq + part * part_rows for part in range(tq // part_rows)]
    n_tiles = [row // tk + 1 for row in first_row]
    diag_offset = [row % tk for row in first_row]
    chains = [(pp, part, head) for pp in range(len(pairs)) for part in range(len(first_row)) for head in range(2)]
    q_aug = {}
    for pp, cols in enumerate(pairs):
        for part in range(len(first_row)):
            rows = slice(part * part_rows, (part + 1) * part_rows)
            q, xq = q_ref[rows, cols], xq_ref[rows, cols]
            q_aug[pp, part, 0], q_aug[pp, part, 1] = jnp.where(lo, q, xq), jnp.where(lo, xq, q)

    def score_products(j):
        rows = slice(j * tk, (j + 1) * tk)
        k_aug = {}
        for pp, cols in enumerate(pairs):
            ks, xk = k_ref[rows, cols], xk_ref[rows, cols]
            k_aug[pp, 0], k_aug[pp, 1] = jnp.where(lo, ks, xk), jnp.where(lo, xk, ks)
        return {(pp, part, head): lax.dot_general(q_aug[pp, part, head], k_aug[pp, head], nt,
                                                  preferred_element_type=F32)
                for pp, part, head in chains if j < n_tiles[part]}

    maxes = {c: jnp.full((part_rows, 1), NEG, F32) for c in chains}
    accs = {c: jnp.zeros((part_rows, LANES), F32) for c in chains}
    scores = score_products(0)
    for j in range(max(n_tiles)):
        next_scores = score_products(j + 1) if j + 1 < max(n_tiles) else {}
        v_aug = {}
        for pp, cols in enumerate(pairs):
            vs = v_ref[j * tk:(j + 1) * tk, cols]
            one = jnp.ones_like(vs)
            v_aug[pp, 0], v_aug[pp, 1] = jnp.where(lo, vs, one), jnp.where(lo, one, vs)
        for c, s in scores.items():
            pp, part, head = c
            if j == n_tiles[part] - 1:
                s = jnp.where(cc <= rr + diag_offset[part], s, NEG)
            n = jnp.maximum(maxes[c], jnp.max(s, axis=-1, keepdims=True))
            p = jnp.exp2(s - n).astype(BF16)
            accs[c] = accs[c] * jnp.exp2(maxes[c] - n) + jnp.dot(p, v_aug[pp, head], preferred_element_type=F32)
            maxes[c] = n
        scores = next_scores
    for pp, cols in enumerate(pairs):
        for part in range(len(first_row)):
            norm = [accs[pp, part, head] / pltpu.roll(accs[pp, part, head], HEAD_DIM, 1) for head in range(2)]
            o_ref[part * part_rows:(part + 1) * part_rows, cols] = jnp.where(lo, norm[0], norm[1]).astype(BF16)


def _fox_kernel(q_ref, xq_ref, k_ref, xk_ref, v_ref, o_ref, *, tq, tk, nq):
    i = pl.program_id(2)
    for c in range(nq):
        pl.when(i == c)(functools.partial(_fox_tile, c, q_ref, xq_ref, k_ref, xk_ref, v_ref, o_ref, tq=tq, tk=tk))


def _fox_attention(q, xq, k, xk, v, *, seq):
    T = q.shape[0]
    B = T // seq
    tk = min(ATTN_TILE, seq)
    tq = min(FOX_Q_ROWS, seq)
    nq = seq // tq
    width = FOX_PAIRS_PER_STEP * LANES
    tile = pl.BlockSpec((tq, width), lambda b, hp, i: (b * nq + i, hp))
    whole = pl.BlockSpec((seq, width), lambda b, hp, i: (b, hp))
    return pl.pallas_call(
        functools.partial(_fox_kernel, tq=tq, tk=tk, nq=nq),
        grid=(B, FOX_WIDTH // width, nq),
        in_specs=[tile, tile, whole, whole, whole],
        out_specs=tile,
        out_shape=jax.ShapeDtypeStruct((T, FOX_WIDTH), BF16),
        compiler_params=_cparams(3),
        name="fox_attention",
    )(q, xq, k, xk, v)


def _outproj_router_kernel(h_ref, ya_ref, yb_ref, wo_ref, g_ref, wr_ref, br_ref,
                           h1_ref, m_ref, route_ref, cnt_ref, carry_ref, *, tm):
    i = pl.program_id(0)
    half = wo_ref.shape[0] // 2
    nb = ROUTER_BLOCKS
    rb = tm // nb
    blocks = [slice(b * rb, (b + 1) * rb) for b in range(nb)]

    @pl.when(i == 0)
    def _():
        carry_ref[...] = jnp.zeros_like(carry_ref)

    mixes = [jnp.dot(ya_ref[rows, :], wo_ref[0:half, :], preferred_element_type=F32)
             + jnp.dot(yb_ref[rows, :], wo_ref[half:, :], preferred_element_type=F32) for rows in blocks]
    ms = []
    for rows, mix in zip(blocks, mixes):
        h1 = h_ref[rows, :] + mix
        h1_ref[rows, :] = h1
        m = _rms(h1, g_ref[...])
        m_ref[rows, :] = m
        ms.append(m)

    logits = []
    for m in ms:
        m_hi = m.astype(BF16)
        m_lo = (m - m_hi.astype(F32)).astype(BF16)
        hh = jnp.dot(m_hi, wr_ref[...], preferred_element_type=F32)
        lh = jnp.dot(m_lo, wr_ref[:, :LANES], preferred_element_type=F32)
        logits.append(hh[:, :LANES] + (hh[:, LANES:] + lh) + br_ref[...])

    lane_i = lax.broadcasted_iota(jnp.int32, (rb, LANES), 1)
    lane = lane_i.astype(F32)
    group_of_lane = (lane_i // EXPERTS_PER_GROUP).astype(F32)
    is_coarse = (lane_i >= N_EXPERTS) & (lane_i < N_EXPERTS + N_GROUPS)
    picks = []
    for lg in logits:
        coarse = jnp.where(is_coarse, lg, NEG)
        cmax = jnp.max(coarse, axis=-1, keepdims=True)
        gidx = jnp.min(jnp.where(coarse == cmax, lane - N_EXPERTS, float(LANES)), axis=-1, keepdims=True)
        p_g = 1.0 / jnp.sum(jnp.where(is_coarse, jnp.exp(coarse - cmax), 0.0), axis=-1, keepdims=True)
        in_group = (lane_i < N_EXPERTS) & (group_of_lane == gidx)
        fine = jnp.where(in_group, lg, NEG)
        v1 = jnp.max(fine, axis=-1, keepdims=True)
        i1 = jnp.min(jnp.where(fine == v1, lane, float(LANES)), axis=-1, keepdims=True)
        fine2 = jnp.where(lane == i1, NEG, fine)
        v2 = jnp.max(fine2, axis=-1, keepdims=True)
        i2 = jnp.min(jnp.where(fine2 == v2, lane, float(LANES)), axis=-1, keepdims=True)
        e2 = jnp.exp(v2 - v1)
        picks.append((i1, i2, p_g / (1.0 + e2), p_g * e2 / (1.0 + e2)))

    tr = lax.broadcasted_iota(jnp.int32, (rb, rb), 0)
    tc = lax.broadcasted_iota(jnp.int32, (rb, rb), 1)
    strict = jnp.where(tr > tc, 1.0, 0.0).astype(BF16)
    onehots = [jnp.where((lane == i1) | (lane == i2), 1.0, 0.0).astype(F32) for i1, i2, _, _ in picks]
    befores = [jnp.dot(strict, oh.astype(BF16), preferred_element_type=F32) for oh in onehots]
    total = carry_ref[...]
    for rows, (i1, i2, w1, w2), oh, before in zip(blocks, picks, onehots, befores):
        before = before + total
        r1 = jnp.sum(jnp.where(lane == i1, before, 0.0), axis=-1, keepdims=True)
        r2 = jnp.sum(jnp.where(lane == i2, before, 0.0), axis=-1, keepdims=True)
        total = total + jnp.sum(oh, axis=0, keepdims=True)
        route = jnp.where(lane == 0, i1, 0.0)
        route = jnp.where(lane == 1, i2, route)
        route = jnp.where(lane == 2, r1, route)
        route = jnp.where(lane == 3, r2, route)
        route = jnp.where(lane == 4, w1, route)
        route = jnp.where(lane == 5, w2, route)
        route_ref[rows, :] = route
    carry_ref[...] = total
    cnt_ref[...] = jnp.broadcast_to(total, cnt_ref.shape)


def _outproj_router(h, ya, yb, wo, g, wr, br):
    T = h.shape[0]
    tm = min(ROW_TILE, T)
    const = lambda *shape: pl.BlockSpec(shape, lambda i: (0,) * len(shape))
    row = lambda width: pl.BlockSpec((tm, width), lambda i: (i, 0))
    return pl.pallas_call(
        functools.partial(_outproj_router_kernel, tm=tm),
        grid=(T // tm,),
        in_specs=[row(D_MODEL), row(ya.shape[1]), row(yb.shape[1]), const(*wo.shape),
                  const(1, D_MODEL), const(D_MODEL, 2 * LANES), const(1, LANES)],
        out_specs=[row(D_MODEL), row(D_MODEL), row(LANES), const(8, LANES)],
        out_shape=[jax.ShapeDtypeStruct((T, D_MODEL), F32),
                   jax.ShapeDtypeStruct((T, D_MODEL), F32),
                   jax.ShapeDtypeStruct((T, LANES), F32),
                   jax.ShapeDtypeStruct((8, LANES), F32)],
        scratch_shapes=[pltpu.VMEM((1, LANES), F32)],
        compiler_params=_cparams(1),
        name="outproj_router",
    )(h, ya, yb, wo, g, wr, br)


def _dispatch_kernel(pad_ref, pos_ref, m_ref, xs_hbm, zeros_ref, sem, pad_sem, *, tile):
    @pl.when(pl.program_id(0) == 0)
    def _():
        zeros_ref[...] = jnp.zeros_like(zeros_ref)

        def pad_copies(e, wait):
            first, n_single, n_block = pad_ref[0, e], pad_ref[1, e], pad_ref[2, e]
            for r in range(SUBLANES - 1):
                copy = pltpu.make_async_copy(zeros_ref.at[pl.ds(0, 1)], xs_hbm.at[pl.ds(first + r, 1)], pad_sem)
                pl.when(r < n_single)(copy.wait if wait else copy.start)
            done = first + n_single
            for bit in reversed(range(SUBLANE_BITS, PAD_BITS)):
                size = 1 << bit
                taken = (n_block & size) != 0
                copy = pltpu.make_async_copy(zeros_ref.at[pl.ds(0, size)],
                                             xs_hbm.at[pl.ds(pl.multiple_of(done, SUBLANES), size)], pad_sem)
                pl.when(taken)(copy.wait if wait else copy.start)
                done = done + jnp.where(taken, size, 0)

        def start(e, carry):
            pad_copies(e, False)
            return carry

        def finish(e, carry):
            pad_copies(e, True)
            return carry

        def tile_copy(j):
            return pltpu.make_async_copy(
                zeros_ref, xs_hbm.at[pl.ds(pl.multiple_of(j * EXPERT_TILE, EXPERT_TILE), EXPERT_TILE)], pad_sem)

        def start_tile(j, carry):
            tile_copy(j).start()
            return carry

        def finish_tile(j, carry):
            tile_copy(j).wait()
            return carry

        n_tiles = xs_hbm.shape[0] // EXPERT_TILE
        lax.fori_loop(0, N_EXPERTS, start, 0)
        lax.fori_loop(pad_ref[3, 0], n_tiles, start_tile, 0)
        lax.fori_loop(0, N_EXPERTS, finish, 0)
        lax.fori_loop(pad_ref[3, 0], n_tiles, finish_tile, 0)

    def issue(c, carry):
        base = pl.multiple_of(c * DMA_UNROLL, DMA_UNROLL)
        group = m_ref.at[pl.ds(base, DMA_UNROLL)]
        for u in range(DMA_UNROLL):
            for k in range(2):
                dst = xs_hbm.at[pl.ds(pos_ref[0, 0, k * tile + base + u], 1)]
                pltpu.make_async_copy(group.at[pl.ds(u, 1)], dst, sem).start(priority=k)
        return carry

    lax.fori_loop(0, tile // DMA_UNROLL, issue, 0)
    for _ in range(2):
        pltpu.make_async_copy(m_ref, xs_hbm.at[pl.ds(0, tile)], sem).wait()


def _dispatch(pads, pos3, m, n_rows):
    T, width = m.shape
    tile = pos3.shape[2] // 2
    grid_spec = pltpu.PrefetchScalarGridSpec(
        num_scalar_prefetch=1,
        grid=(T // tile,),
        in_specs=[pl.BlockSpec((1, 1, 2 * tile), lambda i, pads: (i, 0, 0), memory_space=pltpu.SMEM),
                  pl.BlockSpec((tile, width), lambda i, pads: (i, 0))],
        out_specs=pl.BlockSpec(memory_space=pl.ANY),
        scratch_shapes=[pltpu.VMEM((EXPERT_TILE, width), m.dtype),
                        pltpu.SemaphoreType.DMA(()), pltpu.SemaphoreType.DMA(())],
    )
    return pl.pallas_call(
        functools.partial(_dispatch_kernel, tile=tile),
        grid_spec=grid_spec,
        out_shape=jax.ShapeDtypeStruct((n_rows, width), m.dtype),
        compiler_params=pltpu.CompilerParams(dimension_semantics=("arbitrary",),
                                             has_side_effects=True, vmem_limit_bytes=VMEM_LIMIT),
        name="moe_dispatch",
    )(pads, pos3, m)


def _experts_kernel(te_ref, nused_ref, xs_hbm, wg_ref, wu_ref, wd_ref, ys_ref, xbuf, sems, wgu_b, wd_b):
    j = pl.program_id(0)
    n_used = nused_ref[0]
    prev = te_ref[jnp.maximum(j - 1, 0)]

    def tile_copy(t):
        start = t * EXPERT_TILE
        rows = pl.ds(start if isinstance(t, int) else pl.multiple_of(start, EXPERT_TILE), EXPERT_TILE)
        return pltpu.make_async_copy(xs_hbm.at[rows], xbuf.at[t % EXPERT_BUFFERS], sems.at[t % EXPERT_BUFFERS])

    @pl.when(j == 0)
    def _():
        for t in range(EXPERT_BUFFERS - 1):
            pl.when(t < n_used)(lambda t=t: tile_copy(t).start())

    @pl.when(j + EXPERT_BUFFERS - 1 < n_used)
    def _():
        tile_copy(j + EXPERT_BUFFERS - 1).start()

    @pl.when((j == 0) | (te_ref[j] != prev))
    def _():
        wgu_b[:, :D_EXPERT] = wg_ref[...].astype(BF16)
        wgu_b[:, D_EXPERT:] = wu_ref[...].astype(BF16)
        wd_b[...] = wd_ref[...].astype(BF16)

    @pl.when(j < n_used)
    def _():
        tile_copy(j).wait()
        gu = jnp.dot(xbuf[j % EXPERT_BUFFERS].astype(BF16), wgu_b[...], preferred_element_type=F32)
        g = gu[:, :D_EXPERT]
        act = g * jax.nn.sigmoid(g) * gu[:, D_EXPERT:]
        ys_ref[...] = jnp.dot(act.astype(BF16), wd_b[...], preferred_element_type=F32)

    @pl.when(j >= n_used)
    def _():
        ys_ref[...] = jnp.zeros_like(ys_ref)


def _experts(layer, tile_expert, n_used, xs, wg, wu, wd):
    n_rows = xs.shape[0]
    nt = n_rows // EXPERT_TILE
    grid_spec = pltpu.PrefetchScalarGridSpec(
        num_scalar_prefetch=2,
        grid=(nt,),
        in_specs=[pl.BlockSpec(memory_space=pl.ANY),
                  pl.BlockSpec((None, None, D_MODEL, D_EXPERT), lambda j, te, nu: (layer, te[j], 0, 0)),
                  pl.BlockSpec((None, None, D_MODEL, D_EXPERT), lambda j, te, nu: (layer, te[j], 0, 0)),
                  pl.BlockSpec((None, None, D_EXPERT, D_MODEL), lambda j, te, nu: (layer, te[j], 0, 0))],
        out_specs=pl.BlockSpec((EXPERT_TILE, D_MODEL), lambda j, te, nu: (j, 0)),
        scratch_shapes=[pltpu.VMEM((EXPERT_BUFFERS, EXPERT_TILE, D_MODEL), F32),
                        pltpu.SemaphoreType.DMA((EXPERT_BUFFERS,)),
                        pltpu.VMEM((D_MODEL, 2 * D_EXPERT), BF16),
                        pltpu.VMEM((D_EXPERT, D_MODEL), BF16)],
    )
    return pl.pallas_call(
        _experts_kernel,
        grid_spec=grid_spec,
        out_shape=jax.ShapeDtypeStruct((n_rows, D_MODEL), F32),
        compiler_params=_cparams(1),
        name="moe_experts",
    )(tile_expert, n_used, xs, wg, wu, wd)


def _combine_ple_kernel(pos_ref, next_pos_ref, route_ref, h1_ref, ys_hbm, p_ref, wp_ref, g_ref, wgate_ref,
                        o_ref, ybuf, sems, *, tile):
    i = pl.program_id(0)
    n = pl.num_programs(0)
    slot = i % 2

    def gather(table, s):
        def issue(c, carry):
            base = pl.multiple_of(c * DMA_UNROLL, DMA_UNROLL)
            for k in range(2):
                group = ybuf.at[s, k, pl.ds(base, DMA_UNROLL)]
                for u in range(DMA_UNROLL):
                    pltpu.make_async_copy(ys_hbm.at[pl.ds(table[0, 0, k * tile + base + u], 1)],
                                          group.at[pl.ds(u, 1)], sems.at[s]).start(priority=k)
            return carry

        lax.fori_loop(0, tile // DMA_UNROLL, issue, 0)

    @pl.when(i == 0)
    def _():
        gather(pos_ref, 0)

    @pl.when(i + 1 < n)
    def _():
        gather(next_pos_ref, 1 - slot)

    rb = tile // COMBINE_BLOCKS
    blocks = [slice(b * rb, (b + 1) * rb) for b in range(COMBINE_BLOCKS)]
    p_b = p_ref[...].astype(BF16)
    ples = [jnp.dot(p_b[rows], wp_ref[...], preferred_element_type=F32) for rows in blocks]
    for k in range(2):
        pltpu.make_async_copy(ys_hbm.at[pl.ds(0, tile)], ybuf.at[slot, k], sems.at[slot]).wait()

    route = route_ref[...]
    h2s = [h1_ref[rows, :] + route[rows, 4:5] * ybuf[slot, 0, rows, :] + route[rows, 5:6] * ybuf[slot, 1, rows, :]
           for rows in blocks]
    normed = [_rms(h2, g_ref[...]).astype(BF16) for h2 in h2s]
    gates = [jnp.dot(x, wgate_ref[...], preferred_element_type=F32) for x in normed]
    for rows, h2, gate, ple in zip(blocks, h2s, gates, ples):
        o_ref[rows, :] = h2 + jax.nn.sigmoid(gate) * ple


def _combine_ple(layer, pos3, route, h1, ys, p, wp, g, wgate):
    T = h1.shape[0]
    n, _, width = pos3.shape
    tile = width // 2
    const = lambda *shape: pl.BlockSpec(shape, lambda i: (0,) * len(shape))
    row = lambda width: pl.BlockSpec((tile, width), lambda i: (i, 0))
    return pl.pallas_call(
        functools.partial(_combine_ple_kernel, tile=tile),
        grid=(n,),
        in_specs=[pl.BlockSpec((1, 1, 2 * tile), lambda i: (0, 0, 0), memory_space=pltpu.SMEM),
                  pl.BlockSpec((1, 1, 2 * tile), lambda i: (jnp.minimum(i + 1, n - 1), 0, 0),
                               memory_space=pltpu.SMEM),
                  row(LANES), row(D_MODEL), pl.BlockSpec(memory_space=pl.ANY),
                  pl.BlockSpec((None, tile, D_PLE), lambda i: (layer, i, 0)),
                  const(D_PLE, D_MODEL), const(1, D_MODEL), const(D_MODEL, D_MODEL)],
        out_specs=row(D_MODEL),
        out_shape=jax.ShapeDtypeStruct((T, D_MODEL), F32),
        scratch_shapes=[pltpu.VMEM((2, 2, tile, D_MODEL), F32), pltpu.SemaphoreType.DMA((2,))],
        compiler_params=_cparams(1),
        name="combine_ple",
    )(pos3, pos3, route, h1, ys, p, wp, g, wgate)


def _rope(z, cos, sin_lo, sin_hi):
    half = ROT_DIM // 2
    outs = []
    for j in range(z.shape[1] // LANES):
        zj = z[:, j * LANES:(j + 1) * LANES]
        outs.append(zj * cos + pltpu.roll(zj, LANES - half, 1) * sin_lo + pltpu.roll(zj, half, 1) * sin_hi)
    return outs[0] if len(outs) == 1 else jnp.concatenate(outs, axis=1)


def _proj_odd_kernel(h_ref, g_ref, w_ref, qg_ref, kg_ref, cos_ref, slo_ref, shi_ref,
                     q_ref, k_ref, v_ref, glu_ref):
    a = _rms(h_ref[...], g_ref[...]).astype(BF16)
    cos, slo, shi = cos_ref[...], slo_ref[...], shi_ref[...]
    v0 = SWA_WIDTH + KV_WIDTH
    d0 = v0 + KV_WIDTH
    zq = jnp.dot(a, w_ref[:, 0:SWA_WIDTH], preferred_element_type=F32)
    zk = jnp.dot(a, w_ref[:, SWA_WIDTH:v0], preferred_element_type=F32)
    zv = jnp.dot(a, w_ref[:, v0:d0], preferred_element_type=F32)
    zd = jnp.dot(a, w_ref[:, d0:d0 + 2 * CONV_CH], preferred_element_type=F32)
    q_ref[...] = (_rope(_head_rms(zq, qg_ref[...]), cos, slo, shi) * (HEAD_DIM ** -0.5)).astype(BF16)
    k_ref[...] = _rope(_head_rms(zk, kg_ref[...]), cos, slo, shi).astype(BF16)
    v_ref[...] = zv.astype(BF16)
    glu_ref[...] = zd[:, :CONV_CH] * jax.nn.sigmoid(zd[:, CONV_CH:])


def _proj_odd(h, g, w, qg, kg, cos, slo, shi, *, seq):
    T = h.shape[0]
    tm = min(ROW_TILE, seq)
    tps = seq // tm
    const = lambda *shape: pl.BlockSpec(shape, lambda i: (0,) * len(shape))
    row = lambda width: pl.BlockSpec((tm, width), lambda i: (i, 0))
    tab = pl.BlockSpec((tm, LANES), lambda i: (i % tps, 0))
    return pl.pallas_call(
        _proj_odd_kernel,
        grid=(T // tm,),
        in_specs=[row(D_MODEL), const(1, D_MODEL), const(*w.shape), const(1, SWA_WIDTH),
                  const(1, KV_WIDTH), tab, tab, tab],
        out_specs=[row(SWA_WIDTH), row(KV_WIDTH), row(KV_WIDTH), row(CONV_CH)],
        out_shape=[jax.ShapeDtypeStruct((T, SWA_WIDTH), BF16),
                   jax.ShapeDtypeStruct((T, KV_WIDTH), BF16),
                   jax.ShapeDtypeStruct((T, KV_WIDTH), BF16),
                   jax.ShapeDtypeStruct((T, CONV_CH), F32)],
        compiler_params=_cparams(1),
        name="proj_odd",
    )(h, g, w, qg, kg, cos, slo, shi)


def _swa_kernel(sink_ref, q_ref, k_ref, v_ref, o_ref, *, seq):
    lo = _lane_lo((1, LANES))
    nt = (((1,), (1,)), ((), ()))
    W = WINDOW
    qi = lax.broadcasted_iota(jnp.int32, (W, 2 * W), 0)
    kj = lax.broadcasted_iota(jnp.int32, (W, 2 * W), 1)
    band = (kj > qi) & (kj <= qi + W)

    n_tiles = SWA_WIDTH // LANES
    heads = [head for j in range(n_tiles) for head in (j, n_tiles + j)]

    def window(n):
        start = max(n - 1, 0) * W
        return slice(start, start + 2 * W), (band if n > 0 else kj <= qi)

    def score_products(n):
        kwin, _ = window(n)
        ks = k_ref[kwin, :]
        scores = []
        for j in range(n_tiles):
            q = q_ref[n * W:(n + 1) * W, j * LANES:(j + 1) * LANES]
            zero = jnp.zeros_like(q)
            scores += [lax.dot_general(qh, ks, nt, preferred_element_type=F32)
                       for qh in (jnp.where(lo, q, zero), jnp.where(lo, zero, q))]
        return scores

    scores = score_products(0)
    for n in range(seq // W):
        next_scores = score_products(n + 1) if (n + 1) * W < seq else None
        kwin, mask = window(n)
        vs = v_ref[kwin, :]
        probs, sums = [], []
        for head, s in zip(heads, scores):
            s = jnp.where(mask, s, NEG)
            sink = sink_ref[head]
            m = jnp.maximum(jnp.max(s, axis=-1, keepdims=True), sink)
            p = jnp.exp(s - m)
            sums.append(jnp.sum(p, axis=-1, keepdims=True) + jnp.exp(sink - m))
            probs.append(p.astype(BF16))
        outs = [jnp.dot(p, vs, preferred_element_type=F32) / l for p, l in zip(probs, sums)]
        for j in range(n_tiles):
            o_ref[n * W:(n + 1) * W, j * LANES:(j + 1) * LANES] = (
                jnp.where(lo, outs[2 * j], outs[2 * j + 1]).astype(BF16))
        scores = next_scores


def _swa_attention(sinks, q, k, v, *, seq):
    T = q.shape[0]
    B = T // seq
    return pl.pallas_call(
        functools.partial(_swa_kernel, seq=seq),
        grid=(B,),
        in_specs=[pl.BlockSpec(memory_space=pltpu.SMEM),
                  pl.BlockSpec((seq, SWA_WIDTH), lambda b: (b, 0)),
                  pl.BlockSpec((seq, KV_WIDTH), lambda b: (b, 0)),
                  pl.BlockSpec((seq, KV_WIDTH), lambda b: (b, 0))],
        out_specs=pl.BlockSpec((seq, SWA_WIDTH), lambda b: (b, 0)),
        out_shape=jax.ShapeDtypeStruct((T, SWA_WIDTH), BF16),
        compiler_params=_cparams(1),
        name="swa_attention",
    )(sinks, q, k, v)


def _conv_kernel(prev_ref, cur_ref, w_ref, g_ref, b_ref, o_ref, shift_ref, *, tile, sub):
    r = pl.program_id(1)
    rows = CONV_HALO + tile
    tail = prev_ref[tile - CONV_HALO:, :]
    shift_ref[0, 0:CONV_HALO, :] = jnp.where(r > 0, tail, jnp.zeros_like(tail))
    shift_ref[0, CONV_HALO:rows, :] = cur_ref[...]
    shift_ref[0, rows:rows + SUBLANES, :] = jnp.zeros((SUBLANES, CONV_CH), F32)
    for o in range(1, SUBLANES):
        shift_ref[o, 0:rows, :] = shift_ref[0, o:o + rows, :]
    w = w_ref[...]
    first = CONV_HALO - (CONV_WIDTH - 1)
    for s in range(tile // sub):
        acc = jnp.zeros((sub, CONV_CH), F32)
        for j in range(CONV_WIDTH):
            start = s * sub + first + j
            o = start % SUBLANES
            acc = acc + shift_ref[o, start - o:start - o + sub, :] * w[j:j + 1, :]
        y = _layernorm(acc, g_ref[...], b_ref[...])
        o_ref[s * sub:(s + 1) * sub, :] = (y * jax.nn.sigmoid(y)).astype(BF16)


def _conv_module(glu, w, g, b, *, seq):
    T = glu.shape[0]
    B = T // seq
    tile = min(CONV_TILE, seq)
    nr = seq // tile
    const = lambda *shape: pl.BlockSpec(shape, lambda bb, r: (0,) * len(shape))
    return pl.pallas_call(
        functools.partial(_conv_kernel, tile=tile, sub=64),
        grid=(B, nr),
        in_specs=[pl.BlockSpec((tile, CONV_CH), lambda bb, r: (bb * nr + jnp.maximum(r - 1, 0), 0)),
                  pl.BlockSpec((tile, CONV_CH), lambda bb, r: (bb * nr + r, 0)),
                  const(CONV_WIDTH, CONV_CH), const(1, CONV_CH), const(1, CONV_CH)],
        out_specs=pl.BlockSpec((tile, CONV_CH), lambda bb, r: (bb * nr + r, 0)),
        out_shape=jax.ShapeDtypeStruct((T, CONV_CH), BF16),
        scratch_shapes=[pltpu.VMEM((SUBLANES, CONV_HALO + tile + SUBLANES, CONV_CH), F32)],
        compiler_params=_cparams(2),
        name="conv_module",
    )(glu, glu, w, g, b)


def _slot_tables_kernel(route_ref, off_ref, disp_ref, comb_ref, *, tm, tc):
    rt = route_ref[...].T
    slots = []
    for k in range(2):
        expert = rt[k:k + 1, :]
        first = jnp.zeros_like(expert)
        for e in range(N_EXPERTS):
            first = jnp.where(expert == float(e), off_ref[0:1, e:e + 1], first)
        slots.append((first + rt[2 + k:3 + k, :]).astype(jnp.int32))
    for k in range(2):
        disp_ref[0, :, k * tm:(k + 1) * tm] = slots[k]
        for b in range(tm // tc):
            comb_ref[b, :, k * tc:(k + 1) * tc] = slots[k][:, b * tc:(b + 1) * tc]


def _slot_tables(route, offset, *, tm, tc):
    T = route.shape[0]
    off = jnp.pad(offset.astype(F32), (0, LANES - N_EXPERTS)).reshape(1, LANES)
    return pl.pallas_call(
        functools.partial(_slot_tables_kernel, tm=tm, tc=tc),
        grid=(T // tm,),
        in_specs=[pl.BlockSpec((tm, LANES), lambda i: (i, 0)), pl.BlockSpec((1, LANES), lambda i: (0, 0))],
        out_specs=[pl.BlockSpec((1, 1, 2 * tm), lambda i: (i, 0, 0)),
                   pl.BlockSpec((tm // tc, 1, 2 * tc), lambda i: (i, 0, 0))],
        out_shape=[jax.ShapeDtypeStruct((T // tm, 1, 2 * tm), jnp.int32),
                   jax.ShapeDtypeStruct((T // tc, 1, 2 * tc), jnp.int32)],
        compiler_params=_cparams(1),
        name="moe_slot_tables",
    )(route, off)


def _routing_tables(counts, n_tiles):
    cnt = counts[0, :N_EXPERTS].astype(jnp.int32)
    tiles = (cnt + EXPERT_TILE - 1) // EXPERT_TILE
    tile_end = jnp.cumsum(tiles)
    offset = (tile_end - tiles) * EXPERT_TILE
    n_used = tile_end[-1]
    tile_id = jnp.minimum(jnp.arange(n_tiles, dtype=jnp.int32), n_used - 1)
    tile_expert = jnp.sum((tile_end[None, :] <= tile_id[:, None]).astype(jnp.int32), axis=1)
    first_pad = offset + cnt
    n_single = (-first_pad) % SUBLANES
    pads = jnp.stack([first_pad, n_single, tiles * EXPERT_TILE - cnt - n_single,
                      jnp.broadcast_to(tile_end[-1], cnt.shape)])
    return offset, pads, tile_expert, n_used.reshape(1).astype(jnp.int32)


def _moe_ple(h, ya, yb, wo, layer, norm_ffn, wr, br, wg, wu, wd, p, wp, ple_norm, wgate):
    T = h.shape[0]
    n_tiles = (2 * T) // EXPERT_TILE + N_EXPERTS
    h1, m, route, counts = _outproj_router(h, ya, yb, wo, norm_ffn, wr, br)
    offset, pads, tile_expert, n_used = _routing_tables(counts, n_tiles)
    disp_slots, comb_slots = _slot_tables(route, offset, tm=min(DISPATCH_TILE, T), tc=min(GATHER_TILE, T))
    xs = _dispatch(pads, disp_slots, m, n_tiles * EXPERT_TILE)
    ys = _experts(layer, tile_expert, n_used, xs, wg, wu, wd)
    return _combine_ple(layer, comb_slots, route, h1, ys, p, wp, ple_norm, wgate)


def _router_weights(w_coarse, b_coarse, w_fine, b_fine):
    wf = w_fine.transpose(1, 0, 2).reshape(D_MODEL, N_EXPERTS)
    wr = jnp.concatenate([wf, w_coarse, jnp.zeros((D_MODEL, LANES - N_EXPERTS - N_GROUPS), F32)], axis=1)
    br = jnp.concatenate([b_fine.reshape(-1), b_coarse, jnp.zeros((LANES - N_EXPERTS - N_GROUPS,), F32)])
    w_hi = wr.astype(BF16)
    w_lo = (wr - w_hi.astype(F32)).astype(BF16)
    return jnp.concatenate([w_hi, w_lo], axis=1), br.reshape(1, LANES)


def _rope_tables(seq):
    half = ROT_DIM // 2
    inv_freq = ROPE_THETA ** (-jnp.arange(half, dtype=F32) * 2.0 / ROT_DIM)
    ang = jnp.arange(seq, dtype=F32)[:, None] * inv_freq[None, :]
    cos, sin = jnp.cos(ang), jnp.sin(ang)
    zeros = jnp.zeros((seq, HEAD_DIM - ROT_DIM), F32)
    z8 = jnp.zeros((seq, half), F32)
    cos_h = jnp.concatenate([cos, cos, zeros + 1.0], axis=1)
    slo_h = jnp.concatenate([-sin, z8, zeros], axis=1)
    shi_h = jnp.concatenate([z8, sin, zeros], axis=1)
    two = lambda t: jnp.concatenate([t, t], axis=1)
    return two(cos_h), two(slo_h), two(shi_h)


def kernel(x, p, norm_mix, even_w_in, fox_b_f, gmlp_ln_g, gmlp_ln_b, gmlp_w_s, gmlp_b_s, fox_q_norm, fox_k_norm, even_w_out, odd_w_in, swa_q_norm, swa_k_norm, swa_sinks, conv_w, conv_ln_g, conv_ln_b, odd_w_out, norm_ffn, moe_w_coarse, moe_b_coarse, moe_w_fine, moe_b_fine, moe_w_gate, moe_w_up, moe_w_down, ple_w_proj, ple_norm, ple_w_gate):
    B, S, D = x.shape
    T = B * S
    h = x.reshape(T, D)
    p = p.reshape(p.shape[0], T, D_PLE)
    row = lambda v: v.reshape(1, -1)

    def moe_args(i):
        wr, br = _router_weights(moe_w_coarse[i], moe_b_coarse[i], moe_w_fine[i], moe_b_fine[i])
        return (i, row(norm_ffn[i]), wr, br, moe_w_gate, moe_w_up, moe_w_down, p,
                ple_w_proj[i].astype(BF16), row(ple_norm[i]), ple_w_gate[i].astype(BF16))

    n_main = 2 * GMLP_WIDTH + 3 * FOX_WIDTH
    w_f = jnp.pad(jnp.tile(even_w_in[0][:, n_main:], (1, BIAS_PIECES)),
                  ((0, 0), (0, LANES - BIAS_PIECES * FOX_HEADS)))
    w_in = jnp.concatenate([even_w_in[0][:, :n_main], w_f], axis=1).astype(BF16)
    b_f = jnp.pad(jnp.tile(fox_b_f[0], BIAS_PIECES), (0, LANES - BIAS_PIECES * FOX_HEADS)).reshape(1, LANES)
    bs_full = jnp.repeat(gmlp_b_s[0].T, HEAD_DIM, axis=1)
    ya, q, k, v, xq, xk = _proj_even(
        h, row(norm_mix[0]), w_in, b_f, row(gmlp_ln_g[0]), row(gmlp_ln_b[0]), gmlp_w_s[0], bs_full,
        row(jnp.tile(fox_q_norm[0], FOX_HEADS)), row(jnp.tile(fox_k_norm[0], FOX_HEADS)), seq=S)
    yb = _fox_attention(q, xq, k, xk, v, seq=S)
    h = _moe_ple(h, ya, yb, even_w_out[0].astype(BF16), *moe_args(0))

    order = jnp.array([0, 4, 1, 5, 2, 6, 3, 7])
    cols = (order[:, None] * HEAD_DIM + jnp.arange(HEAD_DIM)[None, :]).reshape(-1)
    w_odd = jnp.concatenate([odd_w_in[0][:, :SWA_WIDTH][:, cols], odd_w_in[0][:, SWA_WIDTH:]], axis=1).astype(BF16)
    w_out_odd = jnp.concatenate([odd_w_out[0][:SWA_WIDTH][cols], odd_w_out[0][SWA_WIDTH:]], axis=0).astype(BF16)
    cos, slo, shi = _rope_tables(S)
    q, k, v, glu = _proj_odd(h, row(norm_mix[1]), w_odd, row(jnp.tile(swa_q_norm[0], 8)),
                             row(jnp.tile(swa_k_norm[0], 2)), cos, slo, shi, seq=S)
    yc = _swa_attention(swa_sinks[0], q, k, v, seq=S)
    yd = _conv_module(glu, conv_w[0], row(conv_ln_g[0]), row(conv_ln_b[0]), seq=S)
    h = _moe_ple(h, yc, yd, w_out_odd, *moe_args(1))
    return h.reshape(B, S, D)
```
